```python
import math
import jax, jax.numpy as jnp
from jax import lax
import numpy as np

D_MODEL = 1024
BATCH = 8
SEQ = 4096
DEPTH = 2

HEAD_DIM = 64
D_MIX = D_MODEL
ATTN_WIDTH = D_MIX // 2
ATTN_Q_HEADS = ATTN_WIDTH // HEAD_DIM
ATTN_KV_HEADS = ATTN_Q_HEADS // 4
ATTN_GROUP = ATTN_Q_HEADS // ATTN_KV_HEADS
KV_WIDTH = ATTN_KV_HEADS * HEAD_DIM
WINDOW = 128
BLOCK = 128
ROPE_THETA = 500000.0
ROT_DIM = HEAD_DIM // 4
GM_WIDTH = D_MIX // 4
GM_HEADS = GM_WIDTH // HEAD_DIM
CHUNK = 128
CONV_CH = D_MIX - ATTN_WIDTH - GM_WIDTH
CONV_WIDTH = 31
CONV_PAD = CONV_WIDTH // 2
D_FF = ((8 * D_MODEL // 3 + 255) // 256) * 256
PLE_DIM = 256
EPS = 1e-6
NEG_INF = -1e30

Q_OFF = 0
K_OFF = Q_OFF + ATTN_WIDTH
V_OFF = K_OFF + KV_WIDTH
GM_OFF = V_OFF + KV_WIDTH
CONV_OFF = GM_OFF + 2 * GM_WIDTH
IN_COLS = CONV_OFF + 2 * CONV_CH

kernel_name = "hymba_style_hybrid_encoder_block"


def rms_norm(x, g):
    xf = x.astype(jnp.float32)
    y = xf * lax.rsqrt(jnp.mean(xf * xf, axis=-1, keepdims=True) + EPS)
    return (y * g.astype(jnp.float32)).astype(x.dtype)


def layer_norm(x, g, b):
    xf = x.astype(jnp.float32)
    mu = jnp.mean(xf, axis=-1, keepdims=True)
    xc = xf - mu
    y = xc * lax.rsqrt(jnp.mean(xc * xc, axis=-1, keepdims=True) + EPS)
    return (y * g.astype(jnp.float32) + b.astype(jnp.float32)).astype(x.dtype)


def rope_tables(positions):
    inv_freq = ROPE_THETA ** (-jnp.arange(0, ROT_DIM, 2, dtype=jnp.float32) / ROT_DIM)
    ang = positions.astype(jnp.float32)[..., None] * inv_freq
    return jnp.cos(ang)[:, :, None, :], jnp.sin(ang)[:, :, None, :]


def apply_partial_rope(x, cos, sin):
    xr = x[..., :ROT_DIM].astype(jnp.float32)
    half = ROT_DIM // 2
    x1, x2 = xr[..., :half], xr[..., half:]
    rot = jnp.concatenate([x1 * cos - x2 * sin, x2 * cos + x1 * sin], axis=-1)
    return jnp.concatenate([rot.astype(x.dtype), x[..., ROT_DIM:]], axis=-1)


def band_mask(n_blocks, seq):
    qi = jnp.arange(BLOCK)[:, None]
    kj = jnp.arange(3 * BLOCK)[None, :]
    rel = kj - BLOCK - qi
    in_band = jnp.abs(rel) <= WINDOW
    key_abs = jnp.arange(n_blocks)[:, None, None] * BLOCK - BLOCK + kj[None]
    in_range = (key_abs >= 0) & (key_abs < seq)
    return in_band[None] & in_range


def windowed_gqa_with_sink(q, k, v, sink):
    B, S = q.shape[0], q.shape[1]
    nb = S // BLOCK
    qb = q.reshape(B, nb, BLOCK, ATTN_KV_HEADS, ATTN_GROUP, HEAD_DIM)
    pad = ((0, 0), (BLOCK, BLOCK), (0, 0), (0, 0))
    kp = jnp.pad(k, pad).reshape(B, nb + 2, BLOCK, ATTN_KV_HEADS, HEAD_DIM)
    vp = jnp.pad(v, pad).reshape(B, nb + 2, BLOCK, ATTN_KV_HEADS, HEAD_DIM)
    kw = jnp.concatenate([kp[:, :-2], kp[:, 1:-1], kp[:, 2:]], axis=2)
    vw = jnp.concatenate([vp[:, :-2], vp[:, 1:-1], vp[:, 2:]], axis=2)
    s = jnp.einsum('bnqhgd,bnkhd->bnhgqk', qb, kw,
                   preferred_element_type=jnp.float32) * (1.0 / math.sqrt(HEAD_DIM))
    mask = band_mask(nb, S)[None, :, None, None]
    s = jnp.where(mask, s, NEG_INF)
    sink_l = sink.astype(jnp.float32).reshape(ATTN_KV_HEADS, ATTN_GROUP)[None, None, :, :, None, None]
    m = jnp.maximum(jnp.max(s, axis=-1, keepdims=True), sink_l)
    e = jnp.exp(s - m)
    denom = jnp.sum(e, axis=-1, keepdims=True) + jnp.exp(sink_l - m)
    pr = (e / denom).astype(v.dtype)
    o = jnp.einsum('bnhgqk,bnkhd->bnqhgd', pr, vw)
    return o.reshape(B, S, ATTN_WIDTH)


def spatial_gating(uv, ln_g, ln_b, ws, bs):
    B, S = uv.shape[0], uv.shape[1]
    u, v = uv[..., :GM_WIDTH], uv[..., GM_WIDTH:]
    v = layer_norm(v, ln_g, ln_b)
    vb = v.reshape(B, S // CHUNK, CHUNK, GM_HEADS, HEAD_DIM)
    sg = jnp.einsum('hpq,bnqhd->bnphd', ws, vb) + bs.T[None, None, :, :, None]
    return u * sg.reshape(B, S, GM_WIDTH)


def conformer_conv(ag, conv_w, conv_b, ln_g, ln_b):
    a, g = ag[..., :CONV_CH], ag[..., CONV_CH:]
    glu = a * jax.nn.sigmoid(g)
    y = lax.conv_general_dilated(glu, conv_w[:, None, :], window_strides=(1,),
                                 padding=[(CONV_PAD, CONV_PAD)],
                                 dimension_numbers=('NWC', 'WIO', 'NWC'),
                                 feature_group_count=CONV_CH) + conv_b
    y = layer_norm(y, ln_g, ln_b)
    return jax.nn.silu(y)


def setup_inputs(seed: int = 0) -> dict:
    key = jax.random.key(seed)
    ks = jax.random.split(key, 24)
    f32 = jnp.float32

    def nrm(k, shape, scale):
        return jax.random.normal(k, shape, f32) * scale

    def gain(k, shape):
        return 1.0 + 0.02 * jax.random.normal(k, shape, f32)

    x = jax.random.normal(ks[0], (BATCH, SEQ, D_MODEL), f32)
    p = jax.random.normal(ks[1], (DEPTH, BATCH, SEQ, PLE_DIM), f32)
    positions = jnp.broadcast_to(jnp.arange(SEQ, dtype=jnp.int32), (BATCH, SEQ))
    return {
        "x": x,
        "p": p,
        "positions": positions,
        "norm_mix_g": gain(ks[2], (DEPTH, D_MODEL)),
        "w_in": nrm(ks[3], (DEPTH, D_MODEL, IN_COLS), D_MODEL ** -0.5),
        "q_norm_g": gain(ks[4], (DEPTH, HEAD_DIM)),
        "k_norm_g": gain(ks[5], (DEPTH, HEAD_DIM)),
        "sink": nrm(ks[6], (DEPTH, ATTN_Q_HEADS), 0.5),
        "gm_ln_g": gain(ks[7], (DEPTH, GM_WIDTH)),
        "gm_ln_b": nrm(ks[8], (DEPTH, GM_WIDTH), 0.02),
        "gm_ws": nrm(ks[9], (DEPTH, GM_HEADS, CHUNK, CHUNK), CHUNK ** -0.5),
        "gm_bs": gain(ks[10], (DEPTH, GM_HEADS, CHUNK)),
        "conv_w": nrm(ks[11], (DEPTH, CONV_WIDTH, CONV_CH), CONV_WIDTH ** -0.5),
        "conv_b": nrm(ks[12], (DEPTH, CONV_CH), 0.02),
        "conv_ln_g": gain(ks[13], (DEPTH, CONV_CH)),
        "conv_ln_b": nrm(ks[14], (DEPTH, CONV_CH), 0.02),
        "out_norm_g": gain(ks[15], (DEPTH, D_MIX)),
        "w_out": nrm(ks[16], (DEPTH, D_MIX, D_MODEL), D_MIX ** -0.5),
        "norm_ffn_g": gain(ks[17], (DEPTH, D_MODEL)),
        "w_gate_up": nrm(ks[18], (DEPTH, D_MODEL, 2 * D_FF), D_MODEL ** -0.5),
        "w_down": nrm(ks[19], (DEPTH, D_FF, D_MODEL), D_FF ** -0.5),
        "ple_norm_g": gain(ks[20], (DEPTH, D_MODEL)),
        "w_ple_gate": nrm(ks[21], (DEPTH, D_MODEL, D_MODEL), D_MODEL ** -0.5),
        "w_ple_proj": nrm(ks[22], (DEPTH, PLE_DIM, D_MODEL), PLE_DIM ** -0.5),
    }


def reference(x, p, positions, norm_mix_g, w_in, q_norm_g, k_norm_g, sink,
              gm_ln_g, gm_ln_b, gm_ws, gm_bs, conv_w, conv_b, conv_ln_g, conv_ln_b,
              out_norm_g, w_out, norm_ffn_g, w_gate_up, w_down,
              ple_norm_g, w_ple_gate, w_ple_proj):
    B, S = x.shape[0], x.shape[1]
    cos, sin = rope_tables(positions)
    for i in range(DEPTH):
        h = rms_norm(x, norm_mix_g[i])
        z = h @ w_in[i]

        q = z[..., Q_OFF:K_OFF].reshape(B, S, ATTN_Q_HEADS, HEAD_DIM)
        k = z[..., K_OFF:V_OFF].reshape(B, S, ATTN_KV_HEADS, HEAD_DIM)
        v = z[..., V_OFF:GM_OFF].reshape(B, S, ATTN_KV_HEADS, HEAD_DIM)
        q = apply_partial_rope(rms_norm(q, q_norm_g[i]), cos, sin)
        k = apply_partial_rope(rms_norm(k, k_norm_g[i]), cos, sin)
        o_attn = windowed_gqa_with_sink(q, k, v, sink[i])

        uv = jax.nn.gelu(z[..., GM_OFF:CONV_OFF], approximate=False)
        o_gm = spatial_gating(uv, gm_ln_g[i], gm_ln_b[i], gm_ws[i], gm_bs[i])

        o_conv = conformer_conv(z[..., CONV_OFF:IN_COLS], conv_w[i], conv_b[i],
                                conv_ln_g[i], conv_ln_b[i])

        g_out = out_norm_g[i]
        merged = jnp.concatenate([
            rms_norm(o_attn, g_out[:ATTN_WIDTH]),
            rms_norm(o_gm, g_out[ATTN_WIDTH:ATTN_WIDTH + GM_WIDTH]),
            rms_norm(o_conv, g_out[ATTN_WIDTH + GM_WIDTH:]),
        ], axis=-1)
        x = x + merged @ w_out[i]

        hf = rms_norm(x, norm_ffn_g[i]) @ w_gate_up[i]
        x = x + (jax.nn.silu(hf[..., :D_FF]) * hf[..., D_FF:]) @ w_down[i]

        gate = jax.nn.sigmoid(rms_norm(x, ple_norm_g[i]) @ w_ple_gate[i])
        x = x + (p[i] @ w_ple_proj[i]) * gate
    return x
```

```python
import functools
import math

import numpy as np
import jax
import jax.numpy as jnp
from jax import lax
from jax.experimental import pallas as pl
from jax.experimental.pallas import tpu as pltpu

F32 = jnp.float32
BF16 = jnp.bfloat16

D_MODEL = 1024
HEAD_DIM = 64
ATTN_WIDTH = 512
ATTN_Q_HEADS = 8
ATTN_KV_HEADS = 2
ATTN_GROUP = 4
KV_WIDTH = 128
WINDOW = 128
BLOCK = 128
ROPE_THETA = 500000.0
ROT_DIM = 16
GM_WIDTH = 256
GM_HEADS = 4
CHUNK = 128
CONV_CH = 256
CONV_WIDTH = 31
CONV_PAD = 15
D_FF = 2816
PLE_DIM = 256
EPS = 1e-6
NEG_INF = -1e30

Q_OFF = 0
K_OFF = Q_OFF + ATTN_WIDTH
V_OFF = K_OFF + KV_WIDTH
GM_OFF = V_OFF + KV_WIDTH
CONV_OFF = GM_OFF + 2 * GM_WIDTH
IN_COLS = CONV_OFF + 2 * CONV_CH

LANES = 128
CONV_HALO = 16
VMEM_LIMIT = 56 * 1024 * 1024

TM_PROJ = 512
TQ_MIX = 512
TM_FFN = 512
FF_CHUNK = 256

INV_FREQ = [float(ROPE_THETA ** (-(2.0 * j) / ROT_DIM)) for j in range(ROT_DIM // 2)]
SQRT_HALF = float(np.sqrt(0.5))


def _const_spec(shape):
    nd = len(shape)
    return pl.BlockSpec(shape, lambda *_: (0,) * nd, pipeline_mode=pl.Buffered(1))


def _rope_table_kernel(pos_ref, c_ref, s1_ref, s2_ref):
    pos = pos_ref[...].astype(F32)
    lane = lax.broadcasted_iota(jnp.int32, (1, LANES), 1)
    in_head = lane & (HEAD_DIM - 1)
    fidx = lane & (ROT_DIM // 2 - 1)
    inv = jnp.zeros((1, LANES), F32)
    for j, f in enumerate(INV_FREQ):
        inv = jnp.where(fidx == j, f, inv)
    ang = pos * inv
    c = jnp.cos(ang)
    s = jnp.sin(ang)
    c_ref[...] = jnp.where(in_head < ROT_DIM, c, 1.0)
    s1_ref[...] = jnp.where(in_head < ROT_DIM // 2, -s, 0.0)
    s2_ref[...] = jnp.where((in_head >= ROT_DIM // 2) & (in_head < ROT_DIM), s, 0.0)


def _rope_tables(pos_col):
    n = pos_col.shape[0]
    tm = 1024
    tab = jax.ShapeDtypeStruct((n, LANES), F32)
    spec = pl.BlockSpec((tm, LANES), lambda i: (i, 0))
    return pl.pallas_call(
        _rope_table_kernel,
        out_shape=(tab, tab, tab),
        grid=(n // tm,),
        in_specs=[pl.BlockSpec((tm, 1), lambda i: (i, 0))],
        out_specs=(spec, spec, spec),
        name="rope_tables",
    )(pos_col)


def _rms(x, gain):
    ms = jnp.mean(x * x, axis=-1, keepdims=True)
    return x * lax.rsqrt(ms + EPS) * gain


def _layer_norm(x, gain, bias):
    mu = jnp.mean(x, axis=-1, keepdims=True)
    xc = x - mu
    var = jnp.mean(xc * xc, axis=-1, keepdims=True)
    return xc * lax.rsqrt(var + EPS) * gain + bias


def _inproj_kernel(x_ref, g_ref, w_ref, qg_ref, kg_ref, lng_ref, lnb_ref,
                   c_ref, s1_ref, s2_ref,
                   q_out, kv_out, u_out, vln_out, glu_out):
    h = _rms(x_ref[...], g_ref[...]).astype(BF16)

    r = lax.broadcasted_iota(jnp.int32, (LANES, LANES), 0) // HEAD_DIM
    c = lax.broadcasted_iota(jnp.int32, (LANES, LANES), 1) // HEAD_DIM
    head_ones = jnp.where(r == c, 1.0, 0.0).astype(BF16)
    cos_t, sin_a, sin_b = c_ref[...], s1_ref[...], s2_ref[...]

    def head_norm_rope(z, gain):
        sq = z * z
        hi = sq.astype(BF16)
        lo = (sq - hi.astype(F32)).astype(BF16)
        ss = (jnp.dot(hi, head_ones, preferred_element_type=F32)
              + jnp.dot(lo, head_ones, preferred_element_type=F32))
        zn = z * lax.rsqrt(ss * (1.0 / HEAD_DIM) + EPS) * gain
        return (zn * cos_t
                + pltpu.roll(zn, LANES - ROT_DIM // 2, 1) * sin_a
                + pltpu.roll(zn, ROT_DIM // 2, 1) * sin_b)

    zq = jnp.dot(h, w_ref[:, Q_OFF:K_OFF], preferred_element_type=F32)
    q_gain = qg_ref[...] * (1.0 / math.sqrt(HEAD_DIM))
    for b in range(ATTN_WIDTH // LANES):
        sl = slice(b * LANES, (b + 1) * LANES)
        q_out[:, sl] = head_norm_rope(zq[:, sl], q_gain).astype(BF16)

    zkv = jnp.dot(h, w_ref[:, K_OFF:GM_OFF], preferred_element_type=F32)
    kv_out[:, 0:KV_WIDTH] = head_norm_rope(zkv[:, 0:KV_WIDTH], kg_ref[...]).astype(BF16)
    kv_out[:, KV_WIDTH:2 * KV_WIDTH] = zkv[:, KV_WIDTH:2 * KV_WIDTH].astype(BF16)

    zuv = jnp.dot(h, w_ref[:, GM_OFF:CONV_OFF], preferred_element_type=F32)
    uv = 0.5 * zuv * (1.0 + lax.erf(zuv * SQRT_HALF))
    u_out[...] = uv[:, 0:GM_WIDTH]
    vln_out[...] = _layer_norm(uv[:, GM_WIDTH:], lng_ref[...], lnb_ref[...]).astype(BF16)

    zc = jnp.dot(h, w_ref[:, CONV_OFF:IN_COLS], preferred_element_type=F32)
    glu_out[...] = zc[:, 0:CONV_CH] * jax.nn.sigmoid(zc[:, CONV_CH:])


def _inproj(x, g, w, qg, kg, lng, lnb, tabs):
    n = x.shape[0]
    tm = TM_PROJ
    row = lambda width: pl.BlockSpec((tm, width), lambda i: (i, 0))
    out_shape = (
        jax.ShapeDtypeStruct((n, ATTN_WIDTH), BF16),
        jax.ShapeDtypeStruct((n, 2 * KV_WIDTH), BF16),
        jax.ShapeDtypeStruct((n, GM_WIDTH), F32),
        jax.ShapeDtypeStruct((n, GM_WIDTH), BF16),
        jax.ShapeDtypeStruct((n, CONV_CH), F32),
    )
    return pl.pallas_call(
        _inproj_kernel,
        out_shape=out_shape,
        grid=(n // tm,),
        in_specs=[row(D_MODEL), _const_spec((1, D_MODEL)), _const_spec((D_MODEL, IN_COLS)),
                  _const_spec((1, LANES)), _const_spec((1, LANES)),
                  _const_spec((1, GM_WIDTH)), _const_spec((1, GM_WIDTH)),
                  row(LANES), row(LANES), row(LANES)],
        out_specs=(row(ATTN_WIDTH), row(2 * KV_WIDTH), row(GM_WIDTH), row(GM_WIDTH), row(CONV_CH)),
        compiler_params=pltpu.CompilerParams(dimension_semantics=("arbitrary",),
                                             vmem_limit_bytes=VMEM_LIMIT),
        name="in_proj",
    )(x, g, w, qg, kg, lng, lnb, *tabs)


def _mixer_kernel(sink_ref, x_ref, q_ref, kvp_ref, kvc_ref, kvn_ref, u_ref, vln_ref,
                  glup_ref, gluc_ref, glun_ref,
                  ws_ref, gmb_ref, cw_ref, cb_ref, clg_ref, clb_ref, og_ref, wout_ref,
                  o_ref,
                  ka_s, kb_s, va_s, vb_s, glu_s, attn_s, gm_s, merged_s):
    tq = x_ref.shape[0]
    nblk = tq // BLOCK
    j = pl.program_id(1)
    first = j == 0
    last = j == pl.num_programs(1) - 1

    lane = lax.broadcasted_iota(jnp.int32, (1, LANES), 1)
    head0 = lane < HEAD_DIM
    zero = jnp.zeros((), BF16)
    for src, r0, rows in ((kvp_ref, 0, BLOCK), (kvc_ref, BLOCK, tq), (kvn_ref, BLOCK + tq, BLOCK)):
        k = src[:, 0:KV_WIDTH]
        v = src[:, KV_WIDTH:2 * KV_WIDTH]
        ka_s[r0:r0 + rows, :] = jnp.where(head0, k, zero)
        kb_s[r0:r0 + rows, :] = jnp.where(head0, zero, k)
        va_s[r0:r0 + rows, :] = jnp.where(head0, v, zero)
        vb_s[r0:r0 + rows, :] = jnp.where(head0, zero, v)

    qi = lax.broadcasted_iota(jnp.int32, (BLOCK, 3 * BLOCK), 0)
    kj = lax.broadcasted_iota(jnp.int32, (BLOCK, 3 * BLOCK), 1)
    rel = kj - BLOCK - qi
    in_band = (rel >= -WINDOW) & (rel <= WINDOW)

    for nb in range(nblk):
        lo = jnp.where(first, BLOCK, 0) if nb == 0 else 0
        hi = jnp.where(last, 2 * BLOCK, 3 * BLOCK) if nb == nblk - 1 else 3 * BLOCK
        valid = in_band & (kj >= lo) & (kj < hi)

        rows = slice(nb * BLOCK, (nb + 1) * BLOCK)
        win = slice(nb * BLOCK, (nb + 3) * BLOCK)
        q_st = jnp.concatenate(
            [q_ref[rows, g * LANES:(g + 1) * LANES] for g in range(ATTN_GROUP)], axis=0)
        k_st = jnp.concatenate([ka_s[win, :], kb_s[win, :]], axis=0)
        s = lax.dot_general(q_st, k_st, (((1,), (1,)), ((), ())), preferred_element_type=F32)
        p_rows = []
        for g in range(ATTN_GROUP):
            p_halves = []
            for half in range(ATTN_KV_HEADS):
                sink = sink_ref[half * ATTN_GROUP + g]
                sg = s[g * BLOCK:(g + 1) * BLOCK, half * 3 * BLOCK:(half + 1) * 3 * BLOCK]
                sg = jnp.where(valid, sg, NEG_INF)
                m = jnp.maximum(jnp.max(sg, axis=-1, keepdims=True), sink)
                e = jnp.exp(sg - m)
                denom = jnp.sum(e, axis=-1, keepdims=True) + jnp.exp(sink - m)
                p_halves.append((e * (1.0 / denom)).astype(BF16))
            p_rows.append(jnp.concatenate(p_halves, axis=1))
        p_st = jnp.concatenate(p_rows, axis=0)
        v_st = jnp.concatenate([va_s[win, :], vb_s[win, :]], axis=0)
        o = jnp.dot(p_st, v_st, preferred_element_type=F32)
        for g in range(ATTN_GROUP):
            attn_s[rows, g * LANES:(g + 1) * LANES] = o[g * BLOCK:(g + 1) * BLOCK, :]

        for hp in range(GM_HEADS // 2):
            cols = slice(hp * LANES, (hp + 1) * LANES)
            vblk = vln_ref[rows, cols]
            rhs = jnp.concatenate([jnp.where(head0, vblk, zero), jnp.where(head0, zero, vblk)], axis=0)
            sgate = jnp.dot(ws_ref[hp], rhs, preferred_element_type=F32) + gmb_ref[:, cols]
            gm_s[rows, cols] = u_ref[rows, cols] * sgate

    glu_s[0:CONV_HALO, :] = jnp.where(first, 0.0, glup_ref[...])
    glu_s[CONV_HALO:CONV_HALO + tq, :] = gluc_ref[...]
    glu_s[CONV_HALO + tq:, :] = jnp.where(last, 0.0, glun_ref[...])
    rchunk = 64
    for r0 in range(0, tq, rchunk):
        acc = jnp.broadcast_to(cb_ref[...], (rchunk, CONV_CH))
        for t in range(CONV_WIDTH):
            start = r0 + CONV_HALO - CONV_PAD + t
            acc = acc + cw_ref[t:t + 1, :] * glu_s[start:start + rchunk, :]
        y = _layer_norm(acc, clg_ref[...], clb_ref[...])
        y = y * jax.nn.sigmoid(y)
        merged_s[r0:r0 + rchunk, ATTN_WIDTH + GM_WIDTH:] = _rms(
            y, og_ref[:, ATTN_WIDTH + GM_WIDTH:]).astype(BF16)

    merged_s[:, 0:ATTN_WIDTH] = _rms(attn_s[...], og_ref[:, 0:ATTN_WIDTH]).astype(BF16)
    merged_s[:, ATTN_WIDTH:ATTN_WIDTH + GM_WIDTH] = _rms(
        gm_s[...], og_ref[:, ATTN_WIDTH:ATTN_WIDTH + GM_WIDTH]).astype(BF16)
    o_ref[...] = x_ref[...] + jnp.dot(merged_s[...], wout_ref[...], preferred_element_type=F32)


def _mixer(x, q, kv, u, vln, glu, sink, ws_cat, gm_bias, cw, cb, clg, clb, og, wout, batch, seq):
    n = x.shape[0]
    tq = TQ_MIX
    tiles = seq // tq
    blk_per_tile = tq // BLOCK
    blk_per_seq = seq // BLOCK
    halo_per_tile = tq // CONV_HALO
    halo_per_seq = seq // CONV_HALO

    def row(width):
        return pl.BlockSpec((tq, width), lambda b, j, *_: (b * tiles + j, 0))

    def prev(rows, width, per_tile, per_seq):
        return pl.BlockSpec(
            (rows, width), lambda b, j, *_: (b * per_seq + jnp.maximum(j * per_tile - 1, 0), 0))

    def nxt(rows, width, per_tile, per_seq):
        return pl.BlockSpec(
            (rows, width), lambda b, j, *_: (b * per_seq + jnp.minimum((j + 1) * per_tile, per_seq - 1), 0))

    def const(shape):
        nd = len(shape)
        return pl.BlockSpec(shape, lambda *_: (0,) * nd, pipeline_mode=pl.Buffered(1))

    grid_spec = pltpu.PrefetchScalarGridSpec(
        num_scalar_prefetch=1,
        grid=(batch, tiles),
        in_specs=[
            row(D_MODEL), row(ATTN_WIDTH),
            prev(BLOCK, 2 * KV_WIDTH, blk_per_tile, blk_per_seq), row(2 * KV_WIDTH),
            nxt(BLOCK, 2 * KV_WIDTH, blk_per_tile, blk_per_seq),
            row(GM_WIDTH), row(GM_WIDTH),
            prev(CONV_HALO, CONV_CH, halo_per_tile, halo_per_seq), row(CONV_CH),
            nxt(CONV_HALO, CONV_CH, halo_per_tile, halo_per_seq),
            const((GM_HEADS // 2, CHUNK, 2 * CHUNK)), const((CHUNK, GM_WIDTH)),
            const((CONV_WIDTH, CONV_CH)), const((1, CONV_CH)), const((1, CONV_CH)), const((1, CONV_CH)),
            const((1, D_MODEL)), const((D_MODEL, D_MODEL)),
        ],
        out_specs=row(D_MODEL),
        scratch_shapes=[
            pltpu.VMEM((tq + 2 * BLOCK, LANES), BF16), pltpu.VMEM((tq + 2 * BLOCK, LANES), BF16),
            pltpu.VMEM((tq + 2 * BLOCK, LANES), BF16), pltpu.VMEM((tq + 2 * BLOCK, LANES), BF16),
            pltpu.VMEM((tq + 2 * CONV_HALO, CONV_CH), F32),
            pltpu.VMEM((tq, ATTN_WIDTH), F32), pltpu.VMEM((tq, GM_WIDTH), F32),
            pltpu.VMEM((tq, D_MODEL), BF16),
        ],
    )
    return pl.pallas_call(
        _mixer_kernel,
        out_shape=jax.ShapeDtypeStruct((n, D_MODEL), F32),
        grid_spec=grid_spec,
        compiler_params=pltpu.CompilerParams(dimension_semantics=("arbitrary", "arbitrary"),
                                             vmem_limit_bytes=VMEM_LIMIT),
        name="mixer",
    )(sink, x, q, kv, kv, kv, u, vln, glu, glu, glu, ws_cat, gm_bias, cw, cb, clg, clb, og, wout)


def _ffn_kernel(x_ref, p_ref, gf_ref, wgu_ref, wd_ref, gp_ref, wpg_ref, wpp_ref, o_ref, act_s):
    x = x_ref[...]
    hn = _rms(x, gf_ref[...]).astype(BF16)
    for c0 in range(0, D_FF, FF_CHUNK):
        gate = jnp.dot(hn, wgu_ref[:, c0:c0 + FF_CHUNK], preferred_element_type=F32)
        up = jnp.dot(hn, wgu_ref[:, D_FF + c0:D_FF + c0 + FF_CHUNK], preferred_element_type=F32)
        act_s[:, c0:c0 + FF_CHUNK] = (gate * jax.nn.sigmoid(gate) * up).astype(BF16)
    x = x + jnp.dot(act_s[...], wd_ref[...], preferred_element_type=F32)

    hp = _rms(x, gp_ref[...]).astype(BF16)
    gate = jax.nn.sigmoid(jnp.dot(hp, wpg_ref[...], preferred_element_type=F32))
    proj = jnp.dot(p_ref[...].astype(BF16), wpp_ref[...], preferred_element_type=F32)
    o_ref[...] = x + proj * gate


def _ffn(x, p, gf, wgu, wd, gp, wpg, wpp):
    n = x.shape[0]
    tm = TM_FFN
    row = lambda width: pl.BlockSpec((tm, width), lambda i: (i, 0))
    return pl.pallas_call(
        _ffn_kernel,
        out_shape=jax.ShapeDtypeStruct((n, D_MODEL), F32),
        grid=(n // tm,),
        in_specs=[row(D_MODEL), row(PLE_DIM), _const_spec((1, D_MODEL)),
                  _const_spec((D_MODEL, 2 * D_FF)), _const_spec((D_FF, D_MODEL)),
                  _const_spec((1, D_MODEL)), _const_spec((D_MODEL, D_MODEL)),
                  _const_spec((PLE_DIM, D_MODEL))],
        out_specs=row(D_MODEL),
        scratch_shapes=[pltpu.VMEM((tm, D_FF), BF16)],
        compiler_params=pltpu.CompilerParams(dimension_semantics=("arbitrary",),
                                             vmem_limit_bytes=VMEM_LIMIT),
        name="ffn_ple",
    )(x, p, gf, wgu, wd, gp, wpg, wpp)


def _attn_col_perm():
    cols = []
    for g in range(ATTN_GROUP):
        for half in range(ATTN_KV_HEADS):
            h = half * ATTN_GROUP + g
            cols.extend(range(h * HEAD_DIM, (h + 1) * HEAD_DIM))
    return np.asarray(cols, dtype=np.int32)


def kernel(x, p, positions, norm_mix_g, w_in, q_norm_g, k_norm_g, sink, gm_ln_g, gm_ln_b, gm_ws, gm_bs, conv_w, conv_b, conv_ln_g, conv_ln_b, out_norm_g, w_out, norm_ffn_g, w_gate_up, w_down, ple_norm_g, w_ple_gate, w_ple_proj):
    batch, seq, d = x.shape
    depth = w_in.shape[0]
    n = batch * seq
    assert d == D_MODEL and seq % TQ_MIX == 0 and n % TM_PROJ == 0 and n % TM_FFN == 0 and n % 1024 == 0

    perm = _attn_col_perm()
    full_perm = np.concatenate([perm, np.arange(ATTN_WIDTH, D_MODEL, dtype=np.int32)])
    in_perm = np.concatenate([perm, np.arange(ATTN_WIDTH, IN_COLS, dtype=np.int32)])

    tabs = _rope_tables(positions.reshape(n, 1))
    xf = x.reshape(n, d)
    for i in range(depth):
        w_in_i = w_in[i][:, in_perm].astype(BF16)
        qg = jnp.tile(q_norm_g[i], LANES // HEAD_DIM)[None, :]
        kg = jnp.tile(k_norm_g[i], LANES // HEAD_DIM)[None, :]
        q, kv, u, vln, glu = _inproj(xf, norm_mix_g[i][None, :], w_in_i, qg, kg,
                                     gm_ln_g[i][None, :], gm_ln_b[i][None, :], tabs)

        ws = gm_ws[i].astype(BF16)
        ws_cat = jnp.concatenate([ws[0::2], ws[1::2]], axis=-1)
        gm_bias = jnp.repeat(gm_bs[i].T, HEAD_DIM, axis=1)
        og = out_norm_g[i][full_perm][None, :]
        wout = w_out[i][full_perm, :].astype(BF16)
        xf = _mixer(xf, q, kv, u, vln, glu, sink[i], ws_cat, gm_bias, conv_w[i], conv_b[i][None, :],
                    conv_ln_g[i][None, :], conv_ln_b[i][None, :], og, wout, batch, seq)

        xf = _ffn(xf, p[i].reshape(n, PLE_DIM), norm_ffn_g[i][None, :], w_gate_up[i].astype(BF16),
                  w_down[i].astype(BF16), ple_norm_g[i][None, :], w_ple_gate[i].astype(BF16),
                  w_ple_proj[i].astype(BF16))
    return xf.reshape(batch, seq, d)
```

```python
import functools
import math

import numpy as np
import jax
import jax.numpy as jnp
from jax import lax
from jax.experimental import pallas as pl
from jax.experimental.pallas import tpu as pltpu

F32 = jnp.float32
BF16 = jnp.bfloat16

D_MODEL = 1024
HEAD_DIM = 64
ATTN_WIDTH = 512
ATTN_Q_HEADS = 8
ATTN_KV_HEADS = 2
ATTN_GROUP = 4
KV_WIDTH = 128
WINDOW = 128
BLOCK = 128
ROPE_THETA = 500000.0
ROT_DIM = 16
GM_WIDTH = 256
GM_HEADS = 4
CHUNK = 128
CONV_CH = 256
CONV_WIDTH = 31
CONV_PAD = 15
D_FF = 2816
PLE_DIM = 256
EPS = 1e-6
NEG_INF = -1e30

Q_OFF = 0
K_OFF = Q_OFF + ATTN_WIDTH
V_OFF = K_OFF + KV_WIDTH
GM_OFF = V_OFF + KV_WIDTH
CONV_OFF = GM_OFF + 2 * GM_WIDTH
IN_COLS = CONV_OFF + 2 * CONV_CH

LANES = 128
SUBLANES = 8
CONV_HALO = 16
CONV_ROW_STRIDE = 2
CONV_GROUPS = 2
VMEM_LIMIT = 56 * 1024 * 1024

TM_ROPE = 2048
TM_PROJ = 512
TQ_MIX = 512
TM_FFN = 512
FF_CHUNK = 256
SOFTMAX_ROWS = 32

INV_FREQ = [float(ROPE_THETA ** (-(2.0 * j) / ROT_DIM)) for j in range(ROT_DIM // 2)]
SQRT_HALF = float(np.sqrt(0.5))
LOG2E = float(np.log2(np.e))


def _layer_spec(layer, shape):
    nd = len(shape)
    return pl.BlockSpec((None,) + tuple(shape), lambda *_: (layer,) + (0,) * nd,
                        pipeline_mode=pl.Buffered(1))


def _split_bf16(x):
    hi = x.astype(BF16)
    lo = (x - hi.astype(F32)).astype(BF16)
    return hi, lo


def _rope_table_kernel(pos_ref, c_ref, s1_ref, s2_ref):
    half = ROT_DIM // 2
    pos = pos_ref[...].astype(F32)
    row = lax.broadcasted_iota(jnp.int32, (ROT_DIM, 1), 0)
    inv = jnp.zeros((ROT_DIM, 1), F32)
    for j, f in enumerate(INV_FREQ):
        inv = jnp.where((row & (half - 1)) == j, f, inv)
    ang = inv * pos
    cs = jnp.where(row < half, jnp.cos(ang), jnp.sin(ang))

    r = lax.broadcasted_iota(jnp.int32, (ROT_DIM, 3 * LANES), 0)
    col = lax.broadcasted_iota(jnp.int32, (ROT_DIM, 3 * LANES), 1)
    table = col // LANES
    in_head = col & (HEAD_DIM - 1)
    freq = col & (half - 1)
    put_cos = (table == 0) & (in_head < ROT_DIM) & (r == freq)
    put_nsin = (table == 1) & (in_head < half) & (r == freq + half)
    put_sin = (table == 2) & (in_head >= half) & (in_head < ROT_DIM) & (r == freq + half)
    place = jnp.where(put_cos | put_sin, 1.0, jnp.where(put_nsin, -1.0, 0.0)).astype(BF16)

    hi, lo = _split_bf16(cs)
    dn = (((0,), (0,)), ((), ()))
    t = (lax.dot_general(hi, place, dn, preferred_element_type=F32)
         + lax.dot_general(lo, place, dn, preferred_element_type=F32))
    lane = lax.broadcasted_iota(jnp.int32, (1, LANES), 1)
    c_ref[...] = jnp.where((lane & (HEAD_DIM - 1)) < ROT_DIM, t[:, 0:LANES], 1.0)
    s1_ref[...] = t[:, LANES:2 * LANES]
    s2_ref[...] = t[:, 2 * LANES:3 * LANES]


def _rope_tables(pos_row):
    n = pos_row.shape[1]
    tm = TM_ROPE
    tab = jax.ShapeDtypeStruct((n, LANES), F32)
    spec = pl.BlockSpec((tm, LANES), lambda i: (i, 0))
    return pl.pallas_call(
        _rope_table_kernel,
        out_shape=(tab, tab, tab),
        grid=(n // tm,),
        in_specs=[pl.BlockSpec((1, tm), lambda i: (0, i))],
        out_specs=(spec, spec, spec),
        name="rope_tables",
    )(pos_row)


def _rms(x, gain):
    ms = jnp.mean(x * x, axis=-1, keepdims=True)
    return x * lax.rsqrt(ms + EPS) * gain


def _layer_norm(x, gain, bias):
    mu = jnp.mean(x, axis=-1, keepdims=True)
    xc = x - mu
    var = jnp.mean(xc * xc, axis=-1, keepdims=True)
    return xc * lax.rsqrt(var + EPS) * gain + bias


def _inproj_kernel(x_ref, g_ref, w_ref, qg_ref, kg_ref, lng_ref, lnb_ref,
                   c_ref, s1_ref, s2_ref,
                   q_out, kv_out, u_out, vln_out, glu_out):
    h = _rms(x_ref[...], g_ref[...]).astype(BF16)

    r = lax.broadcasted_iota(jnp.int32, (LANES, LANES), 0) // HEAD_DIM
    c = lax.broadcasted_iota(jnp.int32, (LANES, LANES), 1) // HEAD_DIM
    head_ones = jnp.where(r == c, 1.0, 0.0).astype(BF16)
    cos_t, sin_a, sin_b = c_ref[...], s1_ref[...], s2_ref[...]

    def head_norm_rope(z, gain):
        hi, lo = _split_bf16(z * z)
        ss = (jnp.dot(hi, head_ones, preferred_element_type=F32)
              + jnp.dot(lo, head_ones, preferred_element_type=F32))
        zn = z * lax.rsqrt(ss * (1.0 / HEAD_DIM) + EPS) * gain
        return (zn * cos_t
                + pltpu.roll(zn, LANES - ROT_DIM // 2, 1) * sin_a
                + pltpu.roll(zn, ROT_DIM // 2, 1) * sin_b)

    zq = jnp.dot(h, w_ref[:, Q_OFF:K_OFF], preferred_element_type=F32)
    q_gain = qg_ref[...] * (LOG2E / math.sqrt(HEAD_DIM))
    for b in range(ATTN_WIDTH // LANES):
        sl = slice(b * LANES, (b + 1) * LANES)
        q_out[b] = head_norm_rope(zq[:, sl], q_gain).astype(BF16)

    zkv = jnp.dot(h, w_ref[:, K_OFF:GM_OFF], preferred_element_type=F32)
    k = head_norm_rope(zkv[:, 0:KV_WIDTH], kg_ref[...])
    v = zkv[:, KV_WIDTH:2 * KV_WIDTH]
    kv_out[:, 0 * LANES:1 * LANES] = k.astype(BF16)
    kv_out[:, 1 * LANES:2 * LANES] = pltpu.roll(k, HEAD_DIM, 1).astype(BF16)
    kv_out[:, 2 * LANES:3 * LANES] = v.astype(BF16)
    kv_out[:, 3 * LANES:4 * LANES] = pltpu.roll(v, HEAD_DIM, 1).astype(BF16)

    zuv = jnp.dot(h, w_ref[:, GM_OFF:CONV_OFF], preferred_element_type=F32)
    uv = 0.5 * zuv * (1.0 + lax.erf(zuv * SQRT_HALF))
    u_out[...] = uv[:, 0:GM_WIDTH]
    vln_out[...] = _layer_norm(uv[:, GM_WIDTH:], lng_ref[...], lnb_ref[...]).astype(BF16)

    zc = jnp.dot(h, w_ref[:, CONV_OFF:IN_COLS], preferred_element_type=F32)
    glu_out[...] = zc[:, 0:CONV_CH] * jax.nn.sigmoid(zc[:, CONV_CH:])


def _inproj(layer, x, g, w, qg, kg, lng, lnb, tabs):
    n = x.shape[0]
    tm = TM_PROJ
    row = lambda width: pl.BlockSpec((tm, width), lambda i: (i, 0))
    out_shape = (
        jax.ShapeDtypeStruct((ATTN_WIDTH // LANES, n, LANES), BF16),
        jax.ShapeDtypeStruct((n, 4 * LANES), BF16),
        jax.ShapeDtypeStruct((n, GM_WIDTH), F32),
        jax.ShapeDtypeStruct((n, GM_WIDTH), BF16),
        jax.ShapeDtypeStruct((n, CONV_CH), F32),
    )
    return pl.pallas_call(
        _inproj_kernel,
        out_shape=out_shape,
        grid=(n // tm,),
        in_specs=[row(D_MODEL), _layer_spec(layer, (1, D_MODEL)), _layer_spec(layer, (D_MODEL, IN_COLS)),
                  _layer_spec(layer, (1, LANES)), _layer_spec(layer, (1, LANES)),
                  _layer_spec(layer, (1, GM_WIDTH)), _layer_spec(layer, (1, GM_WIDTH)),
                  row(LANES), row(LANES), row(LANES)],
        out_specs=(pl.BlockSpec((ATTN_WIDTH // LANES, tm, LANES), lambda i: (0, i, 0)),
                   row(4 * LANES), row(GM_WIDTH), row(GM_WIDTH), row(CONV_CH)),
        compiler_params=pltpu.CompilerParams(dimension_semantics=("arbitrary",),
                                             vmem_limit_bytes=VMEM_LIMIT),
        name="in_proj",
    )(x, g, w, qg, kg, lng, lnb, *tabs)


def _mixer_kernel(layer, sink_ref, x_ref, q_ref, kvp_ref, kvc_ref, kvn_ref, u_ref, vln_ref,
                  glup_ref, gluc_ref, glun_ref,
                  ws_ref, gmb_ref, cw_ref, cb_ref, clg_ref, clb_ref, og_ref, wout_ref,
                  o_ref,
                  ka_s, kb_s, va_s, vb_s, bias_s, s_s, p_s, scale_s, glu_s, y_s, attn_s, gm_s, merged_s):
    tq = x_ref.shape[0]
    nblk = tq // BLOCK
    j = pl.program_id(1)
    first = j == 0
    last = j == pl.num_programs(1) - 1

    lane = lax.broadcasted_iota(jnp.int32, (1, LANES), 1)
    head0 = lane < HEAD_DIM
    zero = jnp.zeros((), BF16)
    for src, r0, rows in ((kvp_ref, 0, BLOCK), (kvc_ref, BLOCK, tq), (kvn_ref, BLOCK + tq, BLOCK)):
        dst = slice(r0, r0 + rows)
        k, k_sw = src[:, 0 * LANES:1 * LANES], src[:, 1 * LANES:2 * LANES]
        v, v_sw = src[:, 2 * LANES:3 * LANES], src[:, 3 * LANES:4 * LANES]
        ka_s[0, dst, :] = jnp.where(head0, k, zero)
        kb_s[0, dst, :] = jnp.where(head0, zero, k_sw)
        ka_s[1, dst, :] = jnp.where(head0, k_sw, zero)
        kb_s[1, dst, :] = jnp.where(head0, zero, k)
        va_s[0, dst, :] = jnp.where(head0, v, zero)
        vb_s[0, dst, :] = jnp.where(head0, zero, v_sw)
        va_s[1, dst, :] = jnp.where(head0, v_sw, zero)
        vb_s[1, dst, :] = jnp.where(head0, zero, v)

    qi = lax.broadcasted_iota(jnp.int32, (BLOCK, BLOCK), 0)
    kj = lax.broadcasted_iota(jnp.int32, (BLOCK, BLOCK), 1)
    band_lo = jnp.where(kj >= qi, 0.0, NEG_INF)
    band_hi = jnp.where(kj <= qi, 0.0, NEG_INF)
    bias_s[0] = band_lo
    bias_s[1] = band_hi
    bias_s[2] = jnp.where(first, NEG_INF, band_lo)
    bias_s[3] = jnp.where(last, NEG_INF, band_hi)

    n_pairs = ATTN_GROUP // 2

    def scores(nb, h):
        r0 = pl.multiple_of(nb * BLOCK, BLOCK)
        q_st = jnp.concatenate([q_ref[n_pairs * h + pp, pl.ds(r0, BLOCK), :] for pp in range(n_pairs)], axis=0)
        k_st = jnp.concatenate([ka_s[h, pl.ds(r0, 3 * BLOCK), :], kb_s[h, pl.ds(r0, 3 * BLOCK), :]], axis=0)
        s_s[h] = lax.dot_general(q_st, k_st, (((1,), (1,)), ((), ())),
                                 preferred_element_type=F32)

    def softmax_pv(nb, h):
        r0 = pl.multiple_of(nb * BLOCK, BLOCK)
        lo_idx = jnp.where(nb == 0, 2, 0)
        hi_idx = jnp.where(nb == nblk - 1, 3, 1)
        v_st = jnp.concatenate([va_s[h, pl.ds(r0, 3 * BLOCK), :], vb_s[h, pl.ds(r0, 3 * BLOCK), :]], axis=0)
        for pp in range(n_pairs):
            for rc in range(0, BLOCK, SOFTMAX_ROWS):
                rr = slice(pp * BLOCK + rc, pp * BLOCK + rc + SOFTMAX_ROWS)
                br = slice(rc, rc + SOFTMAX_ROWS)
                inv = []
                for half in range(2):
                    sink = sink_ref[layer * ATTN_Q_HEADS + ATTN_GROUP * h + 2 * pp + half] * LOG2E
                    c0 = half * 3 * BLOCK
                    s0 = s_s[h, rr, c0:c0 + BLOCK] + bias_s[lo_idx, br, :]
                    s1 = s_s[h, rr, c0 + BLOCK:c0 + 2 * BLOCK]
                    s2 = s_s[h, rr, c0 + 2 * BLOCK:c0 + 3 * BLOCK] + bias_s[hi_idx, br, :]
                    m = jnp.max(jnp.maximum(jnp.maximum(s0, s1), s2), axis=-1, keepdims=True)
                    m = jnp.maximum(m, sink)
                    e0, e1, e2 = jnp.exp2(s0 - m), jnp.exp2(s1 - m), jnp.exp2(s2 - m)
                    denom = jnp.sum(e0 + e1 + e2, axis=-1, keepdims=True) + jnp.exp2(sink - m)
                    p_s[h, rr, c0:c0 + BLOCK] = e0.astype(BF16)
                    p_s[h, rr, c0 + BLOCK:c0 + 2 * BLOCK] = e1.astype(BF16)
                    p_s[h, rr, c0 + 2 * BLOCK:c0 + 3 * BLOCK] = e2.astype(BF16)
                    inv.append(1.0 / denom)
                scale_s[h, rr, :] = jnp.where(head0, inv[0], inv[1])
            pr = slice(pp * BLOCK, (pp + 1) * BLOCK)
            o = jnp.dot(p_s[h, pr, :], v_st, preferred_element_type=F32) * scale_s[h, pr, :]
            attn_s[n_pairs * h + pp, pl.ds(r0, BLOCK), :] = o

    scores(0, 0)

    def attn_step(nb, carry):
        scores(nb, 1)
        softmax_pv(nb, 0)
        scores(jnp.minimum(nb + 1, nblk - 1), 0)
        softmax_pv(nb, 1)
        return carry

    lax.fori_loop(0, nblk, attn_step, 0)

    for nb in range(nblk):
        rows = slice(nb * BLOCK, (nb + 1) * BLOCK)
        for hp in range(GM_HEADS // 2):
            cols = slice(hp * LANES, (hp + 1) * LANES)
            vblk = vln_ref[rows, cols]
            lhs = jnp.concatenate([ws_ref[2 * hp], ws_ref[2 * hp + 1]], axis=1)
            rhs = jnp.concatenate([jnp.where(head0, vblk, zero), jnp.where(head0, zero, vblk)], axis=0)
            sgate = jnp.dot(lhs, rhs, preferred_element_type=F32) + gmb_ref[:, cols]
            gm_s[rows, cols] = u_ref[rows, cols] * sgate

    n_ch = CONV_CH // LANES
    for ch in range(n_ch):
        cl = slice(ch * LANES, (ch + 1) * LANES)
        glu_s[ch, 0:CONV_HALO, :] = jnp.where(first, 0.0, glup_ref[:, cl])
        glu_s[ch, CONV_HALO:CONV_HALO + tq, :] = gluc_ref[:, cl]
        glu_s[ch, CONV_HALO + tq:, :] = jnp.where(last, 0.0, glun_ref[:, cl])
    group = SUBLANES * CONV_ROW_STRIDE

    def conv_step(i, carry):
        base = pl.multiple_of(i * (CONV_GROUPS * group), CONV_GROUPS * group)
        for ch in range(n_ch):
            cl = slice(ch * LANES, (ch + 1) * LANES)
            offs = [g * group + ph for g in range(CONV_GROUPS) for ph in range(CONV_ROW_STRIDE)]
            accs = [jnp.broadcast_to(cb_ref[:, cl], (SUBLANES, LANES)) for _ in offs]
            for t in range(CONV_WIDTH):
                w = cw_ref[t:t + 1, cl]
                for a, off in enumerate(offs):
                    start = base + (off + CONV_HALO - CONV_PAD + t)
                    accs[a] = accs[a] + w * glu_s[ch, pl.ds(start, SUBLANES, stride=CONV_ROW_STRIDE), :]
            for a, off in enumerate(offs):
                y_s[ch, pl.ds(base + off, SUBLANES, stride=CONV_ROW_STRIDE), :] = accs[a]
        return carry

    lax.fori_loop(0, tq // (CONV_GROUPS * group), conv_step, 0)

    rchunk = 128
    for r0 in range(0, tq, rchunk):
        rs = slice(r0, r0 + rchunk)
        y = jnp.concatenate([y_s[ch, rs, :] for ch in range(n_ch)], axis=1)
        y = _layer_norm(y, clg_ref[...], clb_ref[...])
        y = y * jax.nn.sigmoid(y)
        merged_s[rs, ATTN_WIDTH + GM_WIDTH:] = _rms(y, og_ref[:, ATTN_WIDTH + GM_WIDTH:]).astype(BF16)
        attn = jnp.concatenate([attn_s[pb, rs, :] for pb in range(ATTN_WIDTH // LANES)], axis=1)
        merged_s[rs, 0:ATTN_WIDTH] = _rms(attn, og_ref[:, 0:ATTN_WIDTH]).astype(BF16)
        merged_s[rs, ATTN_WIDTH:ATTN_WIDTH + GM_WIDTH] = _rms(
            gm_s[rs, :], og_ref[:, ATTN_WIDTH:ATTN_WIDTH + GM_WIDTH]).astype(BF16)
    o_ref[...] = x_ref[...] + jnp.dot(merged_s[...], wout_ref[...], preferred_element_type=F32)


def _mixer(layer, x, q, kv, u, vln, glu, sink, ws, gm_bias, cw, cb, clg, clb, og, wout, batch, seq):
    n = x.shape[0]
    tq = TQ_MIX
    tiles = seq // tq
    blk_per_tile = tq // BLOCK
    blk_per_seq = seq // BLOCK
    halo_per_tile = tq // CONV_HALO
    halo_per_seq = seq // CONV_HALO

    def row(width):
        return pl.BlockSpec((tq, width), lambda b, j, *_: (b * tiles + j, 0))

    def prev(rows, width, per_tile, per_seq):
        return pl.BlockSpec(
            (rows, width), lambda b, j, *_: (b * per_seq + jnp.maximum(j * per_tile - 1, 0), 0))

    def nxt(rows, width, per_tile, per_seq):
        return pl.BlockSpec(
            (rows, width), lambda b, j, *_: (b * per_seq + jnp.minimum((j + 1) * per_tile, per_seq - 1), 0))

    const = lambda shape: _layer_spec(layer, shape)
    kv_rows = tq + 2 * BLOCK
    grid_spec = pltpu.PrefetchScalarGridSpec(
        num_scalar_prefetch=1,
        grid=(batch, tiles),
        in_specs=[
            row(D_MODEL), pl.BlockSpec((ATTN_WIDTH // LANES, tq, LANES), lambda b, j, *_: (0, b * tiles + j, 0)),
            prev(BLOCK, 4 * LANES, blk_per_tile, blk_per_seq), row(4 * LANES),
            nxt(BLOCK, 4 * LANES, blk_per_tile, blk_per_seq),
            row(GM_WIDTH), row(GM_WIDTH),
            prev(CONV_HALO, CONV_CH, halo_per_tile, halo_per_seq), row(CONV_CH),
            nxt(CONV_HALO, CONV_CH, halo_per_tile, halo_per_seq),
            const((GM_HEADS, CHUNK, CHUNK)), const((CHUNK, GM_WIDTH)),
            const((CONV_WIDTH, CONV_CH)), const((1, CONV_CH)), const((1, CONV_CH)), const((1, CONV_CH)),
            const((1, D_MODEL)), const((D_MODEL, D_MODEL)),
        ],
        out_specs=row(D_MODEL),
        scratch_shapes=[
            pltpu.VMEM((ATTN_KV_HEADS, kv_rows, LANES), BF16), pltpu.VMEM((ATTN_KV_HEADS, kv_rows, LANES), BF16),
            pltpu.VMEM((ATTN_KV_HEADS, kv_rows, LANES), BF16), pltpu.VMEM((ATTN_KV_HEADS, kv_rows, LANES), BF16),
            pltpu.VMEM((4, BLOCK, BLOCK), F32),
            pltpu.VMEM((2, ATTN_GROUP // 2 * BLOCK, 2 * 3 * BLOCK), F32),
            pltpu.VMEM((2, ATTN_GROUP // 2 * BLOCK, 2 * 3 * BLOCK), BF16),
            pltpu.VMEM((2, ATTN_GROUP // 2 * BLOCK, LANES), F32),
            pltpu.VMEM((CONV_CH // LANES, tq + 2 * CONV_HALO, LANES), F32),
            pltpu.VMEM((CONV_CH // LANES, tq, LANES), F32),
            pltpu.VMEM((ATTN_WIDTH // LANES, tq, LANES), F32), pltpu.VMEM((tq, GM_WIDTH), F32),
            pltpu.VMEM((tq, D_MODEL), BF16),
        ],
    )
    return pl.pallas_call(
        functools.partial(_mixer_kernel, layer),
        out_shape=jax.ShapeDtypeStruct((n, D_MODEL), F32),
        grid_spec=grid_spec,
        compiler_params=pltpu.CompilerParams(dimension_semantics=("arbitrary", "arbitrary"),
                                             vmem_limit_bytes=VMEM_LIMIT),
        name="mixer",
    )(sink, x, q, kv, kv, kv, u, vln, glu, glu, glu, ws, gm_bias, cw, cb, clg, clb, og, wout)


def _ffn_kernel(x_ref, p_ref, gf_ref, wgu_ref, wd_ref, gp_ref, wpg_ref, wpp_ref, o_ref, act_s):
    x = x_ref[...]
    hn = _rms(x, gf_ref[...]).astype(BF16)
    for c0 in range(0, D_FF, FF_CHUNK):
        gate = jnp.dot(hn, wgu_ref[:, c0:c0 + FF_CHUNK], preferred_element_type=F32)
        up = jnp.dot(hn, wgu_ref[:, D_FF + c0:D_FF + c0 + FF_CHUNK], preferred_element_type=F32)
        act_s[:, c0:c0 + FF_CHUNK] = (gate * jax.nn.sigmoid(gate) * up).astype(BF16)
    x = x + jnp.dot(act_s[...], wd_ref[...], preferred_element_type=F32)

    hp = _rms(x, gp_ref[...]).astype(BF16)
    gate = jax.nn.sigmoid(jnp.dot(hp, wpg_ref[...], preferred_element_type=F32))
    proj = jnp.dot(p_ref[...].astype(BF16), wpp_ref[...], preferred_element_type=F32)
    o_ref[...] = x + proj * gate


def _ffn(layer, x, p, gf, wgu, wd, gp, wpg, wpp):
    n = x.shape[0]
    tm = TM_FFN
    row = lambda width: pl.BlockSpec((tm, width), lambda i: (i, 0))
    return pl.pallas_call(
        _ffn_kernel,
        out_shape=jax.ShapeDtypeStruct((n, D_MODEL), F32),
        grid=(n // tm,),
        in_specs=[row(D_MODEL), pl.BlockSpec((None, tm, PLE_DIM), lambda i: (layer, i, 0)),
                  _layer_spec(layer, (1, D_MODEL)),
                  _layer_spec(layer, (D_MODEL, 2 * D_FF)), _layer_spec(layer, (D_FF, D_MODEL)),
                  _layer_spec(layer, (1, D_MODEL)), _layer_spec(layer, (D_MODEL, D_MODEL)),
                  _layer_spec(layer, (PLE_DIM, D_MODEL))],
        out_specs=row(D_MODEL),
        scratch_shapes=[pltpu.VMEM((tm, D_FF), BF16)],
        compiler_params=pltpu.CompilerParams(dimension_semantics=("arbitrary",),
                                             vmem_limit_bytes=VMEM_LIMIT),
        name="ffn_ple",
    )(x, p, gf, wgu, wd, gp, wpg, wpp)


def kernel(x, p, positions, norm_mix_g, w_in, q_norm_g, k_norm_g, sink, gm_ln_g, gm_ln_b, gm_ws, gm_bs, conv_w, conv_b, conv_ln_g, conv_ln_b, out_norm_g, w_out, norm_ffn_g, w_gate_up, w_down, ple_norm_g, w_ple_gate, w_ple_proj):
    batch, seq, d = x.shape
    depth = w_in.shape[0]
    n = batch * seq
    assert d == D_MODEL and seq % TQ_MIX == 0
    assert n % TM_PROJ == 0 and n % TM_FFN == 0 and n % TM_ROPE == 0

    vec = lambda a: a[:, None, :]
    w_in_b, w_out_b = w_in.astype(BF16), w_out.astype(BF16)
    w_gu_b, w_down_b = w_gate_up.astype(BF16), w_down.astype(BF16)
    w_pg_b, w_pp_b = w_ple_gate.astype(BF16), w_ple_proj.astype(BF16)
    ws_b = gm_ws.astype(BF16)
    qg = vec(jnp.tile(q_norm_g, (1, LANES // HEAD_DIM)))
    kg = vec(jnp.tile(k_norm_g, (1, LANES // HEAD_DIM)))
    gm_bias = jnp.repeat(jnp.swapaxes(gm_bs, 1, 2), HEAD_DIM, axis=2)
    sink_flat = sink.reshape(depth * ATTN_Q_HEADS)
    p_flat = p.reshape(depth, n, PLE_DIM)

    tabs = _rope_tables(positions.reshape(1, n))
    xf = x.reshape(n, d)
    for i in range(depth):
        q, kv, u, vln, glu = _inproj(i, xf, vec(norm_mix_g), w_in_b, qg, kg, vec(gm_ln_g), vec(gm_ln_b), tabs)
        xf = _mixer(i, xf, q, kv, u, vln, glu, sink_flat, ws_b, gm_bias, conv_w, vec(conv_b),
                    vec(conv_ln_g), vec(conv_ln_b), vec(out_norm_g), w_out_b, batch, seq)
        xf = _ffn(i, xf, p_flat, vec(norm_ffn_g), w_gu_b, w_down_b, vec(ple_norm_g), w_pg_b, w_pp_b)
    return xf.reshape(batch, seq, d)
```

```python
import functools
import math

import numpy as np
import jax
import jax.numpy as jnp
from jax import lax
from jax.experimental import pallas as pl
from jax.experimental.pallas import tpu as pltpu

F32 = jnp.float32
BF16 = jnp.bfloat16

D_MODEL = 1024
HEAD_DIM = 64
ATTN_WIDTH = 512
ATTN_Q_HEADS = 8
ATTN_KV_HEADS = 2
ATTN_GROUP = 4
KV_WIDTH = 128
WINDOW = 128
BLOCK = 128
ROPE_THETA = 500000.0
ROT_DIM = 16
GM_WIDTH = 256
GM_HEADS = 4
CHUNK = 128
CONV_CH = 256
CONV_WIDTH = 31
CONV_PAD = 15
D_FF = 2816
PLE_DIM = 256
EPS = 1e-6
NEG_INF = -1e30

Q_OFF = 0
K_OFF = Q_OFF + ATTN_WIDTH
V_OFF = K_OFF + KV_WIDTH
GM_OFF = V_OFF + KV_WIDTH
CONV_OFF = GM_OFF + 2 * GM_WIDTH
IN_COLS = CONV_OFF + 2 * CONV_CH

LANES = 128
SUBLANES = 8
MXU_COLS = 256
CONV_HALO = 16
CONV_ROW_STRIDE = 2
CONV_GROUPS = 2
VMEM_LIMIT = 56 * 1024 * 1024

TM_ROPE = 2048
TM_PROJ = 512
TQ_MIX = 512
TM_FFN = 512
FF_CHUNK = 256
SOFTMAX_ROWS = 32

INV_FREQ = [float(ROPE_THETA ** (-(2.0 * j) / ROT_DIM)) for j in range(ROT_DIM // 2)]
SQRT_HALF = float(np.sqrt(0.5))
LOG2E = float(np.log2(np.e))


def _layer_spec(layer, shape):
    nd = len(shape)
    return pl.BlockSpec((None,) + tuple(shape), lambda *_: (layer,) + (0,) * nd,
                        pipeline_mode=pl.Buffered(1))


def _split_bf16(x):
    hi = x.astype(BF16)
    lo = (x - hi.astype(F32)).astype(BF16)
    return hi, lo


def _rope_table_kernel(pos_ref, c_ref, s1_ref, s2_ref):
    half = ROT_DIM // 2
    pos = pos_ref[...].astype(F32)
    row = lax.broadcasted_iota(jnp.int32, (ROT_DIM, 1), 0)
    inv = jnp.zeros((ROT_DIM, 1), F32)
    for j, f in enumerate(INV_FREQ):
        inv = jnp.where((row & (half - 1)) == j, f, inv)
    ang = inv * pos
    cs = jnp.where(row < half, jnp.cos(ang), jnp.sin(ang))

    r = lax.broadcasted_iota(jnp.int32, (ROT_DIM, 3 * LANES), 0)
    col = lax.broadcasted_iota(jnp.int32, (ROT_DIM, 3 * LANES), 1)
    table = col // LANES
    in_head = col & (HEAD_DIM - 1)
    freq = col & (half - 1)
    put_cos = (table == 0) & (in_head < ROT_DIM) & (r == freq)
    put_nsin = (table == 1) & (in_head < half) & (r == freq + half)
    put_sin = (table == 2) & (in_head >= half) & (in_head < ROT_DIM) & (r == freq + half)
    place = jnp.where(put_cos | put_sin, 1.0, jnp.where(put_nsin, -1.0, 0.0)).astype(BF16)

    hi, lo = _split_bf16(cs)
    dn = (((0,), (0,)), ((), ()))
    t = (lax.dot_general(hi, place, dn, preferred_element_type=F32)
         + lax.dot_general(lo, place, dn, preferred_element_type=F32))
    lane = lax.broadcasted_iota(jnp.int32, (1, LANES), 1)
    c_ref[...] = jnp.where((lane & (HEAD_DIM - 1)) < ROT_DIM, t[:, 0:LANES], 1.0)
    s1_ref[...] = t[:, LANES:2 * LANES]
    s2_ref[...] = t[:, 2 * LANES:3 * LANES]


def _rope_tables(pos_row):
    n = pos_row.shape[1]
    tm = TM_ROPE
    tab = jax.ShapeDtypeStruct((n, LANES), F32)
    spec = pl.BlockSpec((tm, LANES), lambda i: (i, 0))
    return pl.pallas_call(
        _rope_table_kernel,
        out_shape=(tab, tab, tab),
        grid=(n // tm,),
        in_specs=[pl.BlockSpec((1, tm), lambda i: (0, i))],
        out_specs=(spec, spec, spec),
        name="rope_tables",
    )(pos_row)


def _rms(x, gain):
    ms = jnp.mean(x * x, axis=-1, keepdims=True)
    return x * lax.rsqrt(ms + EPS) * gain


def _layer_norm(x, gain, bias):
    mu = jnp.mean(x, axis=-1, keepdims=True)
    xc = x - mu
    var = jnp.mean(xc * xc, axis=-1, keepdims=True)
    return xc * lax.rsqrt(var + EPS) * gain + bias


def _inproj_kernel(x_ref, g_ref, w_ref, qg_ref, kg_ref, lng_ref, lnb_ref,
                   c_ref, s1_ref, s2_ref,
                   q_out, kv_out, u_out, vln_out, glu_out, z_even, z_odd, *, last_is_even):
    i = pl.program_id(0)
    n_tiles = pl.num_programs(0) - 1

    def project(z_dst):
        h = _rms(x_ref[...], g_ref[...]).astype(BF16)
        z_dst[...] = jnp.dot(h, w_ref[...], preferred_element_type=F32)

    def finish(z_src):
        cos_t, sin_a, sin_b = c_ref[...], s1_ref[...], s2_ref[...]
        lane = lax.broadcasted_iota(jnp.int32, (1, LANES), 1)
        head0 = lane < HEAD_DIM

        def head_norm_rope(z, gain):
            sq = z * z
            first = jnp.sum(jnp.where(head0, sq, 0.0), axis=-1, keepdims=True)
            both = jnp.sum(sq, axis=-1, keepdims=True)
            ss = jnp.where(head0, first, both - first)
            zn = z * lax.rsqrt(ss * (1.0 / HEAD_DIM) + EPS) * gain
            return (zn * cos_t
                    + pltpu.roll(zn, LANES - ROT_DIM // 2, 1) * sin_a
                    + pltpu.roll(zn, ROT_DIM // 2, 1) * sin_b)

        q_gain = qg_ref[...] * (LOG2E / math.sqrt(HEAD_DIM))
        for b in range(ATTN_WIDTH // LANES):
            q_out[b] = head_norm_rope(z_src[:, Q_OFF + b * LANES:Q_OFF + (b + 1) * LANES], q_gain).astype(BF16)

        k = head_norm_rope(z_src[:, K_OFF:V_OFF], kg_ref[...])
        v = z_src[:, V_OFF:GM_OFF]
        kv_out[:, 0 * LANES:1 * LANES] = k.astype(BF16)
        kv_out[:, 1 * LANES:2 * LANES] = pltpu.roll(k, HEAD_DIM, 1).astype(BF16)
        kv_out[:, 2 * LANES:3 * LANES] = v.astype(BF16)
        kv_out[:, 3 * LANES:4 * LANES] = pltpu.roll(v, HEAD_DIM, 1).astype(BF16)

        zuv = z_src[:, GM_OFF:CONV_OFF]
        uv = 0.5 * zuv * (1.0 + lax.erf(zuv * SQRT_HALF))
        u_out[...] = uv[:, 0:GM_WIDTH]
        vln_out[...] = _layer_norm(uv[:, GM_WIDTH:], lng_ref[...], lnb_ref[...]).astype(BF16)

        glu_out[...] = z_src[:, CONV_OFF:CONV_OFF + CONV_CH] * jax.nn.sigmoid(z_src[:, CONV_OFF + CONV_CH:])

    inner = jnp.logical_and(i > 0, i < n_tiles)

    @pl.when(i == 0)
    def _():
        project(z_even)

    @pl.when(jnp.logical_and(inner, i % 2 == 1))
    def _():
        finish(z_even)
        project(z_odd)

    @pl.when(jnp.logical_and(inner, i % 2 == 0))
    def _():
        finish(z_odd)
        project(z_even)

    @pl.when(i == n_tiles)
    def _():
        finish(z_even if last_is_even else z_odd)


def _inproj(layer, x, g, w, qg, kg, lng, lnb, tabs):
    n = x.shape[0]
    tm = TM_PROJ
    n_tiles = n // tm
    cur = lambda width: pl.BlockSpec((tm, width), lambda i: (jnp.minimum(i, n_tiles - 1), 0))
    lag = lambda width: pl.BlockSpec((tm, width), lambda i: (jnp.maximum(i - 1, 0), 0))
    out_shape = (
        jax.ShapeDtypeStruct((ATTN_WIDTH // LANES, n, LANES), BF16),
        jax.ShapeDtypeStruct((n, 4 * LANES), BF16),
        jax.ShapeDtypeStruct((n, GM_WIDTH), F32),
        jax.ShapeDtypeStruct((n, GM_WIDTH), BF16),
        jax.ShapeDtypeStruct((n, CONV_CH), F32),
    )
    return pl.pallas_call(
        functools.partial(_inproj_kernel, last_is_even=(n_tiles - 1) % 2 == 0),
        out_shape=out_shape,
        grid=(n_tiles + 1,),
        in_specs=[cur(D_MODEL), _layer_spec(layer, (1, D_MODEL)), _layer_spec(layer, (D_MODEL, IN_COLS)),
                  _layer_spec(layer, (1, LANES)), _layer_spec(layer, (1, LANES)),
                  _layer_spec(layer, (1, GM_WIDTH)), _layer_spec(layer, (1, GM_WIDTH)),
                  lag(LANES), lag(LANES), lag(LANES)],
        out_specs=(pl.BlockSpec((ATTN_WIDTH // LANES, tm, LANES), lambda i: (0, jnp.maximum(i - 1, 0), 0)),
                   lag(4 * LANES), lag(GM_WIDTH), lag(GM_WIDTH), lag(CONV_CH)),
        scratch_shapes=[pltpu.VMEM((tm, IN_COLS), F32), pltpu.VMEM((tm, IN_COLS), F32)],
        compiler_params=pltpu.CompilerParams(dimension_semantics=("arbitrary",),
                                             vmem_limit_bytes=VMEM_LIMIT),
        name="in_proj",
    )(x, g, w, qg, kg, lng, lnb, *tabs)


def _mixer_kernel(layer, sink_ref, x_ref, q_ref, kvp_ref, kvc_ref, kvn_ref, u_ref, vln_ref,
                  glup_ref, gluc_ref, glun_ref,
                  ws_ref, gmb_ref, cw_ref, cb_ref, clg_ref, clb_ref, og_ref, wout_ref,
                  o_ref,
                  ka_s, kb_s, va_s, vb_s, bias_s, s_s, p_s, scale_s, glu_s, y_s, attn_s, gm_s, merged_s, convm_s, acc_s):
    tq = x_ref.shape[0]
    nblk = tq // BLOCK
    j = pl.program_id(1)
    first = j == 0
    last = j == pl.num_programs(1) - 1

    lane = lax.broadcasted_iota(jnp.int32, (1, LANES), 1)
    head0 = lane < HEAD_DIM
    zero = jnp.zeros((), BF16)
    for src, r0, rows in ((kvp_ref, 0, BLOCK), (kvc_ref, BLOCK, tq), (kvn_ref, BLOCK + tq, BLOCK)):
        dst = slice(r0, r0 + rows)
        k, k_sw = src[:, 0 * LANES:1 * LANES], src[:, 1 * LANES:2 * LANES]
        v, v_sw = src[:, 2 * LANES:3 * LANES], src[:, 3 * LANES:4 * LANES]
        ka_s[0, dst, :] = jnp.where(head0, k, zero)
        kb_s[0, dst, :] = jnp.where(head0, zero, k_sw)
        ka_s[1, dst, :] = jnp.where(head0, k_sw, zero)
        kb_s[1, dst, :] = jnp.where(head0, zero, k)
        va_s[0, dst, :] = jnp.where(head0, v, zero)
        vb_s[0, dst, :] = jnp.where(head0, zero, v_sw)
        va_s[1, dst, :] = jnp.where(head0, v_sw, zero)
        vb_s[1, dst, :] = jnp.where(head0, zero, v)

    qi = lax.broadcasted_iota(jnp.int32, (BLOCK, BLOCK), 0)
    kj = lax.broadcasted_iota(jnp.int32, (BLOCK, BLOCK), 1)
    band_lo = jnp.where(kj >= qi, 0.0, NEG_INF)
    band_hi = jnp.where(kj <= qi, 0.0, NEG_INF)
    bias_s[0] = band_lo
    bias_s[1] = band_hi
    bias_s[2] = jnp.where(first, NEG_INF, band_lo)
    bias_s[3] = jnp.where(last, NEG_INF, band_hi)

    n_pairs = ATTN_GROUP // 2

    def scores(nb, h):
        r0 = pl.multiple_of(nb * BLOCK, BLOCK)
        q_st = jnp.concatenate([q_ref[n_pairs * h + pp, pl.ds(r0, BLOCK), :] for pp in range(n_pairs)], axis=0)
        k_st = jnp.concatenate([ka_s[h, pl.ds(r0, 3 * BLOCK), :], kb_s[h, pl.ds(r0, 3 * BLOCK), :]], axis=0)
        s_s[h] = lax.dot_general(q_st, k_st, (((1,), (1,)), ((), ())),
                                 preferred_element_type=F32)

    def softmax(nb, h):
        lo_idx = jnp.where(nb == 0, 2, 0)
        hi_idx = jnp.where(nb == nblk - 1, 3, 1)
        for pp in range(n_pairs):
            for rc in range(0, BLOCK, SOFTMAX_ROWS):
                rr = slice(pp * BLOCK + rc, pp * BLOCK + rc + SOFTMAX_ROWS)
                br = slice(rc, rc + SOFTMAX_ROWS)
                inv = []
                for half in range(2):
                    sink = sink_ref[layer * ATTN_Q_HEADS + ATTN_GROUP * h + 2 * pp + half] * LOG2E
                    c0 = half * 3 * BLOCK
                    s0 = s_s[h, rr, c0:c0 + BLOCK] + bias_s[lo_idx, br, :]
                    s1 = s_s[h, rr, c0 + BLOCK:c0 + 2 * BLOCK]
                    s2 = s_s[h, rr, c0 + 2 * BLOCK:c0 + 3 * BLOCK] + bias_s[hi_idx, br, :]
                    m = jnp.max(jnp.maximum(jnp.maximum(s0, s1), s2), axis=-1, keepdims=True)
                    m = jnp.maximum(m, sink)
                    e0, e1, e2 = jnp.exp2(s0 - m), jnp.exp2(s1 - m), jnp.exp2(s2 - m)
                    denom = jnp.sum(e0 + e1 + e2, axis=-1, keepdims=True) + jnp.exp2(sink - m)
                    p_s[h, rr, c0:c0 + BLOCK] = e0.astype(BF16)
                    p_s[h, rr, c0 + BLOCK:c0 + 2 * BLOCK] = e1.astype(BF16)
                    p_s[h, rr, c0 + 2 * BLOCK:c0 + 3 * BLOCK] = e2.astype(BF16)
                    inv.append(1.0 / denom)
                scale_s[h, rr, :] = jnp.where(head0, inv[0], inv[1])

    def weighted_values(nb, h):
        r0 = pl.multiple_of(nb * BLOCK, BLOCK)
        v_st = jnp.concatenate([va_s[h, pl.ds(r0, 3 * BLOCK), :], vb_s[h, pl.ds(r0, 3 * BLOCK), :]], axis=0)
        for pp in range(n_pairs):
            pr = slice(pp * BLOCK, (pp + 1) * BLOCK)
            o = jnp.dot(p_s[h, pr, :], v_st, preferred_element_type=F32) * scale_s[h, pr, :]
            attn_s[n_pairs * h + pp, pl.ds(r0, BLOCK), :] = o

    def gate_block(nb):
        rows = pl.ds(pl.multiple_of(nb * BLOCK, BLOCK), BLOCK)
        for hp in range(GM_HEADS // 2):
            cols = slice(hp * LANES, (hp + 1) * LANES)
            vblk = vln_ref[rows, cols]
            lhs = jnp.concatenate([ws_ref[2 * hp], ws_ref[2 * hp + 1]], axis=1)
            rhs = jnp.concatenate([jnp.where(head0, vblk, zero), jnp.where(head0, zero, vblk)], axis=0)
            sgate = jnp.dot(lhs, rhs, preferred_element_type=F32) + gmb_ref[:, cols]
            gm_s[rows, cols] = u_ref[rows, cols] * sgate

    p_s[1] = jnp.zeros(p_s.shape[1:], BF16)
    scale_s[1] = jnp.zeros(scale_s.shape[1:], F32)
    scores(0, 0)

    def attn_step(nb, carry):
        scores(nb, 1)
        weighted_values(jnp.maximum(nb - 1, 0), 1)
        softmax(nb, 0)
        scores(jnp.minimum(nb + 1, nblk - 1), 0)
        softmax(nb, 1)
        weighted_values(nb, 0)
        gate_block(nb)
        return carry

    lax.fori_loop(0, nblk, attn_step, 0)
    weighted_values(nblk - 1, 1)

    rchunk = 128
    for r0 in range(0, tq, rchunk):
        rs = slice(r0, r0 + rchunk)
        attn = jnp.concatenate([attn_s[pb, rs, :] for pb in range(ATTN_WIDTH // LANES)], axis=1)
        merged_s[rs, 0:ATTN_WIDTH] = _rms(attn, og_ref[:, 0:ATTN_WIDTH]).astype(BF16)
        merged_s[rs, ATTN_WIDTH:] = _rms(gm_s[rs, :], og_ref[:, ATTN_WIDTH:ATTN_WIDTH + GM_WIDTH]).astype(BF16)

    n_ch = CONV_CH // LANES
    for ch in range(n_ch):
        cl = slice(ch * LANES, (ch + 1) * LANES)
        glu_s[ch, 0:CONV_HALO, :] = jnp.where(first, 0.0, glup_ref[:, cl])
        glu_s[ch, CONV_HALO:CONV_HALO + tq, :] = gluc_ref[:, cl]
        glu_s[ch, CONV_HALO + tq:, :] = jnp.where(last, 0.0, glun_ref[:, cl])
    group = SUBLANES * CONV_ROW_STRIDE
    conv_rows = CONV_GROUPS * group

    def conv_rows_at(base):
        for ch in range(n_ch):
            cl = slice(ch * LANES, (ch + 1) * LANES)
            offs = [g * group + ph for g in range(CONV_GROUPS) for ph in range(CONV_ROW_STRIDE)]
            accs = [jnp.broadcast_to(cb_ref[:, cl], (SUBLANES, LANES)) for _ in offs]
            for t in range(CONV_WIDTH):
                w = cw_ref[t:t + 1, cl]
                for a, off in enumerate(offs):
                    start = base + (off + CONV_HALO - CONV_PAD + t)
                    accs[a] = accs[a] + w * glu_s[ch, pl.ds(start, SUBLANES, stride=CONV_ROW_STRIDE), :]
            for a, off in enumerate(offs):
                y_s[ch, pl.ds(base + off, SUBLANES, stride=CONV_ROW_STRIDE), :] = accs[a]

    n_out_tiles = D_MODEL // MXU_COLS
    conv_per_tile = tq // conv_rows // n_out_tiles

    def proj_conv_step(c, carry):
        acc_s[c] = jnp.dot(merged_s[...], wout_ref[c, 0:ATTN_WIDTH + GM_WIDTH, :], preferred_element_type=F32)
        for k in range(conv_per_tile):
            conv_rows_at(pl.multiple_of((c * conv_per_tile + k) * conv_rows, conv_rows))
        return carry

    lax.fori_loop(0, n_out_tiles, proj_conv_step, 0)

    for r0 in range(0, tq, rchunk):
        rs = slice(r0, r0 + rchunk)
        y = jnp.concatenate([y_s[ch, rs, :] for ch in range(n_ch)], axis=1)
        y = _layer_norm(y, clg_ref[...], clb_ref[...])
        y = y * jax.nn.sigmoid(y)
        convm_s[rs, :] = _rms(y, og_ref[:, ATTN_WIDTH + GM_WIDTH:]).astype(BF16)
    for c in range(n_out_tiles):
        cols = slice(c * MXU_COLS, (c + 1) * MXU_COLS)
        o_ref[:, cols] = x_ref[:, cols] + acc_s[c] + jnp.dot(
            convm_s[...], wout_ref[c, ATTN_WIDTH + GM_WIDTH:, :], preferred_element_type=F32)


def _mixer(layer, x, q, kv, u, vln, glu, sink, ws, gm_bias, cw, cb, clg, clb, og, wout, batch, seq):
    n = x.shape[0]
    tq = TQ_MIX
    tiles = seq // tq
    blk_per_tile = tq // BLOCK
    blk_per_seq = seq // BLOCK
    halo_per_tile = tq // CONV_HALO
    halo_per_seq = seq // CONV_HALO

    def row(width):
        return pl.BlockSpec((tq, width), lambda b, j, *_: (b * tiles + j, 0))

    def prev(rows, width, per_tile, per_seq):
        return pl.BlockSpec(
            (rows, width), lambda b, j, *_: (b * per_seq + jnp.maximum(j * per_tile - 1, 0), 0))

    def nxt(rows, width, per_tile, per_seq):
        return pl.BlockSpec(
            (rows, width), lambda b, j, *_: (b * per_seq + jnp.minimum((j + 1) * per_tile, per_seq - 1), 0))

    const = lambda shape: _layer_spec(layer, shape)
    kv_rows = tq + 2 * BLOCK
    grid_spec = pltpu.PrefetchScalarGridSpec(
        num_scalar_prefetch=1,
        grid=(batch, tiles),
        in_specs=[
            row(D_MODEL), pl.BlockSpec((ATTN_WIDTH // LANES, tq, LANES), lambda b, j, *_: (0, b * tiles + j, 0)),
            prev(BLOCK, 4 * LANES, blk_per_tile, blk_per_seq), row(4 * LANES),
            nxt(BLOCK, 4 * LANES, blk_per_tile, blk_per_seq),
            row(GM_WIDTH), row(GM_WIDTH),
            prev(CONV_HALO, CONV_CH, halo_per_tile, halo_per_seq), row(CONV_CH),
            nxt(CONV_HALO, CONV_CH, halo_per_tile, halo_per_seq),
            const((GM_HEADS, CHUNK, CHUNK)), const((CHUNK, GM_WIDTH)),
            const((CONV_WIDTH, CONV_CH)), const((1, CONV_CH)), const((1, CONV_CH)), const((1, CONV_CH)),
            const((1, D_MODEL)), const((D_MODEL // MXU_COLS, D_MODEL, MXU_COLS)),
        ],
        out_specs=row(D_MODEL),
        scratch_shapes=[
            pltpu.VMEM((ATTN_KV_HEADS, kv_rows, LANES), BF16), pltpu.VMEM((ATTN_KV_HEADS, kv_rows, LANES), BF16),
            pltpu.VMEM((ATTN_KV_HEADS, kv_rows, LANES), BF16), pltpu.VMEM((ATTN_KV_HEADS, kv_rows, LANES), BF16),
            pltpu.VMEM((4, BLOCK, BLOCK), F32),
            pltpu.VMEM((2, ATTN_GROUP // 2 * BLOCK, 2 * 3 * BLOCK), F32),
            pltpu.VMEM((2, ATTN_GROUP // 2 * BLOCK, 2 * 3 * BLOCK), BF16),
            pltpu.VMEM((2, ATTN_GROUP // 2 * BLOCK, LANES), F32),
            pltpu.VMEM((CONV_CH // LANES, tq + 2 * CONV_HALO, LANES), F32),
            pltpu.VMEM((CONV_CH // LANES, tq, LANES), F32),
            pltpu.VMEM((ATTN_WIDTH // LANES, tq, LANES), F32), pltpu.VMEM((tq, GM_WIDTH), F32),
            pltpu.VMEM((tq, ATTN_WIDTH + GM_WIDTH), BF16), pltpu.VMEM((tq, CONV_CH), BF16),
            pltpu.VMEM((D_MODEL // MXU_COLS, tq, MXU_COLS), F32),
        ],
    )
    return pl.pallas_call(
        functools.partial(_mixer_kernel, layer),
        out_shape=jax.ShapeDtypeStruct((n, D_MODEL), F32),
        grid_spec=grid_spec,
        compiler_params=pltpu.CompilerParams(dimension_semantics=("arbitrary", "arbitrary"),
                                             vmem_limit_bytes=VMEM_LIMIT),
        name="mixer",
    )(sink, x, q, kv, kv, kv, u, vln, glu, glu, glu, ws, gm_bias, cw, cb, clg, clb, og, wout)


def _ffn_kernel(x_ref, p_ref, gf_ref, wgu_ref, wd_ref, gp_ref, wpg_ref, wpp_ref, o_ref, act_s):
    x = x_ref[...]
    hn = _rms(x, gf_ref[...]).astype(BF16)
    for c0 in range(0, D_FF, FF_CHUNK):
        gate = jnp.dot(hn, wgu_ref[:, c0:c0 + FF_CHUNK], preferred_element_type=F32)
        up = jnp.dot(hn, wgu_ref[:, D_FF + c0:D_FF + c0 + FF_CHUNK], preferred_element_type=F32)
        act_s[:, c0:c0 + FF_CHUNK] = (gate * jax.nn.sigmoid(gate) * up).astype(BF16)
    x = x + jnp.dot(act_s[...], wd_ref[...], preferred_element_type=F32)

    hp = _rms(x, gp_ref[...]).astype(BF16)
    gate = jax.nn.sigmoid(jnp.dot(hp, wpg_ref[...], preferred_element_type=F32))
    proj = jnp.dot(p_ref[...].astype(BF16), wpp_ref[...], preferred_element_type=F32)
    o_ref[...] = x + proj * gate


def _ffn(layer, x, p, gf, wgu, wd, gp, wpg, wpp):
    n = x.shape[0]
    tm = TM_FFN
    row = lambda width: pl.BlockSpec((tm, width), lambda i: (i, 0))
    return pl.pallas_call(
        _ffn_kernel,
        out_shape=jax.ShapeDtypeStruct((n, D_MODEL), F32),
        grid=(n // tm,),
        in_specs=[row(D_MODEL), pl.BlockSpec((None, tm, PLE_DIM), lambda i: (layer, i, 0)),
                  _layer_spec(layer, (1, D_MODEL)),
                  _layer_spec(layer, (D_MODEL, 2 * D_FF)), _layer_spec(layer, (D_FF, D_MODEL)),
                  _layer_spec(layer, (1, D_MODEL)), _layer_spec(layer, (D_MODEL, D_MODEL)),
                  _layer_spec(layer, (PLE_DIM, D_MODEL))],
        out_specs=row(D_MODEL),
        scratch_shapes=[pltpu.VMEM((tm, D_FF), BF16)],
        compiler_params=pltpu.CompilerParams(dimension_semantics=("arbitrary",),
                                             vmem_limit_bytes=VMEM_LIMIT),
        name="ffn_ple",
    )(x, p, gf, wgu, wd, gp, wpg, wpp)


def kernel(x, p, positions, norm_mix_g, w_in, q_norm_g, k_norm_g, sink, gm_ln_g, gm_ln_b, gm_ws, gm_bs, conv_w, conv_b, conv_ln_g, conv_ln_b, out_norm_g, w_out, norm_ffn_g, w_gate_up, w_down, ple_norm_g, w_ple_gate, w_ple_proj):
    batch, seq, d = x.shape
    depth = w_in.shape[0]
    n = batch * seq
    assert d == D_MODEL and seq % TQ_MIX == 0
    assert n % TM_PROJ == 0 and n % TM_FFN == 0 and n % TM_ROPE == 0

    vec = lambda a: a[:, None, :]
    w_in_b = w_in.astype(BF16)
    w_out_b = w_out.astype(BF16).reshape(depth, D_MODEL, D_MODEL // MXU_COLS, MXU_COLS).transpose(0, 2, 1, 3)
    w_gu_b, w_down_b = w_gate_up.astype(BF16), w_down.astype(BF16)
    w_pg_b, w_pp_b = w_ple_gate.astype(BF16), w_ple_proj.astype(BF16)
    ws_b = gm_ws.astype(BF16)
    qg = vec(jnp.tile(q_norm_g, (1, LANES // HEAD_DIM)))
    kg = vec(jnp.tile(k_norm_g, (1, LANES // HEAD_DIM)))
    gm_bias = jnp.repeat(jnp.swapaxes(gm_bs, 1, 2), HEAD_DIM, axis=2)
    sink_flat = sink.reshape(depth * ATTN_Q_HEADS)
    p_flat = p.reshape(depth, n, PLE_DIM)

    tabs = _rope_tables(positions.reshape(1, n))
    xf = x.reshape(n, d)
    for i in range(depth):
        q, kv, u, vln, glu = _inproj(i, xf, vec(norm_mix_g), w_in_b, qg, kg, vec(gm_ln_g), vec(gm_ln_b), tabs)
        xf = _mixer(i, xf, q, kv, u, vln, glu, sink_flat, ws_b, gm_bias, conv_w, vec(conv_b),
                    vec(conv_ln_g), vec(conv_ln_b), vec(out_norm_g), w_out_b, batch, seq)
        xf = _ffn(i, xf, p_flat, vec(norm_ffn_g), w_gu_b, w_down_b, vec(ple_norm_g), w_pg_b, w_pp_b)
    return xf.reshape(batch, seq, d)
```

```python
import functools
import math

import numpy as np
import jax
import jax.numpy as jnp
from jax import lax
from jax.experimental import pallas as pl
from jax.experimental.pallas import tpu as pltpu

F32 = jnp.float32
BF16 = jnp.bfloat16

D_MODEL = 1024
HEAD_DIM = 64
ATTN_WIDTH = 512
ATTN_Q_HEADS = 8
ATTN_KV_HEADS = 2
ATTN_GROUP = 4
KV_WIDTH = 128
WINDOW = 128
BLOCK = 128
ROPE_THETA = 500000.0
ROT_DIM = 16
GM_WIDTH = 256
GM_HEADS = 4
CHUNK = 128
CONV_CH = 256
CONV_WIDTH = 31
CONV_PAD = 15
D_FF = 2816
PLE_DIM = 256
EPS = 1e-6
NEG_INF = -1e30

Q_OFF = 0
K_OFF = Q_OFF + ATTN_WIDTH
V_OFF = K_OFF + KV_WIDTH
GM_OFF = V_OFF + KV_WIDTH
CONV_OFF = GM_OFF + 2 * GM_WIDTH
IN_COLS = CONV_OFF + 2 * CONV_CH

LANES = 128
SUBLANES = 8
CONV_HALO = 16
CONV_ROW_STRIDE = 2
CONV_GROUPS = 2
VMEM_LIMIT = 56 * 1024 * 1024

TM_ROPE = 2048
TM_PROJ = 512
TQ_MIX = 512
TM_FFN = 512
FF_CHUNK = 256
SOFTMAX_ROWS = 32

INV_FREQ = [float(ROPE_THETA ** (-(2.0 * j) / ROT_DIM)) for j in range(ROT_DIM // 2)]
SQRT_HALF = float(np.sqrt(0.5))
LOG2E = float(np.log2(np.e))


def _layer_spec(layer, shape):
    nd = len(shape)
    return pl.BlockSpec((None,) + tuple(shape), lambda *_: (layer,) + (0,) * nd,
                        pipeline_mode=pl.Buffered(1))


def _split_bf16(x):
    hi = x.astype(BF16)
    lo = (x - hi.astype(F32)).astype(BF16)
    return hi, lo


def _rope_table_kernel(pos_ref, c_ref, s1_ref, s2_ref):
    half = ROT_DIM // 2
    pos = pos_ref[...].astype(F32)
    row = lax.broadcasted_iota(jnp.int32, (ROT_DIM, 1), 0)
    inv = jnp.zeros((ROT_DIM, 1), F32)
    for j, f in enumerate(INV_FREQ):
        inv = jnp.where((row & (half - 1)) == j, f, inv)
    ang = inv * pos
    cs = jnp.where(row < half, jnp.cos(ang), jnp.sin(ang))

    r = lax.broadcasted_iota(jnp.int32, (ROT_DIM, 3 * LANES), 0)
    col = lax.broadcasted_iota(jnp.int32, (ROT_DIM, 3 * LANES), 1)
    table = col // LANES
    in_head = col & (HEAD_DIM - 1)
    freq = col & (half - 1)
    put_cos = (table == 0) & (in_head < ROT_DIM) & (r == freq)
    put_nsin = (table == 1) & (in_head < half) & (r == freq + half)
    put_sin = (table == 2) & (in_head >= half) & (in_head < ROT_DIM) & (r == freq + half)
    place = jnp.where(put_cos | put_sin, 1.0, jnp.where(put_nsin, -1.0, 0.0)).astype(BF16)

    hi, lo = _split_bf16(cs)
    dn = (((0,), (0,)), ((), ()))
    t = (lax.dot_general(hi, place, dn, preferred_element_type=F32)
         + lax.dot_general(lo, place, dn, preferred_element_type=F32))
    lane = lax.broadcasted_iota(jnp.int32, (1, LANES), 1)
    c_ref[...] = jnp.where((lane & (HEAD_DIM - 1)) < ROT_DIM, t[:, 0:LANES], 1.0)
    s1_ref[...] = t[:, LANES:2 * LANES]
    s2_ref[...] = t[:, 2 * LANES:3 * LANES]


def _rope_tables(pos_row):
    n = pos_row.shape[1]
    tm = TM_ROPE
    tab = jax.ShapeDtypeStruct((n, LANES), F32)
    spec = pl.BlockSpec((tm, LANES), lambda i: (i, 0))
    return pl.pallas_call(
        _rope_table_kernel,
        out_shape=(tab, tab, tab),
        grid=(n // tm,),
        in_specs=[pl.BlockSpec((1, tm), lambda i: (0, i))],
        out_specs=(spec, spec, spec),
        name="rope_tables",
    )(pos_row)


def _rms(x, gain):
    ms = jnp.mean(x * x, axis=-1, keepdims=True)
    return x * lax.rsqrt(ms + EPS) * gain


def _layer_norm(x, gain, bias):
    mu = jnp.mean(x, axis=-1, keepdims=True)
    xc = x - mu
    var = jnp.mean(xc * xc, axis=-1, keepdims=True)
    return xc * lax.rsqrt(var + EPS) * gain + bias


def _inproj_kernel(x_ref, g_ref, w_ref, qg_ref, kg_ref, lng_ref, lnb_ref,
                   c_ref, s1_ref, s2_ref,
                   q_out, kv_out, u_out, vln_out, glu_out, z_even, z_odd, *, last_is_even):
    i = pl.program_id(0)
    n_tiles = pl.num_programs(0) - 1

    def project(z_dst):
        h = _rms(x_ref[...], g_ref[...]).astype(BF16)
        z_dst[...] = jnp.dot(h, w_ref[...], preferred_element_type=F32)

    def finish(z_src):
        cos_t, sin_a, sin_b = c_ref[...], s1_ref[...], s2_ref[...]
        lane = lax.broadcasted_iota(jnp.int32, (1, LANES), 1)
        head0 = lane < HEAD_DIM

        def head_norm_rope(z, gain):
            sq = z * z
            first = jnp.sum(jnp.where(head0, sq, 0.0), axis=-1, keepdims=True)
            both = jnp.sum(sq, axis=-1, keepdims=True)
            ss = jnp.where(head0, first, both - first)
            zn = z * lax.rsqrt(ss * (1.0 / HEAD_DIM) + EPS) * gain
            return (zn * cos_t
                    + pltpu.roll(zn, LANES - ROT_DIM // 2, 1) * sin_a
                    + pltpu.roll(zn, ROT_DIM // 2, 1) * sin_b)

        q_gain = qg_ref[...] * (LOG2E / math.sqrt(HEAD_DIM))
        for b in range(ATTN_WIDTH // LANES):
            q_out[b] = head_norm_rope(z_src[:, Q_OFF + b * LANES:Q_OFF + (b + 1) * LANES], q_gain).astype(BF16)

        k = head_norm_rope(z_src[:, K_OFF:V_OFF], kg_ref[...])
        v = z_src[:, V_OFF:GM_OFF]
        kv_out[:, 0 * LANES:1 * LANES] = k.astype(BF16)
        kv_out[:, 1 * LANES:2 * LANES] = pltpu.roll(k, HEAD_DIM, 1).astype(BF16)
        kv_out[:, 2 * LANES:3 * LANES] = v.astype(BF16)
        kv_out[:, 3 * LANES:4 * LANES] = pltpu.roll(v, HEAD_DIM, 1).astype(BF16)

        zuv = z_src[:, GM_OFF:CONV_OFF]
        uv = 0.5 * zuv * (1.0 + lax.erf(zuv * SQRT_HALF))
        u_out[...] = uv[:, 0:GM_WIDTH]
        vln_out[...] = _layer_norm(uv[:, GM_WIDTH:], lng_ref[...], lnb_ref[...]).astype(BF16)

        glu_out[...] = z_src[:, CONV_OFF:CONV_OFF + CONV_CH] * jax.nn.sigmoid(z_src[:, CONV_OFF + CONV_CH:])

    inner = jnp.logical_and(i > 0, i < n_tiles)

    @pl.when(i == 0)
    def _():
        project(z_even)

    @pl.when(jnp.logical_and(inner, i % 2 == 1))
    def _():
        finish(z_even)
        project(z_odd)

    @pl.when(jnp.logical_and(inner, i % 2 == 0))
    def _():
        finish(z_odd)
        project(z_even)

    @pl.when(i == n_tiles)
    def _():
        finish(z_even if last_is_even else z_odd)


def _inproj(layer, x, g, w, qg, kg, lng, lnb, tabs):
    n = x.shape[0]
    tm = TM_PROJ
    n_tiles = n // tm
    cur = lambda width: pl.BlockSpec((tm, width), lambda i: (jnp.minimum(i, n_tiles - 1), 0))
    lag = lambda width: pl.BlockSpec((tm, width), lambda i: (jnp.maximum(i - 1, 0), 0))
    out_shape = (
        jax.ShapeDtypeStruct((ATTN_WIDTH // LANES, n, LANES), BF16),
        jax.ShapeDtypeStruct((n, 4 * LANES), BF16),
        jax.ShapeDtypeStruct((n, GM_WIDTH), F32),
        jax.ShapeDtypeStruct((n, GM_WIDTH), BF16),
        jax.ShapeDtypeStruct((n, CONV_CH), F32),
    )
    return pl.pallas_call(
        functools.partial(_inproj_kernel, last_is_even=(n_tiles - 1) % 2 == 0),
        out_shape=out_shape,
        grid=(n_tiles + 1,),
        in_specs=[cur(D_MODEL), _layer_spec(layer, (1, D_MODEL)), _layer_spec(layer, (D_MODEL, IN_COLS)),
                  _layer_spec(layer, (1, LANES)), _layer_spec(layer, (1, LANES)),
                  _layer_spec(layer, (1, GM_WIDTH)), _layer_spec(layer, (1, GM_WIDTH)),
                  lag(LANES), lag(LANES), lag(LANES)],
        out_specs=(pl.BlockSpec((ATTN_WIDTH // LANES, tm, LANES), lambda i: (0, jnp.maximum(i - 1, 0), 0)),
                   lag(4 * LANES), lag(GM_WIDTH), lag(GM_WIDTH), lag(CONV_CH)),
        scratch_shapes=[pltpu.VMEM((tm, IN_COLS), F32), pltpu.VMEM((tm, IN_COLS), F32)],
        compiler_params=pltpu.CompilerParams(dimension_semantics=("arbitrary",),
                                             vmem_limit_bytes=VMEM_LIMIT),
        name="in_proj",
    )(x, g, w, qg, kg, lng, lnb, *tabs)


def _mixer_kernel(layer, sink_ref, x_ref, q_ref, kvp_ref, kvc_ref, kvn_ref, u_ref, vln_ref,
                  glup_ref, gluc_ref, glun_ref,
                  ws_ref, gmb_ref, cw_ref, cb_ref, clg_ref, clb_ref, og_ref, wout_ref,
                  o_ref,
                  ka_s, kb_s, va_s, vb_s, bias_s, s_s, p_s, scale_s, glu_s, y_s, attn_s, gm_s, merged_s):
    tq = x_ref.shape[0]
    nblk = tq // BLOCK
    j = pl.program_id(1)
    first = j == 0
    last = j == pl.num_programs(1) - 1

    lane = lax.broadcasted_iota(jnp.int32, (1, LANES), 1)
    head0 = lane < HEAD_DIM
    zero = jnp.zeros((), BF16)
    for src, r0, rows in ((kvp_ref, 0, BLOCK), (kvc_ref, BLOCK, tq), (kvn_ref, BLOCK + tq, BLOCK)):
        dst = slice(r0, r0 + rows)
        k, k_sw = src[:, 0 * LANES:1 * LANES], src[:, 1 * LANES:2 * LANES]
        v, v_sw = src[:, 2 * LANES:3 * LANES], src[:, 3 * LANES:4 * LANES]
        ka_s[0, dst, :] = jnp.where(head0, k, zero)
        kb_s[0, dst, :] = jnp.where(head0, zero, k_sw)
        ka_s[1, dst, :] = jnp.where(head0, k_sw, zero)
        kb_s[1, dst, :] = jnp.where(head0, zero, k)
        va_s[0, dst, :] = jnp.where(head0, v, zero)
        vb_s[0, dst, :] = jnp.where(head0, zero, v_sw)
        va_s[1, dst, :] = jnp.where(head0, v_sw, zero)
        vb_s[1, dst, :] = jnp.where(head0, zero, v)

    qi = lax.broadcasted_iota(jnp.int32, (BLOCK, BLOCK), 0)
    kj = lax.broadcasted_iota(jnp.int32, (BLOCK, BLOCK), 1)
    band_lo = jnp.where(kj >= qi, 0.0, NEG_INF)
    band_hi = jnp.where(kj <= qi, 0.0, NEG_INF)
    bias_s[0] = band_lo
    bias_s[1] = band_hi
    bias_s[2] = jnp.where(first, NEG_INF, band_lo)
    bias_s[3] = jnp.where(last, NEG_INF, band_hi)

    n_pairs = ATTN_GROUP // 2

    def scores(nb, h):
        r0 = pl.multiple_of(nb * BLOCK, BLOCK)
        q_st = jnp.concatenate([q_ref[n_pairs * h + pp, pl.ds(r0, BLOCK), :] for pp in range(n_pairs)], axis=0)
        k_st = jnp.concatenate([ka_s[h, pl.ds(r0, 3 * BLOCK), :], kb_s[h, pl.ds(r0, 3 * BLOCK), :]], axis=0)
        s_s[h] = lax.dot_general(q_st, k_st, (((1,), (1,)), ((), ())),
                                 preferred_element_type=F32)

    def softmax_pv(nb, h):
        r0 = pl.multiple_of(nb * BLOCK, BLOCK)
        lo_idx = jnp.where(nb == 0, 2, 0)
        hi_idx = jnp.where(nb == nblk - 1, 3, 1)
        v_st = jnp.concatenate([va_s[h, pl.ds(r0, 3 * BLOCK), :], vb_s[h, pl.ds(r0, 3 * BLOCK), :]], axis=0)
        for pp in range(n_pairs):
            for rc in range(0, BLOCK, SOFTMAX_ROWS):
                rr = slice(pp * BLOCK + rc, pp * BLOCK + rc + SOFTMAX_ROWS)
                br = slice(rc, rc + SOFTMAX_ROWS)
                inv = []
                for half in range(2):
                    sink = sink_ref[layer * ATTN_Q_HEADS + ATTN_GROUP * h + 2 * pp + half] * LOG2E
                    c0 = half * 3 * BLOCK
                    s0 = s_s[h, rr, c0:c0 + BLOCK] + bias_s[lo_idx, br, :]
                    s1 = s_s[h, rr, c0 + BLOCK:c0 + 2 * BLOCK]
                    s2 = s_s[h, rr, c0 + 2 * BLOCK:c0 + 3 * BLOCK] + bias_s[hi_idx, br, :]
                    m = jnp.max(jnp.maximum(jnp.maximum(s0, s1), s2), axis=-1, keepdims=True)
                    m = jnp.maximum(m, sink)
                    e0, e1, e2 = jnp.exp2(s0 - m), jnp.exp2(s1 - m), jnp.exp2(s2 - m)
                    denom = jnp.sum(e0 + e1 + e2, axis=-1, keepdims=True) + jnp.exp2(sink - m)
                    p_s[h, rr, c0:c0 + BLOCK] = e0.astype(BF16)
                    p_s[h, rr, c0 + BLOCK:c0 + 2 * BLOCK] = e1.astype(BF16)
                    p_s[h, rr, c0 + 2 * BLOCK:c0 + 3 * BLOCK] = e2.astype(BF16)
                    inv.append(1.0 / denom)
                scale_s[h, rr, :] = jnp.where(head0, inv[0], inv[1])
            pr = slice(pp * BLOCK, (pp + 1) * BLOCK)
            o = jnp.dot(p_s[h, pr, :], v_st, preferred_element_type=F32) * scale_s[h, pr, :]
            attn_s[n_pairs * h + pp, pl.ds(r0, BLOCK), :] = o

    scores(0, 0)

    def attn_step(nb, carry):
        scores(nb, 1)
        softmax_pv(nb, 0)
        scores(jnp.minimum(nb + 1, nblk - 1), 0)
        softmax_pv(nb, 1)
        return carry

    lax.fori_loop(0, nblk, attn_step, 0)

    for nb in range(nblk):
        rows = slice(nb * BLOCK, (nb + 1) * BLOCK)
        for hp in range(GM_HEADS // 2):
            cols = slice(hp * LANES, (hp + 1) * LANES)
            vblk = vln_ref[rows, cols]
            lhs = jnp.concatenate([ws_ref[2 * hp], ws_ref[2 * hp + 1]], axis=1)
            rhs = jnp.concatenate([jnp.where(head0, vblk, zero), jnp.where(head0, zero, vblk)], axis=0)
            sgate = jnp.dot(lhs, rhs, preferred_element_type=F32) + gmb_ref[:, cols]
            gm_s[rows, cols] = u_ref[rows, cols] * sgate

    n_ch = CONV_CH // LANES
    for ch in range(n_ch):
        cl = slice(ch * LANES, (ch + 1) * LANES)
        glu_s[ch, 0:CONV_HALO, :] = jnp.where(first, 0.0, glup_ref[:, cl])
        glu_s[ch, CONV_HALO:CONV_HALO + tq, :] = gluc_ref[:, cl]
        glu_s[ch, CONV_HALO + tq:, :] = jnp.where(last, 0.0, glun_ref[:, cl])
    group = SUBLANES * CONV_ROW_STRIDE

    def conv_step(i, carry):
        base = pl.multiple_of(i * (CONV_GROUPS * group), CONV_GROUPS * group)
        for ch in range(n_ch):
            cl = slice(ch * LANES, (ch + 1) * LANES)
            offs = [g * group + ph for g in range(CONV_GROUPS) for ph in range(CONV_ROW_STRIDE)]
            accs = [jnp.broadcast_to(cb_ref[:, cl], (SUBLANES, LANES)) for _ in offs]
            for t in range(CONV_WIDTH):
                w = cw_ref[t:t + 1, cl]
                for a, off in enumerate(offs):
                    start = base + (off + CONV_HALO - CONV_PAD + t)
                    accs[a] = accs[a] + w * glu_s[ch, pl.ds(start, SUBLANES, stride=CONV_ROW_STRIDE), :]
            for a, off in enumerate(offs):
                y_s[ch, pl.ds(base + off, SUBLANES, stride=CONV_ROW_STRIDE), :] = accs[a]
        return carry

    lax.fori_loop(0, tq // (CONV_GROUPS * group), conv_step, 0)

    rchunk = 128
    for r0 in range(0, tq, rchunk):
        rs = slice(r0, r0 + rchunk)
        y = jnp.concatenate([y_s[ch, rs, :] for ch in range(n_ch)], axis=1)
        y = _layer_norm(y, clg_ref[...], clb_ref[...])
        y = y * jax.nn.sigmoid(y)
        merged_s[rs, ATTN_WIDTH + GM_WIDTH:] = _rms(y, og_ref[:, ATTN_WIDTH + GM_WIDTH:]).astype(BF16)
        attn = jnp.concatenate([attn_s[pb, rs, :] for pb in range(ATTN_WIDTH // LANES)], axis=1)
        merged_s[rs, 0:ATTN_WIDTH] = _rms(attn, og_ref[:, 0:ATTN_WIDTH]).astype(BF16)
        merged_s[rs, ATTN_WIDTH:ATTN_WIDTH + GM_WIDTH] = _rms(
            gm_s[rs, :], og_ref[:, ATTN_WIDTH:ATTN_WIDTH + GM_WIDTH]).astype(BF16)
    o_ref[...] = x_ref[...] + jnp.dot(merged_s[...], wout_ref[...], preferred_element_type=F32)


def _mixer(layer, x, q, kv, u, vln, glu, sink, ws, gm_bias, cw, cb, clg, clb, og, wout, batch, seq):
    n = x.shape[0]
    tq = TQ_MIX
    tiles = seq // tq
    blk_per_tile = tq // BLOCK
    blk_per_seq = seq // BLOCK
    halo_per_tile = tq // CONV_HALO
    halo_per_seq = seq // CONV_HALO

    def row(width):
        return pl.BlockSpec((tq, width), lambda b, j, *_: (b * tiles + j, 0))

    def prev(rows, width, per_tile, per_seq):
        return pl.BlockSpec(
            (rows, width), lambda b, j, *_: (b * per_seq + jnp.maximum(j * per_tile - 1, 0), 0))

    def nxt(rows, width, per_tile, per_seq):
        return pl.BlockSpec(
            (rows, width), lambda b, j, *_: (b * per_seq + jnp.minimum((j + 1) * per_tile, per_seq - 1), 0))

    const = lambda shape: _layer_spec(layer, shape)
    kv_rows = tq + 2 * BLOCK
    grid_spec = pltpu.PrefetchScalarGridSpec(
        num_scalar_prefetch=1,
        grid=(batch, tiles),
        in_specs=[
            row(D_MODEL), pl.BlockSpec((ATTN_WIDTH // LANES, tq, LANES), lambda b, j, *_: (0, b * tiles + j, 0)),
            prev(BLOCK, 4 * LANES, blk_per_tile, blk_per_seq), row(4 * LANES),
            nxt(BLOCK, 4 * LANES, blk_per_tile, blk_per_seq),
            row(GM_WIDTH), row(GM_WIDTH),
            prev(CONV_HALO, CONV_CH, halo_per_tile, halo_per_seq), row(CONV_CH),
            nxt(CONV_HALO, CONV_CH, halo_per_tile, halo_per_seq),
            const((GM_HEADS, CHUNK, CHUNK)), const((CHUNK, GM_WIDTH)),
            const((CONV_WIDTH, CONV_CH)), const((1, CONV_CH)), const((1, CONV_CH)), const((1, CONV_CH)),
            const((1, D_MODEL)), const((D_MODEL, D_MODEL)),
        ],
        out_specs=row(D_MODEL),
        scratch_shapes=[
            pltpu.VMEM((ATTN_KV_HEADS, kv_rows, LANES), BF16), pltpu.VMEM((ATTN_KV_HEADS, kv_rows, LANES), BF16),
            pltpu.VMEM((ATTN_KV_HEADS, kv_rows, LANES), BF16), pltpu.VMEM((ATTN_KV_HEADS, kv_rows, LANES), BF16),
            pltpu.VMEM((4, BLOCK, BLOCK), F32),
            pltpu.VMEM((2, ATTN_GROUP // 2 * BLOCK, 2 * 3 * BLOCK), F32),
            pltpu.VMEM((2, ATTN_GROUP // 2 * BLOCK, 2 * 3 * BLOCK), BF16),
            pltpu.VMEM((2, ATTN_GROUP // 2 * BLOCK, LANES), F32),
            pltpu.VMEM((CONV_CH // LANES, tq + 2 * CONV_HALO, LANES), F32),
            pltpu.VMEM((CONV_CH // LANES, tq, LANES), F32),
            pltpu.VMEM((ATTN_WIDTH // LANES, tq, LANES), F32), pltpu.VMEM((tq, GM_WIDTH), F32),
            pltpu.VMEM((tq, D_MODEL), BF16),
        ],
    )
    return pl.pallas_call(
        functools.partial(_mixer_kernel, layer),
        out_shape=jax.ShapeDtypeStruct((n, D_MODEL), F32),
        grid_spec=grid_spec,
        compiler_params=pltpu.CompilerParams(dimension_semantics=("arbitrary", "arbitrary"),
                                             vmem_limit_bytes=VMEM_LIMIT),
        name="mixer",
    )(sink, x, q, kv, kv, kv, u, vln, glu, glu, glu, ws, gm_bias, cw, cb, clg, clb, og, wout)


def _ffn_kernel(x_ref, p_ref, gf_ref, wgu_ref, wd_ref, gp_ref, wpg_ref, wpp_ref, o_ref, act_s):
    x = x_ref[...]
    hn = _rms(x, gf_ref[...]).astype(BF16)
    for c0 in range(0, D_FF, FF_CHUNK):
        gate = jnp.dot(hn, wgu_ref[:, c0:c0 + FF_CHUNK], preferred_element_type=F32)
        up = jnp.dot(hn, wgu_ref[:, D_FF + c0:D_FF + c0 + FF_CHUNK], preferred_element_type=F32)
        act_s[:, c0:c0 + FF_CHUNK] = (gate * jax.nn.sigmoid(gate) * up).astype(BF16)
    x = x + jnp.dot(act_s[...], wd_ref[...], preferred_element_type=F32)

    hp = _rms(x, gp_ref[...]).astype(BF16)
    gate = jax.nn.sigmoid(jnp.dot(hp, wpg_ref[...], preferred_element_type=F32))
    proj = jnp.dot(p_ref[...].astype(BF16), wpp_ref[...], preferred_element_type=F32)
    o_ref[...] = x + proj * gate


def _ffn(layer, x, p, gf, wgu, wd, gp, wpg, wpp):
    n = x.shape[0]
    tm = TM_FFN
    row = lambda width: pl.BlockSpec((tm, width), lambda i: (i, 0))
    return pl.pallas_call(
        _ffn_kernel,
        out_shape=jax.ShapeDtypeStruct((n, D_MODEL), F32),
        grid=(n // tm,),
        in_specs=[row(D_MODEL), pl.BlockSpec((None, tm, PLE_DIM), lambda i: (layer, i, 0)),
                  _layer_spec(layer, (1, D_MODEL)),
                  _layer_spec(layer, (D_MODEL, 2 * D_FF)), _layer_spec(layer, (D_FF, D_MODEL)),
                  _layer_spec(layer, (1, D_MODEL)), _layer_spec(layer, (D_MODEL, D_MODEL)),
                  _layer_spec(layer, (PLE_DIM, D_MODEL))],
        out_specs=row(D_MODEL),
        scratch_shapes=[pltpu.VMEM((tm, D_FF), BF16)],
        compiler_params=pltpu.CompilerParams(dimension_semantics=("arbitrary",),
                                             vmem_limit_bytes=VMEM_LIMIT),
        name="ffn_ple",
    )(x, p, gf, wgu, wd, gp, wpg, wpp)


def kernel(x, p, positions, norm_mix_g, w_in, q_norm_g, k_norm_g, sink, gm_ln_g, gm_ln_b, gm_ws, gm_bs, conv_w, conv_b, conv_ln_g, conv_ln_b, out_norm_g, w_out, norm_ffn_g, w_gate_up, w_down, ple_norm_g, w_ple_gate, w_ple_proj):
    batch, seq, d = x.shape
    depth = w_in.shape[0]
    n = batch * seq
    assert d == D_MODEL and seq % TQ_MIX == 0
    assert n % TM_PROJ == 0 and n % TM_FFN == 0 and n % TM_ROPE == 0

    vec = lambda a: a[:, None, :]
    w_in_b, w_out_b = w_in.astype(BF16), w_out.astype(BF16)
    w_gu_b, w_down_b = w_gate_up.astype(BF16), w_down.astype(BF16)
    w_pg_b, w_pp_b = w_ple_gate.astype(BF16), w_ple_proj.astype(BF16)
    ws_b = gm_ws.astype(BF16)
    qg = vec(jnp.tile(q_norm_g, (1, LANES // HEAD_DIM)))
    kg = vec(jnp.tile(k_norm_g, (1, LANES // HEAD_DIM)))
    gm_bias = jnp.repeat(jnp.swapaxes(gm_bs, 1, 2), HEAD_DIM, axis=2)
    sink_flat = sink.reshape(depth * ATTN_Q_HEADS)
    p_flat = p.reshape(depth, n, PLE_DIM)

    tabs = _rope_tables(positions.reshape(1, n))
    xf = x.reshape(n, d)
    for i in range(depth):
        q, kv, u, vln, glu = _inproj(i, xf, vec(norm_mix_g), w_in_b, qg, kg, vec(gm_ln_g), vec(gm_ln_b), tabs)
        xf = _mixer(i, xf, q, kv, u, vln, glu, sink_flat, ws_b, gm_bias, conv_w, vec(conv_b),
                    vec(conv_ln_g), vec(conv_ln_b), vec(out_norm_g), w_out_b, batch, seq)
        xf = _ffn(i, xf, p_flat, vec(norm_ffn_g), w_gu_b, w_down_b, vec(ple_norm_g), w_pg_b, w_pp_b)
    return xf.reshape(batch, seq, d)
```

```python
import functools
import math

import numpy as np
import jax
import jax.numpy as jnp
from jax import lax
from jax.experimental import pallas as pl
from jax.experimental.pallas import tpu as pltpu

F32 = jnp.float32
BF16 = jnp.bfloat16

D_MODEL = 1024
HEAD_DIM = 64
ATTN_WIDTH = 512
ATTN_Q_HEADS = 8
ATTN_KV_HEADS = 2
ATTN_GROUP = 4
KV_WIDTH = 128
WINDOW = 128
BLOCK = 128
ROPE_THETA = 500000.0
ROT_DIM = 16
GM_WIDTH = 256
GM_HEADS = 4
CHUNK = 128
CONV_CH = 256
CONV_WIDTH = 31
CONV_PAD = 15
D_FF = 2816
PLE_DIM = 256
EPS = 1e-6
NEG_INF = -1e30

Q_OFF = 0
K_OFF = Q_OFF + ATTN_WIDTH
V_OFF = K_OFF + KV_WIDTH
GM_OFF = V_OFF + KV_WIDTH
CONV_OFF = GM_OFF + 2 * GM_WIDTH
IN_COLS = CONV_OFF + 2 * CONV_CH

LANES = 128
SUBLANES = 8
CONV_HALO = 16
CONV_ROW_STRIDE = 2
CONV_GROUPS = 2
VMEM_LIMIT = 56 * 1024 * 1024

TM_ROPE = 2048
TM_PROJ = 512
TQ_MIX = 512
TM_FFN = 512
FF_CHUNK = 256
SOFTMAX_ROWS = 32

INV_FREQ = [float(ROPE_THETA ** (-(2.0 * j) / ROT_DIM)) for j in range(ROT_DIM // 2)]
SQRT_HALF = float(np.sqrt(0.5))
LOG2E = float(np.log2(np.e))


def _layer_spec(layer, shape):
    nd = len(shape)
    return pl.BlockSpec((None,) + tuple(shape), lambda *_: (layer,) + (0,) * nd,
                        pipeline_mode=pl.Buffered(1))


def _split_bf16(x):
    hi = x.astype(BF16)
    lo = (x - hi.astype(F32)).astype(BF16)
    return hi, lo


def _rope_table_kernel(pos_ref, c_ref, s1_ref, s2_ref):
    half = ROT_DIM // 2
    pos = pos_ref[...].astype(F32)
    row = lax.broadcasted_iota(jnp.int32, (ROT_DIM, 1), 0)
    inv = jnp.zeros((ROT_DIM, 1), F32)
    for j, f in enumerate(INV_FREQ):
        inv = jnp.where((row & (half - 1)) == j, f, inv)
    ang = inv * pos
    cs = jnp.where(row < half, jnp.cos(ang), jnp.sin(ang))

    r = lax.broadcasted_iota(jnp.int32, (ROT_DIM, 3 * LANES), 0)
    col = lax.broadcasted_iota(jnp.int32, (ROT_DIM, 3 * LANES), 1)
    table = col // LANES
    in_head = col & (HEAD_DIM - 1)
    freq = col & (half - 1)
    put_cos = (table == 0) & (in_head < ROT_DIM) & (r == freq)
    put_nsin = (table == 1) & (in_head < half) & (r == freq + half)
    put_sin = (table == 2) & (in_head >= half) & (in_head < ROT_DIM) & (r == freq + half)
    place = jnp.where(put_cos | put_sin, 1.0, jnp.where(put_nsin, -1.0, 0.0)).astype(BF16)

    hi, lo = _split_bf16(cs)
    dn = (((0,), (0,)), ((), ()))
    t = (lax.dot_general(hi, place, dn, preferred_element_type=F32)
         + lax.dot_general(lo, place, dn, preferred_element_type=F32))
    lane = lax.broadcasted_iota(jnp.int32, (1, LANES), 1)
    c_ref[...] = jnp.where((lane & (HEAD_DIM - 1)) < ROT_DIM, t[:, 0:LANES], 1.0)
    s1_ref[...] = t[:, LANES:2 * LANES]
    s2_ref[...] = t[:, 2 * LANES:3 * LANES]


def _rope_tables(pos_row):
    n = pos_row.shape[1]
    tm = TM_ROPE
    tab = jax.ShapeDtypeStruct((n, LANES), F32)
    spec = pl.BlockSpec((tm, LANES), lambda i: (i, 0))
    return pl.pallas_call(
        _rope_table_kernel,
        out_shape=(tab, tab, tab),
        grid=(n // tm,),
        in_specs=[pl.BlockSpec((1, tm), lambda i: (0, i))],
        out_specs=(spec, spec, spec),
        name="rope_tables",
    )(pos_row)


def _rms(x, gain):
    ms = jnp.mean(x * x, axis=-1, keepdims=True)
    return x * lax.rsqrt(ms + EPS) * gain


def _layer_norm(x, gain, bias):
    mu = jnp.mean(x, axis=-1, keepdims=True)
    xc = x - mu
    var = jnp.mean(xc * xc, axis=-1, keepdims=True)
    return xc * lax.rsqrt(var + EPS) * gain + bias


def _inproj_kernel(x_ref, g_ref, w_ref, qg_ref, kg_ref, lng_ref, lnb_ref,
                   c_ref, s1_ref, s2_ref,
                   q_out, kt_out, v_out, u_out, vln_out, glu_out, z_even, z_odd, *, last_is_even):
    i = pl.program_id(0)
    n_tiles = pl.num_programs(0) - 1

    def project(z_dst):
        h = _rms(x_ref[...], g_ref[...]).astype(BF16)
        z_dst[...] = jnp.dot(h, w_ref[...], preferred_element_type=F32)

    def finish(z_src):
        cos_t, sin_a, sin_b = c_ref[...], s1_ref[...], s2_ref[...]
        lane = lax.broadcasted_iota(jnp.int32, (1, LANES), 1)
        head0 = lane < HEAD_DIM

        def head_norm_rope(z, gain):
            sq = z * z
            first = jnp.sum(jnp.where(head0, sq, 0.0), axis=-1, keepdims=True)
            both = jnp.sum(sq, axis=-1, keepdims=True)
            ss = jnp.where(head0, first, both - first)
            zn = z * lax.rsqrt(ss * (1.0 / HEAD_DIM) + EPS) * gain
            return (zn * cos_t
                    + pltpu.roll(zn, LANES - ROT_DIM // 2, 1) * sin_a
                    + pltpu.roll(zn, ROT_DIM // 2, 1) * sin_b)

        q_gain = qg_ref[...] * (LOG2E / math.sqrt(HEAD_DIM))
        for b in range(ATTN_WIDTH // LANES):
            q_out[b] = head_norm_rope(z_src[:, Q_OFF + b * LANES:Q_OFF + (b + 1) * LANES], q_gain).astype(BF16)

        k = head_norm_rope(z_src[:, K_OFF:V_OFF], kg_ref[...])
        v = z_src[:, V_OFF:GM_OFF]
        kt_out[...] = k.T.astype(BF16)
        v_out[:, 0:LANES] = v.astype(BF16)
        v_out[:, LANES:2 * LANES] = pltpu.roll(v, HEAD_DIM, 1).astype(BF16)

        zuv = z_src[:, GM_OFF:CONV_OFF]
        uv = 0.5 * zuv * (1.0 + lax.erf(zuv * SQRT_HALF))
        u_out[...] = uv[:, 0:GM_WIDTH]
        vln_out[...] = _layer_norm(uv[:, GM_WIDTH:], lng_ref[...], lnb_ref[...]).astype(BF16)

        glu_out[...] = z_src[:, CONV_OFF:CONV_OFF + CONV_CH] * jax.nn.sigmoid(z_src[:, CONV_OFF + CONV_CH:])

    inner = jnp.logical_and(i > 0, i < n_tiles)

    @pl.when(i == 0)
    def _():
        project(z_even)

    @pl.when(jnp.logical_and(inner, i % 2 == 1))
    def _():
        finish(z_even)
        project(z_odd)

    @pl.when(jnp.logical_and(inner, i % 2 == 0))
    def _():
        finish(z_odd)
        project(z_even)

    @pl.when(i == n_tiles)
    def _():
        finish(z_even if last_is_even else z_odd)


def _inproj(layer, x, g, w, qg, kg, lng, lnb, tabs):
    n = x.shape[0]
    tm = TM_PROJ
    n_tiles = n // tm
    cur = lambda width: pl.BlockSpec((tm, width), lambda i: (jnp.minimum(i, n_tiles - 1), 0))
    lag = lambda width: pl.BlockSpec((tm, width), lambda i: (jnp.maximum(i - 1, 0), 0))
    out_shape = (
        jax.ShapeDtypeStruct((ATTN_WIDTH // LANES, n, LANES), BF16),
        jax.ShapeDtypeStruct((KV_WIDTH, n), BF16),
        jax.ShapeDtypeStruct((n, 2 * LANES), BF16),
        jax.ShapeDtypeStruct((n, GM_WIDTH), F32),
        jax.ShapeDtypeStruct((n, GM_WIDTH), BF16),
        jax.ShapeDtypeStruct((n, CONV_CH), F32),
    )
    return pl.pallas_call(
        functools.partial(_inproj_kernel, last_is_even=(n_tiles - 1) % 2 == 0),
        out_shape=out_shape,
        grid=(n_tiles + 1,),
        in_specs=[cur(D_MODEL), _layer_spec(layer, (1, D_MODEL)), _layer_spec(layer, (D_MODEL, IN_COLS)),
                  _layer_spec(layer, (1, LANES)), _layer_spec(layer, (1, LANES)),
                  _layer_spec(layer, (1, GM_WIDTH)), _layer_spec(layer, (1, GM_WIDTH)),
                  lag(LANES), lag(LANES), lag(LANES)],
        out_specs=(pl.BlockSpec((ATTN_WIDTH // LANES, tm, LANES), lambda i: (0, jnp.maximum(i - 1, 0), 0)),
                   pl.BlockSpec((KV_WIDTH, tm), lambda i: (0, jnp.maximum(i - 1, 0))),
                   lag(2 * LANES), lag(GM_WIDTH), lag(GM_WIDTH), lag(CONV_CH)),
        scratch_shapes=[pltpu.VMEM((tm, IN_COLS), F32), pltpu.VMEM((tm, IN_COLS), F32)],
        compiler_params=pltpu.CompilerParams(dimension_semantics=("arbitrary",),
                                             vmem_limit_bytes=VMEM_LIMIT),
        name="in_proj",
    )(x, g, w, qg, kg, lng, lnb, *tabs)


def _mixer_kernel(layer, sink_ref, x_ref, q_ref, ktp_ref, ktc_ref, ktn_ref, vp_ref, vc_ref, vn_ref, u_ref, vln_ref,
                  glup_ref, gluc_ref, glun_ref,
                  ws_ref, gmb_ref, cw_ref, cb_ref, clg_ref, clb_ref, og_ref, wout_ref,
                  o_ref,
                  kta_s, ktb_s, va_s, vb_s, bias_s, s_s, p_s, scale_s, glu_s, y_s, attn_s, gm_s, merged_s):
    tq = x_ref.shape[0]
    nblk = tq // BLOCK
    j = pl.program_id(1)
    first = j == 0
    last = j == pl.num_programs(1) - 1

    lane = lax.broadcasted_iota(jnp.int32, (1, LANES), 1)
    head0 = lane < HEAD_DIM
    zero = jnp.zeros((), BF16)
    zero_rows = jnp.zeros((HEAD_DIM, BLOCK), BF16)
    for b in range(nblk + 2):
        if b == 0:
            kt = ktp_ref[...]
        elif b == nblk + 1:
            kt = ktn_ref[...]
        else:
            kt = ktc_ref[:, (b - 1) * BLOCK:b * BLOCK]
        for h in range(ATTN_KV_HEADS):
            kh = kt[h * HEAD_DIM:(h + 1) * HEAD_DIM, :]
            kta_s[h, b] = jnp.concatenate([kh, zero_rows], axis=0)
            ktb_s[h, b] = jnp.concatenate([zero_rows, kh], axis=0)
    for src, r0, rows in ((vp_ref, 0, BLOCK), (vc_ref, BLOCK, tq), (vn_ref, BLOCK + tq, BLOCK)):
        dst = slice(r0, r0 + rows)
        v, v_sw = src[:, 0:LANES], src[:, LANES:2 * LANES]
        va_s[0, dst, :] = jnp.where(head0, v, zero)
        vb_s[0, dst, :] = jnp.where(head0, zero, v_sw)
        va_s[1, dst, :] = jnp.where(head0, v_sw, zero)
        vb_s[1, dst, :] = jnp.where(head0, zero, v)

    qi = lax.broadcasted_iota(jnp.int32, (BLOCK, BLOCK), 0)
    kj = lax.broadcasted_iota(jnp.int32, (BLOCK, BLOCK), 1)
    band_lo = jnp.where(kj >= qi, 0.0, NEG_INF)
    band_hi = jnp.where(kj <= qi, 0.0, NEG_INF)
    bias_s[0] = band_lo
    bias_s[1] = band_hi
    bias_s[2] = jnp.where(first, NEG_INF, band_lo)
    bias_s[3] = jnp.where(last, NEG_INF, band_hi)

    n_pairs = ATTN_GROUP // 2

    def scores(nb, h):
        r0 = pl.multiple_of(nb * BLOCK, BLOCK)
        q_st = jnp.concatenate([q_ref[n_pairs * h + pp, pl.ds(r0, BLOCK), :] for pp in range(n_pairs)], axis=0)
        kt_st = jnp.concatenate([kta_s[h, nb + t] for t in range(3)] + [ktb_s[h, nb + t] for t in range(3)],
                                axis=1)
        s_s[h] = jnp.dot(q_st, kt_st, preferred_element_type=F32)

    def softmax_pv(nb, h):
        r0 = pl.multiple_of(nb * BLOCK, BLOCK)
        lo_idx = jnp.where(nb == 0, 2, 0)
        hi_idx = jnp.where(nb == nblk - 1, 3, 1)
        v_st = jnp.concatenate([va_s[h, pl.ds(r0, 3 * BLOCK), :], vb_s[h, pl.ds(r0, 3 * BLOCK), :]], axis=0)
        for pp in range(n_pairs):
            for rc in range(0, BLOCK, SOFTMAX_ROWS):
                rr = slice(pp * BLOCK + rc, pp * BLOCK + rc + SOFTMAX_ROWS)
                br = slice(rc, rc + SOFTMAX_ROWS)
                inv = []
                for half in range(2):
                    sink = sink_ref[layer * ATTN_Q_HEADS + ATTN_GROUP * h + 2 * pp + half] * LOG2E
                    c0 = half * 3 * BLOCK
                    s0 = s_s[h, rr, c0:c0 + BLOCK] + bias_s[lo_idx, br, :]
                    s1 = s_s[h, rr, c0 + BLOCK:c0 + 2 * BLOCK]
                    s2 = s_s[h, rr, c0 + 2 * BLOCK:c0 + 3 * BLOCK] + bias_s[hi_idx, br, :]
                    m = jnp.max(jnp.maximum(jnp.maximum(s0, s1), s2), axis=-1, keepdims=True)
                    m = jnp.maximum(m, sink)
                    e0, e1, e2 = jnp.exp2(s0 - m), jnp.exp2(s1 - m), jnp.exp2(s2 - m)
                    denom = jnp.sum(e0 + e1 + e2, axis=-1, keepdims=True) + jnp.exp2(sink - m)
                    p_s[h, rr, c0:c0 + BLOCK] = e0.astype(BF16)
                    p_s[h, rr, c0 + BLOCK:c0 + 2 * BLOCK] = e1.astype(BF16)
                    p_s[h, rr, c0 + 2 * BLOCK:c0 + 3 * BLOCK] = e2.astype(BF16)
                    inv.append(1.0 / denom)
                scale_s[h, rr, :] = jnp.where(head0, inv[0], inv[1])
            pr = slice(pp * BLOCK, (pp + 1) * BLOCK)
            o = jnp.dot(p_s[h, pr, :], v_st, preferred_element_type=F32) * scale_s[h, pr, :]
            attn_s[n_pairs * h + pp, pl.ds(r0, BLOCK), :] = o

    scores(0, 0)

    def attn_step(nb, carry):
        scores(nb, 1)
        softmax_pv(nb, 0)
        scores(jnp.minimum(nb + 1, nblk - 1), 0)
        softmax_pv(nb, 1)
        return carry

    lax.fori_loop(0, nblk, attn_step, 0)

    for nb in range(nblk):
        rows = slice(nb * BLOCK, (nb + 1) * BLOCK)
        for hp in range(GM_HEADS // 2):
            cols = slice(hp * LANES, (hp + 1) * LANES)
            vblk = vln_ref[rows, cols]
            lhs = jnp.concatenate([ws_ref[2 * hp], ws_ref[2 * hp + 1]], axis=1)
            rhs = jnp.concatenate([jnp.where(head0, vblk, zero), jnp.where(head0, zero, vblk)], axis=0)
            sgate = jnp.dot(lhs, rhs, preferred_element_type=F32) + gmb_ref[:, cols]
            gm_s[rows, cols] = u_ref[rows, cols] * sgate

    n_ch = CONV_CH // LANES
    for ch in range(n_ch):
        cl = slice(ch * LANES, (ch + 1) * LANES)
        glu_s[ch, 0:CONV_HALO, :] = jnp.where(first, 0.0, glup_ref[:, cl])
        glu_s[ch, CONV_HALO:CONV_HALO + tq, :] = gluc_ref[:, cl]
        glu_s[ch, CONV_HALO + tq:, :] = jnp.where(last, 0.0, glun_ref[:, cl])
    group = SUBLANES * CONV_ROW_STRIDE

    def conv_step(i, carry):
        base = pl.multiple_of(i * (CONV_GROUPS * group), CONV_GROUPS * group)
        for ch in range(n_ch):
            cl = slice(ch * LANES, (ch + 1) * LANES)
            offs = [g * group + ph for g in range(CONV_GROUPS) for ph in range(CONV_ROW_STRIDE)]
            accs = [jnp.broadcast_to(cb_ref[:, cl], (SUBLANES, LANES)) for _ in offs]
            for t in range(CONV_WIDTH):
                w = cw_ref[t:t + 1, cl]
                for a, off in enumerate(offs):
                    start = base + (off + CONV_HALO - CONV_PAD + t)
                    accs[a] = accs[a] + w * glu_s[ch, pl.ds(start, SUBLANES, stride=CONV_ROW_STRIDE), :]
            for a, off in enumerate(offs):
                y_s[ch, pl.ds(base + off, SUBLANES, stride=CONV_ROW_STRIDE), :] = accs[a]
        return carry

    lax.fori_loop(0, tq // (CONV_GROUPS * group), conv_step, 0)

    rchunk = 128
    for r0 in range(0, tq, rchunk):
        rs = slice(r0, r0 + rchunk)
        y = jnp.concatenate([y_s[ch, rs, :] for ch in range(n_ch)], axis=1)
        y = _layer_norm(y, clg_ref[...], clb_ref[...])
        y = y * jax.nn.sigmoid(y)
        merged_s[rs, ATTN_WIDTH + GM_WIDTH:] = _rms(y, og_ref[:, ATTN_WIDTH + GM_WIDTH:]).astype(BF16)
        attn = jnp.concatenate([attn_s[pb, rs, :] for pb in range(ATTN_WIDTH // LANES)], axis=1)
        merged_s[rs, 0:ATTN_WIDTH] = _rms(attn, og_ref[:, 0:ATTN_WIDTH]).astype(BF16)
        merged_s[rs, ATTN_WIDTH:ATTN_WIDTH + GM_WIDTH] = _rms(
            gm_s[rs, :], og_ref[:, ATTN_WIDTH:ATTN_WIDTH + GM_WIDTH]).astype(BF16)
    o_ref[...] = x_ref[...] + jnp.dot(merged_s[...], wout_ref[...], preferred_element_type=F32)


def _mixer(layer, x, q, kt, v, u, vln, glu, sink, ws, gm_bias, cw, cb, clg, clb, og, wout, batch, seq):
    n = x.shape[0]
    tq = TQ_MIX
    tiles = seq // tq
    blk_per_tile = tq // BLOCK
    blk_per_seq = seq // BLOCK
    halo_per_tile = tq // CONV_HALO
    halo_per_seq = seq // CONV_HALO

    def row(width):
        return pl.BlockSpec((tq, width), lambda b, j, *_: (b * tiles + j, 0))

    def prev(rows, width, per_tile, per_seq):
        return pl.BlockSpec(
            (rows, width), lambda b, j, *_: (b * per_seq + jnp.maximum(j * per_tile - 1, 0), 0))

    def nxt(rows, width, per_tile, per_seq):
        return pl.BlockSpec(
            (rows, width), lambda b, j, *_: (b * per_seq + jnp.minimum((j + 1) * per_tile, per_seq - 1), 0))

    const = lambda shape: _layer_spec(layer, shape)
    kv_rows = tq + 2 * BLOCK
    grid_spec = pltpu.PrefetchScalarGridSpec(
        num_scalar_prefetch=1,
        grid=(batch, tiles),
        in_specs=[
            row(D_MODEL), pl.BlockSpec((ATTN_WIDTH // LANES, tq, LANES), lambda b, j, *_: (0, b * tiles + j, 0)),
            pl.BlockSpec((KV_WIDTH, BLOCK),
                         lambda b, j, *_: (0, b * blk_per_seq + jnp.maximum(j * blk_per_tile - 1, 0))),
            pl.BlockSpec((KV_WIDTH, tq), lambda b, j, *_: (0, b * tiles + j)),
            pl.BlockSpec((KV_WIDTH, BLOCK),
                         lambda b, j, *_: (0, b * blk_per_seq + jnp.minimum((j + 1) * blk_per_tile, blk_per_seq - 1))),
            prev(BLOCK, 2 * LANES, blk_per_tile, blk_per_seq), row(2 * LANES),
            nxt(BLOCK, 2 * LANES, blk_per_tile, blk_per_seq),
            row(GM_WIDTH), row(GM_WIDTH),
            prev(CONV_HALO, CONV_CH, halo_per_tile, halo_per_seq), row(CONV_CH),
            nxt(CONV_HALO, CONV_CH, halo_per_tile, halo_per_seq),
            const((GM_HEADS, CHUNK, CHUNK)), const((CHUNK, GM_WIDTH)),
            const((CONV_WIDTH, CONV_CH)), const((1, CONV_CH)), const((1, CONV_CH)), const((1, CONV_CH)),
            const((1, D_MODEL)), const((D_MODEL, D_MODEL)),
        ],
        out_specs=row(D_MODEL),
        scratch_shapes=[
            pltpu.VMEM((ATTN_KV_HEADS, blk_per_tile + 2, KV_WIDTH, BLOCK), BF16),
            pltpu.VMEM((ATTN_KV_HEADS, blk_per_tile + 2, KV_WIDTH, BLOCK), BF16),
            pltpu.VMEM((ATTN_KV_HEADS, kv_rows, LANES), BF16), pltpu.VMEM((ATTN_KV_HEADS, kv_rows, LANES), BF16),
            pltpu.VMEM((4, BLOCK, BLOCK), F32),
            pltpu.VMEM((2, ATTN_GROUP // 2 * BLOCK, 2 * 3 * BLOCK), F32),
            pltpu.VMEM((2, ATTN_GROUP // 2 * BLOCK, 2 * 3 * BLOCK), BF16),
            pltpu.VMEM((2, ATTN_GROUP // 2 * BLOCK, LANES), F32),
            pltpu.VMEM((CONV_CH // LANES, tq + 2 * CONV_HALO, LANES), F32),
            pltpu.VMEM((CONV_CH // LANES, tq, LANES), F32),
            pltpu.VMEM((ATTN_WIDTH // LANES, tq, LANES), F32), pltpu.VMEM((tq, GM_WIDTH), F32),
            pltpu.VMEM((tq, D_MODEL), BF16),
        ],
    )
    return pl.pallas_call(
        functools.partial(_mixer_kernel, layer),
        out_shape=jax.ShapeDtypeStruct((n, D_MODEL), F32),
        grid_spec=grid_spec,
        compiler_params=pltpu.CompilerParams(dimension_semantics=("arbitrary", "arbitrary"),
                                             vmem_limit_bytes=VMEM_LIMIT),
        name="mixer",
    )(sink, x, q, kt, kt, kt, v, v, v, u, vln, glu, glu, glu, ws, gm_bias, cw, cb, clg, clb, og, wout)


def _ffn_kernel(x_ref, p_ref, gf_ref, wgu_ref, wd_ref, gp_ref, wpg_ref, wpp_ref, o_ref, act_s):
    x = x_ref[...]
    hn = _rms(x, gf_ref[...]).astype(BF16)
    for c0 in range(0, D_FF, FF_CHUNK):
        gate = jnp.dot(hn, wgu_ref[:, c0:c0 + FF_CHUNK], preferred_element_type=F32)
        up = jnp.dot(hn, wgu_ref[:, D_FF + c0:D_FF + c0 + FF_CHUNK], preferred_element_type=F32)
        act_s[:, c0:c0 + FF_CHUNK] = (gate * jax.nn.sigmoid(gate) * up).astype(BF16)
    x = x + jnp.dot(act_s[...], wd_ref[...], preferred_element_type=F32)

    hp = _rms(x, gp_ref[...]).astype(BF16)
    gate = jax.nn.sigmoid(jnp.dot(hp, wpg_ref[...], preferred_element_type=F32))
    proj = jnp.dot(p_ref[...].astype(BF16), wpp_ref[...], preferred_element_type=F32)
    o_ref[...] = x + proj * gate


def _ffn(layer, x, p, gf, wgu, wd, gp, wpg, wpp):
    n = x.shape[0]
    tm = TM_FFN
    row = lambda width: pl.BlockSpec((tm, width), lambda i: (i, 0))
    return pl.pallas_call(
        _ffn_kernel,
        out_shape=jax.ShapeDtypeStruct((n, D_MODEL), F32),
        grid=(n // tm,),
        in_specs=[row(D_MODEL), pl.BlockSpec((None, tm, PLE_DIM), lambda i: (layer, i, 0)),
                  _layer_spec(layer, (1, D_MODEL)),
                  _layer_spec(layer, (D_MODEL, 2 * D_FF)), _layer_spec(layer, (D_FF, D_MODEL)),
                  _layer_spec(layer, (1, D_MODEL)), _layer_spec(layer, (D_MODEL, D_MODEL)),
                  _layer_spec(layer, (PLE_DIM, D_MODEL))],
        out_specs=row(D_MODEL),
        scratch_shapes=[pltpu.VMEM((tm, D_FF), BF16)],
        compiler_params=pltpu.CompilerParams(dimension_semantics=("arbitrary",),
                                             vmem_limit_bytes=VMEM_LIMIT),
        name="ffn_ple",
    )(x, p, gf, wgu, wd, gp, wpg, wpp)


def kernel(x, p, positions, norm_mix_g, w_in, q_norm_g, k_norm_g, sink, gm_ln_g, gm_ln_b, gm_ws, gm_bs, conv_w, conv_b, conv_ln_g, conv_ln_b, out_norm_g, w_out, norm_ffn_g, w_gate_up, w_down, ple_norm_g, w_ple_gate, w_ple_proj):
    batch, seq, d = x.shape
    depth = w_in.shape[0]
    n = batch * seq
    assert d == D_MODEL and seq % TQ_MIX == 0
    assert n % TM_PROJ == 0 and n % TM_FFN == 0 and n % TM_ROPE == 0

    vec = lambda a: a[:, None, :]
    w_in_b, w_out_b = w_in.astype(BF16), w_out.astype(BF16)
    w_gu_b, w_down_b = w_gate_up.astype(BF16), w_down.astype(BF16)
    w_pg_b, w_pp_b = w_ple_gate.astype(BF16), w_ple_proj.astype(BF16)
    ws_b = gm_ws.astype(BF16)
    qg = vec(jnp.tile(q_norm_g, (1, LANES // HEAD_DIM)))
    kg = vec(jnp.tile(k_norm_g, (1, LANES // HEAD_DIM)))
    gm_bias = jnp.repeat(jnp.swapaxes(gm_bs, 1, 2), HEAD_DIM, axis=2)
    sink_flat = sink.reshape(depth * ATTN_Q_HEADS)
    p_flat = p.reshape(depth, n, PLE_DIM)

    tabs = _rope_tables(positions.reshape(1, n))
    xf = x.reshape(n, d)
    for i in range(depth):
        q, kt, v, u, vln, glu = _inproj(i, xf, vec(norm_mix_g), w_in_b, qg, kg, vec(gm_ln_g), vec(gm_ln_b), tabs)
        xf = _mixer(i, xf, q, kt, v, u, vln, glu, sink_flat, ws_b, gm_bias, conv_w, vec(conv_b),
                    vec(conv_ln_g), vec(conv_ln_b), vec(out_norm_g), w_out_b, batch, seq)
        xf = _ffn(i, xf, p_flat, vec(norm_ffn_g), w_gu_b, w_down_b, vec(ple_norm_g), w_pg_b, w_pp_b)
    return xf.reshape(batch, seq, d)
```

```python
import functools
import math

import numpy as np
import jax
import jax.numpy as jnp
from jax import lax
from jax.experimental import pallas as pl
from jax.experimental.pallas import tpu as pltpu

F32 = jnp.float32
BF16 = jnp.bfloat16

D_MODEL = 1024
HEAD_DIM = 64
ATTN_WIDTH = 512
ATTN_Q_HEADS = 8
ATTN_KV_HEADS = 2
ATTN_GROUP = 4
KV_WIDTH = 128
WINDOW = 128
BLOCK = 128
ROPE_THETA = 500000.0
ROT_DIM = 16
GM_WIDTH = 256
GM_HEADS = 4
CHUNK = 128
CONV_CH = 256
CONV_WIDTH = 31
CONV_PAD = 15
D_FF = 2816
PLE_DIM = 256
EPS = 1e-6
NEG_INF = -1e30

Q_OFF = 0
K_OFF = Q_OFF + ATTN_WIDTH
V_OFF = K_OFF + KV_WIDTH
GM_OFF = V_OFF + KV_WIDTH
CONV_OFF = GM_OFF + 2 * GM_WIDTH
IN_COLS = CONV_OFF + 2 * CONV_CH

LANES = 128
SUBLANES = 8
CONV_HALO = 16
CONV_ROW_STRIDE = 2
CONV_GROUPS = 2
VMEM_LIMIT = 56 * 1024 * 1024

TM_ROPE = 2048
TM_PROJ = 512
TQ_MIX = 1024
TM_FFN = 1024
FF_CHUNK = 256
SOFTMAX_ROWS = 32

INV_FREQ = [float(ROPE_THETA ** (-(2.0 * j) / ROT_DIM)) for j in range(ROT_DIM // 2)]
SQRT_HALF = float(np.sqrt(0.5))
LOG2E = float(np.log2(np.e))


def _layer_spec(layer, shape):
    nd = len(shape)
    return pl.BlockSpec((None,) + tuple(shape), lambda *_: (layer,) + (0,) * nd,
                        pipeline_mode=pl.Buffered(1))


def _split_bf16(x):
    hi = x.astype(BF16)
    lo = (x - hi.astype(F32)).astype(BF16)
    return hi, lo


def _rope_table_kernel(pos_ref, c_ref, s1_ref, s2_ref):
    half = ROT_DIM // 2
    pos = pos_ref[...].astype(F32)
    row = lax.broadcasted_iota(jnp.int32, (ROT_DIM, 1), 0)
    inv = jnp.zeros((ROT_DIM, 1), F32)
    for j, f in enumerate(INV_FREQ):
        inv = jnp.where((row & (half - 1)) == j, f, inv)
    ang = inv * pos
    cs = jnp.where(row < half, jnp.cos(ang), jnp.sin(ang))

    r = lax.broadcasted_iota(jnp.int32, (ROT_DIM, 3 * LANES), 0)
    col = lax.broadcasted_iota(jnp.int32, (ROT_DIM, 3 * LANES), 1)
    table = col // LANES
    in_head = col & (HEAD_DIM - 1)
    freq = col & (half - 1)
    put_cos = (table == 0) & (in_head < ROT_DIM) & (r == freq)
    put_nsin = (table == 1) & (in_head < half) & (r == freq + half)
    put_sin = (table == 2) & (in_head >= half) & (in_head < ROT_DIM) & (r == freq + half)
    place = jnp.where(put_cos | put_sin, 1.0, jnp.where(put_nsin, -1.0, 0.0)).astype(BF16)

    hi, lo = _split_bf16(cs)
    dn = (((0,), (0,)), ((), ()))
    t = (lax.dot_general(hi, place, dn, preferred_element_type=F32)
         + lax.dot_general(lo, place, dn, preferred_element_type=F32))
    lane = lax.broadcasted_iota(jnp.int32, (1, LANES), 1)
    c_ref[...] = jnp.where((lane & (HEAD_DIM - 1)) < ROT_DIM, t[:, 0:LANES], 1.0)
    s1_ref[...] = t[:, LANES:2 * LANES]
    s2_ref[...] = t[:, 2 * LANES:3 * LANES]


def _rope_tables(pos_row):
    n = pos_row.shape[1]
    tm = TM_ROPE
    tab = jax.ShapeDtypeStruct((n, LANES), F32)
    spec = pl.BlockSpec((tm, LANES), lambda i: (i, 0))
    return pl.pallas_call(
        _rope_table_kernel,
        out_shape=(tab, tab, tab),
        grid=(n // tm,),
        in_specs=[pl.BlockSpec((1, tm), lambda i: (0, i))],
        out_specs=(spec, spec, spec),
        name="rope_tables",
    )(pos_row)


def _rms(x, gain):
    ms = jnp.mean(x * x, axis=-1, keepdims=True)
    return x * lax.rsqrt(ms + EPS) * gain


def _layer_norm(x, gain, bias):
    mu = jnp.mean(x, axis=-1, keepdims=True)
    xc = x - mu
    var = jnp.mean(xc * xc, axis=-1, keepdims=True)
    return xc * lax.rsqrt(var + EPS) * gain + bias


def _inproj_kernel(x_ref, g_ref, w_ref, qg_ref, kg_ref, lng_ref, lnb_ref,
                   c_ref, s1_ref, s2_ref,
                   q_out, kt_out, v_out, u_out, vln_out, glu_out, z_even, z_odd, *, last_is_even):
    i = pl.program_id(0)
    n_tiles = pl.num_programs(0) - 1

    def project(z_dst):
        h = _rms(x_ref[...], g_ref[...]).astype(BF16)
        z_dst[...] = jnp.dot(h, w_ref[...], preferred_element_type=F32)

    def finish(z_src):
        cos_t, sin_a, sin_b = c_ref[...], s1_ref[...], s2_ref[...]
        lane = lax.broadcasted_iota(jnp.int32, (1, LANES), 1)
        head0 = lane < HEAD_DIM

        def head_norm_rope(z, gain):
            sq = z * z
            first = jnp.sum(jnp.where(head0, sq, 0.0), axis=-1, keepdims=True)
            both = jnp.sum(sq, axis=-1, keepdims=True)
            ss = jnp.where(head0, first, both - first)
            zn = z * lax.rsqrt(ss * (1.0 / HEAD_DIM) + EPS) * gain
            return (zn * cos_t
                    + pltpu.roll(zn, LANES - ROT_DIM // 2, 1) * sin_a
                    + pltpu.roll(zn, ROT_DIM // 2, 1) * sin_b)

        q_gain = qg_ref[...] * (LOG2E / math.sqrt(HEAD_DIM))
        for b in range(ATTN_WIDTH // LANES):
            q_out[b] = head_norm_rope(z_src[:, Q_OFF + b * LANES:Q_OFF + (b + 1) * LANES], q_gain).astype(BF16)

        k = head_norm_rope(z_src[:, K_OFF:V_OFF], kg_ref[...])
        v = z_src[:, V_OFF:GM_OFF]
        kt_out[...] = k.T.astype(BF16)
        v_out[:, 0:LANES] = v.astype(BF16)
        v_out[:, LANES:2 * LANES] = pltpu.roll(v, HEAD_DIM, 1).astype(BF16)

        zuv = z_src[:, GM_OFF:CONV_OFF]
        uv = 0.5 * zuv * (1.0 + lax.erf(zuv * SQRT_HALF))
        u_out[...] = uv[:, 0:GM_WIDTH]
        vln_out[...] = _layer_norm(uv[:, GM_WIDTH:], lng_ref[...], lnb_ref[...]).astype(BF16)

        glu_out[...] = z_src[:, CONV_OFF:CONV_OFF + CONV_CH] * jax.nn.sigmoid(z_src[:, CONV_OFF + CONV_CH:])

    inner = jnp.logical_and(i > 0, i < n_tiles)

    @pl.when(i == 0)
    def _():
        project(z_even)

    @pl.when(jnp.logical_and(inner, i % 2 == 1))
    def _():
        finish(z_even)
        project(z_odd)

    @pl.when(jnp.logical_and(inner, i % 2 == 0))
    def _():
        finish(z_odd)
        project(z_even)

    @pl.when(i == n_tiles)
    def _():
        finish(z_even if last_is_even else z_odd)


def _inproj(layer, x, g, w, qg, kg, lng, lnb, tabs):
    n = x.shape[0]
    tm = TM_PROJ
    n_tiles = n // tm
    cur = lambda width: pl.BlockSpec((tm, width), lambda i: (jnp.minimum(i, n_tiles - 1), 0))
    lag = lambda width: pl.BlockSpec((tm, width), lambda i: (jnp.maximum(i - 1, 0), 0))
    out_shape = (
        jax.ShapeDtypeStruct((ATTN_WIDTH // LANES, n, LANES), BF16),
        jax.ShapeDtypeStruct((KV_WIDTH, n), BF16),
        jax.ShapeDtypeStruct((n, 2 * LANES), BF16),
        jax.ShapeDtypeStruct((n, GM_WIDTH), F32),
        jax.ShapeDtypeStruct((n, GM_WIDTH), BF16),
        jax.ShapeDtypeStruct((n, CONV_CH), F32),
    )
    return pl.pallas_call(
        functools.partial(_inproj_kernel, last_is_even=(n_tiles - 1) % 2 == 0),
        out_shape=out_shape,
        grid=(n_tiles + 1,),
        in_specs=[cur(D_MODEL), _layer_spec(layer, (1, D_MODEL)), _layer_spec(layer, (D_MODEL, IN_COLS)),
                  _layer_spec(layer, (1, LANES)), _layer_spec(layer, (1, LANES)),
                  _layer_spec(layer, (1, GM_WIDTH)), _layer_spec(layer, (1, GM_WIDTH)),
                  lag(LANES), lag(LANES), lag(LANES)],
        out_specs=(pl.BlockSpec((ATTN_WIDTH // LANES, tm, LANES), lambda i: (0, jnp.maximum(i - 1, 0), 0)),
                   pl.BlockSpec((KV_WIDTH, tm), lambda i: (0, jnp.maximum(i - 1, 0))),
                   lag(2 * LANES), lag(GM_WIDTH), lag(GM_WIDTH), lag(CONV_CH)),
        scratch_shapes=[pltpu.VMEM((tm, IN_COLS), F32), pltpu.VMEM((tm, IN_COLS), F32)],
        compiler_params=pltpu.CompilerParams(dimension_semantics=("arbitrary",),
                                             vmem_limit_bytes=VMEM_LIMIT),
        name="in_proj",
    )(x, g, w, qg, kg, lng, lnb, *tabs)


def _mixer_kernel(layer, sink_ref, x_ref, q_ref, ktp_ref, ktc_ref, ktn_ref, vp_ref, vc_ref, vn_ref, u_ref, vln_ref,
                  glup_ref, gluc_ref, glun_ref,
                  ws_ref, gmb_ref, cw_ref, cb_ref, clg_ref, clb_ref, og_ref, wout_ref,
                  o_ref,
                  kta_s, ktb_s, va_s, vb_s, bias_s, s_s, p_s, scale_s, glu_s, y_s, attn_s, gm_s, merged_s):
    tq = x_ref.shape[0]
    nblk = tq // BLOCK
    j = pl.program_id(1)
    first = j == 0
    last = j == pl.num_programs(1) - 1

    lane = lax.broadcasted_iota(jnp.int32, (1, LANES), 1)
    head0 = lane < HEAD_DIM
    zero = jnp.zeros((), BF16)
    zero_rows = jnp.zeros((HEAD_DIM, BLOCK), BF16)
    for b in range(nblk + 2):
        if b == 0:
            kt = ktp_ref[...]
        elif b == nblk + 1:
            kt = ktn_ref[...]
        else:
            kt = ktc_ref[:, (b - 1) * BLOCK:b * BLOCK]
        for h in range(ATTN_KV_HEADS):
            kh = kt[h * HEAD_DIM:(h + 1) * HEAD_DIM, :]
            kta_s[h, b] = jnp.concatenate([kh, zero_rows], axis=0)
            ktb_s[h, b] = jnp.concatenate([zero_rows, kh], axis=0)
    for src, r0, rows in ((vp_ref, 0, BLOCK), (vc_ref, BLOCK, tq), (vn_ref, BLOCK + tq, BLOCK)):
        dst = slice(r0, r0 + rows)
        v, v_sw = src[:, 0:LANES], src[:, LANES:2 * LANES]
        va_s[0, dst, :] = jnp.where(head0, v, zero)
        vb_s[0, dst, :] = jnp.where(head0, zero, v_sw)
        va_s[1, dst, :] = jnp.where(head0, v_sw, zero)
        vb_s[1, dst, :] = jnp.where(head0, zero, v)

    qi = lax.broadcasted_iota(jnp.int32, (BLOCK, BLOCK), 0)
    kj = lax.broadcasted_iota(jnp.int32, (BLOCK, BLOCK), 1)
    band_lo = jnp.where(kj >= qi, 0.0, NEG_INF)
    band_hi = jnp.where(kj <= qi, 0.0, NEG_INF)
    bias_s[0] = band_lo
    bias_s[1] = band_hi
    bias_s[2] = jnp.where(first, NEG_INF, band_lo)
    bias_s[3] = jnp.where(last, NEG_INF, band_hi)

    n_pairs = ATTN_GROUP // 2

    def scores(nb, h):
        r0 = pl.multiple_of(nb * BLOCK, BLOCK)
        q_st = jnp.concatenate([q_ref[n_pairs * h + pp, pl.ds(r0, BLOCK), :] for pp in range(n_pairs)], axis=0)
        kt_st = jnp.concatenate([kta_s[h, nb + t] for t in range(3)] + [ktb_s[h, nb + t] for t in range(3)],
                                axis=1)
        s_s[h] = jnp.dot(q_st, kt_st, preferred_element_type=F32)

    def softmax_pv(nb, h):
        r0 = pl.multiple_of(nb * BLOCK, BLOCK)
        lo_idx = jnp.where(nb == 0, 2, 0)
        hi_idx = jnp.where(nb == nblk - 1, 3, 1)
        v_st = jnp.concatenate([va_s[h, pl.ds(r0, 3 * BLOCK), :], vb_s[h, pl.ds(r0, 3 * BLOCK), :]], axis=0)
        for pp in range(n_pairs):
            for rc in range(0, BLOCK, SOFTMAX_ROWS):
                rr = slice(pp * BLOCK + rc, pp * BLOCK + rc + SOFTMAX_ROWS)
                br = slice(rc, rc + SOFTMAX_ROWS)
                inv = []
                for half in range(2):
                    sink = sink_ref[layer * ATTN_Q_HEADS + ATTN_GROUP * h + 2 * pp + half] * LOG2E
                    c0 = half * 3 * BLOCK
                    s0 = s_s[h, rr, c0:c0 + BLOCK] + bias_s[lo_idx, br, :]
                    s1 = s_s[h, rr, c0 + BLOCK:c0 + 2 * BLOCK]
                    s2 = s_s[h, rr, c0 + 2 * BLOCK:c0 + 3 * BLOCK] + bias_s[hi_idx, br, :]
                    m = jnp.max(jnp.maximum(jnp.maximum(s0, s1), s2), axis=-1, keepdims=True)
                    m = jnp.maximum(m, sink)
                    e0, e1, e2 = jnp.exp2(s0 - m), jnp.exp2(s1 - m), jnp.exp2(s2 - m)
                    denom = jnp.sum(e0 + e1 + e2, axis=-1, keepdims=True) + jnp.exp2(sink - m)
                    p_s[h, rr, c0:c0 + BLOCK] = e0.astype(BF16)
                    p_s[h, rr, c0 + BLOCK:c0 + 2 * BLOCK] = e1.astype(BF16)
                    p_s[h, rr, c0 + 2 * BLOCK:c0 + 3 * BLOCK] = e2.astype(BF16)
                    inv.append(1.0 / denom)
                scale_s[h, rr, :] = jnp.where(head0, inv[0], inv[1])
            pr = slice(pp * BLOCK, (pp + 1) * BLOCK)
            o = jnp.dot(p_s[h, pr, :], v_st, preferred_element_type=F32) * scale_s[h, pr, :]
            attn_s[n_pairs * h + pp, pl.ds(r0, BLOCK), :] = o

    scores(0, 0)

    def attn_step(nb, carry):
        scores(nb, 1)
        softmax_pv(nb, 0)
        scores(jnp.minimum(nb + 1, nblk - 1), 0)
        softmax_pv(nb, 1)
        return carry

    lax.fori_loop(0, nblk, attn_step, 0)

    for nb in range(nblk):
        rows = slice(nb * BLOCK, (nb + 1) * BLOCK)
        for hp in range(GM_HEADS // 2):
            cols = slice(hp * LANES, (hp + 1) * LANES)
            vblk = vln_ref[rows, cols]
            lhs = jnp.concatenate([ws_ref[2 * hp], ws_ref[2 * hp + 1]], axis=1)
            rhs = jnp.concatenate([jnp.where(head0, vblk, zero), jnp.where(head0, zero, vblk)], axis=0)
            sgate = jnp.dot(lhs, rhs, preferred_element_type=F32) + gmb_ref[:, cols]
            gm_s[rows, cols] = u_ref[rows, cols] * sgate

    n_ch = CONV_CH // LANES
    for ch in range(n_ch):
        cl = slice(ch * LANES, (ch + 1) * LANES)
        glu_s[ch, 0:CONV_HALO, :] = jnp.where(first, 0.0, glup_ref[:, cl])
        glu_s[ch, CONV_HALO:CONV_HALO + tq, :] = gluc_ref[:, cl]
        glu_s[ch, CONV_HALO + tq:, :] = jnp.where(last, 0.0, glun_ref[:, cl])
    group = SUBLANES * CONV_ROW_STRIDE

    def conv_step(i, carry):
        base = pl.multiple_of(i * (CONV_GROUPS * group), CONV_GROUPS * group)
        for ch in range(n_ch):
            cl = slice(ch * LANES, (ch + 1) * LANES)
            offs = [g * group + ph for g in range(CONV_GROUPS) for ph in range(CONV_ROW_STRIDE)]
            accs = [jnp.broadcast_to(cb_ref[:, cl], (SUBLANES, LANES)) for _ in offs]
            for t in range(CONV_WIDTH):
                w = cw_ref[t:t + 1, cl]
                for a, off in enumerate(offs):
                    start = base + (off + CONV_HALO - CONV_PAD + t)
                    accs[a] = accs[a] + w * glu_s[ch, pl.ds(start, SUBLANES, stride=CONV_ROW_STRIDE), :]
            for a, off in enumerate(offs):
                y_s[ch, pl.ds(base + off, SUBLANES, stride=CONV_ROW_STRIDE), :] = accs[a]
        return carry

    lax.fori_loop(0, tq // (CONV_GROUPS * group), conv_step, 0)

    rchunk = 128
    for r0 in range(0, tq, rchunk):
        rs = slice(r0, r0 + rchunk)
        y = jnp.concatenate([y_s[ch, rs, :] for ch in range(n_ch)], axis=1)
        y = _layer_norm(y, clg_ref[...], clb_ref[...])
        y = y * jax.nn.sigmoid(y)
        merged_s[rs, ATTN_WIDTH + GM_WIDTH:] = _rms(y, og_ref[:, ATTN_WIDTH + GM_WIDTH:]).astype(BF16)
        attn = jnp.concatenate([attn_s[pb, rs, :] for pb in range(ATTN_WIDTH // LANES)], axis=1)
        merged_s[rs, 0:ATTN_WIDTH] = _rms(attn, og_ref[:, 0:ATTN_WIDTH]).astype(BF16)
        merged_s[rs, ATTN_WIDTH:ATTN_WIDTH + GM_WIDTH] = _rms(
            gm_s[rs, :], og_ref[:, ATTN_WIDTH:ATTN_WIDTH + GM_WIDTH]).astype(BF16)
    o_ref[...] = x_ref[...] + jnp.dot(merged_s[...], wout_ref[...], preferred_element_type=F32)


def _mixer(layer, x, q, kt, v, u, vln, glu, sink, ws, gm_bias, cw, cb, clg, clb, og, wout, batch, seq):
    n = x.shape[0]
    tq = TQ_MIX
    tiles = seq // tq
    blk_per_tile = tq // BLOCK
    blk_per_seq = seq // BLOCK
    halo_per_tile = tq // CONV_HALO
    halo_per_seq = seq // CONV_HALO

    def row(width):
        return pl.BlockSpec((tq, width), lambda b, j, *_: (b * tiles + j, 0))

    def prev(rows, width, per_tile, per_seq):
        return pl.BlockSpec(
            (rows, width), lambda b, j, *_: (b * per_seq + jnp.maximum(j * per_tile - 1, 0), 0))

    def nxt(rows, width, per_tile, per_seq):
        return pl.BlockSpec(
            (rows, width), lambda b, j, *_: (b * per_seq + jnp.minimum((j + 1) * per_tile, per_seq - 1), 0))

    const = lambda shape: _layer_spec(layer, shape)
    kv_rows = tq + 2 * BLOCK
    grid_spec = pltpu.PrefetchScalarGridSpec(
        num_scalar_prefetch=1,
        grid=(batch, tiles),
        in_specs=[
            row(D_MODEL), pl.BlockSpec((ATTN_WIDTH // LANES, tq, LANES), lambda b, j, *_: (0, b * tiles + j, 0)),
            pl.BlockSpec((KV_WIDTH, BLOCK),
                         lambda b, j, *_: (0, b * blk_per_seq + jnp.maximum(j * blk_per_tile - 1, 0))),
            pl.BlockSpec((KV_WIDTH, tq), lambda b, j, *_: (0, b * tiles + j)),
            pl.BlockSpec((KV_WIDTH, BLOCK),
                         lambda b, j, *_: (0, b * blk_per_seq + jnp.minimum((j + 1) * blk_per_tile, blk_per_seq - 1))),
            prev(BLOCK, 2 * LANES, blk_per_tile, blk_per_seq), row(2 * LANES),
            nxt(BLOCK, 2 * LANES, blk_per_tile, blk_per_seq),
            row(GM_WIDTH), row(GM_WIDTH),
            prev(CONV_HALO, CONV_CH, halo_per_tile, halo_per_seq), row(CONV_CH),
            nxt(CONV_HALO, CONV_CH, halo_per_tile, halo_per_seq),
            const((GM_HEADS, CHUNK, CHUNK)), const((CHUNK, GM_WIDTH)),
            const((CONV_WIDTH, CONV_CH)), const((1, CONV_CH)), const((1, CONV_CH)), const((1, CONV_CH)),
            const((1, D_MODEL)), const((D_MODEL, D_MODEL)),
        ],
        out_specs=row(D_MODEL),
        scratch_shapes=[
            pltpu.VMEM((ATTN_KV_HEADS, blk_per_tile + 2, KV_WIDTH, BLOCK), BF16),
            pltpu.VMEM((ATTN_KV_HEADS, blk_per_tile + 2, KV_WIDTH, BLOCK), BF16),
            pltpu.VMEM((ATTN_KV_HEADS, kv_rows, LANES), BF16), pltpu.VMEM((ATTN_KV_HEADS, kv_rows, LANES), BF16),
            pltpu.VMEM((4, BLOCK, BLOCK), F32),
            pltpu.VMEM((2, ATTN_GROUP // 2 * BLOCK, 2 * 3 * BLOCK), F32),
            pltpu.VMEM((2, ATTN_GROUP // 2 * BLOCK, 2 * 3 * BLOCK), BF16),
            pltpu.VMEM((2, ATTN_GROUP // 2 * BLOCK, LANES), F32),
            pltpu.VMEM((CONV_CH // LANES, tq + 2 * CONV_HALO, LANES), F32),
            pltpu.VMEM((CONV_CH // LANES, tq, LANES), F32),
            pltpu.VMEM((ATTN_WIDTH // LANES, tq, LANES), F32), pltpu.VMEM((tq, GM_WIDTH), F32),
            pltpu.VMEM((tq, D_MODEL), BF16),
        ],
    )
    return pl.pallas_call(
        functools.partial(_mixer_kernel, layer),
        out_shape=jax.ShapeDtypeStruct((n, D_MODEL), F32),
        grid_spec=grid_spec,
        compiler_params=pltpu.CompilerParams(dimension_semantics=("arbitrary", "arbitrary"),
                                             vmem_limit_bytes=VMEM_LIMIT),
        name="mixer",
    )(sink, x, q, kt, kt, kt, v, v, v, u, vln, glu, glu, glu, ws, gm_bias, cw, cb, clg, clb, og, wout)


def _ffn_kernel(x_ref, p_ref, gf_ref, wgu_ref, wd_ref, gp_ref, wpg_ref, wpp_ref, o_ref, act_s):
    x = x_ref[...]
    hn = _rms(x, gf_ref[...]).astype(BF16)
    for c0 in range(0, D_FF, FF_CHUNK):
        gate = jnp.dot(hn, wgu_ref[:, c0:c0 + FF_CHUNK], preferred_element_type=F32)
        up = jnp.dot(hn, wgu_ref[:, D_FF + c0:D_FF + c0 + FF_CHUNK], preferred_element_type=F32)
        act_s[:, c0:c0 + FF_CHUNK] = (gate * jax.nn.sigmoid(gate) * up).astype(BF16)
    x = x + jnp.dot(act_s[...], wd_ref[...], preferred_element_type=F32)

    hp = _rms(x, gp_ref[...]).astype(BF16)
    gate = jax.nn.sigmoid(jnp.dot(hp, wpg_ref[...], preferred_element_type=F32))
    proj = jnp.dot(p_ref[...].astype(BF16), wpp_ref[...], preferred_element_type=F32)
    o_ref[...] = x + proj * gate


def _ffn(layer, x, p, gf, wgu, wd, gp, wpg, wpp):
    n = x.shape[0]
    tm = TM_FFN
    row = lambda width: pl.BlockSpec((tm, width), lambda i: (i, 0))
    return pl.pallas_call(
        _ffn_kernel,
        out_shape=jax.ShapeDtypeStruct((n, D_MODEL), F32),
        grid=(n // tm,),
        in_specs=[row(D_MODEL), pl.BlockSpec((None, tm, PLE_DIM), lambda i: (layer, i, 0)),
                  _layer_spec(layer, (1, D_MODEL)),
                  _layer_spec(layer, (D_MODEL, 2 * D_FF)), _layer_spec(layer, (D_FF, D_MODEL)),
                  _layer_spec(layer, (1, D_MODEL)), _layer_spec(layer, (D_MODEL, D_MODEL)),
                  _layer_spec(layer, (PLE_DIM, D_MODEL))],
        out_specs=row(D_MODEL),
        scratch_shapes=[pltpu.VMEM((tm, D_FF), BF16)],
        compiler_params=pltpu.CompilerParams(dimension_semantics=("arbitrary",),
                                             vmem_limit_bytes=VMEM_LIMIT),
        name="ffn_ple",
    )(x, p, gf, wgu, wd, gp, wpg, wpp)


def kernel(x, p, positions, norm_mix_g, w_in, q_norm_g, k_norm_g, sink, gm_ln_g, gm_ln_b, gm_ws, gm_bs, conv_w, conv_b, conv_ln_g, conv_ln_b, out_norm_g, w_out, norm_ffn_g, w_gate_up, w_down, ple_norm_g, w_ple_gate, w_ple_proj):
    batch, seq, d = x.shape
    depth = w_in.shape[0]
    n = batch * seq
    assert d == D_MODEL and seq % TQ_MIX == 0
    assert n % TM_PROJ == 0 and n % TM_FFN == 0 and n % TM_ROPE == 0

    vec = lambda a: a[:, None, :]
    w_in_b, w_out_b = w_in.astype(BF16), w_out.astype(BF16)
    w_gu_b, w_down_b = w_gate_up.astype(BF16), w_down.astype(BF16)
    w_pg_b, w_pp_b = w_ple_gate.astype(BF16), w_ple_proj.astype(BF16)
    ws_b = gm_ws.astype(BF16)
    qg = vec(jnp.tile(q_norm_g, (1, LANES // HEAD_DIM)))
    kg = vec(jnp.tile(k_norm_g, (1, LANES // HEAD_DIM)))
    gm_bias = jnp.repeat(jnp.swapaxes(gm_bs, 1, 2), HEAD_DIM, axis=2)
    sink_flat = sink.reshape(depth * ATTN_Q_HEADS)
    p_flat = p.reshape(depth, n, PLE_DIM)

    tabs = _rope_tables(positions.reshape(1, n))
    xf = x.reshape(n, d)
    for i in range(depth):
        q, kt, v, u, vln, glu = _inproj(i, xf, vec(norm_mix_g), w_in_b, qg, kg, vec(gm_ln_g), vec(gm_ln_b), tabs)
        xf = _mixer(i, xf, q, kt, v, u, vln, glu, sink_flat, ws_b, gm_bias, conv_w, vec(conv_b),
                    vec(conv_ln_g), vec(conv_ln_b), vec(out_norm_g), w_out_b, batch, seq)
        xf = _ffn(i, xf, p_flat, vec(norm_ffn_g), w_gu_b, w_down_b, vec(ple_norm_g), w_pg_b, w_pp_b)
    return xf.reshape(batch, seq, d)
```

```python
import functools
import math

import numpy as np
import jax
import jax.numpy as jnp
from jax import lax
from jax.experimental import pallas as pl
from jax.experimental.pallas import tpu as pltpu

F32 = jnp.float32
BF16 = jnp.bfloat16

D_MODEL = 1024
HEAD_DIM = 64
ATTN_WIDTH = 512
ATTN_Q_HEADS = 8
ATTN_KV_HEADS = 2
ATTN_GROUP = 4
KV_WIDTH = 128
WINDOW = 128
BLOCK = 128
ROPE_THETA = 500000.0
ROT_DIM = 16
GM_WIDTH = 256
GM_HEADS = 4
CHUNK = 128
CONV_CH = 256
CONV_WIDTH = 31
CONV_PAD = 15
D_FF = 2816
PLE_DIM = 256
EPS = 1e-6
NEG_INF = -1e30

Q_OFF = 0
K_OFF = Q_OFF + ATTN_WIDTH
V_OFF = K_OFF + KV_WIDTH
GM_OFF = V_OFF + KV_WIDTH
CONV_OFF = GM_OFF + 2 * GM_WIDTH
IN_COLS = CONV_OFF + 2 * CONV_CH

LANES = 128
SUBLANES = 8
BF16_ROWS = 16
CONV_HALO = 16
CONV_ROW_STRIDE = 2
CONV_GROUPS = 2
VMEM_LIMIT = 56 * 1024 * 1024

TM_ROPE = 2048
TM_PROJ = 512
TQ_MIX = 1024
TM_FFN = 1024
FF_CHUNK = 256
SOFTMAX_ROWS = 32

INV_FREQ = [float(ROPE_THETA ** (-(2.0 * j) / ROT_DIM)) for j in range(ROT_DIM // 2)]
SQRT_HALF = float(np.sqrt(0.5))
LOG2E = float(np.log2(np.e))


def _layer_spec(layer, shape):
    nd = len(shape)
    return pl.BlockSpec((None,) + tuple(shape), lambda *_: (layer,) + (0,) * nd,
                        pipeline_mode=pl.Buffered(1))


def _resident_spec(shape):
    nd = len(shape)
    return pl.BlockSpec(tuple(shape), lambda *_: (0,) * nd, pipeline_mode=pl.Buffered(1))


def _cast_block_rows(rows, n_steps):
    r = BF16_ROWS
    while r * n_steps < rows or rows % r:
        r += BF16_ROWS
    return r


def _cast_specs(layer, rows, cols, n_steps, step_of):
    r = _cast_block_rows(rows, n_steps)
    last = rows // r - 1
    src = pl.BlockSpec((None, r, cols), lambda *idx: (layer, jnp.minimum(step_of(*idx), last), 0))
    dst = pl.BlockSpec((r, cols), lambda *idx: (jnp.minimum(step_of(*idx), last), 0))
    return src, dst


def _split_bf16(x):
    hi = x.astype(BF16)
    lo = (x - hi.astype(F32)).astype(BF16)
    return hi, lo


def _rope_table_kernel(pos_ref, c_ref, s1_ref, s2_ref):
    half = ROT_DIM // 2
    pos = pos_ref[...].astype(F32)
    row = lax.broadcasted_iota(jnp.int32, (ROT_DIM, 1), 0)
    inv = jnp.zeros((ROT_DIM, 1), F32)
    for j, f in enumerate(INV_FREQ):
        inv = jnp.where((row & (half - 1)) == j, f, inv)
    ang = inv * pos
    cs = jnp.where(row < half, jnp.cos(ang), jnp.sin(ang))

    r = lax.broadcasted_iota(jnp.int32, (ROT_DIM, 3 * LANES), 0)
    col = lax.broadcasted_iota(jnp.int32, (ROT_DIM, 3 * LANES), 1)
    table = col // LANES
    in_head = col & (HEAD_DIM - 1)
    freq = col & (half - 1)
    put_cos = (table == 0) & (in_head < ROT_DIM) & (r == freq)
    put_nsin = (table == 1) & (in_head < half) & (r == freq + half)
    put_sin = (table == 2) & (in_head >= half) & (in_head < ROT_DIM) & (r == freq + half)
    place = jnp.where(put_cos | put_sin, 1.0, jnp.where(put_nsin, -1.0, 0.0)).astype(BF16)

    hi, lo = _split_bf16(cs)
    dn = (((0,), (0,)), ((), ()))
    t = (lax.dot_general(hi, place, dn, preferred_element_type=F32)
         + lax.dot_general(lo, place, dn, preferred_element_type=F32))
    lane = lax.broadcasted_iota(jnp.int32, (1, LANES), 1)
    c_ref[...] = jnp.where((lane & (HEAD_DIM - 1)) < ROT_DIM, t[:, 0:LANES], 1.0)
    s1_ref[...] = t[:, LANES:2 * LANES]
    s2_ref[...] = t[:, 2 * LANES:3 * LANES]


def _rope_tables(pos_row):
    n = pos_row.shape[1]
    tm = TM_ROPE
    tab = jax.ShapeDtypeStruct((n, LANES), F32)
    spec = pl.BlockSpec((tm, LANES), lambda i: (i, 0))
    return pl.pallas_call(
        _rope_table_kernel,
        out_shape=(tab, tab, tab),
        grid=(n // tm,),
        in_specs=[pl.BlockSpec((1, tm), lambda i: (0, i))],
        out_specs=(spec, spec, spec),
        name="rope_tables",
    )(pos_row)


def _rms(x, gain):
    ms = jnp.mean(x * x, axis=-1, keepdims=True)
    return x * lax.rsqrt(ms + EPS) * gain


def _layer_norm(x, gain, bias):
    mu = jnp.mean(x, axis=-1, keepdims=True)
    xc = x - mu
    var = jnp.mean(xc * xc, axis=-1, keepdims=True)
    return xc * lax.rsqrt(var + EPS) * gain + bias


def _inproj_kernel(x_ref, g_ref, w_ref, qg_ref, kg_ref, lng_ref, lnb_ref,
                   c_ref, s1_ref, s2_ref, wout_ref,
                   q_out, kt_out, v_out, u_out, vln_out, glu_out, wout_bf_out, w_bf, z_even, z_odd, *, last_is_even):
    i = pl.program_id(0)
    n_tiles = pl.num_programs(0) - 1

    wout_bf_out[...] = wout_ref[...].astype(BF16)

    def project(z_dst):
        h = _rms(x_ref[...], g_ref[...]).astype(BF16)
        z_dst[...] = jnp.dot(h, w_bf[...], preferred_element_type=F32)

    def finish(z_src):
        cos_t, sin_a, sin_b = c_ref[...], s1_ref[...], s2_ref[...]
        lane = lax.broadcasted_iota(jnp.int32, (1, LANES), 1)
        head0 = lane < HEAD_DIM

        def head_norm_rope(z, gain):
            sq = z * z
            first = jnp.sum(jnp.where(head0, sq, 0.0), axis=-1, keepdims=True)
            both = jnp.sum(sq, axis=-1, keepdims=True)
            ss = jnp.where(head0, first, both - first)
            zn = z * lax.rsqrt(ss * (1.0 / HEAD_DIM) + EPS) * gain
            return (zn * cos_t
                    + pltpu.roll(zn, LANES - ROT_DIM // 2, 1) * sin_a
                    + pltpu.roll(zn, ROT_DIM // 2, 1) * sin_b)

        q_gain = qg_ref[...] * (LOG2E / math.sqrt(HEAD_DIM))
        for b in range(ATTN_WIDTH // LANES):
            q_out[b] = head_norm_rope(z_src[:, Q_OFF + b * LANES:Q_OFF + (b + 1) * LANES], q_gain).astype(BF16)

        k = head_norm_rope(z_src[:, K_OFF:V_OFF], kg_ref[...])
        v = z_src[:, V_OFF:GM_OFF]
        kt_out[...] = k.T.astype(BF16)
        v_out[:, 0:LANES] = v.astype(BF16)
        v_out[:, LANES:2 * LANES] = pltpu.roll(v, HEAD_DIM, 1).astype(BF16)

        zuv = z_src[:, GM_OFF:CONV_OFF]
        uv = 0.5 * zuv * (1.0 + lax.erf(zuv * SQRT_HALF))
        u_out[...] = uv[:, 0:GM_WIDTH]
        vln_out[...] = _layer_norm(uv[:, GM_WIDTH:], lng_ref[...], lnb_ref[...]).astype(BF16)

        glu_out[...] = z_src[:, CONV_OFF:CONV_OFF + CONV_CH] * jax.nn.sigmoid(z_src[:, CONV_OFF + CONV_CH:])

    inner = jnp.logical_and(i > 0, i < n_tiles)

    @pl.when(i == 0)
    def _():
        w_bf[...] = w_ref[...].astype(BF16)
        project(z_even)

    @pl.when(jnp.logical_and(inner, i % 2 == 1))
    def _():
        finish(z_even)
        project(z_odd)

    @pl.when(jnp.logical_and(inner, i % 2 == 0))
    def _():
        finish(z_odd)
        project(z_even)

    @pl.when(i == n_tiles)
    def _():
        finish(z_even if last_is_even else z_odd)


def _inproj(layer, x, g, w, qg, kg, lng, lnb, tabs, wout):
    n = x.shape[0]
    tm = TM_PROJ
    n_tiles = n // tm
    cur = lambda width: pl.BlockSpec((tm, width), lambda i: (jnp.minimum(i, n_tiles - 1), 0))
    lag = lambda width: pl.BlockSpec((tm, width), lambda i: (jnp.maximum(i - 1, 0), 0))
    out_shape = (
        jax.ShapeDtypeStruct((ATTN_WIDTH // LANES, n, LANES), BF16),
        jax.ShapeDtypeStruct((KV_WIDTH, n), BF16),
        jax.ShapeDtypeStruct((n, 2 * LANES), BF16),
        jax.ShapeDtypeStruct((n, GM_WIDTH), F32),
        jax.ShapeDtypeStruct((n, GM_WIDTH), BF16),
        jax.ShapeDtypeStruct((n, CONV_CH), F32),
        jax.ShapeDtypeStruct((D_MODEL, D_MODEL), BF16),
    )
    wout_src, wout_dst = _cast_specs(layer, D_MODEL, D_MODEL, n_tiles + 1, lambda i: i)
    return pl.pallas_call(
        functools.partial(_inproj_kernel, last_is_even=(n_tiles - 1) % 2 == 0),
        out_shape=out_shape,
        grid=(n_tiles + 1,),
        in_specs=[cur(D_MODEL), _layer_spec(layer, (1, D_MODEL)), _layer_spec(layer, (D_MODEL, IN_COLS)),
                  _layer_spec(layer, (1, LANES)), _layer_spec(layer, (1, LANES)),
                  _layer_spec(layer, (1, GM_WIDTH)), _layer_spec(layer, (1, GM_WIDTH)),
                  lag(LANES), lag(LANES), lag(LANES), wout_src],
        out_specs=(pl.BlockSpec((ATTN_WIDTH // LANES, tm, LANES), lambda i: (0, jnp.maximum(i - 1, 0), 0)),
                   pl.BlockSpec((KV_WIDTH, tm), lambda i: (0, jnp.maximum(i - 1, 0))),
                   lag(2 * LANES), lag(GM_WIDTH), lag(GM_WIDTH), lag(CONV_CH), wout_dst),
        scratch_shapes=[pltpu.VMEM((D_MODEL, IN_COLS), BF16),
                        pltpu.VMEM((tm, IN_COLS), F32), pltpu.VMEM((tm, IN_COLS), F32)],
        compiler_params=pltpu.CompilerParams(dimension_semantics=("arbitrary",),
                                             vmem_limit_bytes=VMEM_LIMIT),
        name="in_proj",
    )(x, g, w, qg, kg, lng, lnb, *tabs, wout)


def _mixer_kernel(layer, sink_ref, x_ref, q_ref, ktp_ref, ktc_ref, ktn_ref, vp_ref, vc_ref, vn_ref, u_ref, vln_ref,
                  glup_ref, gluc_ref, glun_ref,
                  ws_ref, gmb_ref, cw_ref, cb_ref, clg_ref, clb_ref, og_ref, wout_ref,
                  wgu_ref, wd_ref, wpg_ref, wpp_ref,
                  o_ref, wgu_bf_out, wd_bf_out, wpg_bf_out, wpp_bf_out,
                  kta_s, ktb_s, va_s, vb_s, bias_s, s_s, p_s, scale_s, glu_s, y_s, attn_s, gm_s, merged_s):
    tq = x_ref.shape[0]
    nblk = tq // BLOCK
    j = pl.program_id(1)
    first = j == 0
    last = j == pl.num_programs(1) - 1

    wgu_bf_out[...] = wgu_ref[...].astype(BF16)
    wd_bf_out[...] = wd_ref[...].astype(BF16)
    wpg_bf_out[...] = wpg_ref[...].astype(BF16)
    wpp_bf_out[...] = wpp_ref[...].astype(BF16)

    lane = lax.broadcasted_iota(jnp.int32, (1, LANES), 1)
    head0 = lane < HEAD_DIM
    zero = jnp.zeros((), BF16)
    zero_rows = jnp.zeros((HEAD_DIM, BLOCK), BF16)
    for b in range(nblk + 2):
        if b == 0:
            kt = ktp_ref[...]
        elif b == nblk + 1:
            kt = ktn_ref[...]
        else:
            kt = ktc_ref[:, (b - 1) * BLOCK:b * BLOCK]
        for h in range(ATTN_KV_HEADS):
            kh = kt[h * HEAD_DIM:(h + 1) * HEAD_DIM, :]
            kta_s[h, b] = jnp.concatenate([kh, zero_rows], axis=0)
            ktb_s[h, b] = jnp.concatenate([zero_rows, kh], axis=0)
    for src, r0, rows in ((vp_ref, 0, BLOCK), (vc_ref, BLOCK, tq), (vn_ref, BLOCK + tq, BLOCK)):
        dst = slice(r0, r0 + rows)
        v, v_sw = src[:, 0:LANES], src[:, LANES:2 * LANES]
        va_s[0, dst, :] = jnp.where(head0, v, zero)
        vb_s[0, dst, :] = jnp.where(head0, zero, v_sw)
        va_s[1, dst, :] = jnp.where(head0, v_sw, zero)
        vb_s[1, dst, :] = jnp.where(head0, zero, v)

    qi = lax.broadcasted_iota(jnp.int32, (BLOCK, BLOCK), 0)
    kj = lax.broadcasted_iota(jnp.int32, (BLOCK, BLOCK), 1)
    band_lo = jnp.where(kj >= qi, 0.0, NEG_INF)
    band_hi = jnp.where(kj <= qi, 0.0, NEG_INF)
    bias_s[0] = band_lo
    bias_s[1] = band_hi
    bias_s[2] = jnp.where(first, NEG_INF, band_lo)
    bias_s[3] = jnp.where(last, NEG_INF, band_hi)

    n_pairs = ATTN_GROUP // 2

    def scores(nb, h):
        r0 = pl.multiple_of(nb * BLOCK, BLOCK)
        q_st = jnp.concatenate([q_ref[n_pairs * h + pp, pl.ds(r0, BLOCK), :] for pp in range(n_pairs)], axis=0)
        kt_st = jnp.concatenate([kta_s[h, nb + t] for t in range(3)] + [ktb_s[h, nb + t] for t in range(3)],
                                axis=1)
        s_s[h] = jnp.dot(q_st, kt_st, preferred_element_type=F32)

    def softmax_pv(nb, h):
        r0 = pl.multiple_of(nb * BLOCK, BLOCK)
        lo_idx = jnp.where(nb == 0, 2, 0)
        hi_idx = jnp.where(nb == nblk - 1, 3, 1)
        v_st = jnp.concatenate([va_s[h, pl.ds(r0, 3 * BLOCK), :], vb_s[h, pl.ds(r0, 3 * BLOCK), :]], axis=0)
        for pp in range(n_pairs):
            for rc in range(0, BLOCK, SOFTMAX_ROWS):
                rr = slice(pp * BLOCK + rc, pp * BLOCK + rc + SOFTMAX_ROWS)
                br = slice(rc, rc + SOFTMAX_ROWS)
                inv = []
                for half in range(2):
                    sink = sink_ref[layer * ATTN_Q_HEADS + ATTN_GROUP * h + 2 * pp + half] * LOG2E
                    c0 = half * 3 * BLOCK
                    s0 = s_s[h, rr, c0:c0 + BLOCK] + bias_s[lo_idx, br, :]
                    s1 = s_s[h, rr, c0 + BLOCK:c0 + 2 * BLOCK]
                    s2 = s_s[h, rr, c0 + 2 * BLOCK:c0 + 3 * BLOCK] + bias_s[hi_idx, br, :]
                    m = jnp.max(jnp.maximum(jnp.maximum(s0, s1), s2), axis=-1, keepdims=True)
                    m = jnp.maximum(m, sink)
                    e0, e1, e2 = jnp.exp2(s0 - m), jnp.exp2(s1 - m), jnp.exp2(s2 - m)
                    denom = jnp.sum(e0 + e1 + e2, axis=-1, keepdims=True) + jnp.exp2(sink - m)
                    p_s[h, rr, c0:c0 + BLOCK] = e0.astype(BF16)
                    p_s[h, rr, c0 + BLOCK:c0 + 2 * BLOCK] = e1.astype(BF16)
                    p_s[h, rr, c0 + 2 * BLOCK:c0 + 3 * BLOCK] = e2.astype(BF16)
                    inv.append(1.0 / denom)
                scale_s[h, rr, :] = jnp.where(head0, inv[0], inv[1])
            pr = slice(pp * BLOCK, (pp + 1) * BLOCK)
            o = jnp.dot(p_s[h, pr, :], v_st, preferred_element_type=F32) * scale_s[h, pr, :]
            attn_s[n_pairs * h + pp, pl.ds(r0, BLOCK), :] = o

    scores(0, 0)

    def attn_step(nb, carry):
        scores(nb, 1)
        softmax_pv(nb, 0)
        scores(jnp.minimum(nb + 1, nblk - 1), 0)
        softmax_pv(nb, 1)
        return carry

    lax.fori_loop(0, nblk, attn_step, 0)

    for nb in range(nblk):
        rows = slice(nb * BLOCK, (nb + 1) * BLOCK)
        for hp in range(GM_HEADS // 2):
            cols = slice(hp * LANES, (hp + 1) * LANES)
            vblk = vln_ref[rows, cols]
            lhs = jnp.concatenate([ws_ref[2 * hp], ws_ref[2 * hp + 1]], axis=1)
            rhs = jnp.concatenate([jnp.where(head0, vblk, zero), jnp.where(head0, zero, vblk)], axis=0)
            sgate = jnp.dot(lhs, rhs, preferred_element_type=F32) + gmb_ref[:, cols]
            gm_s[rows, cols] = u_ref[rows, cols] * sgate

    n_ch = CONV_CH // LANES
    for ch in range(n_ch):
        cl = slice(ch * LANES, (ch + 1) * LANES)
        glu_s[ch, 0:CONV_HALO, :] = jnp.where(first, 0.0, glup_ref[:, cl])
        glu_s[ch, CONV_HALO:CONV_HALO + tq, :] = gluc_ref[:, cl]
        glu_s[ch, CONV_HALO + tq:, :] = jnp.where(last, 0.0, glun_ref[:, cl])
    group = SUBLANES * CONV_ROW_STRIDE

    def conv_step(i, carry):
        base = pl.multiple_of(i * (CONV_GROUPS * group), CONV_GROUPS * group)
        for ch in range(n_ch):
            cl = slice(ch * LANES, (ch + 1) * LANES)
            offs = [g * group + ph for g in range(CONV_GROUPS) for ph in range(CONV_ROW_STRIDE)]
            accs = [jnp.broadcast_to(cb_ref[:, cl], (SUBLANES, LANES)) for _ in offs]
            for t in range(CONV_WIDTH):
                w = cw_ref[t:t + 1, cl]
                for a, off in enumerate(offs):
                    start = base + (off + CONV_HALO - CONV_PAD + t)
                    accs[a] = accs[a] + w * glu_s[ch, pl.ds(start, SUBLANES, stride=CONV_ROW_STRIDE), :]
            for a, off in enumerate(offs):
                y_s[ch, pl.ds(base + off, SUBLANES, stride=CONV_ROW_STRIDE), :] = accs[a]
        return carry

    lax.fori_loop(0, tq // (CONV_GROUPS * group), conv_step, 0)

    rchunk = 128
    for r0 in range(0, tq, rchunk):
        rs = slice(r0, r0 + rchunk)
        y = jnp.concatenate([y_s[ch, rs, :] for ch in range(n_ch)], axis=1)
        y = _layer_norm(y, clg_ref[...], clb_ref[...])
        y = y * jax.nn.sigmoid(y)
        merged_s[rs, ATTN_WIDTH + GM_WIDTH:] = _rms(y, og_ref[:, ATTN_WIDTH + GM_WIDTH:]).astype(BF16)
        attn = jnp.concatenate([attn_s[pb, rs, :] for pb in range(ATTN_WIDTH // LANES)], axis=1)
        merged_s[rs, 0:ATTN_WIDTH] = _rms(attn, og_ref[:, 0:ATTN_WIDTH]).astype(BF16)
        merged_s[rs, ATTN_WIDTH:ATTN_WIDTH + GM_WIDTH] = _rms(
            gm_s[rs, :], og_ref[:, ATTN_WIDTH:ATTN_WIDTH + GM_WIDTH]).astype(BF16)
    o_ref[...] = x_ref[...] + jnp.dot(merged_s[...], wout_ref[...], preferred_element_type=F32)


def _mixer(layer, x, q, kt, v, u, vln, glu, sink, ws, gm_bias, cw, cb, clg, clb, og, wout,
           wgu, wd, wpg, wpp, batch, seq):
    n = x.shape[0]
    tq = TQ_MIX
    tiles = seq // tq
    blk_per_tile = tq // BLOCK
    blk_per_seq = seq // BLOCK
    halo_per_tile = tq // CONV_HALO
    halo_per_seq = seq // CONV_HALO

    def row(width):
        return pl.BlockSpec((tq, width), lambda b, j, *_: (b * tiles + j, 0))

    def prev(rows, width, per_tile, per_seq):
        return pl.BlockSpec(
            (rows, width), lambda b, j, *_: (b * per_seq + jnp.maximum(j * per_tile - 1, 0), 0))

    def nxt(rows, width, per_tile, per_seq):
        return pl.BlockSpec(
            (rows, width), lambda b, j, *_: (b * per_seq + jnp.minimum((j + 1) * per_tile, per_seq - 1), 0))

    const = lambda shape: _layer_spec(layer, shape)
    kv_rows = tq + 2 * BLOCK
    step_of = lambda b, j, *_: b * tiles + j
    casts = [_cast_specs(layer, rows, cols, batch * tiles, step_of)
             for rows, cols in ((D_MODEL, 2 * D_FF), (D_FF, D_MODEL), (D_MODEL, D_MODEL), (PLE_DIM, D_MODEL))]
    grid_spec = pltpu.PrefetchScalarGridSpec(
        num_scalar_prefetch=1,
        grid=(batch, tiles),
        in_specs=[
            row(D_MODEL), pl.BlockSpec((ATTN_WIDTH // LANES, tq, LANES), lambda b, j, *_: (0, b * tiles + j, 0)),
            pl.BlockSpec((KV_WIDTH, BLOCK),
                         lambda b, j, *_: (0, b * blk_per_seq + jnp.maximum(j * blk_per_tile - 1, 0))),
            pl.BlockSpec((KV_WIDTH, tq), lambda b, j, *_: (0, b * tiles + j)),
            pl.BlockSpec((KV_WIDTH, BLOCK),
                         lambda b, j, *_: (0, b * blk_per_seq + jnp.minimum((j + 1) * blk_per_tile, blk_per_seq - 1))),
            prev(BLOCK, 2 * LANES, blk_per_tile, blk_per_seq), row(2 * LANES),
            nxt(BLOCK, 2 * LANES, blk_per_tile, blk_per_seq),
            row(GM_WIDTH), row(GM_WIDTH),
            prev(CONV_HALO, CONV_CH, halo_per_tile, halo_per_seq), row(CONV_CH),
            nxt(CONV_HALO, CONV_CH, halo_per_tile, halo_per_seq),
            const((GM_HEADS, CHUNK, CHUNK)), const((CHUNK, GM_WIDTH)),
            const((CONV_WIDTH, CONV_CH)), const((1, CONV_CH)), const((1, CONV_CH)), const((1, CONV_CH)),
            const((1, D_MODEL)), _resident_spec((D_MODEL, D_MODEL)),
            *[src for src, _ in casts],
        ],
        out_specs=(row(D_MODEL), *[dst for _, dst in casts]),
        scratch_shapes=[
            pltpu.VMEM((ATTN_KV_HEADS, blk_per_tile + 2, KV_WIDTH, BLOCK), BF16),
            pltpu.VMEM((ATTN_KV_HEADS, blk_per_tile + 2, KV_WIDTH, BLOCK), BF16),
            pltpu.VMEM((ATTN_KV_HEADS, kv_rows, LANES), BF16), pltpu.VMEM((ATTN_KV_HEADS, kv_rows, LANES), BF16),
            pltpu.VMEM((4, BLOCK, BLOCK), F32),
            pltpu.VMEM((2, ATTN_GROUP // 2 * BLOCK, 2 * 3 * BLOCK), F32),
            pltpu.VMEM((2, ATTN_GROUP // 2 * BLOCK, 2 * 3 * BLOCK), BF16),
            pltpu.VMEM((2, ATTN_GROUP // 2 * BLOCK, LANES), F32),
            pltpu.VMEM((CONV_CH // LANES, tq + 2 * CONV_HALO, LANES), F32),
            pltpu.VMEM((CONV_CH // LANES, tq, LANES), F32),
            pltpu.VMEM((ATTN_WIDTH // LANES, tq, LANES), F32), pltpu.VMEM((tq, GM_WIDTH), F32),
            pltpu.VMEM((tq, D_MODEL), BF16),
        ],
    )
    return pl.pallas_call(
        functools.partial(_mixer_kernel, layer),
        out_shape=(jax.ShapeDtypeStruct((n, D_MODEL), F32),
                   jax.ShapeDtypeStruct((D_MODEL, 2 * D_FF), BF16), jax.ShapeDtypeStruct((D_FF, D_MODEL), BF16),
                   jax.ShapeDtypeStruct((D_MODEL, D_MODEL), BF16), jax.ShapeDtypeStruct((PLE_DIM, D_MODEL), BF16)),
        grid_spec=grid_spec,
        compiler_params=pltpu.CompilerParams(dimension_semantics=("arbitrary", "arbitrary"),
                                             vmem_limit_bytes=VMEM_LIMIT),
        name="mixer",
    )(sink, x, q, kt, kt, kt, v, v, v, u, vln, glu, glu, glu, ws, gm_bias, cw, cb, clg, clb, og, wout,
      wgu, wd, wpg, wpp)


def _ffn_kernel(x_ref, p_ref, gf_ref, wgu_ref, wd_ref, gp_ref, wpg_ref, wpp_ref, o_ref, act_s):
    x = x_ref[...]
    hn = _rms(x, gf_ref[...]).astype(BF16)
    for c0 in range(0, D_FF, FF_CHUNK):
        gate = jnp.dot(hn, wgu_ref[:, c0:c0 + FF_CHUNK], preferred_element_type=F32)
        up = jnp.dot(hn, wgu_ref[:, D_FF + c0:D_FF + c0 + FF_CHUNK], preferred_element_type=F32)
        act_s[:, c0:c0 + FF_CHUNK] = (gate * jax.nn.sigmoid(gate) * up).astype(BF16)
    x = x + jnp.dot(act_s[...], wd_ref[...], preferred_element_type=F32)

    hp = _rms(x, gp_ref[...]).astype(BF16)
    gate = jax.nn.sigmoid(jnp.dot(hp, wpg_ref[...], preferred_element_type=F32))
    proj = jnp.dot(p_ref[...].astype(BF16), wpp_ref[...], preferred_element_type=F32)
    o_ref[...] = x + proj * gate


def _ffn(layer, x, p, gf, wgu, wd, gp, wpg, wpp):
    n = x.shape[0]
    tm = TM_FFN
    row = lambda width: pl.BlockSpec((tm, width), lambda i: (i, 0))
    return pl.pallas_call(
        _ffn_kernel,
        out_shape=jax.ShapeDtypeStruct((n, D_MODEL), F32),
        grid=(n // tm,),
        in_specs=[row(D_MODEL), pl.BlockSpec((None, tm, PLE_DIM), lambda i: (layer, i, 0)),
                  _layer_spec(layer, (1, D_MODEL)),
                  _resident_spec((D_MODEL, 2 * D_FF)), _resident_spec((D_FF, D_MODEL)),
                  _layer_spec(layer, (1, D_MODEL)), _resident_spec((D_MODEL, D_MODEL)),
                  _resident_spec((PLE_DIM, D_MODEL))],
        out_specs=row(D_MODEL),
        scratch_shapes=[pltpu.VMEM((tm, D_FF), BF16)],
        compiler_params=pltpu.CompilerParams(dimension_semantics=("arbitrary",),
                                             vmem_limit_bytes=VMEM_LIMIT),
        name="ffn_ple",
    )(x, p, gf, wgu, wd, gp, wpg, wpp)


def kernel(x, p, positions, norm_mix_g, w_in, q_norm_g, k_norm_g, sink, gm_ln_g, gm_ln_b, gm_ws, gm_bs, conv_w, conv_b, conv_ln_g, conv_ln_b, out_norm_g, w_out, norm_ffn_g, w_gate_up, w_down, ple_norm_g, w_ple_gate, w_ple_proj):
    batch, seq, d = x.shape
    depth = w_in.shape[0]
    n = batch * seq
    assert d == D_MODEL and seq % TQ_MIX == 0
    assert n % TM_PROJ == 0 and n % TM_FFN == 0 and n % TM_ROPE == 0

    vec = lambda a: a[:, None, :]
    ws_b = gm_ws.astype(BF16)
    qg = vec(jnp.tile(q_norm_g, (1, LANES // HEAD_DIM)))
    kg = vec(jnp.tile(k_norm_g, (1, LANES // HEAD_DIM)))
    gm_bias = jnp.repeat(jnp.swapaxes(gm_bs, 1, 2), HEAD_DIM, axis=2)
    sink_flat = sink.reshape(depth * ATTN_Q_HEADS)
    p_flat = p.reshape(depth, n, PLE_DIM)

    tabs = _rope_tables(positions.reshape(1, n))
    xf = x.reshape(n, d)
    for i in range(depth):
        q, kt, v, u, vln, glu, w_out_b = _inproj(i, xf, vec(norm_mix_g), w_in, qg, kg, vec(gm_ln_g),
                                                 vec(gm_ln_b), tabs, w_out)
        xf, w_gu_b, w_down_b, w_pg_b, w_pp_b = _mixer(
            i, xf, q, kt, v, u, vln, glu, sink_flat, ws_b, gm_bias, conv_w, vec(conv_b), vec(conv_ln_g),
            vec(conv_ln_b), vec(out_norm_g), w_out_b, w_gate_up, w_down, w_ple_gate, w_ple_proj, batch, seq)
        xf = _ffn(i, xf, p_flat, vec(norm_ffn_g), w_gu_b, w_down_b, vec(ple_norm_g), w_pg_b, w_pp_b)
    return xf.reshape(batch, seq, d)
```

```python
import functools
import math

import numpy as np
import jax
import jax.numpy as jnp
from jax import lax
from jax.experimental import pallas as pl
from jax.experimental.pallas import tpu as pltpu

F32 = jnp.float32
BF16 = jnp.bfloat16

D_MODEL = 1024
HEAD_DIM = 64
ATTN_WIDTH = 512
ATTN_Q_HEADS = 8
ATTN_KV_HEADS = 2
ATTN_GROUP = 4
KV_WIDTH = 128
WINDOW = 128
BLOCK = 128
ROPE_THETA = 500000.0
ROT_DIM = 16
GM_WIDTH = 256
GM_HEADS = 4
CHUNK = 128
CONV_CH = 256
CONV_WIDTH = 31
CONV_PAD = 15
D_FF = 2816
PLE_DIM = 256
EPS = 1e-6
NEG_INF = -1e30

Q_OFF = 0
K_OFF = Q_OFF + ATTN_WIDTH
V_OFF = K_OFF + KV_WIDTH
GM_OFF = V_OFF + KV_WIDTH
CONV_OFF = GM_OFF + 2 * GM_WIDTH
IN_COLS = CONV_OFF + 2 * CONV_CH

LANES = 128
SUBLANES = 8
BF16_ROWS = 16
CONV_HALO = 16
CONV_ROW_STRIDE = 2
CONV_GROUPS = 2
VMEM_LIMIT = 56 * 1024 * 1024

TM_ROPE = 2048
TM_PROJ = 512
TQ_MIX = 1024
TM_FFN = 1024
FF_CHUNK = 256
SOFTMAX_ROWS = 32

INV_FREQ = [float(ROPE_THETA ** (-(2.0 * j) / ROT_DIM)) for j in range(ROT_DIM // 2)]
SQRT_HALF = float(np.sqrt(0.5))
LOG2E = float(np.log2(np.e))


def _layer_spec(layer, shape):
    nd = len(shape)
    return pl.BlockSpec((None,) + tuple(shape), lambda *_: (layer,) + (0,) * nd,
                        pipeline_mode=pl.Buffered(1))


def _resident_spec(shape):
    nd = len(shape)
    return pl.BlockSpec(tuple(shape), lambda *_: (0,) * nd, pipeline_mode=pl.Buffered(1))


def _cast_block_rows(rows, n_steps):
    r = BF16_ROWS
    while r * n_steps < rows or rows % r:
        r += BF16_ROWS
    return r


def _cast_specs(layer, rows, cols, n_steps, step_of):
    r = _cast_block_rows(rows, n_steps)
    last = rows // r - 1
    src = pl.BlockSpec((None, r, cols), lambda *idx: (layer, jnp.minimum(step_of(*idx), last), 0))
    dst = pl.BlockSpec((r, cols), lambda *idx: (jnp.minimum(step_of(*idx), last), 0))
    return src, dst


def _split_bf16(x):
    hi = x.astype(BF16)
    lo = (x - hi.astype(F32)).astype(BF16)
    return hi, lo


def _rope_table_kernel(pos_ref, c_ref, s1_ref, s2_ref):
    half = ROT_DIM // 2
    pos = pos_ref[...].astype(F32)
    row = lax.broadcasted_iota(jnp.int32, (ROT_DIM, 1), 0)
    inv = jnp.zeros((ROT_DIM, 1), F32)
    for j, f in enumerate(INV_FREQ):
        inv = jnp.where((row & (half - 1)) == j, f, inv)
    ang = inv * pos
    cs = jnp.where(row < half, jnp.cos(ang), jnp.sin(ang))

    r = lax.broadcasted_iota(jnp.int32, (ROT_DIM, 3 * LANES), 0)
    col = lax.broadcasted_iota(jnp.int32, (ROT_DIM, 3 * LANES), 1)
    table = col // LANES
    in_head = col & (HEAD_DIM - 1)
    freq = col & (half - 1)
    put_cos = (table == 0) & (in_head < ROT_DIM) & (r == freq)
    put_nsin = (table == 1) & (in_head < half) & (r == freq + half)
    put_sin = (table == 2) & (in_head >= half) & (in_head < ROT_DIM) & (r == freq + half)
    place = jnp.where(put_cos | put_sin, 1.0, jnp.where(put_nsin, -1.0, 0.0)).astype(BF16)

    hi, lo = _split_bf16(cs)
    dn = (((0,), (0,)), ((), ()))
    t = (lax.dot_general(hi, place, dn, preferred_element_type=F32)
         + lax.dot_general(lo, place, dn, preferred_element_type=F32))
    lane = lax.broadcasted_iota(jnp.int32, (1, LANES), 1)
    c_ref[...] = jnp.where((lane & (HEAD_DIM - 1)) < ROT_DIM, t[:, 0:LANES], 1.0)
    s1_ref[...] = t[:, LANES:2 * LANES]
    s2_ref[...] = t[:, 2 * LANES:3 * LANES]


def _rope_tables(pos_row):
    n = pos_row.shape[1]
    tm = TM_ROPE
    tab = jax.ShapeDtypeStruct((n, LANES), F32)
    spec = pl.BlockSpec((tm, LANES), lambda i: (i, 0))
    return pl.pallas_call(
        _rope_table_kernel,
        out_shape=(tab, tab, tab),
        grid=(n // tm,),
        in_specs=[pl.BlockSpec((1, tm), lambda i: (0, i))],
        out_specs=(spec, spec, spec),
        name="rope_tables",
    )(pos_row)


def _rms(x, gain):
    ms = jnp.mean(x * x, axis=-1, keepdims=True)
    return x * lax.rsqrt(ms + EPS) * gain


def _layer_norm(x, gain, bias):
    mu = jnp.mean(x, axis=-1, keepdims=True)
    xc = x - mu
    var = jnp.mean(xc * xc, axis=-1, keepdims=True)
    return xc * lax.rsqrt(var + EPS) * gain + bias


def _inproj_kernel(x_ref, g_ref, w_ref, qg_ref, kg_ref, lng_ref, lnb_ref,
                   c_ref, s1_ref, s2_ref, wout_ref,
                   q_out, kt_out, v_out, u_out, vln_out, glu_out, wout_bf_out, w_bf, z_even, z_odd, *, last_is_even):
    i = pl.program_id(0)
    n_tiles = pl.num_programs(0) - 1

    wout_bf_out[...] = wout_ref[...].astype(BF16)

    def project(z_dst):
        h = _rms(x_ref[...], g_ref[...]).astype(BF16)
        z_dst[...] = jnp.dot(h, w_bf[...], preferred_element_type=F32)

    def finish(z_src):
        cos_t, sin_a, sin_b = c_ref[...], s1_ref[...], s2_ref[...]
        lane = lax.broadcasted_iota(jnp.int32, (1, LANES), 1)
        head0 = lane < HEAD_DIM

        def head_norm_rope(z, gain):
            sq = z * z
            first = jnp.sum(jnp.where(head0, sq, 0.0), axis=-1, keepdims=True)
            both = jnp.sum(sq, axis=-1, keepdims=True)
            ss = jnp.where(head0, first, both - first)
            zn = z * lax.rsqrt(ss * (1.0 / HEAD_DIM) + EPS) * gain
            return (zn * cos_t
                    + pltpu.roll(zn, LANES - ROT_DIM // 2, 1) * sin_a
                    + pltpu.roll(zn, ROT_DIM // 2, 1) * sin_b)

        q_gain = qg_ref[...] * (LOG2E / math.sqrt(HEAD_DIM))
        for b in range(ATTN_WIDTH // LANES):
            q_out[b] = head_norm_rope(z_src[:, Q_OFF + b * LANES:Q_OFF + (b + 1) * LANES], q_gain).astype(BF16)

        k = head_norm_rope(z_src[:, K_OFF:V_OFF], kg_ref[...])
        v = z_src[:, V_OFF:GM_OFF]
        kt_out[...] = k.T.astype(BF16)
        v_out[:, 0:LANES] = v.astype(BF16)
        v_out[:, LANES:2 * LANES] = pltpu.roll(v, HEAD_DIM, 1).astype(BF16)

        zuv = z_src[:, GM_OFF:CONV_OFF]
        uv = 0.5 * zuv * (1.0 + lax.erf(zuv * SQRT_HALF))
        u_out[...] = uv[:, 0:GM_WIDTH]
        vln_out[...] = _layer_norm(uv[:, GM_WIDTH:], lng_ref[...], lnb_ref[...]).astype(BF16)

        glu_out[...] = z_src[:, CONV_OFF:CONV_OFF + CONV_CH] * jax.nn.sigmoid(z_src[:, CONV_OFF + CONV_CH:])

    inner = jnp.logical_and(i > 0, i < n_tiles)

    @pl.when(i == 0)
    def _():
        w_bf[...] = w_ref[...].astype(BF16)
        project(z_even)

    @pl.when(jnp.logical_and(inner, i % 2 == 1))
    def _():
        finish(z_even)
        project(z_odd)

    @pl.when(jnp.logical_and(inner, i % 2 == 0))
    def _():
        finish(z_odd)
        project(z_even)

    @pl.when(i == n_tiles)
    def _():
        finish(z_even if last_is_even else z_odd)


def _inproj(layer, x, g, w, qg, kg, lng, lnb, tabs, wout):
    n = x.shape[0]
    tm = TM_PROJ
    n_tiles = n // tm
    cur = lambda width: pl.BlockSpec((tm, width), lambda i: (jnp.minimum(i, n_tiles - 1), 0))
    lag = lambda width: pl.BlockSpec((tm, width), lambda i: (jnp.maximum(i - 1, 0), 0))
    out_shape = (
        jax.ShapeDtypeStruct((ATTN_WIDTH // LANES, n, LANES), BF16),
        jax.ShapeDtypeStruct((KV_WIDTH, n), BF16),
        jax.ShapeDtypeStruct((n, 2 * LANES), BF16),
        jax.ShapeDtypeStruct((n, GM_WIDTH), F32),
        jax.ShapeDtypeStruct((n, GM_WIDTH), BF16),
        jax.ShapeDtypeStruct((n, CONV_CH), F32),
        jax.ShapeDtypeStruct((D_MODEL, D_MODEL), BF16),
    )
    wout_src, wout_dst = _cast_specs(layer, D_MODEL, D_MODEL, n_tiles + 1, lambda i: i)
    return pl.pallas_call(
        functools.partial(_inproj_kernel, last_is_even=(n_tiles - 1) % 2 == 0),
        out_shape=out_shape,
        grid=(n_tiles + 1,),
        in_specs=[cur(D_MODEL), _layer_spec(layer, (1, D_MODEL)), _layer_spec(layer, (D_MODEL, IN_COLS)),
                  _layer_spec(layer, (1, LANES)), _layer_spec(layer, (1, LANES)),
                  _layer_spec(layer, (1, GM_WIDTH)), _layer_spec(layer, (1, GM_WIDTH)),
                  lag(LANES), lag(LANES), lag(LANES), wout_src],
        out_specs=(pl.BlockSpec((ATTN_WIDTH // LANES, tm, LANES), lambda i: (0, jnp.maximum(i - 1, 0), 0)),
                   pl.BlockSpec((KV_WIDTH, tm), lambda i: (0, jnp.maximum(i - 1, 0))),
                   lag(2 * LANES), lag(GM_WIDTH), lag(GM_WIDTH), lag(CONV_CH), wout_dst),
        scratch_shapes=[pltpu.VMEM((D_MODEL, IN_COLS), BF16),
                        pltpu.VMEM((tm, IN_COLS), F32), pltpu.VMEM((tm, IN_COLS), F32)],
        compiler_params=pltpu.CompilerParams(dimension_semantics=("arbitrary",),
                                             vmem_limit_bytes=VMEM_LIMIT),
        name="in_proj",
    )(x, g, w, qg, kg, lng, lnb, *tabs, wout)


def _mixer_kernel(layer, sink_ref, x_ref, q_ref, ktp_ref, ktc_ref, ktn_ref, vp_ref, vc_ref, vn_ref, u_ref, vln_ref,
                  glup_ref, gluc_ref, glun_ref,
                  ws_ref, gmb_ref, cw_ref, cb_ref, clg_ref, clb_ref, og_ref, wout_ref,
                  wgu_ref, wd_ref, wpg_ref, wpp_ref,
                  o_ref, wgu_bf_out, wd_bf_out, wpg_bf_out, wpp_bf_out,
                  kta_s, ktb_s, va_s, vb_s, bias_s, s_s, p_s, scale_s, glu_s, y_s, attn_s, gm_s, merged_s):
    tq = x_ref.shape[0]
    nblk = tq // BLOCK
    j = pl.program_id(1)
    first = j == 0
    last = j == pl.num_programs(1) - 1

    wgu_bf_out[...] = wgu_ref[...].astype(BF16)
    wd_bf_out[...] = wd_ref[...].astype(BF16)
    wpg_bf_out[...] = wpg_ref[...].astype(BF16)
    wpp_bf_out[...] = wpp_ref[...].astype(BF16)

    lane = lax.broadcasted_iota(jnp.int32, (1, LANES), 1)
    head0 = lane < HEAD_DIM
    zero = jnp.zeros((), BF16)
    zero_rows = jnp.zeros((HEAD_DIM, BLOCK), BF16)
    for b in range(nblk + 2):
        if b == 0:
            kt = ktp_ref[...]
        elif b == nblk + 1:
            kt = ktn_ref[...]
        else:
            kt = ktc_ref[:, (b - 1) * BLOCK:b * BLOCK]
        for h in range(ATTN_KV_HEADS):
            kh = kt[h * HEAD_DIM:(h + 1) * HEAD_DIM, :]
            kta_s[h, b] = jnp.concatenate([kh, zero_rows], axis=0)
            ktb_s[h, b] = jnp.concatenate([zero_rows, kh], axis=0)
    for src, r0, rows in ((vp_ref, 0, BLOCK), (vc_ref, BLOCK, tq), (vn_ref, BLOCK + tq, BLOCK)):
        dst = slice(r0, r0 + rows)
        v, v_sw = src[:, 0:LANES], src[:, LANES:2 * LANES]
        va_s[0, dst, :] = jnp.where(head0, v, zero)
        vb_s[0, dst, :] = jnp.where(head0, zero, v_sw)
        va_s[1, dst, :] = jnp.where(head0, v_sw, zero)
        vb_s[1, dst, :] = jnp.where(head0, zero, v)

    qi = lax.broadcasted_iota(jnp.int32, (BLOCK, BLOCK), 0)
    kj = lax.broadcasted_iota(jnp.int32, (BLOCK, BLOCK), 1)
    band_lo = jnp.where(kj >= qi, 0.0, NEG_INF)
    band_hi = jnp.where(kj <= qi, 0.0, NEG_INF)
    bias_s[0] = band_lo
    bias_s[1] = band_hi
    bias_s[2] = jnp.where(first, NEG_INF, band_lo)
    bias_s[3] = jnp.where(last, NEG_INF, band_hi)

    n_pairs = ATTN_GROUP // 2

    def scores(nb, h):
        r0 = pl.multiple_of(nb * BLOCK, BLOCK)
        q_st = jnp.concatenate([q_ref[n_pairs * h + pp, pl.ds(r0, BLOCK), :] for pp in range(n_pairs)], axis=0)
        kt_st = jnp.concatenate([kta_s[h, nb + t] for t in range(3)] + [ktb_s[h, nb + t] for t in range(3)],
                                axis=1)
        s_s[h] = jnp.dot(q_st, kt_st, preferred_element_type=F32)

    def softmax_pv(nb, h):
        r0 = pl.multiple_of(nb * BLOCK, BLOCK)
        lo_idx = jnp.where(nb == 0, 2, 0)
        hi_idx = jnp.where(nb == nblk - 1, 3, 1)
        v_st = jnp.concatenate([va_s[h, pl.ds(r0, 3 * BLOCK), :], vb_s[h, pl.ds(r0, 3 * BLOCK), :]], axis=0)
        for pp in range(n_pairs):
            for rc in range(0, BLOCK, SOFTMAX_ROWS):
                rr = slice(pp * BLOCK + rc, pp * BLOCK + rc + SOFTMAX_ROWS)
                br = slice(rc, rc + SOFTMAX_ROWS)
                inv = []
                for half in range(2):
                    sink = sink_ref[layer * ATTN_Q_HEADS + ATTN_GROUP * h + 2 * pp + half] * LOG2E
                    c0 = half * 3 * BLOCK
                    s0 = s_s[h, rr, c0:c0 + BLOCK] + bias_s[lo_idx, br, :]
                    s1 = s_s[h, rr, c0 + BLOCK:c0 + 2 * BLOCK]
                    s2 = s_s[h, rr, c0 + 2 * BLOCK:c0 + 3 * BLOCK] + bias_s[hi_idx, br, :]
                    m = jnp.max(jnp.maximum(jnp.maximum(s0, s1), s2), axis=-1, keepdims=True)
                    m = jnp.maximum(m, sink)
                    e0, e1, e2 = jnp.exp2(s0 - m), jnp.exp2(s1 - m), jnp.exp2(s2 - m)
                    denom = jnp.sum(e0 + e1 + e2, axis=-1, keepdims=True) + jnp.exp2(sink - m)
                    p_s[h, rr, c0:c0 + BLOCK] = e0.astype(BF16)
                    p_s[h, rr, c0 + BLOCK:c0 + 2 * BLOCK] = e1.astype(BF16)
                    p_s[h, rr, c0 + 2 * BLOCK:c0 + 3 * BLOCK] = e2.astype(BF16)
                    inv.append(1.0 / denom)
                scale_s[h, rr, :] = jnp.where(head0, inv[0], inv[1])
        o = jnp.dot(p_s[h], v_st, preferred_element_type=F32) * scale_s[h]
        for pp in range(n_pairs):
            attn_s[n_pairs * h + pp, pl.ds(r0, BLOCK), :] = o[pp * BLOCK:(pp + 1) * BLOCK, :]

    scores(0, 0)

    def attn_step(nb, carry):
        scores(nb, 1)
        softmax_pv(nb, 0)
        scores(jnp.minimum(nb + 1, nblk - 1), 0)
        softmax_pv(nb, 1)
        return carry

    lax.fori_loop(0, nblk, attn_step, 0)

    for nb in range(nblk):
        rows = slice(nb * BLOCK, (nb + 1) * BLOCK)
        for hp in range(GM_HEADS // 2):
            cols = slice(hp * LANES, (hp + 1) * LANES)
            vblk = vln_ref[rows, cols]
            lhs = jnp.concatenate([ws_ref[2 * hp], ws_ref[2 * hp + 1]], axis=1)
            rhs = jnp.concatenate([jnp.where(head0, vblk, zero), jnp.where(head0, zero, vblk)], axis=0)
            sgate = jnp.dot(lhs, rhs, preferred_element_type=F32) + gmb_ref[:, cols]
            gm_s[rows, cols] = u_ref[rows, cols] * sgate

    n_ch = CONV_CH // LANES
    for ch in range(n_ch):
        cl = slice(ch * LANES, (ch + 1) * LANES)
        glu_s[ch, 0:CONV_HALO, :] = jnp.where(first, 0.0, glup_ref[:, cl])
        glu_s[ch, CONV_HALO:CONV_HALO + tq, :] = gluc_ref[:, cl]
        glu_s[ch, CONV_HALO + tq:, :] = jnp.where(last, 0.0, glun_ref[:, cl])
    group = SUBLANES * CONV_ROW_STRIDE

    def conv_step(i, carry):
        base = pl.multiple_of(i * (CONV_GROUPS * group), CONV_GROUPS * group)
        for ch in range(n_ch):
            cl = slice(ch * LANES, (ch + 1) * LANES)
            offs = [g * group + ph for g in range(CONV_GROUPS) for ph in range(CONV_ROW_STRIDE)]
            accs = [jnp.broadcast_to(cb_ref[:, cl], (SUBLANES, LANES)) for _ in offs]
            for t in range(CONV_WIDTH):
                w = cw_ref[t:t + 1, cl]
                for a, off in enumerate(offs):
                    start = base + (off + CONV_HALO - CONV_PAD + t)
                    accs[a] = accs[a] + w * glu_s[ch, pl.ds(start, SUBLANES, stride=CONV_ROW_STRIDE), :]
            for a, off in enumerate(offs):
                y_s[ch, pl.ds(base + off, SUBLANES, stride=CONV_ROW_STRIDE), :] = accs[a]
        return carry

    lax.fori_loop(0, tq // (CONV_GROUPS * group), conv_step, 0)

    rchunk = 128
    for r0 in range(0, tq, rchunk):
        rs = slice(r0, r0 + rchunk)
        y = jnp.concatenate([y_s[ch, rs, :] for ch in range(n_ch)], axis=1)
        y = _layer_norm(y, clg_ref[...], clb_ref[...])
        y = y * jax.nn.sigmoid(y)
        merged_s[rs, ATTN_WIDTH + GM_WIDTH:] = _rms(y, og_ref[:, ATTN_WIDTH + GM_WIDTH:]).astype(BF16)
        attn = jnp.concatenate([attn_s[pb, rs, :] for pb in range(ATTN_WIDTH // LANES)], axis=1)
        merged_s[rs, 0:ATTN_WIDTH] = _rms(attn, og_ref[:, 0:ATTN_WIDTH]).astype(BF16)
        merged_s[rs, ATTN_WIDTH:ATTN_WIDTH + GM_WIDTH] = _rms(
            gm_s[rs, :], og_ref[:, ATTN_WIDTH:ATTN_WIDTH + GM_WIDTH]).astype(BF16)
    o_ref[...] = x_ref[...] + jnp.dot(merged_s[...], wout_ref[...], preferred_element_type=F32)


def _mixer(layer, x, q, kt, v, u, vln, glu, sink, ws, gm_bias, cw, cb, clg, clb, og, wout,
           wgu, wd, wpg, wpp, batch, seq):
    n = x.shape[0]
    tq = TQ_MIX
    tiles = seq // tq
    blk_per_tile = tq // BLOCK
    blk_per_seq = seq // BLOCK
    halo_per_tile = tq // CONV_HALO
    halo_per_seq = seq // CONV_HALO

    def row(width):
        return pl.BlockSpec((tq, width), lambda b, j, *_: (b * tiles + j, 0))

    def prev(rows, width, per_tile, per_seq):
        return pl.BlockSpec(
            (rows, width), lambda b, j, *_: (b * per_seq + jnp.maximum(j * per_tile - 1, 0), 0))

    def nxt(rows, width, per_tile, per_seq):
        return pl.BlockSpec(
            (rows, width), lambda b, j, *_: (b * per_seq + jnp.minimum((j + 1) * per_tile, per_seq - 1), 0))

    const = lambda shape: _layer_spec(layer, shape)
    kv_rows = tq + 2 * BLOCK
    step_of = lambda b, j, *_: b * tiles + j
    casts = [_cast_specs(layer, rows, cols, batch * tiles, step_of)
             for rows, cols in ((D_MODEL, 2 * D_FF), (D_FF, D_MODEL), (D_MODEL, D_MODEL), (PLE_DIM, D_MODEL))]
    grid_spec = pltpu.PrefetchScalarGridSpec(
        num_scalar_prefetch=1,
        grid=(batch, tiles),
        in_specs=[
            row(D_MODEL), pl.BlockSpec((ATTN_WIDTH // LANES, tq, LANES), lambda b, j, *_: (0, b * tiles + j, 0)),
            pl.BlockSpec((KV_WIDTH, BLOCK),
                         lambda b, j, *_: (0, b * blk_per_seq + jnp.maximum(j * blk_per_tile - 1, 0))),
            pl.BlockSpec((KV_WIDTH, tq), lambda b, j, *_: (0, b * tiles + j)),
            pl.BlockSpec((KV_WIDTH, BLOCK),
                         lambda b, j, *_: (0, b * blk_per_seq + jnp.minimum((j + 1) * blk_per_tile, blk_per_seq - 1))),
            prev(BLOCK, 2 * LANES, blk_per_tile, blk_per_seq), row(2 * LANES),
            nxt(BLOCK, 2 * LANES, blk_per_tile, blk_per_seq),
            row(GM_WIDTH), row(GM_WIDTH),
            prev(CONV_HALO, CONV_CH, halo_per_tile, halo_per_seq), row(CONV_CH),
            nxt(CONV_HALO, CONV_CH, halo_per_tile, halo_per_seq),
            const((GM_HEADS, CHUNK, CHUNK)), const((CHUNK, GM_WIDTH)),
            const((CONV_WIDTH, CONV_CH)), const((1, CONV_CH)), const((1, CONV_CH)), const((1, CONV_CH)),
            const((1, D_MODEL)), _resident_spec((D_MODEL, D_MODEL)),
            *[src for src, _ in casts],
        ],
        out_specs=(row(D_MODEL), *[dst for _, dst in casts]),
        scratch_shapes=[
            pltpu.VMEM((ATTN_KV_HEADS, blk_per_tile + 2, KV_WIDTH, BLOCK), BF16),
            pltpu.VMEM((ATTN_KV_HEADS, blk_per_tile + 2, KV_WIDTH, BLOCK), BF16),
            pltpu.VMEM((ATTN_KV_HEADS, kv_rows, LANES), BF16), pltpu.VMEM((ATTN_KV_HEADS, kv_rows, LANES), BF16),
            pltpu.VMEM((4, BLOCK, BLOCK), F32),
            pltpu.VMEM((2, ATTN_GROUP // 2 * BLOCK, 2 * 3 * BLOCK), F32),
            pltpu.VMEM((2, ATTN_GROUP // 2 * BLOCK, 2 * 3 * BLOCK), BF16),
            pltpu.VMEM((2, ATTN_GROUP // 2 * BLOCK, LANES), F32),
            pltpu.VMEM((CONV_CH // LANES, tq + 2 * CONV_HALO, LANES), F32),
            pltpu.VMEM((CONV_CH // LANES, tq, LANES), F32),
            pltpu.VMEM((ATTN_WIDTH // LANES, tq, LANES), F32), pltpu.VMEM((tq, GM_WIDTH), F32),
            pltpu.VMEM((tq, D_MODEL), BF16),
        ],
    )
    return pl.pallas_call(
        functools.partial(_mixer_kernel, layer),
        out_shape=(jax.ShapeDtypeStruct((n, D_MODEL), F32),
                   jax.ShapeDtypeStruct((D_MODEL, 2 * D_FF), BF16), jax.ShapeDtypeStruct((D_FF, D_MODEL), BF16),
                   jax.ShapeDtypeStruct((D_MODEL, D_MODEL), BF16), jax.ShapeDtypeStruct((PLE_DIM, D_MODEL), BF16)),
        grid_spec=grid_spec,
        compiler_params=pltpu.CompilerParams(dimension_semantics=("arbitrary", "arbitrary"),
                                             vmem_limit_bytes=VMEM_LIMIT),
        name="mixer",
    )(sink, x, q, kt, kt, kt, v, v, v, u, vln, glu, glu, glu, ws, gm_bias, cw, cb, clg, clb, og, wout,
      wgu, wd, wpg, wpp)


def _ffn_kernel(x_ref, p_ref, gf_ref, wgu_ref, wd_ref, gp_ref, wpg_ref, wpp_ref, o_ref, act_s):
    x = x_ref[...]
    hn = _rms(x, gf_ref[...]).astype(BF16)
    for c0 in range(0, D_FF, FF_CHUNK):
        gate = jnp.dot(hn, wgu_ref[:, c0:c0 + FF_CHUNK], preferred_element_type=F32)
        up = jnp.dot(hn, wgu_ref[:, D_FF + c0:D_FF + c0 + FF_CHUNK], preferred_element_type=F32)
        act_s[:, c0:c0 + FF_CHUNK] = (gate * jax.nn.sigmoid(gate) * up).astype(BF16)
    x = x + jnp.dot(act_s[...], wd_ref[...], preferred_element_type=F32)

    hp = _rms(x, gp_ref[...]).astype(BF16)
    gate = jax.nn.sigmoid(jnp.dot(hp, wpg_ref[...], preferred_element_type=F32))
    proj = jnp.dot(p_ref[...].astype(BF16), wpp_ref[...], preferred_element_type=F32)
    o_ref[...] = x + proj * gate


def _ffn(layer, x, p, gf, wgu, wd, gp, wpg, wpp):
    n = x.shape[0]
    tm = TM_FFN
    row = lambda width: pl.BlockSpec((tm, width), lambda i: (i, 0))
    return pl.pallas_call(
        _ffn_kernel,
        out_shape=jax.ShapeDtypeStruct((n, D_MODEL), F32),
        grid=(n // tm,),
        in_specs=[row(D_MODEL), pl.BlockSpec((None, tm, PLE_DIM), lambda i: (layer, i, 0)),
                  _layer_spec(layer, (1, D_MODEL)),
                  _resident_spec((D_MODEL, 2 * D_FF)), _resident_spec((D_FF, D_MODEL)),
                  _layer_spec(layer, (1, D_MODEL)), _resident_spec((D_MODEL, D_MODEL)),
                  _resident_spec((PLE_DIM, D_MODEL))],
        out_specs=row(D_MODEL),
        scratch_shapes=[pltpu.VMEM((tm, D_FF), BF16)],
        compiler_params=pltpu.CompilerParams(dimension_semantics=("arbitrary",),
                                             vmem_limit_bytes=VMEM_LIMIT),
        name="ffn_ple",
    )(x, p, gf, wgu, wd, gp, wpg, wpp)


def kernel(x, p, positions, norm_mix_g, w_in, q_norm_g, k_norm_g, sink, gm_ln_g, gm_ln_b, gm_ws, gm_bs, conv_w, conv_b, conv_ln_g, conv_ln_b, out_norm_g, w_out, norm_ffn_g, w_gate_up, w_down, ple_norm_g, w_ple_gate, w_ple_proj):
    batch, seq, d = x.shape
    depth = w_in.shape[0]
    n = batch * seq
    assert d == D_MODEL and seq % TQ_MIX == 0
    assert n % TM_PROJ == 0 and n % TM_FFN == 0 and n % TM_ROPE == 0

    vec = lambda a: a[:, None, :]
    ws_b = gm_ws.astype(BF16)
    qg = vec(jnp.tile(q_norm_g, (1, LANES // HEAD_DIM)))
    kg = vec(jnp.tile(k_norm_g, (1, LANES // HEAD_DIM)))
    gm_bias = jnp.repeat(jnp.swapaxes(gm_bs, 1, 2), HEAD_DIM, axis=2)
    sink_flat = sink.reshape(depth * ATTN_Q_HEADS)
    p_flat = p.reshape(depth, n, PLE_DIM)

    tabs = _rope_tables(positions.reshape(1, n))
    xf = x.reshape(n, d)
    for i in range(depth):
        q, kt, v, u, vln, glu, w_out_b = _inproj(i, xf, vec(norm_mix_g), w_in, qg, kg, vec(gm_ln_g),
                                                 vec(gm_ln_b), tabs, w_out)
        xf, w_gu_b, w_down_b, w_pg_b, w_pp_b = _mixer(
            i, xf, q, kt, v, u, vln, glu, sink_flat, ws_b, gm_bias, conv_w, vec(conv_b), vec(conv_ln_g),
            vec(conv_ln_b), vec(out_norm_g), w_out_b, w_gate_up, w_down, w_ple_gate, w_ple_proj, batch, seq)
        xf = _ffn(i, xf, p_flat, vec(norm_ffn_g), w_gu_b, w_down_b, vec(ple_norm_g), w_pg_b, w_pp_b)
    return xf.reshape(batch, seq, d)
```

```python
import functools
import math

import numpy as np
import jax
import jax.numpy as jnp
from jax import lax
from jax.experimental import pallas as pl
from jax.experimental.pallas import tpu as pltpu

F32 = jnp.float32
BF16 = jnp.bfloat16

D_MODEL = 1024
HEAD_DIM = 64
ATTN_WIDTH = 512
ATTN_Q_HEADS = 8
ATTN_KV_HEADS = 2
ATTN_GROUP = 4
KV_WIDTH = 128
WINDOW = 128
BLOCK = 128
ROPE_THETA = 500000.0
ROT_DIM = 16
GM_WIDTH = 256
GM_HEADS = 4
CHUNK = 128
CONV_CH = 256
CONV_WIDTH = 31
CONV_PAD = 15
D_FF = 2816
PLE_DIM = 256
EPS = 1e-6
NEG_INF = -1e30

Q_OFF = 0
K_OFF = Q_OFF + ATTN_WIDTH
V_OFF = K_OFF + KV_WIDTH
GM_OFF = V_OFF + KV_WIDTH
CONV_OFF = GM_OFF + 2 * GM_WIDTH
IN_COLS = CONV_OFF + 2 * CONV_CH

LANES = 128
SUBLANES = 8
BF16_ROWS = 16
CONV_HALO = 16
CONV_ROW_STRIDE = 2
CONV_GROUPS = 4
VMEM_LIMIT = 56 * 1024 * 1024

TM_ROPE = 2048
TM_PROJ = 512
TQ_MIX = 1024
TM_FFN = 1024
FF_CHUNK = 256
SOFTMAX_ROWS = 32

INV_FREQ = [float(ROPE_THETA ** (-(2.0 * j) / ROT_DIM)) for j in range(ROT_DIM // 2)]
SQRT_HALF = float(np.sqrt(0.5))
LOG2E = float(np.log2(np.e))


def _layer_spec(layer, shape):
    nd = len(shape)
    return pl.BlockSpec((None,) + tuple(shape), lambda *_: (layer,) + (0,) * nd,
                        pipeline_mode=pl.Buffered(1))


def _resident_spec(shape):
    nd = len(shape)
    return pl.BlockSpec(tuple(shape), lambda *_: (0,) * nd, pipeline_mode=pl.Buffered(1))


def _cast_block_rows(rows, n_steps):
    r = BF16_ROWS
    while r * n_steps < rows or rows % r:
        r += BF16_ROWS
    return r


def _cast_specs(layer, rows, cols, n_steps, step_of):
    r = _cast_block_rows(rows, n_steps)
    last = rows // r - 1
    src = pl.BlockSpec((None, r, cols), lambda *idx: (layer, jnp.minimum(step_of(*idx), last), 0))
    dst = pl.BlockSpec((r, cols), lambda *idx: (jnp.minimum(step_of(*idx), last), 0))
    return src, dst


def _split_bf16(x):
    hi = x.astype(BF16)
    lo = (x - hi.astype(F32)).astype(BF16)
    return hi, lo


def _rope_table_kernel(pos_ref, c_ref, s_ref):
    half = ROT_DIM // 2
    pos = pos_ref[...].astype(F32)
    row = lax.broadcasted_iota(jnp.int32, (ROT_DIM, 1), 0)
    inv = jnp.zeros((ROT_DIM, 1), F32)
    for j, f in enumerate(INV_FREQ):
        inv = jnp.where((row & (half - 1)) == j, f, inv)
    ang = inv * pos
    cs = jnp.where(row < half, jnp.cos(ang), jnp.sin(ang))

    r = lax.broadcasted_iota(jnp.int32, (ROT_DIM, 2 * LANES), 0)
    col = lax.broadcasted_iota(jnp.int32, (ROT_DIM, 2 * LANES), 1)
    table = col // LANES
    in_head = col & (HEAD_DIM - 1)
    freq = col & (half - 1)
    put_cos = (table == 0) & (in_head < ROT_DIM) & (r == freq)
    put_nsin = (table == 1) & (in_head < half) & (r == freq + half)
    put_sin = (table == 1) & (in_head >= half) & (in_head < ROT_DIM) & (r == freq + half)
    place = jnp.where(put_cos | put_sin, 1.0, jnp.where(put_nsin, -1.0, 0.0)).astype(BF16)

    hi, lo = _split_bf16(cs)
    dn = (((0,), (0,)), ((), ()))
    t = (lax.dot_general(hi, place, dn, preferred_element_type=F32)
         + lax.dot_general(lo, place, dn, preferred_element_type=F32))
    lane = lax.broadcasted_iota(jnp.int32, (1, LANES), 1)
    c_ref[...] = jnp.where((lane & (HEAD_DIM - 1)) < ROT_DIM, t[:, 0:LANES], 1.0)
    s_ref[...] = t[:, LANES:2 * LANES]


def _rope_tables(pos_row):
    n = pos_row.shape[1]
    tm = TM_ROPE
    tab = jax.ShapeDtypeStruct((n, LANES), F32)
    spec = pl.BlockSpec((tm, LANES), lambda i: (i, 0))
    return pl.pallas_call(
        _rope_table_kernel,
        out_shape=(tab, tab),
        grid=(n // tm,),
        in_specs=[pl.BlockSpec((1, tm), lambda i: (0, i))],
        out_specs=(spec, spec),
        name="rope_tables",
    )(pos_row)


def _rms(x, gain):
    ms = jnp.mean(x * x, axis=-1, keepdims=True)
    return x * lax.rsqrt(ms + EPS) * gain


def _layer_norm(x, gain, bias):
    mu = jnp.mean(x, axis=-1, keepdims=True)
    xc = x - mu
    var = jnp.mean(xc * xc, axis=-1, keepdims=True)
    return xc * lax.rsqrt(var + EPS) * gain + bias


def _inproj_kernel(x_ref, g_ref, w_ref, qg_ref, kg_ref, lng_ref, lnb_ref,
                   c_ref, s_ref, wout_ref,
                   q_out, kt_out, v_out, u_out, vln_out, glu_out, wout_bf_out, w_bf, z_even, z_odd, *, last_is_even):
    i = pl.program_id(0)
    n_tiles = pl.num_programs(0) - 1

    wout_bf_out[...] = wout_ref[...].astype(BF16)

    def project(z_dst):
        h = _rms(x_ref[...], g_ref[...]).astype(BF16)
        z_dst[...] = jnp.dot(h, w_bf[...], preferred_element_type=F32)

    def finish(z_src):
        cos_t, sin_t = c_ref[...], s_ref[...]
        lane = lax.broadcasted_iota(jnp.int32, (1, LANES), 1)
        head0 = lane < HEAD_DIM

        def head_norm_rope(z, gain):
            sq = z * z
            first = jnp.sum(jnp.where(head0, sq, 0.0), axis=-1, keepdims=True)
            both = jnp.sum(sq, axis=-1, keepdims=True)
            ss = jnp.where(head0, first, both - first)
            zn = z * lax.rsqrt(ss * (1.0 / HEAD_DIM) + EPS) * gain
            partner = jnp.where((lane & (ROT_DIM // 2)) == 0,
                                pltpu.roll(zn, LANES - ROT_DIM // 2, 1), pltpu.roll(zn, ROT_DIM // 2, 1))
            return zn * cos_t + partner * sin_t

        q_gain = qg_ref[...] * (LOG2E / math.sqrt(HEAD_DIM))
        for b in range(ATTN_WIDTH // LANES):
            q_out[b] = head_norm_rope(z_src[:, Q_OFF + b * LANES:Q_OFF + (b + 1) * LANES], q_gain).astype(BF16)

        k = head_norm_rope(z_src[:, K_OFF:V_OFF], kg_ref[...])
        v = z_src[:, V_OFF:GM_OFF]
        kt_out[...] = k.T.astype(BF16)
        v_out[:, 0:LANES] = v.astype(BF16)
        v_out[:, LANES:2 * LANES] = pltpu.roll(v, HEAD_DIM, 1).astype(BF16)

        zuv = z_src[:, GM_OFF:CONV_OFF]
        uv = 0.5 * zuv * (1.0 + lax.erf(zuv * SQRT_HALF))
        u_out[...] = uv[:, 0:GM_WIDTH]
        vln_out[...] = _layer_norm(uv[:, GM_WIDTH:], lng_ref[...], lnb_ref[...]).astype(BF16)

        glu_out[...] = z_src[:, CONV_OFF:CONV_OFF + CONV_CH] * jax.nn.sigmoid(z_src[:, CONV_OFF + CONV_CH:])

    inner = jnp.logical_and(i > 0, i < n_tiles)

    @pl.when(i == 0)
    def _():
        w_bf[...] = w_ref[...].astype(BF16)
        project(z_even)

    @pl.when(jnp.logical_and(inner, i % 2 == 1))
    def _():
        finish(z_even)
        project(z_odd)

    @pl.when(jnp.logical_and(inner, i % 2 == 0))
    def _():
        finish(z_odd)
        project(z_even)

    @pl.when(i == n_tiles)
    def _():
        finish(z_even if last_is_even else z_odd)


def _inproj(layer, x, g, w, qg, kg, lng, lnb, tabs, wout):
    n = x.shape[0]
    tm = TM_PROJ
    n_tiles = n // tm
    cur = lambda width: pl.BlockSpec((tm, width), lambda i: (jnp.minimum(i, n_tiles - 1), 0))
    lag = lambda width: pl.BlockSpec((tm, width), lambda i: (jnp.maximum(i - 1, 0), 0))
    out_shape = (
        jax.ShapeDtypeStruct((ATTN_WIDTH // LANES, n, LANES), BF16),
        jax.ShapeDtypeStruct((KV_WIDTH, n), BF16),
        jax.ShapeDtypeStruct((n, 2 * LANES), BF16),
        jax.ShapeDtypeStruct((n, GM_WIDTH), F32),
        jax.ShapeDtypeStruct((n, GM_WIDTH), BF16),
        jax.ShapeDtypeStruct((n, CONV_CH), F32),
        jax.ShapeDtypeStruct((D_MODEL, D_MODEL), BF16),
    )
    wout_src, wout_dst = _cast_specs(layer, D_MODEL, D_MODEL, n_tiles + 1, lambda i: i)
    return pl.pallas_call(
        functools.partial(_inproj_kernel, last_is_even=(n_tiles - 1) % 2 == 0),
        out_shape=out_shape,
        grid=(n_tiles + 1,),
        in_specs=[cur(D_MODEL), _layer_spec(layer, (1, D_MODEL)), _layer_spec(layer, (D_MODEL, IN_COLS)),
                  _layer_spec(layer, (1, LANES)), _layer_spec(layer, (1, LANES)),
                  _layer_spec(layer, (1, GM_WIDTH)), _layer_spec(layer, (1, GM_WIDTH)),
                  lag(LANES), lag(LANES), wout_src],
        out_specs=(pl.BlockSpec((ATTN_WIDTH // LANES, tm, LANES), lambda i: (0, jnp.maximum(i - 1, 0), 0)),
                   pl.BlockSpec((KV_WIDTH, tm), lambda i: (0, jnp.maximum(i - 1, 0))),
                   lag(2 * LANES), lag(GM_WIDTH), lag(GM_WIDTH), lag(CONV_CH), wout_dst),
        scratch_shapes=[pltpu.VMEM((D_MODEL, IN_COLS), BF16),
                        pltpu.VMEM((tm, IN_COLS), F32), pltpu.VMEM((tm, IN_COLS), F32)],
        compiler_params=pltpu.CompilerParams(dimension_semantics=("arbitrary",),
                                             vmem_limit_bytes=VMEM_LIMIT),
        name="in_proj",
    )(x, g, w, qg, kg, lng, lnb, *tabs, wout)


def _mixer_kernel(layer, sink_ref, x_ref, q_ref, ktp_ref, ktc_ref, ktn_ref, vp_ref, vc_ref, vn_ref, u_ref, vln_ref,
                  glup_ref, gluc_ref, glun_ref,
                  ws_ref, gmb_ref, cw_ref, cb_ref, clg_ref, clb_ref, og_ref, wout_ref,
                  wgu_ref, wd_ref, wpg_ref, wpp_ref,
                  o_ref, wgu_bf_out, wd_bf_out, wpg_bf_out, wpp_bf_out,
                  kta_s, ktb_s, va_s, vb_s, bias_s, s_s, p_s, scale_s, glu_s, y_s, attn_s, gm_s, merged_s):
    tq = x_ref.shape[0]
    nblk = tq // BLOCK
    j = pl.program_id(1)
    first = j == 0
    last = j == pl.num_programs(1) - 1

    wgu_bf_out[...] = wgu_ref[...].astype(BF16)
    wd_bf_out[...] = wd_ref[...].astype(BF16)
    wpg_bf_out[...] = wpg_ref[...].astype(BF16)
    wpp_bf_out[...] = wpp_ref[...].astype(BF16)

    lane = lax.broadcasted_iota(jnp.int32, (1, LANES), 1)
    head0 = lane < HEAD_DIM
    zero = jnp.zeros((), BF16)
    zero_rows = jnp.zeros((HEAD_DIM, BLOCK), BF16)
    for b in range(nblk + 2):
        if b == 0:
            kt = ktp_ref[...]
        elif b == nblk + 1:
            kt = ktn_ref[...]
        else:
            kt = ktc_ref[:, (b - 1) * BLOCK:b * BLOCK]
        for h in range(ATTN_KV_HEADS):
            kh = kt[h * HEAD_DIM:(h + 1) * HEAD_DIM, :]
            kta_s[h, b] = jnp.concatenate([kh, zero_rows], axis=0)
            ktb_s[h, b] = jnp.concatenate([zero_rows, kh], axis=0)
    for src, r0, rows in ((vp_ref, 0, BLOCK), (vc_ref, BLOCK, tq), (vn_ref, BLOCK + tq, BLOCK)):
        dst = slice(r0, r0 + rows)
        v, v_sw = src[:, 0:LANES], src[:, LANES:2 * LANES]
        va_s[0, dst, :] = jnp.where(head0, v, zero)
        vb_s[0, dst, :] = jnp.where(head0, zero, v_sw)
        va_s[1, dst, :] = jnp.where(head0, v_sw, zero)
        vb_s[1, dst, :] = jnp.where(head0, zero, v)

    qi = lax.broadcasted_iota(jnp.int32, (BLOCK, BLOCK), 0)
    kj = lax.broadcasted_iota(jnp.int32, (BLOCK, BLOCK), 1)
    band_lo = jnp.where(kj >= qi, 0.0, NEG_INF)
    band_hi = jnp.where(kj <= qi, 0.0, NEG_INF)
    bias_s[0] = band_lo
    bias_s[1] = band_hi
    bias_s[2] = jnp.where(first, NEG_INF, band_lo)
    bias_s[3] = jnp.where(last, NEG_INF, band_hi)

    n_pairs = ATTN_GROUP // 2

    def scores(nb, h):
        r0 = pl.multiple_of(nb * BLOCK, BLOCK)
        q_st = jnp.concatenate([q_ref[n_pairs * h + pp, pl.ds(r0, BLOCK), :] for pp in range(n_pairs)], axis=0)
        kt_st = jnp.concatenate([kta_s[h, nb + t] for t in range(3)] + [ktb_s[h, nb + t] for t in range(3)],
                                axis=1)
        s_s[h] = jnp.dot(q_st, kt_st, preferred_element_type=F32)

    def softmax_pv(nb, h):
        r0 = pl.multiple_of(nb * BLOCK, BLOCK)
        lo_idx = jnp.where(nb == 0, 2, 0)
        hi_idx = jnp.where(nb == nblk - 1, 3, 1)
        v_st = jnp.concatenate([va_s[h, pl.ds(r0, 3 * BLOCK), :], vb_s[h, pl.ds(r0, 3 * BLOCK), :]], axis=0)
        for pp in range(n_pairs):
            for rc in range(0, BLOCK, SOFTMAX_ROWS):
                rr = slice(pp * BLOCK + rc, pp * BLOCK + rc + SOFTMAX_ROWS)
                br = slice(rc, rc + SOFTMAX_ROWS)
                inv = []
                for half in range(2):
                    sink = sink_ref[layer * ATTN_Q_HEADS + ATTN_GROUP * h + 2 * pp + half] * LOG2E
                    c0 = half * 3 * BLOCK
                    s0 = s_s[h, rr, c0:c0 + BLOCK] + bias_s[lo_idx, br, :]
                    s1 = s_s[h, rr, c0 + BLOCK:c0 + 2 * BLOCK]
                    s2 = s_s[h, rr, c0 + 2 * BLOCK:c0 + 3 * BLOCK] + bias_s[hi_idx, br, :]
                    m = jnp.max(jnp.maximum(jnp.maximum(s0, s1), s2), axis=-1, keepdims=True)
                    m = jnp.maximum(m, sink)
                    e0, e1, e2 = jnp.exp2(s0 - m), jnp.exp2(s1 - m), jnp.exp2(s2 - m)
                    denom = jnp.sum(e0 + e1 + e2, axis=-1, keepdims=True) + jnp.exp2(sink - m)
                    p_s[h, rr, c0:c0 + BLOCK] = e0.astype(BF16)
                    p_s[h, rr, c0 + BLOCK:c0 + 2 * BLOCK] = e1.astype(BF16)
                    p_s[h, rr, c0 + 2 * BLOCK:c0 + 3 * BLOCK] = e2.astype(BF16)
                    inv.append(1.0 / denom)
                scale_s[h, rr, :] = jnp.where(head0, inv[0], inv[1])
        o = jnp.dot(p_s[h], v_st, preferred_element_type=F32) * scale_s[h]
        for pp in range(n_pairs):
            attn_s[n_pairs * h + pp, pl.ds(r0, BLOCK), :] = o[pp * BLOCK:(pp + 1) * BLOCK, :]

    scores(0, 0)

    def attn_step(nb, carry):
        scores(nb, 1)
        softmax_pv(nb, 0)
        scores(jnp.minimum(nb + 1, nblk - 1), 0)
        softmax_pv(nb, 1)
        return carry

    lax.fori_loop(0, nblk, attn_step, 0)

    for nb in range(nblk):
        rows = slice(nb * BLOCK, (nb + 1) * BLOCK)
        for hp in range(GM_HEADS // 2):
            cols = slice(hp * LANES, (hp + 1) * LANES)
            vblk = vln_ref[rows, cols]
            lhs = jnp.concatenate([ws_ref[2 * hp], ws_ref[2 * hp + 1]], axis=1)
            rhs = jnp.concatenate([jnp.where(head0, vblk, zero), jnp.where(head0, zero, vblk)], axis=0)
            sgate = jnp.dot(lhs, rhs, preferred_element_type=F32) + gmb_ref[:, cols]
            gm_s[rows, cols] = u_ref[rows, cols] * sgate

    n_ch = CONV_CH // LANES
    for ch in range(n_ch):
        cl = slice(ch * LANES, (ch + 1) * LANES)
        glu_s[ch, 0:CONV_HALO, :] = jnp.where(first, 0.0, glup_ref[:, cl])
        glu_s[ch, CONV_HALO:CONV_HALO + tq, :] = gluc_ref[:, cl]
        glu_s[ch, CONV_HALO + tq:, :] = jnp.where(last, 0.0, glun_ref[:, cl])
    group = SUBLANES * CONV_ROW_STRIDE

    def conv_step(i, carry):
        base = pl.multiple_of(i * (CONV_GROUPS * group), CONV_GROUPS * group)
        for ch in range(n_ch):
            cl = slice(ch * LANES, (ch + 1) * LANES)
            offs = [g * group + ph for g in range(CONV_GROUPS) for ph in range(CONV_ROW_STRIDE)]
            accs = [jnp.broadcast_to(cb_ref[:, cl], (SUBLANES, LANES)) for _ in offs]
            for t in range(CONV_WIDTH):
                w = cw_ref[t:t + 1, cl]
                for a, off in enumerate(offs):
                    start = base + (off + CONV_HALO - CONV_PAD + t)
                    accs[a] = accs[a] + w * glu_s[ch, pl.ds(start, SUBLANES, stride=CONV_ROW_STRIDE), :]
            for a, off in enumerate(offs):
                y_s[ch, pl.ds(base + off, SUBLANES, stride=CONV_ROW_STRIDE), :] = accs[a]
        return carry

    lax.fori_loop(0, tq // (CONV_GROUPS * group), conv_step, 0)

    rchunk = 128
    for r0 in range(0, tq, rchunk):
        rs = slice(r0, r0 + rchunk)
        y = jnp.concatenate([y_s[ch, rs, :] for ch in range(n_ch)], axis=1)
        y = _layer_norm(y, clg_ref[...], clb_ref[...])
        y = y * jax.nn.sigmoid(y)
        merged_s[rs, ATTN_WIDTH + GM_WIDTH:] = _rms(y, og_ref[:, ATTN_WIDTH + GM_WIDTH:]).astype(BF16)
        attn = jnp.concatenate([attn_s[pb, rs, :] for pb in range(ATTN_WIDTH // LANES)], axis=1)
        merged_s[rs, 0:ATTN_WIDTH] = _rms(attn, og_ref[:, 0:ATTN_WIDTH]).astype(BF16)
        merged_s[rs, ATTN_WIDTH:ATTN_WIDTH + GM_WIDTH] = _rms(
            gm_s[rs, :], og_ref[:, ATTN_WIDTH:ATTN_WIDTH + GM_WIDTH]).astype(BF16)
    o_ref[...] = x_ref[...] + jnp.dot(merged_s[...], wout_ref[...], preferred_element_type=F32)


def _mixer(layer, x, q, kt, v, u, vln, glu, sink, ws, gm_bias, cw, cb, clg, clb, og, wout,
           wgu, wd, wpg, wpp, batch, seq):
    n = x.shape[0]
    tq = TQ_MIX
    tiles = seq // tq
    blk_per_tile = tq // BLOCK
    blk_per_seq = seq // BLOCK
    halo_per_tile = tq // CONV_HALO
    halo_per_seq = seq // CONV_HALO

    def row(width):
        return pl.BlockSpec((tq, width), lambda b, j, *_: (b * tiles + j, 0))

    def prev(rows, width, per_tile, per_seq):
        return pl.BlockSpec(
            (rows, width), lambda b, j, *_: (b * per_seq + jnp.maximum(j * per_tile - 1, 0), 0))

    def nxt(rows, width, per_tile, per_seq):
        return pl.BlockSpec(
            (rows, width), lambda b, j, *_: (b * per_seq + jnp.minimum((j + 1) * per_tile, per_seq - 1), 0))

    const = lambda shape: _layer_spec(layer, shape)
    kv_rows = tq + 2 * BLOCK
    step_of = lambda b, j, *_: b * tiles + j
    casts = [_cast_specs(layer, rows, cols, batch * tiles, step_of)
             for rows, cols in ((D_MODEL, 2 * D_FF), (D_FF, D_MODEL), (D_MODEL, D_MODEL), (PLE_DIM, D_MODEL))]
    grid_spec = pltpu.PrefetchScalarGridSpec(
        num_scalar_prefetch=1,
        grid=(batch, tiles),
        in_specs=[
            row(D_MODEL), pl.BlockSpec((ATTN_WIDTH // LANES, tq, LANES), lambda b, j, *_: (0, b * tiles + j, 0)),
            pl.BlockSpec((KV_WIDTH, BLOCK),
                         lambda b, j, *_: (0, b * blk_per_seq + jnp.maximum(j * blk_per_tile - 1, 0))),
            pl.BlockSpec((KV_WIDTH, tq), lambda b, j, *_: (0, b * tiles + j)),
            pl.BlockSpec((KV_WIDTH, BLOCK),
                         lambda b, j, *_: (0, b * blk_per_seq + jnp.minimum((j + 1) * blk_per_tile, blk_per_seq - 1))),
            prev(BLOCK, 2 * LANES, blk_per_tile, blk_per_seq), row(2 * LANES),
            nxt(BLOCK, 2 * LANES, blk_per_tile, blk_per_seq),
            row(GM_WIDTH), row(GM_WIDTH),
            prev(CONV_HALO, CONV_CH, halo_per_tile, halo_per_seq), row(CONV_CH),
            nxt(CONV_HALO, CONV_CH, halo_per_tile, halo_per_seq),
            const((GM_HEADS, CHUNK, CHUNK)), const((CHUNK, GM_WIDTH)),
            const((CONV_WIDTH, CONV_CH)), const((1, CONV_CH)), const((1, CONV_CH)), const((1, CONV_CH)),
            const((1, D_MODEL)), _resident_spec((D_MODEL, D_MODEL)),
            *[src for src, _ in casts],
        ],
        out_specs=(row(D_MODEL), *[dst for _, dst in casts]),
        scratch_shapes=[
            pltpu.VMEM((ATTN_KV_HEADS, blk_per_tile + 2, KV_WIDTH, BLOCK), BF16),
            pltpu.VMEM((ATTN_KV_HEADS, blk_per_tile + 2, KV_WIDTH, BLOCK), BF16),
            pltpu.VMEM((ATTN_KV_HEADS, kv_rows, LANES), BF16), pltpu.VMEM((ATTN_KV_HEADS, kv_rows, LANES), BF16),
            pltpu.VMEM((4, BLOCK, BLOCK), F32),
            pltpu.VMEM((2, ATTN_GROUP // 2 * BLOCK, 2 * 3 * BLOCK), F32),
            pltpu.VMEM((2, ATTN_GROUP // 2 * BLOCK, 2 * 3 * BLOCK), BF16),
            pltpu.VMEM((2, ATTN_GROUP // 2 * BLOCK, LANES), F32),
            pltpu.VMEM((CONV_CH // LANES, tq + 2 * CONV_HALO, LANES), F32),
            pltpu.VMEM((CONV_CH // LANES, tq, LANES), F32),
            pltpu.VMEM((ATTN_WIDTH // LANES, tq, LANES), F32), pltpu.VMEM((tq, GM_WIDTH), F32),
            pltpu.VMEM((tq, D_MODEL), BF16),
        ],
    )
    return pl.pallas_call(
        functools.partial(_mixer_kernel, layer),
        out_shape=(jax.ShapeDtypeStruct((n, D_MODEL), F32),
                   jax.ShapeDtypeStruct((D_MODEL, 2 * D_FF), BF16), jax.ShapeDtypeStruct((D_FF, D_MODEL), BF16),
                   jax.ShapeDtypeStruct((D_MODEL, D_MODEL), BF16), jax.ShapeDtypeStruct((PLE_DIM, D_MODEL), BF16)),
        grid_spec=grid_spec,
        compiler_params=pltpu.CompilerParams(dimension_semantics=("arbitrary", "arbitrary"),
                                             vmem_limit_bytes=VMEM_LIMIT),
        name="mixer",
    )(sink, x, q, kt, kt, kt, v, v, v, u, vln, glu, glu, glu, ws, gm_bias, cw, cb, clg, clb, og, wout,
      wgu, wd, wpg, wpp)


def _ffn_kernel(x_ref, p_ref, gf_ref, wgu_ref, wd_ref, gp_ref, wpg_ref, wpp_ref, o_ref, act_s):
    x = x_ref[...]
    hn = _rms(x, gf_ref[...]).astype(BF16)
    for c0 in range(0, D_FF, FF_CHUNK):
        gate = jnp.dot(hn, wgu_ref[:, c0:c0 + FF_CHUNK], preferred_element_type=F32)
        up = jnp.dot(hn, wgu_ref[:, D_FF + c0:D_FF + c0 + FF_CHUNK], preferred_element_type=F32)
        act_s[:, c0:c0 + FF_CHUNK] = (gate * jax.nn.sigmoid(gate) * up).astype(BF16)
    x = x + jnp.dot(act_s[...], wd_ref[...], preferred_element_type=F32)

    hp = _rms(x, gp_ref[...]).astype(BF16)
    gate = jax.nn.sigmoid(jnp.dot(hp, wpg_ref[...], preferred_element_type=F32))
    proj = jnp.dot(p_ref[...].astype(BF16), wpp_ref[...], preferred_element_type=F32)
    o_ref[...] = x + proj * gate


def _ffn(layer, x, p, gf, wgu, wd, gp, wpg, wpp):
    n = x.shape[0]
    tm = TM_FFN
    row = lambda width: pl.BlockSpec((tm, width), lambda i: (i, 0))
    return pl.pallas_call(
        _ffn_kernel,
        out_shape=jax.ShapeDtypeStruct((n, D_MODEL), F32),
        grid=(n // tm,),
        in_specs=[row(D_MODEL), pl.BlockSpec((None, tm, PLE_DIM), lambda i: (layer, i, 0)),
                  _layer_spec(layer, (1, D_MODEL)),
                  _resident_spec((D_MODEL, 2 * D_FF)), _resident_spec((D_FF, D_MODEL)),
                  _layer_spec(layer, (1, D_MODEL)), _resident_spec((D_MODEL, D_MODEL)),
                  _resident_spec((PLE_DIM, D_MODEL))],
        out_specs=row(D_MODEL),
        scratch_shapes=[pltpu.VMEM((tm, D_FF), BF16)],
        compiler_params=pltpu.CompilerParams(dimension_semantics=("arbitrary",),
                                             vmem_limit_bytes=VMEM_LIMIT),
        name="ffn_ple",
    )(x, p, gf, wgu, wd, gp, wpg, wpp)


def kernel(x, p, positions, norm_mix_g, w_in, q_norm_g, k_norm_g, sink, gm_ln_g, gm_ln_b, gm_ws, gm_bs, conv_w, conv_b, conv_ln_g, conv_ln_b, out_norm_g, w_out, norm_ffn_g, w_gate_up, w_down, ple_norm_g, w_ple_gate, w_ple_proj):
    batch, seq, d = x.shape
    depth = w_in.shape[0]
    n = batch * seq
    assert d == D_MODEL and seq % TQ_MIX == 0
    assert n % TM_PROJ == 0 and n % TM_FFN == 0 and n % TM_ROPE == 0

    vec = lambda a: a[:, None, :]
    ws_b = gm_ws.astype(BF16)
    qg = vec(jnp.tile(q_norm_g, (1, LANES // HEAD_DIM)))
    kg = vec(jnp.tile(k_norm_g, (1, LANES // HEAD_DIM)))
    gm_bias = jnp.repeat(jnp.swapaxes(gm_bs, 1, 2), HEAD_DIM, axis=2)
    sink_flat = sink.reshape(depth * ATTN_Q_HEADS)
    p_flat = p.reshape(depth, n, PLE_DIM)

    tabs = _rope_tables(positions.reshape(1, n))
    xf = x.reshape(n, d)
    for i in range(depth):
        q, kt, v, u, vln, glu, w_out_b = _inproj(i, xf, vec(norm_mix_g), w_in, qg, kg, vec(gm_ln_g),
                                                 vec(gm_ln_b), tabs, w_out)
        xf, w_gu_b, w_down_b, w_pg_b, w_pp_b = _mixer(
            i, xf, q, kt, v, u, vln, glu, sink_flat, ws_b, gm_bias, conv_w, vec(conv_b), vec(conv_ln_g),
            vec(conv_ln_b), vec(out_norm_g), w_out_b, w_gate_up, w_down, w_ple_gate, w_ple_proj, batch, seq)
        xf = _ffn(i, xf, p_flat, vec(norm_ffn_g), w_gu_b, w_down_b, vec(ple_norm_g), w_pg_b, w_pp_b)
    return xf.reshape(batch, seq, d)
```

```python
import functools
import math

import numpy as np
import jax
import jax.numpy as jnp
from jax import lax
from jax.experimental import pallas as pl
from jax.experimental.pallas import tpu as pltpu

F32 = jnp.float32
BF16 = jnp.bfloat16

D_MODEL = 1024
HEAD_DIM = 64
ATTN_WIDTH = 512
ATTN_Q_HEADS = 8
ATTN_KV_HEADS = 2
ATTN_GROUP = 4
KV_WIDTH = 128
WINDOW = 128
BLOCK = 128
ROPE_THETA = 500000.0
ROT_DIM = 16
GM_WIDTH = 256
GM_HEADS = 4
CHUNK = 128
CONV_CH = 256
CONV_WIDTH = 31
CONV_PAD = 15
D_FF = 2816
PLE_DIM = 256
EPS = 1e-6
NEG_INF = -1e30

Q_OFF = 0
K_OFF = Q_OFF + ATTN_WIDTH
V_OFF = K_OFF + KV_WIDTH
GM_OFF = V_OFF + KV_WIDTH
CONV_OFF = GM_OFF + 2 * GM_WIDTH
IN_COLS = CONV_OFF + 2 * CONV_CH

LANES = 128
SUBLANES = 8
BF16_ROWS = 16
CONV_HALO = 16
CONV_ROW_STRIDE = 2
CONV_GROUPS = 4
VMEM_LIMIT = 56 * 1024 * 1024

TM_ROPE = 8192
TM_PROJ = 512
TQ_MIX = 1024
TM_FFN = 1024
FF_CHUNK = 256
SOFTMAX_ROWS = 32

INV_FREQ = [float(ROPE_THETA ** (-(2.0 * j) / ROT_DIM)) for j in range(ROT_DIM // 2)]
SQRT_HALF = float(np.sqrt(0.5))
LOG2E = float(np.log2(np.e))


def _layer_spec(layer, shape):
    nd = len(shape)
    return pl.BlockSpec((None,) + tuple(shape), lambda *_: (layer,) + (0,) * nd,
                        pipeline_mode=pl.Buffered(1))


def _resident_spec(shape):
    nd = len(shape)
    return pl.BlockSpec(tuple(shape), lambda *_: (0,) * nd, pipeline_mode=pl.Buffered(1))


def _cast_block_rows(rows, n_steps):
    r = BF16_ROWS
    while r * n_steps < rows or rows % r:
        r += BF16_ROWS
    return r


def _cast_specs(layer, rows, cols, n_steps, step_of):
    r = _cast_block_rows(rows, n_steps)
    last = rows // r - 1
    src = pl.BlockSpec((None, r, cols), lambda *idx: (layer, jnp.minimum(step_of(*idx), last), 0))
    dst = pl.BlockSpec((r, cols), lambda *idx: (jnp.minimum(step_of(*idx), last), 0))
    return src, dst


def _split_bf16(x):
    hi = x.astype(BF16)
    lo = (x - hi.astype(F32)).astype(BF16)
    return hi, lo


def _rope_table_kernel(pos_ref, c_ref, s_ref):
    half = ROT_DIM // 2
    pos = pos_ref[...].astype(F32)
    row = lax.broadcasted_iota(jnp.int32, (ROT_DIM, 1), 0)
    inv = jnp.zeros((ROT_DIM, 1), F32)
    for j, f in enumerate(INV_FREQ):
        inv = jnp.where((row & (half - 1)) == j, f, inv)
    ang = inv * pos
    cs = jnp.where(row < half, jnp.cos(ang), jnp.sin(ang))

    r = lax.broadcasted_iota(jnp.int32, (ROT_DIM, 2 * LANES), 0)
    col = lax.broadcasted_iota(jnp.int32, (ROT_DIM, 2 * LANES), 1)
    table = col // LANES
    in_head = col & (HEAD_DIM - 1)
    freq = col & (half - 1)
    put_cos = (table == 0) & (in_head < ROT_DIM) & (r == freq)
    put_nsin = (table == 1) & (in_head < half) & (r == freq + half)
    put_sin = (table == 1) & (in_head >= half) & (in_head < ROT_DIM) & (r == freq + half)
    place = jnp.where(put_cos | put_sin, 1.0, jnp.where(put_nsin, -1.0, 0.0)).astype(BF16)

    hi, lo = _split_bf16(cs)
    dn = (((0,), (0,)), ((), ()))
    t = (lax.dot_general(hi, place, dn, preferred_element_type=F32)
         + lax.dot_general(lo, place, dn, preferred_element_type=F32))
    lane = lax.broadcasted_iota(jnp.int32, (1, LANES), 1)
    c_ref[...] = jnp.where((lane & (HEAD_DIM - 1)) < ROT_DIM, t[:, 0:LANES], 1.0)
    s_ref[...] = t[:, LANES:2 * LANES]


def _rope_tables(pos_row):
    n = pos_row.shape[1]
    tm = TM_ROPE
    tab = jax.ShapeDtypeStruct((n, LANES), F32)
    spec = pl.BlockSpec((tm, LANES), lambda i: (i, 0))
    return pl.pallas_call(
        _rope_table_kernel,
        out_shape=(tab, tab),
        grid=(n // tm,),
        in_specs=[pl.BlockSpec((1, tm), lambda i: (0, i))],
        out_specs=(spec, spec),
        name="rope_tables",
    )(pos_row)


def _rms(x, gain):
    ms = jnp.mean(x * x, axis=-1, keepdims=True)
    return x * lax.rsqrt(ms + EPS) * gain


def _layer_norm(x, gain, bias):
    mu = jnp.mean(x, axis=-1, keepdims=True)
    xc = x - mu
    var = jnp.mean(xc * xc, axis=-1, keepdims=True)
    return xc * lax.rsqrt(var + EPS) * gain + bias


def _inproj_kernel(x_ref, g_ref, w_ref, qg_ref, kg_ref, lng_ref, lnb_ref,
                   c_ref, s_ref, wout_ref,
                   q_out, kt_out, v_out, u_out, vln_out, glu_out, wout_bf_out, w_bf, z_even, z_odd, *, last_is_even):
    i = pl.program_id(0)
    n_tiles = pl.num_programs(0) - 1

    wout_bf_out[...] = wout_ref[...].astype(BF16)

    def project(z_dst):
        h = _rms(x_ref[...], g_ref[...]).astype(BF16)
        z_dst[...] = jnp.dot(h, w_bf[...], preferred_element_type=F32)

    def finish(z_src):
        cos_t, sin_t = c_ref[...], s_ref[...]
        lane = lax.broadcasted_iota(jnp.int32, (1, LANES), 1)
        head0 = lane < HEAD_DIM

        def head_norm_rope(z, gain):
            sq = z * z
            first = jnp.sum(jnp.where(head0, sq, 0.0), axis=-1, keepdims=True)
            both = jnp.sum(sq, axis=-1, keepdims=True)
            ss = jnp.where(head0, first, both - first)
            zn = z * lax.rsqrt(ss * (1.0 / HEAD_DIM) + EPS) * gain
            partner = jnp.where((lane & (ROT_DIM // 2)) == 0,
                                pltpu.roll(zn, LANES - ROT_DIM // 2, 1), pltpu.roll(zn, ROT_DIM // 2, 1))
            return zn * cos_t + partner * sin_t

        q_gain = qg_ref[...] * (LOG2E / math.sqrt(HEAD_DIM))
        for b in range(ATTN_WIDTH // LANES):
            q_out[b] = head_norm_rope(z_src[:, Q_OFF + b * LANES:Q_OFF + (b + 1) * LANES], q_gain).astype(BF16)

        k = head_norm_rope(z_src[:, K_OFF:V_OFF], kg_ref[...])
        v = z_src[:, V_OFF:GM_OFF]
        kt_out[...] = k.T.astype(BF16)
        v_out[:, 0:LANES] = v.astype(BF16)
        v_out[:, LANES:2 * LANES] = pltpu.roll(v, HEAD_DIM, 1).astype(BF16)

        zuv = z_src[:, GM_OFF:CONV_OFF]
        uv = 0.5 * zuv * (1.0 + lax.erf(zuv * SQRT_HALF))
        u_out[...] = uv[:, 0:GM_WIDTH]
        vln_out[...] = _layer_norm(uv[:, GM_WIDTH:], lng_ref[...], lnb_ref[...]).astype(BF16)

        glu_out[...] = z_src[:, CONV_OFF:CONV_OFF + CONV_CH] * jax.nn.sigmoid(z_src[:, CONV_OFF + CONV_CH:])

    inner = jnp.logical_and(i > 0, i < n_tiles)

    @pl.when(i == 0)
    def _():
        w_bf[...] = w_ref[...].astype(BF16)
        project(z_even)

    @pl.when(jnp.logical_and(inner, i % 2 == 1))
    def _():
        finish(z_even)
        project(z_odd)

    @pl.when(jnp.logical_and(inner, i % 2 == 0))
    def _():
        finish(z_odd)
        project(z_even)

    @pl.when(i == n_tiles)
    def _():
        finish(z_even if last_is_even else z_odd)


def _inproj(layer, x, g, w, qg, kg, lng, lnb, tabs, wout):
    n = x.shape[0]
    tm = TM_PROJ
    n_tiles = n // tm
    cur = lambda width: pl.BlockSpec((tm, width), lambda i: (jnp.minimum(i, n_tiles - 1), 0))
    lag = lambda width: pl.BlockSpec((tm, width), lambda i: (jnp.maximum(i - 1, 0), 0))
    out_shape = (
        jax.ShapeDtypeStruct((ATTN_WIDTH // LANES, n, LANES), BF16),
        jax.ShapeDtypeStruct((KV_WIDTH, n), BF16),
        jax.ShapeDtypeStruct((n, 2 * LANES), BF16),
        jax.ShapeDtypeStruct((n, GM_WIDTH), F32),
        jax.ShapeDtypeStruct((n, GM_WIDTH), BF16),
        jax.ShapeDtypeStruct((n, CONV_CH), F32),
        jax.ShapeDtypeStruct((D_MODEL, D_MODEL), BF16),
    )
    wout_src, wout_dst = _cast_specs(layer, D_MODEL, D_MODEL, n_tiles + 1, lambda i: i)
    return pl.pallas_call(
        functools.partial(_inproj_kernel, last_is_even=(n_tiles - 1) % 2 == 0),
        out_shape=out_shape,
        grid=(n_tiles + 1,),
        in_specs=[cur(D_MODEL), _layer_spec(layer, (1, D_MODEL)), _layer_spec(layer, (D_MODEL, IN_COLS)),
                  _layer_spec(layer, (1, LANES)), _layer_spec(layer, (1, LANES)),
                  _layer_spec(layer, (1, GM_WIDTH)), _layer_spec(layer, (1, GM_WIDTH)),
                  lag(LANES), lag(LANES), wout_src],
        out_specs=(pl.BlockSpec((ATTN_WIDTH // LANES, tm, LANES), lambda i: (0, jnp.maximum(i - 1, 0), 0)),
                   pl.BlockSpec((KV_WIDTH, tm), lambda i: (0, jnp.maximum(i - 1, 0))),
                   lag(2 * LANES), lag(GM_WIDTH), lag(GM_WIDTH), lag(CONV_CH), wout_dst),
        scratch_shapes=[pltpu.VMEM((D_MODEL, IN_COLS), BF16),
                        pltpu.VMEM((tm, IN_COLS), F32), pltpu.VMEM((tm, IN_COLS), F32)],
        compiler_params=pltpu.CompilerParams(dimension_semantics=("arbitrary",),
                                             vmem_limit_bytes=VMEM_LIMIT),
        name="in_proj",
    )(x, g, w, qg, kg, lng, lnb, *tabs, wout)


def _mixer_kernel(layer, sink_ref, x_ref, q_ref, ktp_ref, ktc_ref, ktn_ref, vp_ref, vc_ref, vn_ref, u_ref, vln_ref,
                  glup_ref, gluc_ref, glun_ref,
                  ws_ref, gmb_ref, cw_ref, cb_ref, clg_ref, clb_ref, og_ref, wout_ref,
                  wgu_ref, wd_ref, wpg_ref, wpp_ref,
                  o_ref, wgu_bf_out, wd_bf_out, wpg_bf_out, wpp_bf_out,
                  kta_s, ktb_s, va_s, vb_s, bias_s, s_s, p_s, scale_s, glu_s, y_s, attn_s, gm_s, merged_s):
    tq = x_ref.shape[0]
    nblk = tq // BLOCK
    j = pl.program_id(1)
    first = j == 0
    last = j == pl.num_programs(1) - 1

    wgu_bf_out[...] = wgu_ref[...].astype(BF16)
    wd_bf_out[...] = wd_ref[...].astype(BF16)
    wpg_bf_out[...] = wpg_ref[...].astype(BF16)
    wpp_bf_out[...] = wpp_ref[...].astype(BF16)

    lane = lax.broadcasted_iota(jnp.int32, (1, LANES), 1)
    head0 = lane < HEAD_DIM
    zero = jnp.zeros((), BF16)
    zero_rows = jnp.zeros((HEAD_DIM, BLOCK), BF16)
    for b in range(nblk + 2):
        if b == 0:
            kt = ktp_ref[...]
        elif b == nblk + 1:
            kt = ktn_ref[...]
        else:
            kt = ktc_ref[:, (b - 1) * BLOCK:b * BLOCK]
        for h in range(ATTN_KV_HEADS):
            kh = kt[h * HEAD_DIM:(h + 1) * HEAD_DIM, :]
            kta_s[h, b] = jnp.concatenate([kh, zero_rows], axis=0)
            ktb_s[h, b] = jnp.concatenate([zero_rows, kh], axis=0)
    for src, r0, rows in ((vp_ref, 0, BLOCK), (vc_ref, BLOCK, tq), (vn_ref, BLOCK + tq, BLOCK)):
        dst = slice(r0, r0 + rows)
        v, v_sw = src[:, 0:LANES], src[:, LANES:2 * LANES]
        va_s[0, dst, :] = jnp.where(head0, v, zero)
        vb_s[0, dst, :] = jnp.where(head0, zero, v_sw)
        va_s[1, dst, :] = jnp.where(head0, v_sw, zero)
        vb_s[1, dst, :] = jnp.where(head0, zero, v)

    qi = lax.broadcasted_iota(jnp.int32, (BLOCK, BLOCK), 0)
    kj = lax.broadcasted_iota(jnp.int32, (BLOCK, BLOCK), 1)
    band_lo = jnp.where(kj >= qi, 0.0, NEG_INF)
    band_hi = jnp.where(kj <= qi, 0.0, NEG_INF)
    bias_s[0] = band_lo
    bias_s[1] = band_hi
    bias_s[2] = jnp.where(first, NEG_INF, band_lo)
    bias_s[3] = jnp.where(last, NEG_INF, band_hi)

    n_pairs = ATTN_GROUP // 2

    def scores(nb, h):
        r0 = pl.multiple_of(nb * BLOCK, BLOCK)
        q_st = jnp.concatenate([q_ref[n_pairs * h + pp, pl.ds(r0, BLOCK), :] for pp in range(n_pairs)], axis=0)
        kt_st = jnp.concatenate([kta_s[h, nb + t] for t in range(3)] + [ktb_s[h, nb + t] for t in range(3)],
                                axis=1)
        s_s[h] = jnp.dot(q_st, kt_st, preferred_element_type=F32)

    def softmax_pv(nb, h):
        r0 = pl.multiple_of(nb * BLOCK, BLOCK)
        lo_idx = jnp.where(nb == 0, 2, 0)
        hi_idx = jnp.where(nb == nblk - 1, 3, 1)
        v_st = jnp.concatenate([va_s[h, pl.ds(r0, 3 * BLOCK), :], vb_s[h, pl.ds(r0, 3 * BLOCK), :]], axis=0)
        for pp in range(n_pairs):
            for rc in range(0, BLOCK, SOFTMAX_ROWS):
                rr = slice(pp * BLOCK + rc, pp * BLOCK + rc + SOFTMAX_ROWS)
                br = slice(rc, rc + SOFTMAX_ROWS)
                inv = []
                for half in range(2):
                    sink = sink_ref[layer * ATTN_Q_HEADS + ATTN_GROUP * h + 2 * pp + half] * LOG2E
                    c0 = half * 3 * BLOCK
                    s0 = s_s[h, rr, c0:c0 + BLOCK] + bias_s[lo_idx, br, :]
                    s1 = s_s[h, rr, c0 + BLOCK:c0 + 2 * BLOCK]
                    s2 = s_s[h, rr, c0 + 2 * BLOCK:c0 + 3 * BLOCK] + bias_s[hi_idx, br, :]
                    m = jnp.max(jnp.maximum(jnp.maximum(s0, s1), s2), axis=-1, keepdims=True)
                    m = jnp.maximum(m, sink)
                    e0, e1, e2 = jnp.exp2(s0 - m), jnp.exp2(s1 - m), jnp.exp2(s2 - m)
                    denom = jnp.sum(e0 + e1 + e2, axis=-1, keepdims=True) + jnp.exp2(sink - m)
                    p_s[h, rr, c0:c0 + BLOCK] = e0.astype(BF16)
                    p_s[h, rr, c0 + BLOCK:c0 + 2 * BLOCK] = e1.astype(BF16)
                    p_s[h, rr, c0 + 2 * BLOCK:c0 + 3 * BLOCK] = e2.astype(BF16)
                    inv.append(1.0 / denom)
                scale_s[h, rr, :] = jnp.where(head0, inv[0], inv[1])
        o = jnp.dot(p_s[h], v_st, preferred_element_type=F32) * scale_s[h]
        for pp in range(n_pairs):
            attn_s[n_pairs * h + pp, pl.ds(r0, BLOCK), :] = o[pp * BLOCK:(pp + 1) * BLOCK, :]

    scores(0, 0)

    def attn_step(nb, carry):
        scores(nb, 1)
        softmax_pv(nb, 0)
        scores(jnp.minimum(nb + 1, nblk - 1), 0)
        softmax_pv(nb, 1)
        return carry

    lax.fori_loop(0, nblk, attn_step, 0)

    for nb in range(nblk):
        rows = slice(nb * BLOCK, (nb + 1) * BLOCK)
        for hp in range(GM_HEADS // 2):
            cols = slice(hp * LANES, (hp + 1) * LANES)
            vblk = vln_ref[rows, cols]
            lhs = jnp.concatenate([ws_ref[2 * hp], ws_ref[2 * hp + 1]], axis=1)
            rhs = jnp.concatenate([jnp.where(head0, vblk, zero), jnp.where(head0, zero, vblk)], axis=0)
            sgate = jnp.dot(lhs, rhs, preferred_element_type=F32) + gmb_ref[:, cols]
            gm_s[rows, cols] = u_ref[rows, cols] * sgate

    n_ch = CONV_CH // LANES
    for ch in range(n_ch):
        cl = slice(ch * LANES, (ch + 1) * LANES)
        glu_s[ch, 0:CONV_HALO, :] = jnp.where(first, 0.0, glup_ref[:, cl])
        glu_s[ch, CONV_HALO:CONV_HALO + tq, :] = gluc_ref[:, cl]
        glu_s[ch, CONV_HALO + tq:, :] = jnp.where(last, 0.0, glun_ref[:, cl])
    group = SUBLANES * CONV_ROW_STRIDE

    def conv_step(i, carry):
        base = pl.multiple_of(i * (CONV_GROUPS * group), CONV_GROUPS * group)
        for ch in range(n_ch):
            cl = slice(ch * LANES, (ch + 1) * LANES)
            offs = [g * group + ph for g in range(CONV_GROUPS) for ph in range(CONV_ROW_STRIDE)]
            accs = [jnp.broadcast_to(cb_ref[:, cl], (SUBLANES, LANES)) for _ in offs]
            for t in range(CONV_WIDTH):
                w = cw_ref[t:t + 1, cl]
                for a, off in enumerate(offs):
                    start = base + (off + CONV_HALO - CONV_PAD + t)
                    accs[a] = accs[a] + w * glu_s[ch, pl.ds(start, SUBLANES, stride=CONV_ROW_STRIDE), :]
            for a, off in enumerate(offs):
                y_s[ch, pl.ds(base + off, SUBLANES, stride=CONV_ROW_STRIDE), :] = accs[a]
        return carry

    lax.fori_loop(0, tq // (CONV_GROUPS * group), conv_step, 0)

    rchunk = 128
    for r0 in range(0, tq, rchunk):
        rs = slice(r0, r0 + rchunk)
        y = jnp.concatenate([y_s[ch, rs, :] for ch in range(n_ch)], axis=1)
        y = _layer_norm(y, clg_ref[...], clb_ref[...])
        y = y * jax.nn.sigmoid(y)
        merged_s[rs, ATTN_WIDTH + GM_WIDTH:] = _rms(y, og_ref[:, ATTN_WIDTH + GM_WIDTH:]).astype(BF16)
        attn = jnp.concatenate([attn_s[pb, rs, :] for pb in range(ATTN_WIDTH // LANES)], axis=1)
        merged_s[rs, 0:ATTN_WIDTH] = _rms(attn, og_ref[:, 0:ATTN_WIDTH]).astype(BF16)
        merged_s[rs, ATTN_WIDTH:ATTN_WIDTH + GM_WIDTH] = _rms(
            gm_s[rs, :], og_ref[:, ATTN_WIDTH:ATTN_WIDTH + GM_WIDTH]).astype(BF16)
    o_ref[...] = x_ref[...] + jnp.dot(merged_s[...], wout_ref[...], preferred_element_type=F32)


def _mixer(layer, x, q, kt, v, u, vln, glu, sink, ws, gm_bias, cw, cb, clg, clb, og, wout,
           wgu, wd, wpg, wpp, batch, seq):
    n = x.shape[0]
    tq = TQ_MIX
    tiles = seq // tq
    blk_per_tile = tq // BLOCK
    blk_per_seq = seq // BLOCK
    halo_per_tile = tq // CONV_HALO
    halo_per_seq = seq // CONV_HALO

    def row(width):
        return pl.BlockSpec((tq, width), lambda b, j, *_: (b * tiles + j, 0))

    def prev(rows, width, per_tile, per_seq):
        return pl.BlockSpec(
            (rows, width), lambda b, j, *_: (b * per_seq + jnp.maximum(j * per_tile - 1, 0), 0))

    def nxt(rows, width, per_tile, per_seq):
        return pl.BlockSpec(
            (rows, width), lambda b, j, *_: (b * per_seq + jnp.minimum((j + 1) * per_tile, per_seq - 1), 0))

    const = lambda shape: _layer_spec(layer, shape)
    kv_rows = tq + 2 * BLOCK
    step_of = lambda b, j, *_: b * tiles + j
    casts = [_cast_specs(layer, rows, cols, batch * tiles, step_of)
             for rows, cols in ((D_MODEL, 2 * D_FF), (D_FF, D_MODEL), (D_MODEL, D_MODEL), (PLE_DIM, D_MODEL))]
    grid_spec = pltpu.PrefetchScalarGridSpec(
        num_scalar_prefetch=1,
        grid=(batch, tiles),
        in_specs=[
            row(D_MODEL), pl.BlockSpec((ATTN_WIDTH // LANES, tq, LANES), lambda b, j, *_: (0, b * tiles + j, 0)),
            pl.BlockSpec((KV_WIDTH, BLOCK),
                         lambda b, j, *_: (0, b * blk_per_seq + jnp.maximum(j * blk_per_tile - 1, 0))),
            pl.BlockSpec((KV_WIDTH, tq), lambda b, j, *_: (0, b * tiles + j)),
            pl.BlockSpec((KV_WIDTH, BLOCK),
                         lambda b, j, *_: (0, b * blk_per_seq + jnp.minimum((j + 1) * blk_per_tile, blk_per_seq - 1))),
            prev(BLOCK, 2 * LANES, blk_per_tile, blk_per_seq), row(2 * LANES),
            nxt(BLOCK, 2 * LANES, blk_per_tile, blk_per_seq),
            row(GM_WIDTH), row(GM_WIDTH),
            prev(CONV_HALO, CONV_CH, halo_per_tile, halo_per_seq), row(CONV_CH),
            nxt(CONV_HALO, CONV_CH, halo_per_tile, halo_per_seq),
            const((GM_HEADS, CHUNK, CHUNK)), const((CHUNK, GM_WIDTH)),
            const((CONV_WIDTH, CONV_CH)), const((1, CONV_CH)), const((1, CONV_CH)), const((1, CONV_CH)),
            const((1, D_MODEL)), _resident_spec((D_MODEL, D_MODEL)),
            *[src for src, _ in casts],
        ],
        out_specs=(row(D_MODEL), *[dst for _, dst in casts]),
        scratch_shapes=[
            pltpu.VMEM((ATTN_KV_HEADS, blk_per_tile + 2, KV_WIDTH, BLOCK), BF16),
            pltpu.VMEM((ATTN_KV_HEADS, blk_per_tile + 2, KV_WIDTH, BLOCK), BF16),
            pltpu.VMEM((ATTN_KV_HEADS, kv_rows, LANES), BF16), pltpu.VMEM((ATTN_KV_HEADS, kv_rows, LANES), BF16),
            pltpu.VMEM((4, BLOCK, BLOCK), F32),
            pltpu.VMEM((2, ATTN_GROUP // 2 * BLOCK, 2 * 3 * BLOCK), F32),
            pltpu.VMEM((2, ATTN_GROUP // 2 * BLOCK, 2 * 3 * BLOCK), BF16),
            pltpu.VMEM((2, ATTN_GROUP // 2 * BLOCK, LANES), F32),
            pltpu.VMEM((CONV_CH // LANES, tq + 2 * CONV_HALO, LANES), F32),
            pltpu.VMEM((CONV_CH // LANES, tq, LANES), F32),
            pltpu.VMEM((ATTN_WIDTH // LANES, tq, LANES), F32), pltpu.VMEM((tq, GM_WIDTH), F32),
            pltpu.VMEM((tq, D_MODEL), BF16),
        ],
    )
    return pl.pallas_call(
        functools.partial(_mixer_kernel, layer),
        out_shape=(jax.ShapeDtypeStruct((n, D_MODEL), F32),
                   jax.ShapeDtypeStruct((D_MODEL, 2 * D_FF), BF16), jax.ShapeDtypeStruct((D_FF, D_MODEL), BF16),
                   jax.ShapeDtypeStruct((D_MODEL, D_MODEL), BF16), jax.ShapeDtypeStruct((PLE_DIM, D_MODEL), BF16)),
        grid_spec=grid_spec,
        compiler_params=pltpu.CompilerParams(dimension_semantics=("arbitrary", "arbitrary"),
                                             vmem_limit_bytes=VMEM_LIMIT),
        name="mixer",
    )(sink, x, q, kt, kt, kt, v, v, v, u, vln, glu, glu, glu, ws, gm_bias, cw, cb, clg, clb, og, wout,
      wgu, wd, wpg, wpp)


def _ffn_kernel(x_ref, p_ref, gf_ref, wgu_ref, wd_ref, gp_ref, wpg_ref, wpp_ref, o_ref, act_s):
    x = x_ref[...]
    hn = _rms(x, gf_ref[...]).astype(BF16)
    for c0 in range(0, D_FF, FF_CHUNK):
        gate = jnp.dot(hn, wgu_ref[:, c0:c0 + FF_CHUNK], preferred_element_type=F32)
        up = jnp.dot(hn, wgu_ref[:, D_FF + c0:D_FF + c0 + FF_CHUNK], preferred_element_type=F32)
        act_s[:, c0:c0 + FF_CHUNK] = (gate * jax.nn.sigmoid(gate) * up).astype(BF16)
    x = x + jnp.dot(act_s[...], wd_ref[...], preferred_element_type=F32)

    hp = _rms(x, gp_ref[...]).astype(BF16)
    gate = jax.nn.sigmoid(jnp.dot(hp, wpg_ref[...], preferred_element_type=F32))
    proj = jnp.dot(p_ref[...].astype(BF16), wpp_ref[...], preferred_element_type=F32)
    o_ref[...] = x + proj * gate


def _ffn(layer, x, p, gf, wgu, wd, gp, wpg, wpp):
    n = x.shape[0]
    tm = TM_FFN
    row = lambda width: pl.BlockSpec((tm, width), lambda i: (i, 0))
    return pl.pallas_call(
        _ffn_kernel,
        out_shape=jax.ShapeDtypeStruct((n, D_MODEL), F32),
        grid=(n // tm,),
        in_specs=[row(D_MODEL), pl.BlockSpec((None, tm, PLE_DIM), lambda i: (layer, i, 0)),
                  _layer_spec(layer, (1, D_MODEL)),
                  _resident_spec((D_MODEL, 2 * D_FF)), _resident_spec((D_FF, D_MODEL)),
                  _layer_spec(layer, (1, D_MODEL)), _resident_spec((D_MODEL, D_MODEL)),
                  _resident_spec((PLE_DIM, D_MODEL))],
        out_specs=row(D_MODEL),
        scratch_shapes=[pltpu.VMEM((tm, D_FF), BF16)],
        compiler_params=pltpu.CompilerParams(dimension_semantics=("arbitrary",),
                                             vmem_limit_bytes=VMEM_LIMIT),
        name="ffn_ple",
    )(x, p, gf, wgu, wd, gp, wpg, wpp)


def kernel(x, p, positions, norm_mix_g, w_in, q_norm_g, k_norm_g, sink, gm_ln_g, gm_ln_b, gm_ws, gm_bs, conv_w, conv_b, conv_ln_g, conv_ln_b, out_norm_g, w_out, norm_ffn_g, w_gate_up, w_down, ple_norm_g, w_ple_gate, w_ple_proj):
    batch, seq, d = x.shape
    depth = w_in.shape[0]
    n = batch * seq
    assert d == D_MODEL and seq % TQ_MIX == 0
    assert n % TM_PROJ == 0 and n % TM_FFN == 0 and n % TM_ROPE == 0

    vec = lambda a: a[:, None, :]
    ws_b = gm_ws.astype(BF16)
    qg = vec(jnp.tile(q_norm_g, (1, LANES // HEAD_DIM)))
    kg = vec(jnp.tile(k_norm_g, (1, LANES // HEAD_DIM)))
    gm_bias = jnp.repeat(jnp.swapaxes(gm_bs, 1, 2), HEAD_DIM, axis=2)
    sink_flat = sink.reshape(depth * ATTN_Q_HEADS)
    p_flat = p.reshape(depth, n, PLE_DIM)

    tabs = _rope_tables(positions.reshape(1, n))
    xf = x.reshape(n, d)
    for i in range(depth):
        q, kt, v, u, vln, glu, w_out_b = _inproj(i, xf, vec(norm_mix_g), w_in, qg, kg, vec(gm_ln_g),
                                                 vec(gm_ln_b), tabs, w_out)
        xf, w_gu_b, w_down_b, w_pg_b, w_pp_b = _mixer(
            i, xf, q, kt, v, u, vln, glu, sink_flat, ws_b, gm_bias, conv_w, vec(conv_b), vec(conv_ln_g),
            vec(conv_ln_b), vec(out_norm_g), w_out_b, w_gate_up, w_down, w_ple_gate, w_ple_proj, batch, seq)
        xf = _ffn(i, xf, p_flat, vec(norm_ffn_g), w_gu_b, w_down_b, vec(ple_norm_g), w_pg_b, w_pp_b)
    return xf.reshape(batch, seq, d)
```

```python
import functools
import math

import numpy as np
import jax
import jax.numpy as jnp
from jax import lax
from jax.experimental import pallas as pl
from jax.experimental.pallas import tpu as pltpu

F32 = jnp.float32
BF16 = jnp.bfloat16

D_MODEL = 1024
HEAD_DIM = 64
ATTN_WIDTH = 512
ATTN_Q_HEADS = 8
ATTN_KV_HEADS = 2
ATTN_GROUP = 4
KV_WIDTH = 128
WINDOW = 128
BLOCK = 128
ROPE_THETA = 500000.0
ROT_DIM = 16
GM_WIDTH = 256
GM_HEADS = 4
CHUNK = 128
CONV_CH = 256
CONV_WIDTH = 31
CONV_PAD = 15
D_FF = 2816
PLE_DIM = 256
EPS = 1e-6
NEG_INF = -1e30

Q_OFF = 0
K_OFF = Q_OFF + ATTN_WIDTH
V_OFF = K_OFF + KV_WIDTH
GM_OFF = V_OFF + KV_WIDTH
CONV_OFF = GM_OFF + 2 * GM_WIDTH
IN_COLS = CONV_OFF + 2 * CONV_CH

LANES = 128
SUBLANES = 8
BF16_ROWS = 16
CONV_HALO = 16
CONV_ROW_STRIDE = 2
CONV_GROUPS = 4
VMEM_BYTES_V7X = 64 * 1024 * 1024
VMEM_LIMIT = VMEM_BYTES_V7X * 7 // 8

TM_ROPE = 8192
TM_PROJ = 512
TQ_MIX = 1024
TM_FFN = 1024
FF_CHUNK = 256
SOFTMAX_ROWS = 32

INV_FREQ = [float(ROPE_THETA ** (-(2.0 * j) / ROT_DIM)) for j in range(ROT_DIM // 2)]
SQRT_HALF = float(np.sqrt(0.5))
LOG2E = float(np.log2(np.e))


def _layer_spec(layer, shape):
    nd = len(shape)
    return pl.BlockSpec((None,) + tuple(shape), lambda *_: (layer,) + (0,) * nd,
                        pipeline_mode=pl.Buffered(1))


def _resident_spec(shape):
    nd = len(shape)
    return pl.BlockSpec(tuple(shape), lambda *_: (0,) * nd, pipeline_mode=pl.Buffered(1))


def _cast_block_rows(rows, n_steps):
    r = BF16_ROWS
    while r * n_steps < rows or rows % r:
        r += BF16_ROWS
    return r


def _cast_specs(layer, rows, cols, n_steps, step_of):
    r = _cast_block_rows(rows, n_steps)
    last = rows // r - 1
    src = pl.BlockSpec((None, r, cols), lambda *idx: (layer, jnp.minimum(step_of(*idx), last), 0))
    dst = pl.BlockSpec((r, cols), lambda *idx: (jnp.minimum(step_of(*idx), last), 0))
    return src, dst


def _split_bf16(x):
    hi = x.astype(BF16)
    lo = (x - hi.astype(F32)).astype(BF16)
    return hi, lo


def _rope_table_kernel(pos_ref, c_ref, s_ref):
    half = ROT_DIM // 2
    pos = pos_ref[...].astype(F32)
    row = lax.broadcasted_iota(jnp.int32, (ROT_DIM, 1), 0)
    inv = jnp.zeros((ROT_DIM, 1), F32)
    for j, f in enumerate(INV_FREQ):
        inv = jnp.where((row & (half - 1)) == j, f, inv)
    ang = inv * pos
    cs = jnp.where(row < half, jnp.cos(ang), jnp.sin(ang))

    r = lax.broadcasted_iota(jnp.int32, (ROT_DIM, 2 * LANES), 0)
    col = lax.broadcasted_iota(jnp.int32, (ROT_DIM, 2 * LANES), 1)
    table = col // LANES
    in_head = col & (HEAD_DIM - 1)
    freq = col & (half - 1)
    put_cos = (table == 0) & (in_head < ROT_DIM) & (r == freq)
    put_nsin = (table == 1) & (in_head < half) & (r == freq + half)
    put_sin = (table == 1) & (in_head >= half) & (in_head < ROT_DIM) & (r == freq + half)
    place = jnp.where(put_cos | put_sin, 1.0, jnp.where(put_nsin, -1.0, 0.0)).astype(BF16)

    hi, lo = _split_bf16(cs)
    dn = (((0,), (0,)), ((), ()))
    t = (lax.dot_general(hi, place, dn, preferred_element_type=F32)
         + lax.dot_general(lo, place, dn, preferred_element_type=F32))
    lane = lax.broadcasted_iota(jnp.int32, (1, LANES), 1)
    c_ref[...] = jnp.where((lane & (HEAD_DIM - 1)) < ROT_DIM, t[:, 0:LANES], 1.0)
    s_ref[...] = t[:, LANES:2 * LANES]


def _rope_tables(pos_row):
    n = pos_row.shape[1]
    tm = TM_ROPE
    tab = jax.ShapeDtypeStruct((n, LANES), F32)
    spec = pl.BlockSpec((tm, LANES), lambda i: (i, 0))
    return pl.pallas_call(
        _rope_table_kernel,
        out_shape=(tab, tab),
        grid=(n // tm,),
        in_specs=[pl.BlockSpec((1, tm), lambda i: (0, i))],
        out_specs=(spec, spec),
        name="rope_tables",
    )(pos_row)


def _rms(x, gain):
    ms = jnp.mean(x * x, axis=-1, keepdims=True)
    return x * lax.rsqrt(ms + EPS) * gain


def _layer_norm(x, gain, bias):
    mu = jnp.mean(x, axis=-1, keepdims=True)
    xc = x - mu
    var = jnp.mean(xc * xc, axis=-1, keepdims=True)
    return xc * lax.rsqrt(var + EPS) * gain + bias


def _inproj_kernel(x_ref, g_ref, w_ref, qg_ref, kg_ref, lng_ref, lnb_ref,
                   c_ref, s_ref, wout_ref,
                   q_out, kv_out, u_out, vln_out, glu_out, wout_bf_out, w_bf, z_even, z_odd, *, last_is_even):
    i = pl.program_id(0)
    n_tiles = pl.num_programs(0) - 1

    wout_bf_out[...] = wout_ref[...].astype(BF16)

    def project(z_dst):
        h = _rms(x_ref[...], g_ref[...]).astype(BF16)
        z_dst[...] = jnp.dot(h, w_bf[...], preferred_element_type=F32)

    def finish(z_src):
        cos_t, sin_t = c_ref[...], s_ref[...]
        lane = lax.broadcasted_iota(jnp.int32, (1, LANES), 1)
        head0 = lane < HEAD_DIM

        def head_norm_rope(z, gain):
            sq = z * z
            first = jnp.sum(jnp.where(head0, sq, 0.0), axis=-1, keepdims=True)
            both = jnp.sum(sq, axis=-1, keepdims=True)
            ss = jnp.where(head0, first, both - first)
            zn = z * lax.rsqrt(ss * (1.0 / HEAD_DIM) + EPS) * gain
            partner = jnp.where((lane & (ROT_DIM // 2)) == 0,
                                pltpu.roll(zn, LANES - ROT_DIM // 2, 1), pltpu.roll(zn, ROT_DIM // 2, 1))
            return zn * cos_t + partner * sin_t

        q_gain = qg_ref[...] * (LOG2E / math.sqrt(HEAD_DIM))
        for b in range(ATTN_WIDTH // LANES):
            q_out[b] = head_norm_rope(z_src[:, Q_OFF + b * LANES:Q_OFF + (b + 1) * LANES], q_gain).astype(BF16)

        k = head_norm_rope(z_src[:, K_OFF:V_OFF], kg_ref[...])
        v = z_src[:, V_OFF:GM_OFF]
        kv_out[:, 0 * LANES:1 * LANES] = k.astype(BF16)
        kv_out[:, 1 * LANES:2 * LANES] = pltpu.roll(k, HEAD_DIM, 1).astype(BF16)
        kv_out[:, 2 * LANES:3 * LANES] = v.astype(BF16)
        kv_out[:, 3 * LANES:4 * LANES] = pltpu.roll(v, HEAD_DIM, 1).astype(BF16)

        zuv = z_src[:, GM_OFF:CONV_OFF]
        uv = 0.5 * zuv * (1.0 + lax.erf(zuv * SQRT_HALF))
        u_out[...] = uv[:, 0:GM_WIDTH]
        vln_out[...] = _layer_norm(uv[:, GM_WIDTH:], lng_ref[...], lnb_ref[...]).astype(BF16)

        glu_out[...] = z_src[:, CONV_OFF:CONV_OFF + CONV_CH] * jax.nn.sigmoid(z_src[:, CONV_OFF + CONV_CH:])

    inner = jnp.logical_and(i > 0, i < n_tiles)

    @pl.when(i == 0)
    def _():
        w_bf[...] = w_ref[...].astype(BF16)
        project(z_even)

    @pl.when(jnp.logical_and(inner, i % 2 == 1))
    def _():
        finish(z_even)
        project(z_odd)

    @pl.when(jnp.logical_and(inner, i % 2 == 0))
    def _():
        finish(z_odd)
        project(z_even)

    @pl.when(i == n_tiles)
    def _():
        finish(z_even if last_is_even else z_odd)


def _inproj(layer, x, g, w, qg, kg, lng, lnb, tabs, wout):
    n = x.shape[0]
    tm = TM_PROJ
    n_tiles = n // tm
    cur = lambda width: pl.BlockSpec((tm, width), lambda i: (jnp.minimum(i, n_tiles - 1), 0))
    lag = lambda width: pl.BlockSpec((tm, width), lambda i: (jnp.maximum(i - 1, 0), 0))
    out_shape = (
        jax.ShapeDtypeStruct((ATTN_WIDTH // LANES, n, LANES), BF16),
        jax.ShapeDtypeStruct((n, 4 * LANES), BF16),
        jax.ShapeDtypeStruct((n, GM_WIDTH), F32),
        jax.ShapeDtypeStruct((n, GM_WIDTH), BF16),
        jax.ShapeDtypeStruct((n, CONV_CH), F32),
        jax.ShapeDtypeStruct((D_MODEL, D_MODEL), BF16),
    )
    wout_src, wout_dst = _cast_specs(layer, D_MODEL, D_MODEL, n_tiles + 1, lambda i: i)
    return pl.pallas_call(
        functools.partial(_inproj_kernel, last_is_even=(n_tiles - 1) % 2 == 0),
        out_shape=out_shape,
        grid=(n_tiles + 1,),
        in_specs=[cur(D_MODEL), _layer_spec(layer, (1, D_MODEL)), _layer_spec(layer, (D_MODEL, IN_COLS)),
                  _layer_spec(layer, (1, LANES)), _layer_spec(layer, (1, LANES)),
                  _layer_spec(layer, (1, GM_WIDTH)), _layer_spec(layer, (1, GM_WIDTH)),
                  lag(LANES), lag(LANES), wout_src],
        out_specs=(pl.BlockSpec((ATTN_WIDTH // LANES, tm, LANES), lambda i: (0, jnp.maximum(i - 1, 0), 0)),
                   lag(4 * LANES), lag(GM_WIDTH), lag(GM_WIDTH), lag(CONV_CH), wout_dst),
        scratch_shapes=[pltpu.VMEM((D_MODEL, IN_COLS), BF16),
                        pltpu.VMEM((tm, IN_COLS), F32), pltpu.VMEM((tm, IN_COLS), F32)],
        compiler_params=pltpu.CompilerParams(dimension_semantics=("arbitrary",),
                                             vmem_limit_bytes=VMEM_LIMIT),
        name="in_proj",
    )(x, g, w, qg, kg, lng, lnb, *tabs, wout)


def _mixer_kernel(layer, sink_ref, x_ref, q_ref, kvp_ref, kvc_ref, kvn_ref, u_ref, vln_ref,
                  glup_ref, gluc_ref, glun_ref,
                  ws_ref, gmb_ref, cw_ref, cb_ref, clg_ref, clb_ref, og_ref, wout_ref,
                  wgu_ref, wd_ref, wpg_ref, wpp_ref,
                  o_ref, wgu_bf_out, wd_bf_out, wpg_bf_out, wpp_bf_out,
                  ka_s, kb_s, va_s, vb_s, bias_s, s_s, p_s, scale_s, glu_s, y_s, attn_s, gm_s, merged_s):
    tq = x_ref.shape[0]
    nblk = tq // BLOCK
    j = pl.program_id(1)
    first = j == 0
    last = j == pl.num_programs(1) - 1

    wgu_bf_out[...] = wgu_ref[...].astype(BF16)
    wd_bf_out[...] = wd_ref[...].astype(BF16)
    wpg_bf_out[...] = wpg_ref[...].astype(BF16)
    wpp_bf_out[...] = wpp_ref[...].astype(BF16)

    lane = lax.broadcasted_iota(jnp.int32, (1, LANES), 1)
    head0 = lane < HEAD_DIM
    zero = jnp.zeros((), BF16)
    for src, r0, rows in ((kvp_ref, 0, BLOCK), (kvc_ref, BLOCK, tq), (kvn_ref, BLOCK + tq, BLOCK)):
        dst = slice(r0, r0 + rows)
        k, k_sw = src[:, 0 * LANES:1 * LANES], src[:, 1 * LANES:2 * LANES]
        v, v_sw = src[:, 2 * LANES:3 * LANES], src[:, 3 * LANES:4 * LANES]
        ka_s[0, dst, :] = jnp.where(head0, k, zero)
        kb_s[0, dst, :] = jnp.where(head0, zero, k_sw)
        ka_s[1, dst, :] = jnp.where(head0, k_sw, zero)
        kb_s[1, dst, :] = jnp.where(head0, zero, k)
        va_s[0, dst, :] = jnp.where(head0, v, zero)
        vb_s[0, dst, :] = jnp.where(head0, zero, v_sw)
        va_s[1, dst, :] = jnp.where(head0, v_sw, zero)
        vb_s[1, dst, :] = jnp.where(head0, zero, v)

    qi = lax.broadcasted_iota(jnp.int32, (BLOCK, BLOCK), 0)
    kj = lax.broadcasted_iota(jnp.int32, (BLOCK, BLOCK), 1)
    band_lo = jnp.where(kj >= qi, 0.0, NEG_INF)
    band_hi = jnp.where(kj <= qi, 0.0, NEG_INF)
    bias_s[0] = band_lo
    bias_s[1] = band_hi
    bias_s[2] = jnp.where(first, NEG_INF, band_lo)
    bias_s[3] = jnp.where(last, NEG_INF, band_hi)

    n_pairs = ATTN_GROUP // 2

    def scores(nb, h):
        r0 = pl.multiple_of(nb * BLOCK, BLOCK)
        q_st = jnp.concatenate([q_ref[n_pairs * h + pp, pl.ds(r0, BLOCK), :] for pp in range(n_pairs)], axis=0)
        k_st = jnp.concatenate([ka_s[h, pl.ds(r0, 3 * BLOCK), :], kb_s[h, pl.ds(r0, 3 * BLOCK), :]], axis=0)
        s_s[h] = lax.dot_general(q_st, k_st, (((1,), (1,)), ((), ())),
                                 preferred_element_type=F32)

    def softmax_pv(nb, h):
        r0 = pl.multiple_of(nb * BLOCK, BLOCK)
        lo_idx = jnp.where(nb == 0, 2, 0)
        hi_idx = jnp.where(nb == nblk - 1, 3, 1)
        v_st = jnp.concatenate([va_s[h, pl.ds(r0, 3 * BLOCK), :], vb_s[h, pl.ds(r0, 3 * BLOCK), :]], axis=0)
        for pp in range(n_pairs):
            for rc in range(0, BLOCK, SOFTMAX_ROWS):
                rr = slice(pp * BLOCK + rc, pp * BLOCK + rc + SOFTMAX_ROWS)
                br = slice(rc, rc + SOFTMAX_ROWS)
                inv = []
                for half in range(2):
                    sink = sink_ref[layer * ATTN_Q_HEADS + ATTN_GROUP * h + 2 * pp + half] * LOG2E
                    c0 = half * 3 * BLOCK
                    s0 = s_s[h, rr, c0:c0 + BLOCK] + bias_s[lo_idx, br, :]
                    s1 = s_s[h, rr, c0 + BLOCK:c0 + 2 * BLOCK]
                    s2 = s_s[h, rr, c0 + 2 * BLOCK:c0 + 3 * BLOCK] + bias_s[hi_idx, br, :]
                    m = jnp.max(jnp.maximum(jnp.maximum(s0, s1), s2), axis=-1, keepdims=True)
                    m = jnp.maximum(m, sink)
                    e0, e1, e2 = jnp.exp2(s0 - m), jnp.exp2(s1 - m), jnp.exp2(s2 - m)
                    denom = jnp.sum(e0 + e1 + e2, axis=-1, keepdims=True) + jnp.exp2(sink - m)
                    p_s[h, rr, c0:c0 + BLOCK] = e0.astype(BF16)
                    p_s[h, rr, c0 + BLOCK:c0 + 2 * BLOCK] = e1.astype(BF16)
                    p_s[h, rr, c0 + 2 * BLOCK:c0 + 3 * BLOCK] = e2.astype(BF16)
                    inv.append(1.0 / denom)
                scale_s[h, rr, :] = jnp.where(head0, inv[0], inv[1])
        o = jnp.dot(p_s[h], v_st, preferred_element_type=F32) * scale_s[h]
        for pp in range(n_pairs):
            attn_s[n_pairs * h + pp, pl.ds(r0, BLOCK), :] = o[pp * BLOCK:(pp + 1) * BLOCK, :]

    scores(0, 0)

    def attn_step(nb, carry):
        scores(nb, 1)
        softmax_pv(nb, 0)
        scores(jnp.minimum(nb + 1, nblk - 1), 0)
        softmax_pv(nb, 1)
        return carry

    lax.fori_loop(0, nblk, attn_step, 0)

    for nb in range(nblk):
        rows = slice(nb * BLOCK, (nb + 1) * BLOCK)
        for hp in range(GM_HEADS // 2):
            cols = slice(hp * LANES, (hp + 1) * LANES)
            vblk = vln_ref[rows, cols]
            lhs = jnp.concatenate([ws_ref[2 * hp], ws_ref[2 * hp + 1]], axis=1)
            rhs = jnp.concatenate([jnp.where(head0, vblk, zero), jnp.where(head0, zero, vblk)], axis=0)
            sgate = jnp.dot(lhs, rhs, preferred_element_type=F32) + gmb_ref[:, cols]
            gm_s[rows, cols] = u_ref[rows, cols] * sgate

    n_ch = CONV_CH // LANES
    for ch in range(n_ch):
        cl = slice(ch * LANES, (ch + 1) * LANES)
        glu_s[ch, 0:CONV_HALO, :] = jnp.where(first, 0.0, glup_ref[:, cl])
        glu_s[ch, CONV_HALO:CONV_HALO + tq, :] = gluc_ref[:, cl]
        glu_s[ch, CONV_HALO + tq:, :] = jnp.where(last, 0.0, glun_ref[:, cl])
    group = SUBLANES * CONV_ROW_STRIDE

    def conv_step(i, carry):
        base = pl.multiple_of(i * (CONV_GROUPS * group), CONV_GROUPS * group)
        for ch in range(n_ch):
            cl = slice(ch * LANES, (ch + 1) * LANES)
            offs = [g * group + ph for g in range(CONV_GROUPS) for ph in range(CONV_ROW_STRIDE)]
            accs = [jnp.broadcast_to(cb_ref[:, cl], (SUBLANES, LANES)) for _ in offs]
            for t in range(CONV_WIDTH):
                w = cw_ref[t:t + 1, cl]
                for a, off in enumerate(offs):
                    start = base + (off + CONV_HALO - CONV_PAD + t)
                    accs[a] = accs[a] + w * glu_s[ch, pl.ds(start, SUBLANES, stride=CONV_ROW_STRIDE), :]
            for a, off in enumerate(offs):
                y_s[ch, pl.ds(base + off, SUBLANES, stride=CONV_ROW_STRIDE), :] = accs[a]
        return carry

    lax.fori_loop(0, tq // (CONV_GROUPS * group), conv_step, 0)

    rchunk = 128
    for r0 in range(0, tq, rchunk):
        rs = slice(r0, r0 + rchunk)
        y = jnp.concatenate([y_s[ch, rs, :] for ch in range(n_ch)], axis=1)
        y = _layer_norm(y, clg_ref[...], clb_ref[...])
        y = y * jax.nn.sigmoid(y)
        merged_s[rs, ATTN_WIDTH + GM_WIDTH:] = _rms(y, og_ref[:, ATTN_WIDTH + GM_WIDTH:]).astype(BF16)
        attn = jnp.concatenate([attn_s[pb, rs, :] for pb in range(ATTN_WIDTH // LANES)], axis=1)
        merged_s[rs, 0:ATTN_WIDTH] = _rms(attn, og_ref[:, 0:ATTN_WIDTH]).astype(BF16)
        merged_s[rs, ATTN_WIDTH:ATTN_WIDTH + GM_WIDTH] = _rms(
            gm_s[rs, :], og_ref[:, ATTN_WIDTH:ATTN_WIDTH + GM_WIDTH]).astype(BF16)
    o_ref[...] = x_ref[...] + jnp.dot(merged_s[...], wout_ref[...], preferred_element_type=F32)


def _mixer(layer, x, q, kv, u, vln, glu, sink, ws, gm_bias, cw, cb, clg, clb, og, wout,
           wgu, wd, wpg, wpp, batch, seq):
    n = x.shape[0]
    tq = TQ_MIX
    tiles = seq // tq
    blk_per_tile = tq // BLOCK
    blk_per_seq = seq // BLOCK
    halo_per_tile = tq // CONV_HALO
    halo_per_seq = seq // CONV_HALO

    def row(width):
        return pl.BlockSpec((tq, width), lambda b, j, *_: (b * tiles + j, 0))

    def prev(rows, width, per_tile, per_seq):
        return pl.BlockSpec(
            (rows, width), lambda b, j, *_: (b * per_seq + jnp.maximum(j * per_tile - 1, 0), 0))

    def nxt(rows, width, per_tile, per_seq):
        return pl.BlockSpec(
            (rows, width), lambda b, j, *_: (b * per_seq + jnp.minimum((j + 1) * per_tile, per_seq - 1), 0))

    const = lambda shape: _layer_spec(layer, shape)
    kv_rows = tq + 2 * BLOCK
    step_of = lambda b, j, *_: b * tiles + j
    casts = [_cast_specs(layer, rows, cols, batch * tiles, step_of)
             for rows, cols in ((D_MODEL, 2 * D_FF), (D_FF, D_MODEL), (D_MODEL, D_MODEL), (PLE_DIM, D_MODEL))]
    grid_spec = pltpu.PrefetchScalarGridSpec(
        num_scalar_prefetch=1,
        grid=(batch, tiles),
        in_specs=[
            row(D_MODEL), pl.BlockSpec((ATTN_WIDTH // LANES, tq, LANES), lambda b, j, *_: (0, b * tiles + j, 0)),
            prev(BLOCK, 4 * LANES, blk_per_tile, blk_per_seq), row(4 * LANES),
            nxt(BLOCK, 4 * LANES, blk_per_tile, blk_per_seq),
            row(GM_WIDTH), row(GM_WIDTH),
            prev(CONV_HALO, CONV_CH, halo_per_tile, halo_per_seq), row(CONV_CH),
            nxt(CONV_HALO, CONV_CH, halo_per_tile, halo_per_seq),
            const((GM_HEADS, CHUNK, CHUNK)), const((CHUNK, GM_WIDTH)),
            const((CONV_WIDTH, CONV_CH)), const((1, CONV_CH)), const((1, CONV_CH)), const((1, CONV_CH)),
            const((1, D_MODEL)), _resident_spec((D_MODEL, D_MODEL)),
            *[src for src, _ in casts],
        ],
        out_specs=(row(D_MODEL), *[dst for _, dst in casts]),
        scratch_shapes=[
            pltpu.VMEM((ATTN_KV_HEADS, kv_rows, LANES), BF16), pltpu.VMEM((ATTN_KV_HEADS, kv_rows, LANES), BF16),
            pltpu.VMEM((ATTN_KV_HEADS, kv_rows, LANES), BF16), pltpu.VMEM((ATTN_KV_HEADS, kv_rows, LANES), BF16),
            pltpu.VMEM((4, BLOCK, BLOCK), F32),
            pltpu.VMEM((2, ATTN_GROUP // 2 * BLOCK, 2 * 3 * BLOCK), F32),
            pltpu.VMEM((2, ATTN_GROUP // 2 * BLOCK, 2 * 3 * BLOCK), BF16),
            pltpu.VMEM((2, ATTN_GROUP // 2 * BLOCK, LANES), F32),
            pltpu.VMEM((CONV_CH // LANES, tq + 2 * CONV_HALO, LANES), F32),
            pltpu.VMEM((CONV_CH // LANES, tq, LANES), F32),
            pltpu.VMEM((ATTN_WIDTH // LANES, tq, LANES), F32), pltpu.VMEM((tq, GM_WIDTH), F32),
            pltpu.VMEM((tq, D_MODEL), BF16),
        ],
    )
    return pl.pallas_call(
        functools.partial(_mixer_kernel, layer),
        out_shape=(jax.ShapeDtypeStruct((n, D_MODEL), F32),
                   jax.ShapeDtypeStruct((D_MODEL, 2 * D_FF), BF16), jax.ShapeDtypeStruct((D_FF, D_MODEL), BF16),
                   jax.ShapeDtypeStruct((D_MODEL, D_MODEL), BF16), jax.ShapeDtypeStruct((PLE_DIM, D_MODEL), BF16)),
        grid_spec=grid_spec,
        compiler_params=pltpu.CompilerParams(dimension_semantics=("arbitrary", "arbitrary"),
                                             vmem_limit_bytes=VMEM_LIMIT),
        name="mixer",
    )(sink, x, q, kv, kv, kv, u, vln, glu, glu, glu, ws, gm_bias, cw, cb, clg, clb, og, wout,
      wgu, wd, wpg, wpp)


def _ffn_kernel(x_ref, p_ref, gf_ref, wgu_ref, wd_ref, gp_ref, wpg_ref, wpp_ref, o_ref, act_s):
    x = x_ref[...]
    hn = _rms(x, gf_ref[...]).astype(BF16)
    for c0 in range(0, D_FF, FF_CHUNK):
        gate = jnp.dot(hn, wgu_ref[:, c0:c0 + FF_CHUNK], preferred_element_type=F32)
        up = jnp.dot(hn, wgu_ref[:, D_FF + c0:D_FF + c0 + FF_CHUNK], preferred_element_type=F32)
        act_s[:, c0:c0 + FF_CHUNK] = (gate * jax.nn.sigmoid(gate) * up).astype(BF16)
    x = x + jnp.dot(act_s[...], wd_ref[...], preferred_element_type=F32)

    hp = _rms(x, gp_ref[...]).astype(BF16)
    gate = jax.nn.sigmoid(jnp.dot(hp, wpg_ref[...], preferred_element_type=F32))
    proj = jnp.dot(p_ref[...].astype(BF16), wpp_ref[...], preferred_element_type=F32)
    o_ref[...] = x + proj * gate


def _ffn(layer, x, p, gf, wgu, wd, gp, wpg, wpp):
    n = x.shape[0]
    tm = TM_FFN
    row = lambda width: pl.BlockSpec((tm, width), lambda i: (i, 0))
    return pl.pallas_call(
        _ffn_kernel,
        out_shape=jax.ShapeDtypeStruct((n, D_MODEL), F32),
        grid=(n // tm,),
        in_specs=[row(D_MODEL), pl.BlockSpec((None, tm, PLE_DIM), lambda i: (layer, i, 0)),
                  _layer_spec(layer, (1, D_MODEL)),
                  _resident_spec((D_MODEL, 2 * D_FF)), _resident_spec((D_FF, D_MODEL)),
                  _layer_spec(layer, (1, D_MODEL)), _resident_spec((D_MODEL, D_MODEL)),
                  _resident_spec((PLE_DIM, D_MODEL))],
        out_specs=row(D_MODEL),
        scratch_shapes=[pltpu.VMEM((tm, D_FF), BF16)],
        compiler_params=pltpu.CompilerParams(dimension_semantics=("arbitrary",),
                                             vmem_limit_bytes=VMEM_LIMIT),
        name="ffn_ple",
    )(x, p, gf, wgu, wd, gp, wpg, wpp)


def kernel(x, p, positions, norm_mix_g, w_in, q_norm_g, k_norm_g, sink, gm_ln_g, gm_ln_b, gm_ws, gm_bs, conv_w, conv_b, conv_ln_g, conv_ln_b, out_norm_g, w_out, norm_ffn_g, w_gate_up, w_down, ple_norm_g, w_ple_gate, w_ple_proj):
    batch, seq, d = x.shape
    depth = w_in.shape[0]
    n = batch * seq
    assert d == D_MODEL and seq % TQ_MIX == 0
    assert n % TM_PROJ == 0 and n % TM_FFN == 0 and n % TM_ROPE == 0

    vec = lambda a: a[:, None, :]
    ws_b = gm_ws.astype(BF16)
    qg = vec(jnp.tile(q_norm_g, (1, LANES // HEAD_DIM)))
    kg = vec(jnp.tile(k_norm_g, (1, LANES // HEAD_DIM)))
    gm_bias = jnp.repeat(jnp.swapaxes(gm_bs, 1, 2), HEAD_DIM, axis=2)
    sink_flat = sink.reshape(depth * ATTN_Q_HEADS)
    p_flat = p.reshape(depth, n, PLE_DIM)

    tabs = _rope_tables(positions.reshape(1, n))
    xf = x.reshape(n, d)
    for i in range(depth):
        q, kv, u, vln, glu, w_out_b = _inproj(i, xf, vec(norm_mix_g), w_in, qg, kg, vec(gm_ln_g),
                                                 vec(gm_ln_b), tabs, w_out)
        xf, w_gu_b, w_down_b, w_pg_b, w_pp_b = _mixer(
            i, xf, q, kv, u, vln, glu, sink_flat, ws_b, gm_bias, conv_w, vec(conv_b), vec(conv_ln_g),
            vec(conv_ln_b), vec(out_norm_g), w_out_b, w_gate_up, w_down, w_ple_gate, w_ple_proj, batch, seq)
        xf = _ffn(i, xf, p_flat, vec(norm_ffn_g), w_gu_b, w_down_b, vec(ple_norm_g), w_pg_b, w_pp_b)
    return xf.reshape(batch, seq, d)
```

```python
import functools
import math

import numpy as np
import jax
import jax.numpy as jnp
from jax import lax
from jax.experimental import pallas as pl
from jax.experimental.pallas import tpu as pltpu

F32 = jnp.float32
BF16 = jnp.bfloat16

D_MODEL = 1024
HEAD_DIM = 64
ATTN_WIDTH = 512
ATTN_Q_HEADS = 8
ATTN_KV_HEADS = 2
ATTN_GROUP = 4
KV_WIDTH = 128
WINDOW = 128
BLOCK = 128
ROPE_THETA = 500000.0
ROT_DIM = 16
GM_WIDTH = 256
GM_HEADS = 4
CHUNK = 128
CONV_CH = 256
CONV_WIDTH = 31
CONV_PAD = 15
D_FF = 2816
PLE_DIM = 256
EPS = 1e-6
NEG_INF = -1e30

Q_OFF = 0
K_OFF = Q_OFF + ATTN_WIDTH
V_OFF = K_OFF + KV_WIDTH
GM_OFF = V_OFF + KV_WIDTH
CONV_OFF = GM_OFF + 2 * GM_WIDTH
IN_COLS = CONV_OFF + 2 * CONV_CH

LANES = 128
SUBLANES = 8
BF16_ROWS = 16
CONV_HALO = 16
CONV_ROW_STRIDE = 2
CONV_GROUPS = 4
VMEM_BYTES_V7X = 64 * 1024 * 1024
VMEM_LIMIT = VMEM_BYTES_V7X * 7 // 8

TM_ROPE = 8192
TM_PROJ = 512
TQ_MIX = 1024
TM_FFN = 1024
FF_CHUNK = 256
SOFTMAX_ROWS = 32
ATTN_UNROLL = 2

INV_FREQ = [float(ROPE_THETA ** (-(2.0 * j) / ROT_DIM)) for j in range(ROT_DIM // 2)]
SQRT_HALF = float(np.sqrt(0.5))
LOG2E = float(np.log2(np.e))


def _layer_spec(layer, shape):
    nd = len(shape)
    return pl.BlockSpec((None,) + tuple(shape), lambda *_: (layer,) + (0,) * nd,
                        pipeline_mode=pl.Buffered(1))


def _resident_spec(shape):
    nd = len(shape)
    return pl.BlockSpec(tuple(shape), lambda *_: (0,) * nd, pipeline_mode=pl.Buffered(1))


def _cast_block_rows(rows, n_steps):
    r = BF16_ROWS
    while r * n_steps < rows or rows % r:
        r += BF16_ROWS
    return r


def _cast_specs(layer, rows, cols, n_steps, step_of):
    r = _cast_block_rows(rows, n_steps)
    last = rows // r - 1
    src = pl.BlockSpec((None, r, cols), lambda *idx: (layer, jnp.minimum(step_of(*idx), last), 0))
    dst = pl.BlockSpec((r, cols), lambda *idx: (jnp.minimum(step_of(*idx), last), 0))
    return src, dst


def _split_bf16(x):
    hi = x.astype(BF16)
    lo = (x - hi.astype(F32)).astype(BF16)
    return hi, lo


def _rope_table_kernel(pos_ref, c_ref, s_ref):
    half = ROT_DIM // 2
    pos = pos_ref[...].astype(F32)
    row = lax.broadcasted_iota(jnp.int32, (ROT_DIM, 1), 0)
    inv = jnp.zeros((ROT_DIM, 1), F32)
    for j, f in enumerate(INV_FREQ):
        inv = jnp.where((row & (half - 1)) == j, f, inv)
    ang = inv * pos
    cs = jnp.where(row < half, jnp.cos(ang), jnp.sin(ang))

    r = lax.broadcasted_iota(jnp.int32, (ROT_DIM, 2 * LANES), 0)
    col = lax.broadcasted_iota(jnp.int32, (ROT_DIM, 2 * LANES), 1)
    table = col // LANES
    in_head = col & (HEAD_DIM - 1)
    freq = col & (half - 1)
    put_cos = (table == 0) & (in_head < ROT_DIM) & (r == freq)
    put_nsin = (table == 1) & (in_head < half) & (r == freq + half)
    put_sin = (table == 1) & (in_head >= half) & (in_head < ROT_DIM) & (r == freq + half)
    place = jnp.where(put_cos | put_sin, 1.0, jnp.where(put_nsin, -1.0, 0.0)).astype(BF16)

    hi, lo = _split_bf16(cs)
    dn = (((0,), (0,)), ((), ()))
    t = (lax.dot_general(hi, place, dn, preferred_element_type=F32)
         + lax.dot_general(lo, place, dn, preferred_element_type=F32))
    lane = lax.broadcasted_iota(jnp.int32, (1, LANES), 1)
    c_ref[...] = jnp.where((lane & (HEAD_DIM - 1)) < ROT_DIM, t[:, 0:LANES], 1.0)
    s_ref[...] = t[:, LANES:2 * LANES]


def _rope_tables(pos_row):
    n = pos_row.shape[1]
    tm = TM_ROPE
    tab = jax.ShapeDtypeStruct((n, LANES), F32)
    spec = pl.BlockSpec((tm, LANES), lambda i: (i, 0))
    return pl.pallas_call(
        _rope_table_kernel,
        out_shape=(tab, tab),
        grid=(n // tm,),
        in_specs=[pl.BlockSpec((1, tm), lambda i: (0, i))],
        out_specs=(spec, spec),
        name="rope_tables",
    )(pos_row)


def _rms(x, gain):
    ms = jnp.mean(x * x, axis=-1, keepdims=True)
    return x * lax.rsqrt(ms + EPS) * gain


def _layer_norm(x, gain, bias):
    mu = jnp.mean(x, axis=-1, keepdims=True)
    xc = x - mu
    var = jnp.mean(xc * xc, axis=-1, keepdims=True)
    return xc * lax.rsqrt(var + EPS) * gain + bias


def _inproj_kernel(x_ref, g_ref, w_ref, qg_ref, kg_ref, lng_ref, lnb_ref,
                   c_ref, s_ref, wout_ref,
                   q_out, kv_out, u_out, vln_out, glu_out, wout_bf_out, w_bf, z_even, z_odd, *, last_is_even):
    i = pl.program_id(0)
    n_tiles = pl.num_programs(0) - 1

    wout_bf_out[...] = wout_ref[...].astype(BF16)

    def project(z_dst):
        h = _rms(x_ref[...], g_ref[...]).astype(BF16)
        z_dst[...] = jnp.dot(h, w_bf[...], preferred_element_type=F32)

    def finish(z_src):
        cos_t, sin_t = c_ref[...], s_ref[...]
        lane = lax.broadcasted_iota(jnp.int32, (1, LANES), 1)
        head0 = lane < HEAD_DIM

        def head_norm_rope(z, gain):
            sq = z * z
            first = jnp.sum(jnp.where(head0, sq, 0.0), axis=-1, keepdims=True)
            both = jnp.sum(sq, axis=-1, keepdims=True)
            ss = jnp.where(head0, first, both - first)
            zn = z * lax.rsqrt(ss * (1.0 / HEAD_DIM) + EPS) * gain
            partner = jnp.where((lane & (ROT_DIM // 2)) == 0,
                                pltpu.roll(zn, LANES - ROT_DIM // 2, 1), pltpu.roll(zn, ROT_DIM // 2, 1))
            return zn * cos_t + partner * sin_t

        q_gain = qg_ref[...] * (LOG2E / math.sqrt(HEAD_DIM))
        for b in range(ATTN_WIDTH // LANES):
            q_out[b] = head_norm_rope(z_src[:, Q_OFF + b * LANES:Q_OFF + (b + 1) * LANES], q_gain).astype(BF16)

        k = head_norm_rope(z_src[:, K_OFF:V_OFF], kg_ref[...])
        v = z_src[:, V_OFF:GM_OFF]
        kv_out[:, 0 * LANES:1 * LANES] = k.astype(BF16)
        kv_out[:, 1 * LANES:2 * LANES] = pltpu.roll(k, HEAD_DIM, 1).astype(BF16)
        kv_out[:, 2 * LANES:3 * LANES] = v.astype(BF16)
        kv_out[:, 3 * LANES:4 * LANES] = pltpu.roll(v, HEAD_DIM, 1).astype(BF16)

        zuv = z_src[:, GM_OFF:CONV_OFF]
        uv = 0.5 * zuv * (1.0 + lax.erf(zuv * SQRT_HALF))
        u_out[...] = uv[:, 0:GM_WIDTH]
        vln_out[...] = _layer_norm(uv[:, GM_WIDTH:], lng_ref[...], lnb_ref[...]).astype(BF16)

        glu_out[...] = z_src[:, CONV_OFF:CONV_OFF + CONV_CH] * jax.nn.sigmoid(z_src[:, CONV_OFF + CONV_CH:])

    inner = jnp.logical_and(i > 0, i < n_tiles)

    @pl.when(i == 0)
    def _():
        w_bf[...] = w_ref[...].astype(BF16)
        project(z_even)

    @pl.when(jnp.logical_and(inner, i % 2 == 1))
    def _():
        finish(z_even)
        project(z_odd)

    @pl.when(jnp.logical_and(inner, i % 2 == 0))
    def _():
        finish(z_odd)
        project(z_even)

    @pl.when(i == n_tiles)
    def _():
        finish(z_even if last_is_even else z_odd)


def _inproj(layer, x, g, w, qg, kg, lng, lnb, tabs, wout):
    n = x.shape[0]
    tm = TM_PROJ
    n_tiles = n // tm
    cur = lambda width: pl.BlockSpec((tm, width), lambda i: (jnp.minimum(i, n_tiles - 1), 0))
    lag = lambda width: pl.BlockSpec((tm, width), lambda i: (jnp.maximum(i - 1, 0), 0))
    out_shape = (
        jax.ShapeDtypeStruct((ATTN_WIDTH // LANES, n, LANES), BF16),
        jax.ShapeDtypeStruct((n, 4 * LANES), BF16),
        jax.ShapeDtypeStruct((n, GM_WIDTH), F32),
        jax.ShapeDtypeStruct((n, GM_WIDTH), BF16),
        jax.ShapeDtypeStruct((n, CONV_CH), F32),
        jax.ShapeDtypeStruct((D_MODEL, D_MODEL), BF16),
    )
    wout_src, wout_dst = _cast_specs(layer, D_MODEL, D_MODEL, n_tiles + 1, lambda i: i)
    return pl.pallas_call(
        functools.partial(_inproj_kernel, last_is_even=(n_tiles - 1) % 2 == 0),
        out_shape=out_shape,
        grid=(n_tiles + 1,),
        in_specs=[cur(D_MODEL), _layer_spec(layer, (1, D_MODEL)), _layer_spec(layer, (D_MODEL, IN_COLS)),
                  _layer_spec(layer, (1, LANES)), _layer_spec(layer, (1, LANES)),
                  _layer_spec(layer, (1, GM_WIDTH)), _layer_spec(layer, (1, GM_WIDTH)),
                  lag(LANES), lag(LANES), wout_src],
        out_specs=(pl.BlockSpec((ATTN_WIDTH // LANES, tm, LANES), lambda i: (0, jnp.maximum(i - 1, 0), 0)),
                   lag(4 * LANES), lag(GM_WIDTH), lag(GM_WIDTH), lag(CONV_CH), wout_dst),
        scratch_shapes=[pltpu.VMEM((D_MODEL, IN_COLS), BF16),
                        pltpu.VMEM((tm, IN_COLS), F32), pltpu.VMEM((tm, IN_COLS), F32)],
        compiler_params=pltpu.CompilerParams(dimension_semantics=("arbitrary",),
                                             vmem_limit_bytes=VMEM_LIMIT),
        name="in_proj",
    )(x, g, w, qg, kg, lng, lnb, *tabs, wout)


def _mixer_kernel(layer, sink_ref, x_ref, q_ref, kvp_ref, kvc_ref, kvn_ref, u_ref, vln_ref,
                  glup_ref, gluc_ref, glun_ref,
                  ws_ref, gmb_ref, cw_ref, cb_ref, clg_ref, clb_ref, og_ref, wout_ref,
                  wgu_ref, wd_ref, wpg_ref, wpp_ref,
                  o_ref, wgu_bf_out, wd_bf_out, wpg_bf_out, wpp_bf_out,
                  ka_s, kb_s, va_s, vb_s, bias_s, s_s, p_s, scale_s, glu_s, y_s, attn_s, gm_s, merged_s):
    tq = x_ref.shape[0]
    nblk = tq // BLOCK
    j = pl.program_id(1)
    first = j == 0
    last = j == pl.num_programs(1) - 1

    wgu_bf_out[...] = wgu_ref[...].astype(BF16)
    wd_bf_out[...] = wd_ref[...].astype(BF16)
    wpg_bf_out[...] = wpg_ref[...].astype(BF16)
    wpp_bf_out[...] = wpp_ref[...].astype(BF16)

    lane = lax.broadcasted_iota(jnp.int32, (1, LANES), 1)
    head0 = lane < HEAD_DIM
    zero = jnp.zeros((), BF16)
    for src, r0, rows in ((kvp_ref, 0, BLOCK), (kvc_ref, BLOCK, tq), (kvn_ref, BLOCK + tq, BLOCK)):
        dst = slice(r0, r0 + rows)
        k, k_sw = src[:, 0 * LANES:1 * LANES], src[:, 1 * LANES:2 * LANES]
        v, v_sw = src[:, 2 * LANES:3 * LANES], src[:, 3 * LANES:4 * LANES]
        ka_s[0, dst, :] = jnp.where(head0, k, zero)
        kb_s[0, dst, :] = jnp.where(head0, zero, k_sw)
        ka_s[1, dst, :] = jnp.where(head0, k_sw, zero)
        kb_s[1, dst, :] = jnp.where(head0, zero, k)
        va_s[0, dst, :] = jnp.where(head0, v, zero)
        vb_s[0, dst, :] = jnp.where(head0, zero, v_sw)
        va_s[1, dst, :] = jnp.where(head0, v_sw, zero)
        vb_s[1, dst, :] = jnp.where(head0, zero, v)

    qi = lax.broadcasted_iota(jnp.int32, (BLOCK, BLOCK), 0)
    kj = lax.broadcasted_iota(jnp.int32, (BLOCK, BLOCK), 1)
    band_lo = jnp.where(kj >= qi, 0.0, NEG_INF)
    band_hi = jnp.where(kj <= qi, 0.0, NEG_INF)
    bias_s[0] = band_lo
    bias_s[1] = band_hi
    bias_s[2] = jnp.where(first, NEG_INF, band_lo)
    bias_s[3] = jnp.where(last, NEG_INF, band_hi)

    n_pairs = ATTN_GROUP // 2

    def scores(nb, h):
        r0 = pl.multiple_of(nb * BLOCK, BLOCK)
        q_st = jnp.concatenate([q_ref[n_pairs * h + pp, pl.ds(r0, BLOCK), :] for pp in range(n_pairs)], axis=0)
        k_st = jnp.concatenate([ka_s[h, pl.ds(r0, 3 * BLOCK), :], kb_s[h, pl.ds(r0, 3 * BLOCK), :]], axis=0)
        s_s[h] = lax.dot_general(q_st, k_st, (((1,), (1,)), ((), ())),
                                 preferred_element_type=F32)

    def softmax_pv(nb, h):
        r0 = pl.multiple_of(nb * BLOCK, BLOCK)
        lo_idx = jnp.where(nb == 0, 2, 0)
        hi_idx = jnp.where(nb == nblk - 1, 3, 1)
        v_st = jnp.concatenate([va_s[h, pl.ds(r0, 3 * BLOCK), :], vb_s[h, pl.ds(r0, 3 * BLOCK), :]], axis=0)
        for pp in range(n_pairs):
            for rc in range(0, BLOCK, SOFTMAX_ROWS):
                rr = slice(pp * BLOCK + rc, pp * BLOCK + rc + SOFTMAX_ROWS)
                br = slice(rc, rc + SOFTMAX_ROWS)
                inv = []
                for half in range(2):
                    sink = sink_ref[layer * ATTN_Q_HEADS + ATTN_GROUP * h + 2 * pp + half] * LOG2E
                    c0 = half * 3 * BLOCK
                    s0 = s_s[h, rr, c0:c0 + BLOCK] + bias_s[lo_idx, br, :]
                    s1 = s_s[h, rr, c0 + BLOCK:c0 + 2 * BLOCK]
                    s2 = s_s[h, rr, c0 + 2 * BLOCK:c0 + 3 * BLOCK] + bias_s[hi_idx, br, :]
                    m = jnp.max(jnp.maximum(jnp.maximum(s0, s1), s2), axis=-1, keepdims=True)
                    m = jnp.maximum(m, sink)
                    e0, e1, e2 = jnp.exp2(s0 - m), jnp.exp2(s1 - m), jnp.exp2(s2 - m)
                    denom = jnp.sum(e0 + e1 + e2, axis=-1, keepdims=True) + jnp.exp2(sink - m)
                    p_s[h, rr, c0:c0 + BLOCK] = e0.astype(BF16)
                    p_s[h, rr, c0 + BLOCK:c0 + 2 * BLOCK] = e1.astype(BF16)
                    p_s[h, rr, c0 + 2 * BLOCK:c0 + 3 * BLOCK] = e2.astype(BF16)
                    inv.append(1.0 / denom)
                scale_s[h, rr, :] = jnp.where(head0, inv[0], inv[1])
        o = jnp.dot(p_s[h], v_st, preferred_element_type=F32) * scale_s[h]
        for pp in range(n_pairs):
            attn_s[n_pairs * h + pp, pl.ds(r0, BLOCK), :] = o[pp * BLOCK:(pp + 1) * BLOCK, :]

    scores(0, 0)

    def attn_step(nb, carry):
        scores(nb, 1)
        softmax_pv(nb, 0)
        scores(jnp.minimum(nb + 1, nblk - 1), 0)
        softmax_pv(nb, 1)
        return carry

    lax.fori_loop(0, nblk, attn_step, 0, unroll=ATTN_UNROLL)

    for nb in range(nblk):
        rows = slice(nb * BLOCK, (nb + 1) * BLOCK)
        for hp in range(GM_HEADS // 2):
            cols = slice(hp * LANES, (hp + 1) * LANES)
            vblk = vln_ref[rows, cols]
            lhs = jnp.concatenate([ws_ref[2 * hp], ws_ref[2 * hp + 1]], axis=1)
            rhs = jnp.concatenate([jnp.where(head0, vblk, zero), jnp.where(head0, zero, vblk)], axis=0)
            sgate = jnp.dot(lhs, rhs, preferred_element_type=F32) + gmb_ref[:, cols]
            gm_s[rows, cols] = u_ref[rows, cols] * sgate

    n_ch = CONV_CH // LANES
    for ch in range(n_ch):
        cl = slice(ch * LANES, (ch + 1) * LANES)
        glu_s[ch, 0:CONV_HALO, :] = jnp.where(first, 0.0, glup_ref[:, cl])
        glu_s[ch, CONV_HALO:CONV_HALO + tq, :] = gluc_ref[:, cl]
        glu_s[ch, CONV_HALO + tq:, :] = jnp.where(last, 0.0, glun_ref[:, cl])
    group = SUBLANES * CONV_ROW_STRIDE

    def conv_step(i, carry):
        base = pl.multiple_of(i * (CONV_GROUPS * group), CONV_GROUPS * group)
        for ch in range(n_ch):
            cl = slice(ch * LANES, (ch + 1) * LANES)
            offs = [g * group + ph for g in range(CONV_GROUPS) for ph in range(CONV_ROW_STRIDE)]
            accs = [jnp.broadcast_to(cb_ref[:, cl], (SUBLANES, LANES)) for _ in offs]
            for t in range(CONV_WIDTH):
                w = cw_ref[t:t + 1, cl]
                for a, off in enumerate(offs):
                    start = base + (off + CONV_HALO - CONV_PAD + t)
                    accs[a] = accs[a] + w * glu_s[ch, pl.ds(start, SUBLANES, stride=CONV_ROW_STRIDE), :]
            for a, off in enumerate(offs):
                y_s[ch, pl.ds(base + off, SUBLANES, stride=CONV_ROW_STRIDE), :] = accs[a]
        return carry

    lax.fori_loop(0, tq // (CONV_GROUPS * group), conv_step, 0)

    rchunk = 128
    for r0 in range(0, tq, rchunk):
        rs = slice(r0, r0 + rchunk)
        y = jnp.concatenate([y_s[ch, rs, :] for ch in range(n_ch)], axis=1)
        y = _layer_norm(y, clg_ref[...], clb_ref[...])
        y = y * jax.nn.sigmoid(y)
        merged_s[rs, ATTN_WIDTH + GM_WIDTH:] = _rms(y, og_ref[:, ATTN_WIDTH + GM_WIDTH:]).astype(BF16)
        attn = jnp.concatenate([attn_s[pb, rs, :] for pb in range(ATTN_WIDTH // LANES)], axis=1)
        merged_s[rs, 0:ATTN_WIDTH] = _rms(attn, og_ref[:, 0:ATTN_WIDTH]).astype(BF16)
        merged_s[rs, ATTN_WIDTH:ATTN_WIDTH + GM_WIDTH] = _rms(
            gm_s[rs, :], og_ref[:, ATTN_WIDTH:ATTN_WIDTH + GM_WIDTH]).astype(BF16)
    o_ref[...] = x_ref[...] + jnp.dot(merged_s[...], wout_ref[...], preferred_element_type=F32)


def _mixer(layer, x, q, kv, u, vln, glu, sink, ws, gm_bias, cw, cb, clg, clb, og, wout,
           wgu, wd, wpg, wpp, batch, seq):
    n = x.shape[0]
    tq = TQ_MIX
    tiles = seq // tq
    blk_per_tile = tq // BLOCK
    blk_per_seq = seq // BLOCK
    halo_per_tile = tq // CONV_HALO
    halo_per_seq = seq // CONV_HALO

    def row(width):
        return pl.BlockSpec((tq, width), lambda b, j, *_: (b * tiles + j, 0))

    def prev(rows, width, per_tile, per_seq):
        return pl.BlockSpec(
            (rows, width), lambda b, j, *_: (b * per_seq + jnp.maximum(j * per_tile - 1, 0), 0))

    def nxt(rows, width, per_tile, per_seq):
        return pl.BlockSpec(
            (rows, width), lambda b, j, *_: (b * per_seq + jnp.minimum((j + 1) * per_tile, per_seq - 1), 0))

    const = lambda shape: _layer_spec(layer, shape)
    kv_rows = tq + 2 * BLOCK
    step_of = lambda b, j, *_: b * tiles + j
    casts = [_cast_specs(layer, rows, cols, batch * tiles, step_of)
             for rows, cols in ((D_MODEL, 2 * D_FF), (D_FF, D_MODEL), (D_MODEL, D_MODEL), (PLE_DIM, D_MODEL))]
    grid_spec = pltpu.PrefetchScalarGridSpec(
        num_scalar_prefetch=1,
        grid=(batch, tiles),
        in_specs=[
            row(D_MODEL), pl.BlockSpec((ATTN_WIDTH // LANES, tq, LANES), lambda b, j, *_: (0, b * tiles + j, 0)),
            prev(BLOCK, 4 * LANES, blk_per_tile, blk_per_seq), row(4 * LANES),
            nxt(BLOCK, 4 * LANES, blk_per_tile, blk_per_seq),
            row(GM_WIDTH), row(GM_WIDTH),
            prev(CONV_HALO, CONV_CH, halo_per_tile, halo_per_seq), row(CONV_CH),
            nxt(CONV_HALO, CONV_CH, halo_per_tile, halo_per_seq),
            const((GM_HEADS, CHUNK, CHUNK)), const((CHUNK, GM_WIDTH)),
            const((CONV_WIDTH, CONV_CH)), const((1, CONV_CH)), const((1, CONV_CH)), const((1, CONV_CH)),
            const((1, D_MODEL)), _resident_spec((D_MODEL, D_MODEL)),
            *[src for src, _ in casts],
        ],
        out_specs=(row(D_MODEL), *[dst for _, dst in casts]),
        scratch_shapes=[
            pltpu.VMEM((ATTN_KV_HEADS, kv_rows, LANES), BF16), pltpu.VMEM((ATTN_KV_HEADS, kv_rows, LANES), BF16),
            pltpu.VMEM((ATTN_KV_HEADS, kv_rows, LANES), BF16), pltpu.VMEM((ATTN_KV_HEADS, kv_rows, LANES), BF16),
            pltpu.VMEM((4, BLOCK, BLOCK), F32),
            pltpu.VMEM((2, ATTN_GROUP // 2 * BLOCK, 2 * 3 * BLOCK), F32),
            pltpu.VMEM((2, ATTN_GROUP // 2 * BLOCK, 2 * 3 * BLOCK), BF16),
            pltpu.VMEM((2, ATTN_GROUP // 2 * BLOCK, LANES), F32),
            pltpu.VMEM((CONV_CH // LANES, tq + 2 * CONV_HALO, LANES), F32),
            pltpu.VMEM((CONV_CH // LANES, tq, LANES), F32),
            pltpu.VMEM((ATTN_WIDTH // LANES, tq, LANES), F32), pltpu.VMEM((tq, GM_WIDTH), F32),
            pltpu.VMEM((tq, D_MODEL), BF16),
        ],
    )
    return pl.pallas_call(
        functools.partial(_mixer_kernel, layer),
        out_shape=(jax.ShapeDtypeStruct((n, D_MODEL), F32),
                   jax.ShapeDtypeStruct((D_MODEL, 2 * D_FF), BF16), jax.ShapeDtypeStruct((D_FF, D_MODEL), BF16),
                   jax.ShapeDtypeStruct((D_MODEL, D_MODEL), BF16), jax.ShapeDtypeStruct((PLE_DIM, D_MODEL), BF16)),
        grid_spec=grid_spec,
        compiler_params=pltpu.CompilerParams(dimension_semantics=("arbitrary", "arbitrary"),
                                             vmem_limit_bytes=VMEM_LIMIT),
        name="mixer",
    )(sink, x, q, kv, kv, kv, u, vln, glu, glu, glu, ws, gm_bias, cw, cb, clg, clb, og, wout,
      wgu, wd, wpg, wpp)


def _ffn_kernel(x_ref, p_ref, gf_ref, wgu_ref, wd_ref, gp_ref, wpg_ref, wpp_ref, o_ref, act_s):
    x = x_ref[...]
    hn = _rms(x, gf_ref[...]).astype(BF16)
    for c0 in range(0, D_FF, FF_CHUNK):
        gate = jnp.dot(hn, wgu_ref[:, c0:c0 + FF_CHUNK], preferred_element_type=F32)
        up = jnp.dot(hn, wgu_ref[:, D_FF + c0:D_FF + c0 + FF_CHUNK], preferred_element_type=F32)
        act_s[:, c0:c0 + FF_CHUNK] = (gate * jax.nn.sigmoid(gate) * up).astype(BF16)
    x = x + jnp.dot(act_s[...], wd_ref[...], preferred_element_type=F32)

    hp = _rms(x, gp_ref[...]).astype(BF16)
    gate = jax.nn.sigmoid(jnp.dot(hp, wpg_ref[...], preferred_element_type=F32))
    proj = jnp.dot(p_ref[...].astype(BF16), wpp_ref[...], preferred_element_type=F32)
    o_ref[...] = x + proj * gate


def _ffn(layer, x, p, gf, wgu, wd, gp, wpg, wpp):
    n = x.shape[0]
    tm = TM_FFN
    row = lambda width: pl.BlockSpec((tm, width), lambda i: (i, 0))
    return pl.pallas_call(
        _ffn_kernel,
        out_shape=jax.ShapeDtypeStruct((n, D_MODEL), F32),
        grid=(n // tm,),
        in_specs=[row(D_MODEL), pl.BlockSpec((None, tm, PLE_DIM), lambda i: (layer, i, 0)),
                  _layer_spec(layer, (1, D_MODEL)),
                  _resident_spec((D_MODEL, 2 * D_FF)), _resident_spec((D_FF, D_MODEL)),
                  _layer_spec(layer, (1, D_MODEL)), _resident_spec((D_MODEL, D_MODEL)),
                  _resident_spec((PLE_DIM, D_MODEL))],
        out_specs=row(D_MODEL),
        scratch_shapes=[pltpu.VMEM((tm, D_FF), BF16)],
        compiler_params=pltpu.CompilerParams(dimension_semantics=("arbitrary",),
                                             vmem_limit_bytes=VMEM_LIMIT),
        name="ffn_ple",
    )(x, p, gf, wgu, wd, gp, wpg, wpp)


def kernel(x, p, positions, norm_mix_g, w_in, q_norm_g, k_norm_g, sink, gm_ln_g, gm_ln_b, gm_ws, gm_bs, conv_w, conv_b, conv_ln_g, conv_ln_b, out_norm_g, w_out, norm_ffn_g, w_gate_up, w_down, ple_norm_g, w_ple_gate, w_ple_proj):
    batch, seq, d = x.shape
    depth = w_in.shape[0]
    n = batch * seq
    assert d == D_MODEL and seq % TQ_MIX == 0
    assert n % TM_PROJ == 0 and n % TM_FFN == 0 and n % TM_ROPE == 0

    vec = lambda a: a[:, None, :]
    ws_b = gm_ws.astype(BF16)
    qg = vec(jnp.tile(q_norm_g, (1, LANES // HEAD_DIM)))
    kg = vec(jnp.tile(k_norm_g, (1, LANES // HEAD_DIM)))
    gm_bias = jnp.repeat(jnp.swapaxes(gm_bs, 1, 2), HEAD_DIM, axis=2)
    sink_flat = sink.reshape(depth * ATTN_Q_HEADS)
    p_flat = p.reshape(depth, n, PLE_DIM)

    tabs = _rope_tables(positions.reshape(1, n))
    xf = x.reshape(n, d)
    for i in range(depth):
        q, kv, u, vln, glu, w_out_b = _inproj(i, xf, vec(norm_mix_g), w_in, qg, kg, vec(gm_ln_g),
                                                 vec(gm_ln_b), tabs, w_out)
        xf, w_gu_b, w_down_b, w_pg_b, w_pp_b = _mixer(
            i, xf, q, kv, u, vln, glu, sink_flat, ws_b, gm_bias, conv_w, vec(conv_b), vec(conv_ln_g),
            vec(conv_ln_b), vec(out_norm_g), w_out_b, w_gate_up, w_down, w_ple_gate, w_ple_proj, batch, seq)
        xf = _ffn(i, xf, p_flat, vec(norm_ffn_g), w_gu_b, w_down_b, vec(ple_norm_g), w_pg_b, w_pp_b)
    return xf.reshape(batch, seq, d)
```

```python
import functools
import math

import numpy as np
import jax
import jax.numpy as jnp
from jax import lax
from jax.experimental import pallas as pl
from jax.experimental.pallas import tpu as pltpu

F32 = jnp.float32
BF16 = jnp.bfloat16

D_MODEL = 1024
HEAD_DIM = 64
ATTN_WIDTH = 512
ATTN_Q_HEADS = 8
ATTN_KV_HEADS = 2
ATTN_GROUP = 4
KV_WIDTH = 128
WINDOW = 128
BLOCK = 128
ROPE_THETA = 500000.0
ROT_DIM = 16
GM_WIDTH = 256
GM_HEADS = 4
CHUNK = 128
CONV_CH = 256
CONV_WIDTH = 31
CONV_PAD = 15
D_FF = 2816
PLE_DIM = 256
EPS = 1e-6
NEG_INF = -1e30

Q_OFF = 0
K_OFF = Q_OFF + ATTN_WIDTH
V_OFF = K_OFF + KV_WIDTH
GM_OFF = V_OFF + KV_WIDTH
CONV_OFF = GM_OFF + 2 * GM_WIDTH
IN_COLS = CONV_OFF + 2 * CONV_CH

LANES = 128
SUBLANES = 8
BF16_ROWS = 16
CONV_HALO = 16
CONV_ROW_STRIDE = 2
CONV_GROUPS = 4
VMEM_BYTES_V7X = 64 * 1024 * 1024
VMEM_LIMIT = VMEM_BYTES_V7X * 7 // 8

TM_ROPE = 8192
TM_PROJ = 512
TQ_MIX = 1024
TM_FFN = 1024
FF_CHUNK = 256
SOFTMAX_ROWS = 32
ATTN_UNROLL = 4

INV_FREQ = [float(ROPE_THETA ** (-(2.0 * j) / ROT_DIM)) for j in range(ROT_DIM // 2)]
SQRT_HALF = float(np.sqrt(0.5))
LOG2E = float(np.log2(np.e))


def _layer_spec(layer, shape):
    nd = len(shape)
    return pl.BlockSpec((None,) + tuple(shape), lambda *_: (layer,) + (0,) * nd,
                        pipeline_mode=pl.Buffered(1))


def _resident_spec(shape):
    nd = len(shape)
    return pl.BlockSpec(tuple(shape), lambda *_: (0,) * nd, pipeline_mode=pl.Buffered(1))


def _cast_block_rows(rows, n_steps):
    r = BF16_ROWS
    while r * n_steps < rows or rows % r:
        r += BF16_ROWS
    return r


def _cast_specs(layer, rows, cols, n_steps, step_of):
    r = _cast_block_rows(rows, n_steps)
    last = rows // r - 1
    src = pl.BlockSpec((None, r, cols), lambda *idx: (layer, jnp.minimum(step_of(*idx), last), 0))
    dst = pl.BlockSpec((r, cols), lambda *idx: (jnp.minimum(step_of(*idx), last), 0))
    return src, dst


def _split_bf16(x):
    hi = x.astype(BF16)
    lo = (x - hi.astype(F32)).astype(BF16)
    return hi, lo


def _rope_table_kernel(pos_ref, c_ref, s_ref):
    half = ROT_DIM // 2
    pos = pos_ref[...].astype(F32)
    row = lax.broadcasted_iota(jnp.int32, (ROT_DIM, 1), 0)
    inv = jnp.zeros((ROT_DIM, 1), F32)
    for j, f in enumerate(INV_FREQ):
        inv = jnp.where((row & (half - 1)) == j, f, inv)
    ang = inv * pos
    cs = jnp.where(row < half, jnp.cos(ang), jnp.sin(ang))

    r = lax.broadcasted_iota(jnp.int32, (ROT_DIM, 2 * LANES), 0)
    col = lax.broadcasted_iota(jnp.int32, (ROT_DIM, 2 * LANES), 1)
    table = col // LANES
    in_head = col & (HEAD_DIM - 1)
    freq = col & (half - 1)
    put_cos = (table == 0) & (in_head < ROT_DIM) & (r == freq)
    put_nsin = (table == 1) & (in_head < half) & (r == freq + half)
    put_sin = (table == 1) & (in_head >= half) & (in_head < ROT_DIM) & (r == freq + half)
    place = jnp.where(put_cos | put_sin, 1.0, jnp.where(put_nsin, -1.0, 0.0)).astype(BF16)

    hi, lo = _split_bf16(cs)
    dn = (((0,), (0,)), ((), ()))
    t = (lax.dot_general(hi, place, dn, preferred_element_type=F32)
         + lax.dot_general(lo, place, dn, preferred_element_type=F32))
    lane = lax.broadcasted_iota(jnp.int32, (1, LANES), 1)
    c_ref[...] = jnp.where((lane & (HEAD_DIM - 1)) < ROT_DIM, t[:, 0:LANES], 1.0)
    s_ref[...] = t[:, LANES:2 * LANES]


def _rope_tables(pos_row):
    n = pos_row.shape[1]
    tm = TM_ROPE
    tab = jax.ShapeDtypeStruct((n, LANES), F32)
    spec = pl.BlockSpec((tm, LANES), lambda i: (i, 0))
    return pl.pallas_call(
        _rope_table_kernel,
        out_shape=(tab, tab),
        grid=(n // tm,),
        in_specs=[pl.BlockSpec((1, tm), lambda i: (0, i))],
        out_specs=(spec, spec),
        name="rope_tables",
    )(pos_row)


def _rms(x, gain):
    ms = jnp.mean(x * x, axis=-1, keepdims=True)
    return x * lax.rsqrt(ms + EPS) * gain


def _layer_norm(x, gain, bias):
    mu = jnp.mean(x, axis=-1, keepdims=True)
    xc = x - mu
    var = jnp.mean(xc * xc, axis=-1, keepdims=True)
    return xc * lax.rsqrt(var + EPS) * gain + bias


def _inproj_kernel(x_ref, g_ref, w_ref, qg_ref, kg_ref, lng_ref, lnb_ref,
                   c_ref, s_ref, wout_ref,
                   q_out, kv_out, u_out, vln_out, glu_out, wout_bf_out, w_bf, z_even, z_odd, *, last_is_even):
    i = pl.program_id(0)
    n_tiles = pl.num_programs(0) - 1

    wout_bf_out[...] = wout_ref[...].astype(BF16)

    def project(z_dst):
        h = _rms(x_ref[...], g_ref[...]).astype(BF16)
        z_dst[...] = jnp.dot(h, w_bf[...], preferred_element_type=F32)

    def finish(z_src):
        cos_t, sin_t = c_ref[...], s_ref[...]
        lane = lax.broadcasted_iota(jnp.int32, (1, LANES), 1)
        head0 = lane < HEAD_DIM

        def head_norm_rope(z, gain):
            sq = z * z
            first = jnp.sum(jnp.where(head0, sq, 0.0), axis=-1, keepdims=True)
            both = jnp.sum(sq, axis=-1, keepdims=True)
            ss = jnp.where(head0, first, both - first)
            zn = z * lax.rsqrt(ss * (1.0 / HEAD_DIM) + EPS) * gain
            partner = jnp.where((lane & (ROT_DIM // 2)) == 0,
                                pltpu.roll(zn, LANES - ROT_DIM // 2, 1), pltpu.roll(zn, ROT_DIM // 2, 1))
            return zn * cos_t + partner * sin_t

        q_gain = qg_ref[...] * (LOG2E / math.sqrt(HEAD_DIM))
        for b in range(ATTN_WIDTH // LANES):
            q_out[b] = head_norm_rope(z_src[:, Q_OFF + b * LANES:Q_OFF + (b + 1) * LANES], q_gain).astype(BF16)

        k = head_norm_rope(z_src[:, K_OFF:V_OFF], kg_ref[...])
        v = z_src[:, V_OFF:GM_OFF]
        kv_out[:, 0 * LANES:1 * LANES] = k.astype(BF16)
        kv_out[:, 1 * LANES:2 * LANES] = pltpu.roll(k, HEAD_DIM, 1).astype(BF16)
        kv_out[:, 2 * LANES:3 * LANES] = v.astype(BF16)
        kv_out[:, 3 * LANES:4 * LANES] = pltpu.roll(v, HEAD_DIM, 1).astype(BF16)

        zuv = z_src[:, GM_OFF:CONV_OFF]
        uv = 0.5 * zuv * (1.0 + lax.erf(zuv * SQRT_HALF))
        u_out[...] = uv[:, 0:GM_WIDTH]
        vln_out[...] = _layer_norm(uv[:, GM_WIDTH:], lng_ref[...], lnb_ref[...]).astype(BF16)

        glu_out[...] = z_src[:, CONV_OFF:CONV_OFF + CONV_CH] * jax.nn.sigmoid(z_src[:, CONV_OFF + CONV_CH:])

    inner = jnp.logical_and(i > 0, i < n_tiles)

    @pl.when(i == 0)
    def _():
        w_bf[...] = w_ref[...].astype(BF16)
        project(z_even)

    @pl.when(jnp.logical_and(inner, i % 2 == 1))
    def _():
        finish(z_even)
        project(z_odd)

    @pl.when(jnp.logical_and(inner, i % 2 == 0))
    def _():
        finish(z_odd)
        project(z_even)

    @pl.when(i == n_tiles)
    def _():
        finish(z_even if last_is_even else z_odd)


def _inproj(layer, x, g, w, qg, kg, lng, lnb, tabs, wout):
    n = x.shape[0]
    tm = TM_PROJ
    n_tiles = n // tm
    cur = lambda width: pl.BlockSpec((tm, width), lambda i: (jnp.minimum(i, n_tiles - 1), 0))
    lag = lambda width: pl.BlockSpec((tm, width), lambda i: (jnp.maximum(i - 1, 0), 0))
    out_shape = (
        jax.ShapeDtypeStruct((ATTN_WIDTH // LANES, n, LANES), BF16),
        jax.ShapeDtypeStruct((n, 4 * LANES), BF16),
        jax.ShapeDtypeStruct((n, GM_WIDTH), F32),
        jax.ShapeDtypeStruct((n, GM_WIDTH), BF16),
        jax.ShapeDtypeStruct((n, CONV_CH), F32),
        jax.ShapeDtypeStruct((D_MODEL, D_MODEL), BF16),
    )
    wout_src, wout_dst = _cast_specs(layer, D_MODEL, D_MODEL, n_tiles + 1, lambda i: i)
    return pl.pallas_call(
        functools.partial(_inproj_kernel, last_is_even=(n_tiles - 1) % 2 == 0),
        out_shape=out_shape,
        grid=(n_tiles + 1,),
        in_specs=[cur(D_MODEL), _layer_spec(layer, (1, D_MODEL)), _layer_spec(layer, (D_MODEL, IN_COLS)),
                  _layer_spec(layer, (1, LANES)), _layer_spec(layer, (1, LANES)),
                  _layer_spec(layer, (1, GM_WIDTH)), _layer_spec(layer, (1, GM_WIDTH)),
                  lag(LANES), lag(LANES), wout_src],
        out_specs=(pl.BlockSpec((ATTN_WIDTH // LANES, tm, LANES), lambda i: (0, jnp.maximum(i - 1, 0), 0)),
                   lag(4 * LANES), lag(GM_WIDTH), lag(GM_WIDTH), lag(CONV_CH), wout_dst),
        scratch_shapes=[pltpu.VMEM((D_MODEL, IN_COLS), BF16),
                        pltpu.VMEM((tm, IN_COLS), F32), pltpu.VMEM((tm, IN_COLS), F32)],
        compiler_params=pltpu.CompilerParams(dimension_semantics=("arbitrary",),
                                             vmem_limit_bytes=VMEM_LIMIT),
        name="in_proj",
    )(x, g, w, qg, kg, lng, lnb, *tabs, wout)


def _mixer_kernel(layer, sink_ref, x_ref, q_ref, kvp_ref, kvc_ref, kvn_ref, u_ref, vln_ref,
                  glup_ref, gluc_ref, glun_ref,
                  ws_ref, gmb_ref, cw_ref, cb_ref, clg_ref, clb_ref, og_ref, wout_ref,
                  wgu_ref, wd_ref, wpg_ref, wpp_ref,
                  o_ref, wgu_bf_out, wd_bf_out, wpg_bf_out, wpp_bf_out,
                  ka_s, kb_s, va_s, vb_s, bias_s, s_s, p_s, scale_s, glu_s, y_s, attn_s, gm_s, merged_s):
    tq = x_ref.shape[0]
    nblk = tq // BLOCK
    j = pl.program_id(1)
    first = j == 0
    last = j == pl.num_programs(1) - 1

    wgu_bf_out[...] = wgu_ref[...].astype(BF16)
    wd_bf_out[...] = wd_ref[...].astype(BF16)
    wpg_bf_out[...] = wpg_ref[...].astype(BF16)
    wpp_bf_out[...] = wpp_ref[...].astype(BF16)

    lane = lax.broadcasted_iota(jnp.int32, (1, LANES), 1)
    head0 = lane < HEAD_DIM
    zero = jnp.zeros((), BF16)
    for src, r0, rows in ((kvp_ref, 0, BLOCK), (kvc_ref, BLOCK, tq), (kvn_ref, BLOCK + tq, BLOCK)):
        dst = slice(r0, r0 + rows)
        k, k_sw = src[:, 0 * LANES:1 * LANES], src[:, 1 * LANES:2 * LANES]
        v, v_sw = src[:, 2 * LANES:3 * LANES], src[:, 3 * LANES:4 * LANES]
        ka_s[0, dst, :] = jnp.where(head0, k, zero)
        kb_s[0, dst, :] = jnp.where(head0, zero, k_sw)
        ka_s[1, dst, :] = jnp.where(head0, k_sw, zero)
        kb_s[1, dst, :] = jnp.where(head0, zero, k)
        va_s[0, dst, :] = jnp.where(head0, v, zero)
        vb_s[0, dst, :] = jnp.where(head0, zero, v_sw)
        va_s[1, dst, :] = jnp.where(head0, v_sw, zero)
        vb_s[1, dst, :] = jnp.where(head0, zero, v)

    qi = lax.broadcasted_iota(jnp.int32, (BLOCK, BLOCK), 0)
    kj = lax.broadcasted_iota(jnp.int32, (BLOCK, BLOCK), 1)
    band_lo = jnp.where(kj >= qi, 0.0, NEG_INF)
    band_hi = jnp.where(kj <= qi, 0.0, NEG_INF)
    bias_s[0] = band_lo
    bias_s[1] = band_hi
    bias_s[2] = jnp.where(first, NEG_INF, band_lo)
    bias_s[3] = jnp.where(last, NEG_INF, band_hi)

    n_pairs = ATTN_GROUP // 2

    def scores(nb, h):
        r0 = pl.multiple_of(nb * BLOCK, BLOCK)
        q_st = jnp.concatenate([q_ref[n_pairs * h + pp, pl.ds(r0, BLOCK), :] for pp in range(n_pairs)], axis=0)
        k_st = jnp.concatenate([ka_s[h, pl.ds(r0, 3 * BLOCK), :], kb_s[h, pl.ds(r0, 3 * BLOCK), :]], axis=0)
        s_s[h] = lax.dot_general(q_st, k_st, (((1,), (1,)), ((), ())),
                                 preferred_element_type=F32)

    def softmax_pv(nb, h):
        r0 = pl.multiple_of(nb * BLOCK, BLOCK)
        lo_idx = jnp.where(nb == 0, 2, 0)
        hi_idx = jnp.where(nb == nblk - 1, 3, 1)
        v_st = jnp.concatenate([va_s[h, pl.ds(r0, 3 * BLOCK), :], vb_s[h, pl.ds(r0, 3 * BLOCK), :]], axis=0)
        for pp in range(n_pairs):
            for rc in range(0, BLOCK, SOFTMAX_ROWS):
                rr = slice(pp * BLOCK + rc, pp * BLOCK + rc + SOFTMAX_ROWS)
                br = slice(rc, rc + SOFTMAX_ROWS)
                inv = []
                for half in range(2):
                    sink = sink_ref[layer * ATTN_Q_HEADS + ATTN_GROUP * h + 2 * pp + half] * LOG2E
                    c0 = half * 3 * BLOCK
                    s0 = s_s[h, rr, c0:c0 + BLOCK] + bias_s[lo_idx, br, :]
                    s1 = s_s[h, rr, c0 + BLOCK:c0 + 2 * BLOCK]
                    s2 = s_s[h, rr, c0 + 2 * BLOCK:c0 + 3 * BLOCK] + bias_s[hi_idx, br, :]
                    m = jnp.max(jnp.maximum(jnp.maximum(s0, s1), s2), axis=-1, keepdims=True)
                    m = jnp.maximum(m, sink)
                    e0, e1, e2 = jnp.exp2(s0 - m), jnp.exp2(s1 - m), jnp.exp2(s2 - m)
                    denom = jnp.sum(e0 + e1 + e2, axis=-1, keepdims=True) + jnp.exp2(sink - m)
                    p_s[h, rr, c0:c0 + BLOCK] = e0.astype(BF16)
                    p_s[h, rr, c0 + BLOCK:c0 + 2 * BLOCK] = e1.astype(BF16)
                    p_s[h, rr, c0 + 2 * BLOCK:c0 + 3 * BLOCK] = e2.astype(BF16)
                    inv.append(1.0 / denom)
                scale_s[h, rr, :] = jnp.where(head0, inv[0], inv[1])
        o = jnp.dot(p_s[h], v_st, preferred_element_type=F32) * scale_s[h]
        for pp in range(n_pairs):
            attn_s[n_pairs * h + pp, pl.ds(r0, BLOCK), :] = o[pp * BLOCK:(pp + 1) * BLOCK, :]

    scores(0, 0)

    def attn_step(nb, carry):
        scores(nb, 1)
        softmax_pv(nb, 0)
        scores(jnp.minimum(nb + 1, nblk - 1), 0)
        softmax_pv(nb, 1)
        return carry

    lax.fori_loop(0, nblk, attn_step, 0, unroll=ATTN_UNROLL)

    for nb in range(nblk):
        rows = slice(nb * BLOCK, (nb + 1) * BLOCK)
        for hp in range(GM_HEADS // 2):
            cols = slice(hp * LANES, (hp + 1) * LANES)
            vblk = vln_ref[rows, cols]
            lhs = jnp.concatenate([ws_ref[2 * hp], ws_ref[2 * hp + 1]], axis=1)
            rhs = jnp.concatenate([jnp.where(head0, vblk, zero), jnp.where(head0, zero, vblk)], axis=0)
            sgate = jnp.dot(lhs, rhs, preferred_element_type=F32) + gmb_ref[:, cols]
            gm_s[rows, cols] = u_ref[rows, cols] * sgate

    n_ch = CONV_CH // LANES
    for ch in range(n_ch):
        cl = slice(ch * LANES, (ch + 1) * LANES)
        glu_s[ch, 0:CONV_HALO, :] = jnp.where(first, 0.0, glup_ref[:, cl])
        glu_s[ch, CONV_HALO:CONV_HALO + tq, :] = gluc_ref[:, cl]
        glu_s[ch, CONV_HALO + tq:, :] = jnp.where(last, 0.0, glun_ref[:, cl])
    group = SUBLANES * CONV_ROW_STRIDE

    def conv_step(i, carry):
        base = pl.multiple_of(i * (CONV_GROUPS * group), CONV_GROUPS * group)
        for ch in range(n_ch):
            cl = slice(ch * LANES, (ch + 1) * LANES)
            offs = [g * group + ph for g in range(CONV_GROUPS) for ph in range(CONV_ROW_STRIDE)]
            accs = [jnp.broadcast_to(cb_ref[:, cl], (SUBLANES, LANES)) for _ in offs]
            for t in range(CONV_WIDTH):
                w = cw_ref[t:t + 1, cl]
                for a, off in enumerate(offs):
                    start = base + (off + CONV_HALO - CONV_PAD + t)
                    accs[a] = accs[a] + w * glu_s[ch, pl.ds(start, SUBLANES, stride=CONV_ROW_STRIDE), :]
            for a, off in enumerate(offs):
                y_s[ch, pl.ds(base + off, SUBLANES, stride=CONV_ROW_STRIDE), :] = accs[a]
        return carry

    lax.fori_loop(0, tq // (CONV_GROUPS * group), conv_step, 0)

    rchunk = 128
    for r0 in range(0, tq, rchunk):
        rs = slice(r0, r0 + rchunk)
        y = jnp.concatenate([y_s[ch, rs, :] for ch in range(n_ch)], axis=1)
        y = _layer_norm(y, clg_ref[...], clb_ref[...])
        y = y * jax.nn.sigmoid(y)
        merged_s[rs, ATTN_WIDTH + GM_WIDTH:] = _rms(y, og_ref[:, ATTN_WIDTH + GM_WIDTH:]).astype(BF16)
        attn = jnp.concatenate([attn_s[pb, rs, :] for pb in range(ATTN_WIDTH // LANES)], axis=1)
        merged_s[rs, 0:ATTN_WIDTH] = _rms(attn, og_ref[:, 0:ATTN_WIDTH]).astype(BF16)
        merged_s[rs, ATTN_WIDTH:ATTN_WIDTH + GM_WIDTH] = _rms(
            gm_s[rs, :], og_ref[:, ATTN_WIDTH:ATTN_WIDTH + GM_WIDTH]).astype(BF16)
    o_ref[...] = x_ref[...] + jnp.dot(merged_s[...], wout_ref[...], preferred_element_type=F32)


def _mixer(layer, x, q, kv, u, vln, glu, sink, ws, gm_bias, cw, cb, clg, clb, og, wout,
           wgu, wd, wpg, wpp, batch, seq):
    n = x.shape[0]
    tq = TQ_MIX
    tiles = seq // tq
    blk_per_tile = tq // BLOCK
    blk_per_seq = seq // BLOCK
    halo_per_tile = tq // CONV_HALO
    halo_per_seq = seq // CONV_HALO

    def row(width):
        return pl.BlockSpec((tq, width), lambda b, j, *_: (b * tiles + j, 0))

    def prev(rows, width, per_tile, per_seq):
        return pl.BlockSpec(
            (rows, width), lambda b, j, *_: (b * per_seq + jnp.maximum(j * per_tile - 1, 0), 0))

    def nxt(rows, width, per_tile, per_seq):
        return pl.BlockSpec(
            (rows, width), lambda b, j, *_: (b * per_seq + jnp.minimum((j + 1) * per_tile, per_seq - 1), 0))

    const = lambda shape: _layer_spec(layer, shape)
    kv_rows = tq + 2 * BLOCK
    step_of = lambda b, j, *_: b * tiles + j
    casts = [_cast_specs(layer, rows, cols, batch * tiles, step_of)
             for rows, cols in ((D_MODEL, 2 * D_FF), (D_FF, D_MODEL), (D_MODEL, D_MODEL), (PLE_DIM, D_MODEL))]
    grid_spec = pltpu.PrefetchScalarGridSpec(
        num_scalar_prefetch=1,
        grid=(batch, tiles),
        in_specs=[
            row(D_MODEL), pl.BlockSpec((ATTN_WIDTH // LANES, tq, LANES), lambda b, j, *_: (0, b * tiles + j, 0)),
            prev(BLOCK, 4 * LANES, blk_per_tile, blk_per_seq), row(4 * LANES),
            nxt(BLOCK, 4 * LANES, blk_per_tile, blk_per_seq),
            row(GM_WIDTH), row(GM_WIDTH),
            prev(CONV_HALO, CONV_CH, halo_per_tile, halo_per_seq), row(CONV_CH),
            nxt(CONV_HALO, CONV_CH, halo_per_tile, halo_per_seq),
            const((GM_HEADS, CHUNK, CHUNK)), const((CHUNK, GM_WIDTH)),
            const((CONV_WIDTH, CONV_CH)), const((1, CONV_CH)), const((1, CONV_CH)), const((1, CONV_CH)),
            const((1, D_MODEL)), _resident_spec((D_MODEL, D_MODEL)),
            *[src for src, _ in casts],
        ],
        out_specs=(row(D_MODEL), *[dst for _, dst in casts]),
        scratch_shapes=[
            pltpu.VMEM((ATTN_KV_HEADS, kv_rows, LANES), BF16), pltpu.VMEM((ATTN_KV_HEADS, kv_rows, LANES), BF16),
            pltpu.VMEM((ATTN_KV_HEADS, kv_rows, LANES), BF16), pltpu.VMEM((ATTN_KV_HEADS, kv_rows, LANES), BF16),
            pltpu.VMEM((4, BLOCK, BLOCK), F32),
            pltpu.VMEM((2, ATTN_GROUP // 2 * BLOCK, 2 * 3 * BLOCK), F32),
            pltpu.VMEM((2, ATTN_GROUP // 2 * BLOCK, 2 * 3 * BLOCK), BF16),
            pltpu.VMEM((2, ATTN_GROUP // 2 * BLOCK, LANES), F32),
            pltpu.VMEM((CONV_CH // LANES, tq + 2 * CONV_HALO, LANES), F32),
            pltpu.VMEM((CONV_CH // LANES, tq, LANES), F32),
            pltpu.VMEM((ATTN_WIDTH // LANES, tq, LANES), F32), pltpu.VMEM((tq, GM_WIDTH), F32),
            pltpu.VMEM((tq, D_MODEL), BF16),
        ],
    )
    return pl.pallas_call(
        functools.partial(_mixer_kernel, layer),
        out_shape=(jax.ShapeDtypeStruct((n, D_MODEL), F32),
                   jax.ShapeDtypeStruct((D_MODEL, 2 * D_FF), BF16), jax.ShapeDtypeStruct((D_FF, D_MODEL), BF16),
                   jax.ShapeDtypeStruct((D_MODEL, D_MODEL), BF16), jax.ShapeDtypeStruct((PLE_DIM, D_MODEL), BF16)),
        grid_spec=grid_spec,
        compiler_params=pltpu.CompilerParams(dimension_semantics=("arbitrary", "arbitrary"),
                                             vmem_limit_bytes=VMEM_LIMIT),
        name="mixer",
    )(sink, x, q, kv, kv, kv, u, vln, glu, glu, glu, ws, gm_bias, cw, cb, clg, clb, og, wout,
      wgu, wd, wpg, wpp)


def _ffn_kernel(x_ref, p_ref, gf_ref, wgu_ref, wd_ref, gp_ref, wpg_ref, wpp_ref, o_ref, act_s):
    x = x_ref[...]
    hn = _rms(x, gf_ref[...]).astype(BF16)
    for c0 in range(0, D_FF, FF_CHUNK):
        gate = jnp.dot(hn, wgu_ref[:, c0:c0 + FF_CHUNK], preferred_element_type=F32)
        up = jnp.dot(hn, wgu_ref[:, D_FF + c0:D_FF + c0 + FF_CHUNK], preferred_element_type=F32)
        act_s[:, c0:c0 + FF_CHUNK] = (gate * jax.nn.sigmoid(gate) * up).astype(BF16)
    x = x + jnp.dot(act_s[...], wd_ref[...], preferred_element_type=F32)

    hp = _rms(x, gp_ref[...]).astype(BF16)
    gate = jax.nn.sigmoid(jnp.dot(hp, wpg_ref[...], preferred_element_type=F32))
    proj = jnp.dot(p_ref[...].astype(BF16), wpp_ref[...], preferred_element_type=F32)
    o_ref[...] = x + proj * gate


def _ffn(layer, x, p, gf, wgu, wd, gp, wpg, wpp):
    n = x.shape[0]
    tm = TM_FFN
    row = lambda width: pl.BlockSpec((tm, width), lambda i: (i, 0))
    return pl.pallas_call(
        _ffn_kernel,
        out_shape=jax.ShapeDtypeStruct((n, D_MODEL), F32),
        grid=(n // tm,),
        in_specs=[row(D_MODEL), pl.BlockSpec((None, tm, PLE_DIM), lambda i: (layer, i, 0)),
                  _layer_spec(layer, (1, D_MODEL)),
                  _resident_spec((D_MODEL, 2 * D_FF)), _resident_spec((D_FF, D_MODEL)),
                  _layer_spec(layer, (1, D_MODEL)), _resident_spec((D_MODEL, D_MODEL)),
                  _resident_spec((PLE_DIM, D_MODEL))],
        out_specs=row(D_MODEL),
        scratch_shapes=[pltpu.VMEM((tm, D_FF), BF16)],
        compiler_params=pltpu.CompilerParams(dimension_semantics=("arbitrary",),
                                             vmem_limit_bytes=VMEM_LIMIT),
        name="ffn_ple",
    )(x, p, gf, wgu, wd, gp, wpg, wpp)


def kernel(x, p, positions, norm_mix_g, w_in, q_norm_g, k_norm_g, sink, gm_ln_g, gm_ln_b, gm_ws, gm_bs, conv_w, conv_b, conv_ln_g, conv_ln_b, out_norm_g, w_out, norm_ffn_g, w_gate_up, w_down, ple_norm_g, w_ple_gate, w_ple_proj):
    batch, seq, d = x.shape
    depth = w_in.shape[0]
    n = batch * seq
    assert d == D_MODEL and seq % TQ_MIX == 0
    assert n % TM_PROJ == 0 and n % TM_FFN == 0 and n % TM_ROPE == 0

    vec = lambda a: a[:, None, :]
    ws_b = gm_ws.astype(BF16)
    qg = vec(jnp.tile(q_norm_g, (1, LANES // HEAD_DIM)))
    kg = vec(jnp.tile(k_norm_g, (1, LANES // HEAD_DIM)))
    gm_bias = jnp.repeat(jnp.swapaxes(gm_bs, 1, 2), HEAD_DIM, axis=2)
    sink_flat = sink.reshape(depth * ATTN_Q_HEADS)
    p_flat = p.reshape(depth, n, PLE_DIM)

    tabs = _rope_tables(positions.reshape(1, n))
    xf = x.reshape(n, d)
    for i in range(depth):
        q, kv, u, vln, glu, w_out_b = _inproj(i, xf, vec(norm_mix_g), w_in, qg, kg, vec(gm_ln_g),
                                                 vec(gm_ln_b), tabs, w_out)
        xf, w_gu_b, w_down_b, w_pg_b, w_pp_b = _mixer(
            i, xf, q, kv, u, vln, glu, sink_flat, ws_b, gm_bias, conv_w, vec(conv_b), vec(conv_ln_g),
            vec(conv_ln_b), vec(out_norm_g), w_out_b, w_gate_up, w_down, w_ple_gate, w_ple_proj, batch, seq)
        xf = _ffn(i, xf, p_flat, vec(norm_ffn_g), w_gu_b, w_down_b, vec(ple_norm_g), w_pg_b, w_pp_b)
    return xf.reshape(batch, seq, d)
```

```python
import functools
import math

import numpy as np
import jax
import jax.numpy as jnp
from jax import lax
from jax.experimental import pallas as pl
from jax.experimental.pallas import tpu as pltpu

F32 = jnp.float32
BF16 = jnp.bfloat16

D_MODEL = 1024
HEAD_DIM = 64
ATTN_WIDTH = 512
ATTN_Q_HEADS = 8
ATTN_KV_HEADS = 2
ATTN_GROUP = 4
KV_WIDTH = 128
WINDOW = 128
BLOCK = 128
ROPE_THETA = 500000.0
ROT_DIM = 16
GM_WIDTH = 256
GM_HEADS = 4
CHUNK = 128
CONV_CH = 256
CONV_WIDTH = 31
CONV_PAD = 15
D_FF = 2816
PLE_DIM = 256
EPS = 1e-6
NEG_INF = -1e30

Q_OFF = 0
K_OFF = Q_OFF + ATTN_WIDTH
V_OFF = K_OFF + KV_WIDTH
GM_OFF = V_OFF + KV_WIDTH
CONV_OFF = GM_OFF + 2 * GM_WIDTH
IN_COLS = CONV_OFF + 2 * CONV_CH

LANES = 128
SUBLANES = 8
BF16_ROWS = 16
CONV_HALO = 16
CONV_ROW_STRIDE = 2
CONV_GROUPS = 4
VMEM_BYTES_V7X = 64 * 1024 * 1024
VMEM_LIMIT = VMEM_BYTES_V7X * 7 // 8

TM_ROPE = 8192
TM_PROJ = 512
TQ_MIX = 1024
TM_FFN = 1024
FF_CHUNK = 256
SOFTMAX_ROWS = 32
ATTN_UNROLL = 4

INV_FREQ = [float(ROPE_THETA ** (-(2.0 * j) / ROT_DIM)) for j in range(ROT_DIM // 2)]
SQRT_HALF = float(np.sqrt(0.5))
LOG2E = float(np.log2(np.e))


def _layer_spec(layer, shape):
    nd = len(shape)
    return pl.BlockSpec((None,) + tuple(shape), lambda *_: (layer,) + (0,) * nd,
                        pipeline_mode=pl.Buffered(1))


def _resident_spec(shape):
    nd = len(shape)
    return pl.BlockSpec(tuple(shape), lambda *_: (0,) * nd, pipeline_mode=pl.Buffered(1))


def _cast_block_rows(rows, n_steps):
    r = BF16_ROWS
    while r * n_steps < rows or rows % r:
        r += BF16_ROWS
    return r


def _cast_specs(layer, rows, cols, n_steps, step_of):
    r = _cast_block_rows(rows, n_steps)
    last = rows // r - 1
    src = pl.BlockSpec((None, r, cols), lambda *idx: (layer, jnp.minimum(step_of(*idx), last), 0))
    dst = pl.BlockSpec((r, cols), lambda *idx: (jnp.minimum(step_of(*idx), last), 0))
    return src, dst


def _split_bf16(x):
    hi = x.astype(BF16)
    lo = (x - hi.astype(F32)).astype(BF16)
    return hi, lo


def _rope_table_kernel(pos_ref, c_ref, s_ref):
    half = ROT_DIM // 2
    pos = pos_ref[...].astype(F32)
    row = lax.broadcasted_iota(jnp.int32, (ROT_DIM, 1), 0)
    inv = jnp.zeros((ROT_DIM, 1), F32)
    for j, f in enumerate(INV_FREQ):
        inv = jnp.where((row & (half - 1)) == j, f, inv)
    ang = inv * pos
    cs = jnp.where(row < half, jnp.cos(ang), jnp.sin(ang))

    r = lax.broadcasted_iota(jnp.int32, (ROT_DIM, 2 * LANES), 0)
    col = lax.broadcasted_iota(jnp.int32, (ROT_DIM, 2 * LANES), 1)
    table = col // LANES
    in_head = col & (HEAD_DIM - 1)
    freq = col & (half - 1)
    put_cos = (table == 0) & (in_head < ROT_DIM) & (r == freq)
    put_nsin = (table == 1) & (in_head < half) & (r == freq + half)
    put_sin = (table == 1) & (in_head >= half) & (in_head < ROT_DIM) & (r == freq + half)
    place = jnp.where(put_cos | put_sin, 1.0, jnp.where(put_nsin, -1.0, 0.0)).astype(BF16)

    hi, lo = _split_bf16(cs)
    dn = (((0,), (0,)), ((), ()))
    t = (lax.dot_general(hi, place, dn, preferred_element_type=F32)
         + lax.dot_general(lo, place, dn, preferred_element_type=F32))
    lane = lax.broadcasted_iota(jnp.int32, (1, LANES), 1)
    c_ref[...] = jnp.where((lane & (HEAD_DIM - 1)) < ROT_DIM, t[:, 0:LANES], 1.0)
    s_ref[...] = t[:, LANES:2 * LANES]


def _rope_tables(pos_row):
    n = pos_row.shape[1]
    tm = TM_ROPE
    tab = jax.ShapeDtypeStruct((n, LANES), F32)
    spec = pl.BlockSpec((tm, LANES), lambda i: (i, 0))
    return pl.pallas_call(
        _rope_table_kernel,
        out_shape=(tab, tab),
        grid=(n // tm,),
        in_specs=[pl.BlockSpec((1, tm), lambda i: (0, i))],
        out_specs=(spec, spec),
        name="rope_tables",
    )(pos_row)


def _rms(x, gain):
    ms = jnp.mean(x * x, axis=-1, keepdims=True)
    return x * lax.rsqrt(ms + EPS) * gain


def _layer_norm(x, gain, bias):
    mu = jnp.mean(x, axis=-1, keepdims=True)
    xc = x - mu
    var = jnp.mean(xc * xc, axis=-1, keepdims=True)
    return xc * lax.rsqrt(var + EPS) * gain + bias


def _inproj_kernel(x_ref, g_ref, w_ref, qg_ref, kg_ref, lng_ref, lnb_ref,
                   c_ref, s_ref, wout_ref,
                   q_out, kv_out, u_out, vln_out, glu_out, wout_bf_out, w_bf, z_even, z_odd, *, last_is_even):
    i = pl.program_id(0)
    n_tiles = pl.num_programs(0) - 1

    wout_bf_out[...] = wout_ref[...].astype(BF16)

    def project(z_dst):
        h = _rms(x_ref[...], g_ref[...]).astype(BF16)
        z_dst[...] = jnp.dot(h, w_bf[...], preferred_element_type=F32)

    def finish(z_src):
        cos_t, sin_t = c_ref[...], s_ref[...]
        lane = lax.broadcasted_iota(jnp.int32, (1, LANES), 1)
        head0 = lane < HEAD_DIM

        def head_norm_rope(z, gain):
            sq = z * z
            first = jnp.sum(jnp.where(head0, sq, 0.0), axis=-1, keepdims=True)
            both = jnp.sum(sq, axis=-1, keepdims=True)
            ss = jnp.where(head0, first, both - first)
            zn = z * lax.rsqrt(ss * (1.0 / HEAD_DIM) + EPS) * gain
            partner = jnp.where((lane & (ROT_DIM // 2)) == 0,
                                pltpu.roll(zn, LANES - ROT_DIM // 2, 1), pltpu.roll(zn, ROT_DIM // 2, 1))
            return zn * cos_t + partner * sin_t

        q_gain = qg_ref[...] * (LOG2E / math.sqrt(HEAD_DIM))
        for b in range(ATTN_WIDTH // LANES):
            q_out[b] = head_norm_rope(z_src[:, Q_OFF + b * LANES:Q_OFF + (b + 1) * LANES], q_gain).astype(BF16)

        k = head_norm_rope(z_src[:, K_OFF:V_OFF], kg_ref[...])
        v = z_src[:, V_OFF:GM_OFF]
        kv_out[:, 0 * LANES:1 * LANES] = k.astype(BF16)
        kv_out[:, 1 * LANES:2 * LANES] = pltpu.roll(k, HEAD_DIM, 1).astype(BF16)
        kv_out[:, 2 * LANES:3 * LANES] = v.astype(BF16)
        kv_out[:, 3 * LANES:4 * LANES] = pltpu.roll(v, HEAD_DIM, 1).astype(BF16)

        zuv = z_src[:, GM_OFF:CONV_OFF]
        uv = 0.5 * zuv * (1.0 + lax.erf(zuv * SQRT_HALF))
        u_out[...] = uv[:, 0:GM_WIDTH]
        vln_out[...] = _layer_norm(uv[:, GM_WIDTH:], lng_ref[...], lnb_ref[...]).astype(BF16)

        glu_out[...] = z_src[:, CONV_OFF:CONV_OFF + CONV_CH] * jax.nn.sigmoid(z_src[:, CONV_OFF + CONV_CH:])

    inner = jnp.logical_and(i > 0, i < n_tiles)

    @pl.when(i == 0)
    def _():
        w_bf[...] = w_ref[...].astype(BF16)
        project(z_even)

    @pl.when(jnp.logical_and(inner, i % 2 == 1))
    def _():
        project(z_odd)
        finish(z_even)

    @pl.when(jnp.logical_and(inner, i % 2 == 0))
    def _():
        project(z_even)
        finish(z_odd)

    @pl.when(i == n_tiles)
    def _():
        finish(z_even if last_is_even else z_odd)


def _inproj(layer, x, g, w, qg, kg, lng, lnb, tabs, wout):
    n = x.shape[0]
    tm = TM_PROJ
    n_tiles = n // tm
    cur = lambda width: pl.BlockSpec((tm, width), lambda i: (jnp.minimum(i, n_tiles - 1), 0))
    lag = lambda width: pl.BlockSpec((tm, width), lambda i: (jnp.maximum(i - 1, 0), 0))
    out_shape = (
        jax.ShapeDtypeStruct((ATTN_WIDTH // LANES, n, LANES), BF16),
        jax.ShapeDtypeStruct((n, 4 * LANES), BF16),
        jax.ShapeDtypeStruct((n, GM_WIDTH), F32),
        jax.ShapeDtypeStruct((n, GM_WIDTH), BF16),
        jax.ShapeDtypeStruct((n, CONV_CH), F32),
        jax.ShapeDtypeStruct((D_MODEL, D_MODEL), BF16),
    )
    wout_src, wout_dst = _cast_specs(layer, D_MODEL, D_MODEL, n_tiles + 1, lambda i: i)
    return pl.pallas_call(
        functools.partial(_inproj_kernel, last_is_even=(n_tiles - 1) % 2 == 0),
        out_shape=out_shape,
        grid=(n_tiles + 1,),
        in_specs=[cur(D_MODEL), _layer_spec(layer, (1, D_MODEL)), _layer_spec(layer, (D_MODEL, IN_COLS)),
                  _layer_spec(layer, (1, LANES)), _layer_spec(layer, (1, LANES)),
                  _layer_spec(layer, (1, GM_WIDTH)), _layer_spec(layer, (1, GM_WIDTH)),
                  lag(LANES), lag(LANES), wout_src],
        out_specs=(pl.BlockSpec((ATTN_WIDTH // LANES, tm, LANES), lambda i: (0, jnp.maximum(i - 1, 0), 0)),
                   lag(4 * LANES), lag(GM_WIDTH), lag(GM_WIDTH), lag(CONV_CH), wout_dst),
        scratch_shapes=[pltpu.VMEM((D_MODEL, IN_COLS), BF16),
                        pltpu.VMEM((tm, IN_COLS), F32), pltpu.VMEM((tm, IN_COLS), F32)],
        compiler_params=pltpu.CompilerParams(dimension_semantics=("arbitrary",),
                                             vmem_limit_bytes=VMEM_LIMIT),
        name="in_proj",
    )(x, g, w, qg, kg, lng, lnb, *tabs, wout)


def _mixer_kernel(layer, sink_ref, x_ref, q_ref, kvp_ref, kvc_ref, kvn_ref, u_ref, vln_ref,
                  glup_ref, gluc_ref, glun_ref,
                  ws_ref, gmb_ref, cw_ref, cb_ref, clg_ref, clb_ref, og_ref, wout_ref,
                  wgu_ref, wd_ref, wpg_ref, wpp_ref,
                  o_ref, wgu_bf_out, wd_bf_out, wpg_bf_out, wpp_bf_out,
                  ka_s, kb_s, va_s, vb_s, bias_s, s_s, p_s, scale_s, glu_s, y_s, attn_s, gm_s, merged_s):
    tq = x_ref.shape[0]
    nblk = tq // BLOCK
    j = pl.program_id(1)
    first = j == 0
    last = j == pl.num_programs(1) - 1

    wgu_bf_out[...] = wgu_ref[...].astype(BF16)
    wd_bf_out[...] = wd_ref[...].astype(BF16)
    wpg_bf_out[...] = wpg_ref[...].astype(BF16)
    wpp_bf_out[...] = wpp_ref[...].astype(BF16)

    lane = lax.broadcasted_iota(jnp.int32, (1, LANES), 1)
    head0 = lane < HEAD_DIM
    zero = jnp.zeros((), BF16)
    for src, r0, rows in ((kvp_ref, 0, BLOCK), (kvc_ref, BLOCK, tq), (kvn_ref, BLOCK + tq, BLOCK)):
        dst = slice(r0, r0 + rows)
        k, k_sw = src[:, 0 * LANES:1 * LANES], src[:, 1 * LANES:2 * LANES]
        v, v_sw = src[:, 2 * LANES:3 * LANES], src[:, 3 * LANES:4 * LANES]
        ka_s[0, dst, :] = jnp.where(head0, k, zero)
        kb_s[0, dst, :] = jnp.where(head0, zero, k_sw)
        ka_s[1, dst, :] = jnp.where(head0, k_sw, zero)
        kb_s[1, dst, :] = jnp.where(head0, zero, k)
        va_s[0, dst, :] = jnp.where(head0, v, zero)
        vb_s[0, dst, :] = jnp.where(head0, zero, v_sw)
        va_s[1, dst, :] = jnp.where(head0, v_sw, zero)
        vb_s[1, dst, :] = jnp.where(head0, zero, v)

    qi = lax.broadcasted_iota(jnp.int32, (BLOCK, BLOCK), 0)
    kj = lax.broadcasted_iota(jnp.int32, (BLOCK, BLOCK), 1)
    band_lo = jnp.where(kj >= qi, 0.0, NEG_INF)
    band_hi = jnp.where(kj <= qi, 0.0, NEG_INF)
    bias_s[0] = band_lo
    bias_s[1] = band_hi
    bias_s[2] = jnp.where(first, NEG_INF, band_lo)
    bias_s[3] = jnp.where(last, NEG_INF, band_hi)

    n_pairs = ATTN_GROUP // 2

    def scores(nb, h):
        r0 = pl.multiple_of(nb * BLOCK, BLOCK)
        q_st = jnp.concatenate([q_ref[n_pairs * h + pp, pl.ds(r0, BLOCK), :] for pp in range(n_pairs)], axis=0)
        k_st = jnp.concatenate([ka_s[h, pl.ds(r0, 3 * BLOCK), :], kb_s[h, pl.ds(r0, 3 * BLOCK), :]], axis=0)
        s_s[h] = lax.dot_general(q_st, k_st, (((1,), (1,)), ((), ())),
                                 preferred_element_type=F32)

    def softmax_pv(nb, h):
        r0 = pl.multiple_of(nb * BLOCK, BLOCK)
        lo_idx = jnp.where(nb == 0, 2, 0)
        hi_idx = jnp.where(nb == nblk - 1, 3, 1)
        v_st = jnp.concatenate([va_s[h, pl.ds(r0, 3 * BLOCK), :], vb_s[h, pl.ds(r0, 3 * BLOCK), :]], axis=0)
        for pp in range(n_pairs):
            for rc in range(0, BLOCK, SOFTMAX_ROWS):
                rr = slice(pp * BLOCK + rc, pp * BLOCK + rc + SOFTMAX_ROWS)
                br = slice(rc, rc + SOFTMAX_ROWS)
                inv = []
                for half in range(2):
                    sink = sink_ref[layer * ATTN_Q_HEADS + ATTN_GROUP * h + 2 * pp + half] * LOG2E
                    c0 = half * 3 * BLOCK
                    s0 = s_s[h, rr, c0:c0 + BLOCK] + bias_s[lo_idx, br, :]
                    s1 = s_s[h, rr, c0 + BLOCK:c0 + 2 * BLOCK]
                    s2 = s_s[h, rr, c0 + 2 * BLOCK:c0 + 3 * BLOCK] + bias_s[hi_idx, br, :]
                    m = jnp.max(jnp.maximum(jnp.maximum(s0, s1), s2), axis=-1, keepdims=True)
                    m = jnp.maximum(m, sink)
                    e0, e1, e2 = jnp.exp2(s0 - m), jnp.exp2(s1 - m), jnp.exp2(s2 - m)
                    denom = jnp.sum(e0 + e1 + e2, axis=-1, keepdims=True) + jnp.exp2(sink - m)
                    p_s[h, rr, c0:c0 + BLOCK] = e0.astype(BF16)
                    p_s[h, rr, c0 + BLOCK:c0 + 2 * BLOCK] = e1.astype(BF16)
                    p_s[h, rr, c0 + 2 * BLOCK:c0 + 3 * BLOCK] = e2.astype(BF16)
                    inv.append(1.0 / denom)
                scale_s[h, rr, :] = jnp.where(head0, inv[0], inv[1])
        o = jnp.dot(p_s[h], v_st, preferred_element_type=F32) * scale_s[h]
        for pp in range(n_pairs):
            attn_s[n_pairs * h + pp, pl.ds(r0, BLOCK), :] = o[pp * BLOCK:(pp + 1) * BLOCK, :]

    scores(0, 0)

    def attn_step(nb, carry):
        scores(nb, 1)
        softmax_pv(nb, 0)
        scores(jnp.minimum(nb + 1, nblk - 1), 0)
        softmax_pv(nb, 1)
        return carry

    lax.fori_loop(0, nblk, attn_step, 0, unroll=ATTN_UNROLL)

    for nb in range(nblk):
        rows = slice(nb * BLOCK, (nb + 1) * BLOCK)
        for hp in range(GM_HEADS // 2):
            cols = slice(hp * LANES, (hp + 1) * LANES)
            vblk = vln_ref[rows, cols]
            lhs = jnp.concatenate([ws_ref[2 * hp], ws_ref[2 * hp + 1]], axis=1)
            rhs = jnp.concatenate([jnp.where(head0, vblk, zero), jnp.where(head0, zero, vblk)], axis=0)
            sgate = jnp.dot(lhs, rhs, preferred_element_type=F32) + gmb_ref[:, cols]
            gm_s[rows, cols] = u_ref[rows, cols] * sgate

    n_ch = CONV_CH // LANES
    for ch in range(n_ch):
        cl = slice(ch * LANES, (ch + 1) * LANES)
        glu_s[ch, 0:CONV_HALO, :] = jnp.where(first, 0.0, glup_ref[:, cl])
        glu_s[ch, CONV_HALO:CONV_HALO + tq, :] = gluc_ref[:, cl]
        glu_s[ch, CONV_HALO + tq:, :] = jnp.where(last, 0.0, glun_ref[:, cl])
    group = SUBLANES * CONV_ROW_STRIDE

    def conv_step(i, carry):
        base = pl.multiple_of(i * (CONV_GROUPS * group), CONV_GROUPS * group)
        for ch in range(n_ch):
            cl = slice(ch * LANES, (ch + 1) * LANES)
            offs = [g * group + ph for g in range(CONV_GROUPS) for ph in range(CONV_ROW_STRIDE)]
            accs = [jnp.broadcast_to(cb_ref[:, cl], (SUBLANES, LANES)) for _ in offs]
            for t in range(CONV_WIDTH):
                w = cw_ref[t:t + 1, cl]
                for a, off in enumerate(offs):
                    start = base + (off + CONV_HALO - CONV_PAD + t)
                    accs[a] = accs[a] + w * glu_s[ch, pl.ds(start, SUBLANES, stride=CONV_ROW_STRIDE), :]
            for a, off in enumerate(offs):
                y_s[ch, pl.ds(base + off, SUBLANES, stride=CONV_ROW_STRIDE), :] = accs[a]
        return carry

    lax.fori_loop(0, tq // (CONV_GROUPS * group), conv_step, 0)

    rchunk = 128
    for r0 in range(0, tq, rchunk):
        rs = slice(r0, r0 + rchunk)
        y = jnp.concatenate([y_s[ch, rs, :] for ch in range(n_ch)], axis=1)
        y = _layer_norm(y, clg_ref[...], clb_ref[...])
        y = y * jax.nn.sigmoid(y)
        merged_s[rs, ATTN_WIDTH + GM_WIDTH:] = _rms(y, og_ref[:, ATTN_WIDTH + GM_WIDTH:]).astype(BF16)
        attn = jnp.concatenate([attn_s[pb, rs, :] for pb in range(ATTN_WIDTH // LANES)], axis=1)
        merged_s[rs, 0:ATTN_WIDTH] = _rms(attn, og_ref[:, 0:ATTN_WIDTH]).astype(BF16)
        merged_s[rs, ATTN_WIDTH:ATTN_WIDTH + GM_WIDTH] = _rms(
            gm_s[rs, :], og_ref[:, ATTN_WIDTH:ATTN_WIDTH + GM_WIDTH]).astype(BF16)
    o_ref[...] = x_ref[...] + jnp.dot(merged_s[...], wout_ref[...], preferred_element_type=F32)


def _mixer(layer, x, q, kv, u, vln, glu, sink, ws, gm_bias, cw, cb, clg, clb, og, wout,
           wgu, wd, wpg, wpp, batch, seq):
    n = x.shape[0]
    tq = TQ_MIX
    tiles = seq // tq
    blk_per_tile = tq // BLOCK
    blk_per_seq = seq // BLOCK
    halo_per_tile = tq // CONV_HALO
    halo_per_seq = seq // CONV_HALO

    def row(width):
        return pl.BlockSpec((tq, width), lambda b, j, *_: (b * tiles + j, 0))

    def prev(rows, width, per_tile, per_seq):
        return pl.BlockSpec(
            (rows, width), lambda b, j, *_: (b * per_seq + jnp.maximum(j * per_tile - 1, 0), 0))

    def nxt(rows, width, per_tile, per_seq):
        return pl.BlockSpec(
            (rows, width), lambda b, j, *_: (b * per_seq + jnp.minimum((j + 1) * per_tile, per_seq - 1), 0))

    const = lambda shape: _layer_spec(layer, shape)
    kv_rows = tq + 2 * BLOCK
    step_of = lambda b, j, *_: b * tiles + j
    casts = [_cast_specs(layer, rows, cols, batch * tiles, step_of)
             for rows, cols in ((D_MODEL, 2 * D_FF), (D_FF, D_MODEL), (D_MODEL, D_MODEL), (PLE_DIM, D_MODEL))]
    grid_spec = pltpu.PrefetchScalarGridSpec(
        num_scalar_prefetch=1,
        grid=(batch, tiles),
        in_specs=[
            row(D_MODEL), pl.BlockSpec((ATTN_WIDTH // LANES, tq, LANES), lambda b, j, *_: (0, b * tiles + j, 0)),
            prev(BLOCK, 4 * LANES, blk_per_tile, blk_per_seq), row(4 * LANES),
            nxt(BLOCK, 4 * LANES, blk_per_tile, blk_per_seq),
            row(GM_WIDTH), row(GM_WIDTH),
            prev(CONV_HALO, CONV_CH, halo_per_tile, halo_per_seq), row(CONV_CH),
            nxt(CONV_HALO, CONV_CH, halo_per_tile, halo_per_seq),
            const((GM_HEADS, CHUNK, CHUNK)), const((CHUNK, GM_WIDTH)),
            const((CONV_WIDTH, CONV_CH)), const((1, CONV_CH)), const((1, CONV_CH)), const((1, CONV_CH)),
            const((1, D_MODEL)), _resident_spec((D_MODEL, D_MODEL)),
            *[src for src, _ in casts],
        ],
        out_specs=(row(D_MODEL), *[dst for _, dst in casts]),
        scratch_shapes=[
            pltpu.VMEM((ATTN_KV_HEADS, kv_rows, LANES), BF16), pltpu.VMEM((ATTN_KV_HEADS, kv_rows, LANES), BF16),
            pltpu.VMEM((ATTN_KV_HEADS, kv_rows, LANES), BF16), pltpu.VMEM((ATTN_KV_HEADS, kv_rows, LANES), BF16),
            pltpu.VMEM((4, BLOCK, BLOCK), F32),
            pltpu.VMEM((2, ATTN_GROUP // 2 * BLOCK, 2 * 3 * BLOCK), F32),
            pltpu.VMEM((2, ATTN_GROUP // 2 * BLOCK, 2 * 3 * BLOCK), BF16),
            pltpu.VMEM((2, ATTN_GROUP // 2 * BLOCK, LANES), F32),
            pltpu.VMEM((CONV_CH // LANES, tq + 2 * CONV_HALO, LANES), F32),
            pltpu.VMEM((CONV_CH // LANES, tq, LANES), F32),
            pltpu.VMEM((ATTN_WIDTH // LANES, tq, LANES), F32), pltpu.VMEM((tq, GM_WIDTH), F32),
            pltpu.VMEM((tq, D_MODEL), BF16),
        ],
    )
    return pl.pallas_call(
        functools.partial(_mixer_kernel, layer),
        out_shape=(jax.ShapeDtypeStruct((n, D_MODEL), F32),
                   jax.ShapeDtypeStruct((D_MODEL, 2 * D_FF), BF16), jax.ShapeDtypeStruct((D_FF, D_MODEL), BF16),
                   jax.ShapeDtypeStruct((D_MODEL, D_MODEL), BF16), jax.ShapeDtypeStruct((PLE_DIM, D_MODEL), BF16)),
        grid_spec=grid_spec,
        compiler_params=pltpu.CompilerParams(dimension_semantics=("arbitrary", "arbitrary"),
                                             vmem_limit_bytes=VMEM_LIMIT),
        name="mixer",
    )(sink, x, q, kv, kv, kv, u, vln, glu, glu, glu, ws, gm_bias, cw, cb, clg, clb, og, wout,
      wgu, wd, wpg, wpp)


def _ffn_kernel(x_ref, p_ref, gf_ref, wgu_ref, wd_ref, gp_ref, wpg_ref, wpp_ref, o_ref, act_s):
    x = x_ref[...]
    hn = _rms(x, gf_ref[...]).astype(BF16)
    for c0 in range(0, D_FF, FF_CHUNK):
        gate = jnp.dot(hn, wgu_ref[:, c0:c0 + FF_CHUNK], preferred_element_type=F32)
        up = jnp.dot(hn, wgu_ref[:, D_FF + c0:D_FF + c0 + FF_CHUNK], preferred_element_type=F32)
        act_s[:, c0:c0 + FF_CHUNK] = (gate * jax.nn.sigmoid(gate) * up).astype(BF16)
    x = x + jnp.dot(act_s[...], wd_ref[...], preferred_element_type=F32)

    hp = _rms(x, gp_ref[...]).astype(BF16)
    gate = jax.nn.sigmoid(jnp.dot(hp, wpg_ref[...], preferred_element_type=F32))
    proj = jnp.dot(p_ref[...].astype(BF16), wpp_ref[...], preferred_element_type=F32)
    o_ref[...] = x + proj * gate


def _ffn(layer, x, p, gf, wgu, wd, gp, wpg, wpp):
    n = x.shape[0]
    tm = TM_FFN
    row = lambda width: pl.BlockSpec((tm, width), lambda i: (i, 0))
    return pl.pallas_call(
        _ffn_kernel,
        out_shape=jax.ShapeDtypeStruct((n, D_MODEL), F32),
        grid=(n // tm,),
        in_specs=[row(D_MODEL), pl.BlockSpec((None, tm, PLE_DIM), lambda i: (layer, i, 0)),
                  _layer_spec(layer, (1, D_MODEL)),
                  _resident_spec((D_MODEL, 2 * D_FF)), _resident_spec((D_FF, D_MODEL)),
                  _layer_spec(layer, (1, D_MODEL)), _resident_spec((D_MODEL, D_MODEL)),
                  _resident_spec((PLE_DIM, D_MODEL))],
        out_specs=row(D_MODEL),
        scratch_shapes=[pltpu.VMEM((tm, D_FF), BF16)],
        compiler_params=pltpu.CompilerParams(dimension_semantics=("arbitrary",),
                                             vmem_limit_bytes=VMEM_LIMIT),
        name="ffn_ple",
    )(x, p, gf, wgu, wd, gp, wpg, wpp)


def kernel(x, p, positions, norm_mix_g, w_in, q_norm_g, k_norm_g, sink, gm_ln_g, gm_ln_b, gm_ws, gm_bs, conv_w, conv_b, conv_ln_g, conv_ln_b, out_norm_g, w_out, norm_ffn_g, w_gate_up, w_down, ple_norm_g, w_ple_gate, w_ple_proj):
    batch, seq, d = x.shape
    depth = w_in.shape[0]
    n = batch * seq
    assert d == D_MODEL and seq % TQ_MIX == 0
    assert n % TM_PROJ == 0 and n % TM_FFN == 0 and n % TM_ROPE == 0

    vec = lambda a: a[:, None, :]
    ws_b = gm_ws.astype(BF16)
    qg = vec(jnp.tile(q_norm_g, (1, LANES // HEAD_DIM)))
    kg = vec(jnp.tile(k_norm_g, (1, LANES // HEAD_DIM)))
    gm_bias = jnp.repeat(jnp.swapaxes(gm_bs, 1, 2), HEAD_DIM, axis=2)
    sink_flat = sink.reshape(depth * ATTN_Q_HEADS)
    p_flat = p.reshape(depth, n, PLE_DIM)

    tabs = _rope_tables(positions.reshape(1, n))
    xf = x.reshape(n, d)
    for i in range(depth):
        q, kv, u, vln, glu, w_out_b = _inproj(i, xf, vec(norm_mix_g), w_in, qg, kg, vec(gm_ln_g),
                                                 vec(gm_ln_b), tabs, w_out)
        xf, w_gu_b, w_down_b, w_pg_b, w_pp_b = _mixer(
            i, xf, q, kv, u, vln, glu, sink_flat, ws_b, gm_bias, conv_w, vec(conv_b), vec(conv_ln_g),
            vec(conv_ln_b), vec(out_norm_g), w_out_b, w_gate_up, w_down, w_ple_gate, w_ple_proj, batch, seq)
        xf = _ffn(i, xf, p_flat, vec(norm_ffn_g), w_gu_b, w_down_b, vec(ple_norm_g), w_pg_b, w_pp_b)
    return xf.reshape(batch, seq, d)
```

```python
import functools
import math

import numpy as np
import jax
import jax.numpy as jnp
from jax import lax
from jax.experimental import pallas as pl
from jax.experimental.pallas import tpu as pltpu

F32 = jnp.float32
BF16 = jnp.bfloat16

D_MODEL = 1024
HEAD_DIM = 64
ATTN_WIDTH = 512
ATTN_Q_HEADS = 8
ATTN_KV_HEADS = 2
ATTN_GROUP = 4
KV_WIDTH = 128
WINDOW = 128
BLOCK = 128
ROPE_THETA = 500000.0
ROT_DIM = 16
GM_WIDTH = 256
GM_HEADS = 4
CHUNK = 128
CONV_CH = 256
CONV_WIDTH = 31
CONV_PAD = 15
D_FF = 2816
PLE_DIM = 256
EPS = 1e-6
NEG_INF = -1e30

Q_OFF = 0
K_OFF = Q_OFF + ATTN_WIDTH
V_OFF = K_OFF + KV_WIDTH
GM_OFF = V_OFF + KV_WIDTH
CONV_OFF = GM_OFF + 2 * GM_WIDTH
IN_COLS = CONV_OFF + 2 * CONV_CH

LANES = 128
SUBLANES = 8
BF16_ROWS = 16
CONV_HALO = 16
CONV_ROW_STRIDE = 2
CONV_GROUPS = 4
VMEM_BYTES_V7X = 64 * 1024 * 1024
VMEM_LIMIT = VMEM_BYTES_V7X * 7 // 8

TM_ROPE = 8192
TM_PROJ = 512
TQ_MIX = 1024
TM_FFN = 1024
FF_CHUNK = 256
SOFTMAX_ROWS = 32
ATTN_UNROLL = 4

INV_FREQ = [float(ROPE_THETA ** (-(2.0 * j) / ROT_DIM)) for j in range(ROT_DIM // 2)]
SQRT_HALF = float(np.sqrt(0.5))
LOG2E = float(np.log2(np.e))


def _layer_spec(layer, shape):
    nd = len(shape)
    return pl.BlockSpec((None,) + tuple(shape), lambda *_: (layer,) + (0,) * nd,
                        pipeline_mode=pl.Buffered(1))


def _resident_spec(shape):
    nd = len(shape)
    return pl.BlockSpec(tuple(shape), lambda *_: (0,) * nd, pipeline_mode=pl.Buffered(1))


def _cast_block_rows(rows, n_steps):
    r = BF16_ROWS
    while r * n_steps < rows or rows % r:
        r += BF16_ROWS
    return r


def _cast_specs(layer, rows, cols, n_steps, step_of):
    r = _cast_block_rows(rows, n_steps)
    last = rows // r - 1
    src = pl.BlockSpec((None, r, cols), lambda *idx: (layer, jnp.minimum(step_of(*idx), last), 0))
    dst = pl.BlockSpec((r, cols), lambda *idx: (jnp.minimum(step_of(*idx), last), 0))
    return src, dst


def _split_bf16(x):
    hi = x.astype(BF16)
    lo = (x - hi.astype(F32)).astype(BF16)
    return hi, lo


def _rope_table_kernel(pos_ref, c_ref, s_ref):
    half = ROT_DIM // 2
    pos = pos_ref[...].astype(F32)
    row = lax.broadcasted_iota(jnp.int32, (ROT_DIM, 1), 0)
    inv = jnp.zeros((ROT_DIM, 1), F32)
    for j, f in enumerate(INV_FREQ):
        inv = jnp.where((row & (half - 1)) == j, f, inv)
    ang = inv * pos
    cs = jnp.where(row < half, jnp.cos(ang), jnp.sin(ang))

    r = lax.broadcasted_iota(jnp.int32, (ROT_DIM, 2 * LANES), 0)
    col = lax.broadcasted_iota(jnp.int32, (ROT_DIM, 2 * LANES), 1)
    table = col // LANES
    in_head = col & (HEAD_DIM - 1)
    freq = col & (half - 1)
    put_cos = (table == 0) & (in_head < ROT_DIM) & (r == freq)
    put_nsin = (table == 1) & (in_head < half) & (r == freq + half)
    put_sin = (table == 1) & (in_head >= half) & (in_head < ROT_DIM) & (r == freq + half)
    place = jnp.where(put_cos | put_sin, 1.0, jnp.where(put_nsin, -1.0, 0.0)).astype(BF16)

    hi, lo = _split_bf16(cs)
    dn = (((0,), (0,)), ((), ()))
    t = (lax.dot_general(hi, place, dn, preferred_element_type=F32)
         + lax.dot_general(lo, place, dn, preferred_element_type=F32))
    lane = lax.broadcasted_iota(jnp.int32, (1, LANES), 1)
    c_ref[...] = jnp.where((lane & (HEAD_DIM - 1)) < ROT_DIM, t[:, 0:LANES], 1.0)
    s_ref[...] = t[:, LANES:2 * LANES]


def _rope_tables(pos_row):
    n = pos_row.shape[1]
    tm = TM_ROPE
    tab = jax.ShapeDtypeStruct((n, LANES), F32)
    spec = pl.BlockSpec((tm, LANES), lambda i: (i, 0))
    return pl.pallas_call(
        _rope_table_kernel,
        out_shape=(tab, tab),
        grid=(n // tm,),
        in_specs=[pl.BlockSpec((1, tm), lambda i: (0, i))],
        out_specs=(spec, spec),
        name="rope_tables",
    )(pos_row)


def _rms(x, gain):
    ms = jnp.mean(x * x, axis=-1, keepdims=True)
    return x * lax.rsqrt(ms + EPS) * gain


def _layer_norm(x, gain, bias):
    mu = jnp.mean(x, axis=-1, keepdims=True)
    xc = x - mu
    var = jnp.mean(xc * xc, axis=-1, keepdims=True)
    return xc * lax.rsqrt(var + EPS) * gain + bias


def _inproj_kernel(x_ref, g_ref, w_ref, qg_ref, kg_ref, lng_ref, lnb_ref,
                   c_ref, s_ref, wout_ref,
                   q_out, kv_out, u_out, vln_out, glu_out, wout_bf_out, w_bf, z_even, z_odd, *, last_is_even):
    i = pl.program_id(0)
    n_tiles = pl.num_programs(0) - 1

    wout_bf_out[...] = wout_ref[...].astype(BF16)

    def project(z_dst):
        h = _rms(x_ref[...], g_ref[...]).astype(BF16)
        z_dst[...] = jnp.dot(h, w_bf[...], preferred_element_type=F32)

    def finish(z_src):
        cos_t, sin_t = c_ref[...], s_ref[...]
        lane = lax.broadcasted_iota(jnp.int32, (1, LANES), 1)
        head0 = lane < HEAD_DIM

        def head_norm_rope(z, gain):
            sq = z * z
            first = jnp.sum(jnp.where(head0, sq, 0.0), axis=-1, keepdims=True)
            both = jnp.sum(sq, axis=-1, keepdims=True)
            ss = jnp.where(head0, first, both - first)
            zn = z * lax.rsqrt(ss * (1.0 / HEAD_DIM) + EPS) * gain
            partner = jnp.where((lane & (ROT_DIM // 2)) == 0,
                                pltpu.roll(zn, LANES - ROT_DIM // 2, 1), pltpu.roll(zn, ROT_DIM // 2, 1))
            return zn * cos_t + partner * sin_t

        q_gain = qg_ref[...] * (LOG2E / math.sqrt(HEAD_DIM))
        for b in range(ATTN_WIDTH // LANES):
            q_out[b] = head_norm_rope(z_src[:, Q_OFF + b * LANES:Q_OFF + (b + 1) * LANES], q_gain).astype(BF16)

        k = head_norm_rope(z_src[:, K_OFF:V_OFF], kg_ref[...])
        v = z_src[:, V_OFF:GM_OFF]
        kv_out[:, 0 * LANES:1 * LANES] = k.astype(BF16)
        kv_out[:, 1 * LANES:2 * LANES] = pltpu.roll(k, HEAD_DIM, 1).astype(BF16)
        kv_out[:, 2 * LANES:3 * LANES] = v.astype(BF16)
        kv_out[:, 3 * LANES:4 * LANES] = pltpu.roll(v, HEAD_DIM, 1).astype(BF16)

        zuv = z_src[:, GM_OFF:CONV_OFF]
        uv = 0.5 * zuv * (1.0 + lax.erf(zuv * SQRT_HALF))
        u_out[...] = uv[:, 0:GM_WIDTH]
        vln_out[...] = _layer_norm(uv[:, GM_WIDTH:], lng_ref[...], lnb_ref[...]).astype(BF16)

        glu_out[...] = z_src[:, CONV_OFF:CONV_OFF + CONV_CH] * jax.nn.sigmoid(z_src[:, CONV_OFF + CONV_CH:])

    inner = jnp.logical_and(i > 0, i < n_tiles)

    @pl.when(i == 0)
    def _():
        w_bf[...] = w_ref[...].astype(BF16)
        project(z_even)

    @pl.when(jnp.logical_and(inner, i % 2 == 1))
    def _():
        project(z_odd)
        finish(z_even)

    @pl.when(jnp.logical_and(inner, i % 2 == 0))
    def _():
        project(z_even)
        finish(z_odd)

    @pl.when(i == n_tiles)
    def _():
        finish(z_even if last_is_even else z_odd)


def _inproj(layer, x, g, w, qg, kg, lng, lnb, tabs, wout):
    n = x.shape[0]
    tm = TM_PROJ
    n_tiles = n // tm
    cur = lambda width: pl.BlockSpec((tm, width), lambda i: (jnp.minimum(i, n_tiles - 1), 0))
    lag = lambda width: pl.BlockSpec((tm, width), lambda i: (jnp.maximum(i - 1, 0), 0))
    out_shape = (
        jax.ShapeDtypeStruct((ATTN_WIDTH // LANES, n, LANES), BF16),
        jax.ShapeDtypeStruct((n, 4 * LANES), BF16),
        jax.ShapeDtypeStruct((n, GM_WIDTH), F32),
        jax.ShapeDtypeStruct((n, GM_WIDTH), BF16),
        jax.ShapeDtypeStruct((n, CONV_CH), F32),
        jax.ShapeDtypeStruct((D_MODEL, D_MODEL), BF16),
    )
    wout_src, wout_dst = _cast_specs(layer, D_MODEL, D_MODEL, n_tiles + 1, lambda i: i)
    return pl.pallas_call(
        functools.partial(_inproj_kernel, last_is_even=(n_tiles - 1) % 2 == 0),
        out_shape=out_shape,
        grid=(n_tiles + 1,),
        in_specs=[cur(D_MODEL), _layer_spec(layer, (1, D_MODEL)), _layer_spec(layer, (D_MODEL, IN_COLS)),
                  _layer_spec(layer, (1, LANES)), _layer_spec(layer, (1, LANES)),
                  _layer_spec(layer, (1, GM_WIDTH)), _layer_spec(layer, (1, GM_WIDTH)),
                  lag(LANES), lag(LANES), wout_src],
        out_specs=(pl.BlockSpec((ATTN_WIDTH // LANES, tm, LANES), lambda i: (0, jnp.maximum(i - 1, 0), 0)),
                   lag(4 * LANES), lag(GM_WIDTH), lag(GM_WIDTH), lag(CONV_CH), wout_dst),
        scratch_shapes=[pltpu.VMEM((D_MODEL, IN_COLS), BF16),
                        pltpu.VMEM((tm, IN_COLS), F32), pltpu.VMEM((tm, IN_COLS), F32)],
        compiler_params=pltpu.CompilerParams(dimension_semantics=("arbitrary",),
                                             vmem_limit_bytes=VMEM_LIMIT),
        name="in_proj",
    )(x, g, w, qg, kg, lng, lnb, *tabs, wout)


def _mixer_kernel(layer, sink_ref, x_ref, q_ref, kvp_ref, kvc_ref, kvn_ref, u_ref, vln_ref,
                  glup_ref, gluc_ref, glun_ref,
                  ws_ref, gmb_ref, cw_ref, cb_ref, clg_ref, clb_ref, og_ref, wout_ref,
                  wgu_ref, wd_ref, wpg_ref, wpp_ref,
                  o_ref, wgu_bf_out, wd_bf_out, wpg_bf_out, wpp_bf_out,
                  ka_s, kb_s, va_s, vb_s, bias_s, s_s, p_s, scale_s, glu_s, y_s, attn_s, gm_s, merged_s):
    tq = x_ref.shape[0]
    nblk = tq // BLOCK
    j = pl.program_id(1)
    first = j == 0
    last = j == pl.num_programs(1) - 1

    wgu_bf_out[...] = wgu_ref[...].astype(BF16)
    wd_bf_out[...] = wd_ref[...].astype(BF16)
    wpg_bf_out[...] = wpg_ref[...].astype(BF16)
    wpp_bf_out[...] = wpp_ref[...].astype(BF16)

    lane = lax.broadcasted_iota(jnp.int32, (1, LANES), 1)
    head0 = lane < HEAD_DIM
    zero = jnp.zeros((), BF16)
    for src, r0, rows in ((kvp_ref, 0, BLOCK), (kvc_ref, BLOCK, tq), (kvn_ref, BLOCK + tq, BLOCK)):
        dst = slice(r0, r0 + rows)
        k, k_sw = src[:, 0 * LANES:1 * LANES], src[:, 1 * LANES:2 * LANES]
        v, v_sw = src[:, 2 * LANES:3 * LANES], src[:, 3 * LANES:4 * LANES]
        ka_s[0, dst, :] = jnp.where(head0, k, zero)
        kb_s[0, dst, :] = jnp.where(head0, zero, k_sw)
        ka_s[1, dst, :] = jnp.where(head0, k_sw, zero)
        kb_s[1, dst, :] = jnp.where(head0, zero, k)
        va_s[0, dst, :] = jnp.where(head0, v, zero)
        vb_s[0, dst, :] = jnp.where(head0, zero, v_sw)
        va_s[1, dst, :] = jnp.where(head0, v_sw, zero)
        vb_s[1, dst, :] = jnp.where(head0, zero, v)

    qi = lax.broadcasted_iota(jnp.int32, (BLOCK, BLOCK), 0)
    kj = lax.broadcasted_iota(jnp.int32, (BLOCK, BLOCK), 1)
    band_lo = jnp.where(kj >= qi, 0.0, NEG_INF)
    band_hi = jnp.where(kj <= qi, 0.0, NEG_INF)
    bias_s[0] = band_lo
    bias_s[1] = band_hi
    bias_s[2] = jnp.where(first, NEG_INF, band_lo)
    bias_s[3] = jnp.where(last, NEG_INF, band_hi)

    n_pairs = ATTN_GROUP // 2

    def scores(nb, h):
        r0 = pl.multiple_of(nb * BLOCK, BLOCK)
        q_st = jnp.concatenate([q_ref[n_pairs * h + pp, pl.ds(r0, BLOCK), :] for pp in range(n_pairs)], axis=0)
        k_st = jnp.concatenate([ka_s[h, pl.ds(r0, 3 * BLOCK), :], kb_s[h, pl.ds(r0, 3 * BLOCK), :]], axis=0)
        s_s[h] = lax.dot_general(q_st, k_st, (((1,), (1,)), ((), ())),
                                 preferred_element_type=F32)

    def softmax_pv(nb, h):
        r0 = pl.multiple_of(nb * BLOCK, BLOCK)
        lo_idx = jnp.where(nb == 0, 2, 0)
        hi_idx = jnp.where(nb == nblk - 1, 3, 1)
        v_st = jnp.concatenate([va_s[h, pl.ds(r0, 3 * BLOCK), :], vb_s[h, pl.ds(r0, 3 * BLOCK), :]], axis=0)
        for pp in range(n_pairs):
            for rc in range(0, BLOCK, SOFTMAX_ROWS):
                rr = slice(pp * BLOCK + rc, pp * BLOCK + rc + SOFTMAX_ROWS)
                br = slice(rc, rc + SOFTMAX_ROWS)
                sinks, maxes, sums = [], [], []
                for half in range(2):
                    sink = sink_ref[layer * ATTN_Q_HEADS + ATTN_GROUP * h + 2 * pp + half] * LOG2E
                    c0 = half * 3 * BLOCK
                    s0 = s_s[h, rr, c0:c0 + BLOCK] + bias_s[lo_idx, br, :]
                    s1 = s_s[h, rr, c0 + BLOCK:c0 + 2 * BLOCK]
                    s2 = s_s[h, rr, c0 + 2 * BLOCK:c0 + 3 * BLOCK] + bias_s[hi_idx, br, :]
                    m = jnp.max(jnp.maximum(jnp.maximum(s0, s1), s2), axis=-1, keepdims=True)
                    m = jnp.maximum(m, sink)
                    e0, e1, e2 = jnp.exp2(s0 - m), jnp.exp2(s1 - m), jnp.exp2(s2 - m)
                    p_s[h, rr, c0:c0 + BLOCK] = e0.astype(BF16)
                    p_s[h, rr, c0 + BLOCK:c0 + 2 * BLOCK] = e1.astype(BF16)
                    p_s[h, rr, c0 + 2 * BLOCK:c0 + 3 * BLOCK] = e2.astype(BF16)
                    sinks.append(sink)
                    maxes.append(m)
                    sums.append(jnp.sum(e0 + e1 + e2, axis=-1, keepdims=True))
                sink_term = jnp.exp2(jnp.where(head0, sinks[0], sinks[1]) - jnp.where(head0, maxes[0], maxes[1]))
                scale_s[h, rr, :] = 1.0 / (jnp.where(head0, sums[0], sums[1]) + sink_term)
        o = jnp.dot(p_s[h], v_st, preferred_element_type=F32) * scale_s[h]
        for pp in range(n_pairs):
            attn_s[n_pairs * h + pp, pl.ds(r0, BLOCK), :] = o[pp * BLOCK:(pp + 1) * BLOCK, :]

    scores(0, 0)

    def attn_step(nb, carry):
        scores(nb, 1)
        softmax_pv(nb, 0)
        scores(jnp.minimum(nb + 1, nblk - 1), 0)
        softmax_pv(nb, 1)
        return carry

    lax.fori_loop(0, nblk, attn_step, 0, unroll=ATTN_UNROLL)

    for nb in range(nblk):
        rows = slice(nb * BLOCK, (nb + 1) * BLOCK)
        for hp in range(GM_HEADS // 2):
            cols = slice(hp * LANES, (hp + 1) * LANES)
            vblk = vln_ref[rows, cols]
            lhs = jnp.concatenate([ws_ref[2 * hp], ws_ref[2 * hp + 1]], axis=1)
            rhs = jnp.concatenate([jnp.where(head0, vblk, zero), jnp.where(head0, zero, vblk)], axis=0)
            sgate = jnp.dot(lhs, rhs, preferred_element_type=F32) + gmb_ref[:, cols]
            gm_s[rows, cols] = u_ref[rows, cols] * sgate

    n_ch = CONV_CH // LANES
    for ch in range(n_ch):
        cl = slice(ch * LANES, (ch + 1) * LANES)
        glu_s[ch, 0:CONV_HALO, :] = jnp.where(first, 0.0, glup_ref[:, cl])
        glu_s[ch, CONV_HALO:CONV_HALO + tq, :] = gluc_ref[:, cl]
        glu_s[ch, CONV_HALO + tq:, :] = jnp.where(last, 0.0, glun_ref[:, cl])
    group = SUBLANES * CONV_ROW_STRIDE

    def conv_step(i, carry):
        base = pl.multiple_of(i * (CONV_GROUPS * group), CONV_GROUPS * group)
        for ch in range(n_ch):
            cl = slice(ch * LANES, (ch + 1) * LANES)
            offs = [g * group + ph for g in range(CONV_GROUPS) for ph in range(CONV_ROW_STRIDE)]
            accs = [jnp.broadcast_to(cb_ref[:, cl], (SUBLANES, LANES)) for _ in offs]
            for t in range(CONV_WIDTH):
                w = cw_ref[t:t + 1, cl]
                for a, off in enumerate(offs):
                    start = base + (off + CONV_HALO - CONV_PAD + t)
                    accs[a] = accs[a] + w * glu_s[ch, pl.ds(start, SUBLANES, stride=CONV_ROW_STRIDE), :]
            for a, off in enumerate(offs):
                y_s[ch, pl.ds(base + off, SUBLANES, stride=CONV_ROW_STRIDE), :] = accs[a]
        return carry

    lax.fori_loop(0, tq // (CONV_GROUPS * group), conv_step, 0)

    rchunk = 128
    for r0 in range(0, tq, rchunk):
        rs = slice(r0, r0 + rchunk)
        y = jnp.concatenate([y_s[ch, rs, :] for ch in range(n_ch)], axis=1)
        y = _layer_norm(y, clg_ref[...], clb_ref[...])
        y = y * jax.nn.sigmoid(y)
        merged_s[rs, ATTN_WIDTH + GM_WIDTH:] = _rms(y, og_ref[:, ATTN_WIDTH + GM_WIDTH:]).astype(BF16)
        attn = jnp.concatenate([attn_s[pb, rs, :] for pb in range(ATTN_WIDTH // LANES)], axis=1)
        merged_s[rs, 0:ATTN_WIDTH] = _rms(attn, og_ref[:, 0:ATTN_WIDTH]).astype(BF16)
        merged_s[rs, ATTN_WIDTH:ATTN_WIDTH + GM_WIDTH] = _rms(
            gm_s[rs, :], og_ref[:, ATTN_WIDTH:ATTN_WIDTH + GM_WIDTH]).astype(BF16)
    o_ref[...] = x_ref[...] + jnp.dot(merged_s[...], wout_ref[...], preferred_element_type=F32)


def _mixer(layer, x, q, kv, u, vln, glu, sink, ws, gm_bias, cw, cb, clg, clb, og, wout,
           wgu, wd, wpg, wpp, batch, seq):
    n = x.shape[0]
    tq = TQ_MIX
    tiles = seq // tq
    blk_per_tile = tq // BLOCK
    blk_per_seq = seq // BLOCK
    halo_per_tile = tq // CONV_HALO
    halo_per_seq = seq // CONV_HALO

    def row(width):
        return pl.BlockSpec((tq, width), lambda b, j, *_: (b * tiles + j, 0))

    def prev(rows, width, per_tile, per_seq):
        return pl.BlockSpec(
            (rows, width), lambda b, j, *_: (b * per_seq + jnp.maximum(j * per_tile - 1, 0), 0))

    def nxt(rows, width, per_tile, per_seq):
        return pl.BlockSpec(
            (rows, width), lambda b, j, *_: (b * per_seq + jnp.minimum((j + 1) * per_tile, per_seq - 1), 0))

    const = lambda shape: _layer_spec(layer, shape)
    kv_rows = tq + 2 * BLOCK
    step_of = lambda b, j, *_: b * tiles + j
    casts = [_cast_specs(layer, rows, cols, batch * tiles, step_of)
             for rows, cols in ((D_MODEL, 2 * D_FF), (D_FF, D_MODEL), (D_MODEL, D_MODEL), (PLE_DIM, D_MODEL))]
    grid_spec = pltpu.PrefetchScalarGridSpec(
        num_scalar_prefetch=1,
        grid=(batch, tiles),
        in_specs=[
            row(D_MODEL), pl.BlockSpec((ATTN_WIDTH // LANES, tq, LANES), lambda b, j, *_: (0, b * tiles + j, 0)),
            prev(BLOCK, 4 * LANES, blk_per_tile, blk_per_seq), row(4 * LANES),
            nxt(BLOCK, 4 * LANES, blk_per_tile, blk_per_seq),
            row(GM_WIDTH), row(GM_WIDTH),
            prev(CONV_HALO, CONV_CH, halo_per_tile, halo_per_seq), row(CONV_CH),
            nxt(CONV_HALO, CONV_CH, halo_per_tile, halo_per_seq),
            const((GM_HEADS, CHUNK, CHUNK)), const((CHUNK, GM_WIDTH)),
            const((CONV_WIDTH, CONV_CH)), const((1, CONV_CH)), const((1, CONV_CH)), const((1, CONV_CH)),
            const((1, D_MODEL)), _resident_spec((D_MODEL, D_MODEL)),
            *[src for src, _ in casts],
        ],
        out_specs=(row(D_MODEL), *[dst for _, dst in casts]),
        scratch_shapes=[
            pltpu.VMEM((ATTN_KV_HEADS, kv_rows, LANES), BF16), pltpu.VMEM((ATTN_KV_HEADS, kv_rows, LANES), BF16),
            pltpu.VMEM((ATTN_KV_HEADS, kv_rows, LANES), BF16), pltpu.VMEM((ATTN_KV_HEADS, kv_rows, LANES), BF16),
            pltpu.VMEM((4, BLOCK, BLOCK), F32),
            pltpu.VMEM((2, ATTN_GROUP // 2 * BLOCK, 2 * 3 * BLOCK), F32),
            pltpu.VMEM((2, ATTN_GROUP // 2 * BLOCK, 2 * 3 * BLOCK), BF16),
            pltpu.VMEM((2, ATTN_GROUP // 2 * BLOCK, LANES), F32),
            pltpu.VMEM((CONV_CH // LANES, tq + 2 * CONV_HALO, LANES), F32),
            pltpu.VMEM((CONV_CH // LANES, tq, LANES), F32),
            pltpu.VMEM((ATTN_WIDTH // LANES, tq, LANES), F32), pltpu.VMEM((tq, GM_WIDTH), F32),
            pltpu.VMEM((tq, D_MODEL), BF16),
        ],
    )
    return pl.pallas_call(
        functools.partial(_mixer_kernel, layer),
        out_shape=(jax.ShapeDtypeStruct((n, D_MODEL), F32),
                   jax.ShapeDtypeStruct((D_MODEL, 2 * D_FF), BF16), jax.ShapeDtypeStruct((D_FF, D_MODEL), BF16),
                   jax.ShapeDtypeStruct((D_MODEL, D_MODEL), BF16), jax.ShapeDtypeStruct((PLE_DIM, D_MODEL), BF16)),
        grid_spec=grid_spec,
        compiler_params=pltpu.CompilerParams(dimension_semantics=("arbitrary", "arbitrary"),
                                             vmem_limit_bytes=VMEM_LIMIT),
        name="mixer",
    )(sink, x, q, kv, kv, kv, u, vln, glu, glu, glu, ws, gm_bias, cw, cb, clg, clb, og, wout,
      wgu, wd, wpg, wpp)


def _ffn_kernel(x_ref, p_ref, gf_ref, wgu_ref, wd_ref, gp_ref, wpg_ref, wpp_ref, o_ref, act_s):
    x = x_ref[...]
    hn = _rms(x, gf_ref[...]).astype(BF16)
    for c0 in range(0, D_FF, FF_CHUNK):
        gate = jnp.dot(hn, wgu_ref[:, c0:c0 + FF_CHUNK], preferred_element_type=F32)
        up = jnp.dot(hn, wgu_ref[:, D_FF + c0:D_FF + c0 + FF_CHUNK], preferred_element_type=F32)
        act_s[:, c0:c0 + FF_CHUNK] = (gate * jax.nn.sigmoid(gate) * up).astype(BF16)
    x = x + jnp.dot(act_s[...], wd_ref[...], preferred_element_type=F32)

    hp = _rms(x, gp_ref[...]).astype(BF16)
    gate = jax.nn.sigmoid(jnp.dot(hp, wpg_ref[...], preferred_element_type=F32))
    proj = jnp.dot(p_ref[...].astype(BF16), wpp_ref[...], preferred_element_type=F32)
    o_ref[...] = x + proj * gate


def _ffn(layer, x, p, gf, wgu, wd, gp, wpg, wpp):
    n = x.shape[0]
    tm = TM_FFN
    row = lambda width: pl.BlockSpec((tm, width), lambda i: (i, 0))
    return pl.pallas_call(
        _ffn_kernel,
        out_shape=jax.ShapeDtypeStruct((n, D_MODEL), F32),
        grid=(n // tm,),
        in_specs=[row(D_MODEL), pl.BlockSpec((None, tm, PLE_DIM), lambda i: (layer, i, 0)),
                  _layer_spec(layer, (1, D_MODEL)),
                  _resident_spec((D_MODEL, 2 * D_FF)), _resident_spec((D_FF, D_MODEL)),
                  _layer_spec(layer, (1, D_MODEL)), _resident_spec((D_MODEL, D_MODEL)),
                  _resident_spec((PLE_DIM, D_MODEL))],
        out_specs=row(D_MODEL),
        scratch_shapes=[pltpu.VMEM((tm, D_FF), BF16)],
        compiler_params=pltpu.CompilerParams(dimension_semantics=("arbitrary",),
                                             vmem_limit_bytes=VMEM_LIMIT),
        name="ffn_ple",
    )(x, p, gf, wgu, wd, gp, wpg, wpp)


def kernel(x, p, positions, norm_mix_g, w_in, q_norm_g, k_norm_g, sink, gm_ln_g, gm_ln_b, gm_ws, gm_bs, conv_w, conv_b, conv_ln_g, conv_ln_b, out_norm_g, w_out, norm_ffn_g, w_gate_up, w_down, ple_norm_g, w_ple_gate, w_ple_proj):
    batch, seq, d = x.shape
    depth = w_in.shape[0]
    n = batch * seq
    assert d == D_MODEL and seq % TQ_MIX == 0
    assert n % TM_PROJ == 0 and n % TM_FFN == 0 and n % TM_ROPE == 0

    vec = lambda a: a[:, None, :]
    ws_b = gm_ws.astype(BF16)
    qg = vec(jnp.tile(q_norm_g, (1, LANES // HEAD_DIM)))
    kg = vec(jnp.tile(k_norm_g, (1, LANES // HEAD_DIM)))
    gm_bias = jnp.repeat(jnp.swapaxes(gm_bs, 1, 2), HEAD_DIM, axis=2)
    sink_flat = sink.reshape(depth * ATTN_Q_HEADS)
    p_flat = p.reshape(depth, n, PLE_DIM)

    tabs = _rope_tables(positions.reshape(1, n))
    xf = x.reshape(n, d)
    for i in range(depth):
        q, kv, u, vln, glu, w_out_b = _inproj(i, xf, vec(norm_mix_g), w_in, qg, kg, vec(gm_ln_g),
                                                 vec(gm_ln_b), tabs, w_out)
        xf, w_gu_b, w_down_b, w_pg_b, w_pp_b = _mixer(
            i, xf, q, kv, u, vln, glu, sink_flat, ws_b, gm_bias, conv_w, vec(conv_b), vec(conv_ln_g),
            vec(conv_ln_b), vec(out_norm_g), w_out_b, w_gate_up, w_down, w_ple_gate, w_ple_proj, batch, seq)
        xf = _ffn(i, xf, p_flat, vec(norm_ffn_g), w_gu_b, w_down_b, vec(ple_norm_g), w_pg_b, w_pp_b)
    return xf.reshape(batch, seq, d)
```

```python
import functools
import math

import numpy as np
import jax
import jax.numpy as jnp
from jax import lax
from jax.experimental import pallas as pl
from jax.experimental.pallas import tpu as pltpu

F32 = jnp.float32
BF16 = jnp.bfloat16

D_MODEL = 1024
HEAD_DIM = 64
ATTN_WIDTH = 512
ATTN_Q_HEADS = 8
ATTN_KV_HEADS = 2
ATTN_GROUP = 4
KV_WIDTH = 128
WINDOW = 128
BLOCK = 128
ROPE_THETA = 500000.0
ROT_DIM = 16
GM_WIDTH = 256
GM_HEADS = 4
CHUNK = 128
CONV_CH = 256
CONV_WIDTH = 31
CONV_PAD = 15
D_FF = 2816
PLE_DIM = 256
EPS = 1e-6
NEG_INF = -1e30

Q_OFF = 0
K_OFF = Q_OFF + ATTN_WIDTH
V_OFF = K_OFF + KV_WIDTH
GM_OFF = V_OFF + KV_WIDTH
CONV_OFF = GM_OFF + 2 * GM_WIDTH
IN_COLS = CONV_OFF + 2 * CONV_CH

LANES = 128
SUBLANES = 8
BF16_ROWS = 16
CONV_HALO = 16
CONV_ROW_STRIDE = 2
CONV_GROUPS = 4
VMEM_BYTES_V7X = 64 * 1024 * 1024
VMEM_LIMIT = VMEM_BYTES_V7X * 7 // 8

TM_ROPE = 8192
TM_PROJ = 512
TQ_MIX = 1024
TM_FFN = 1024
FF_CHUNK = 256
SOFTMAX_ROWS = 32
PROJ_ROWS = 256
ATTN_UNROLL = 4

INV_FREQ = [float(ROPE_THETA ** (-(2.0 * j) / ROT_DIM)) for j in range(ROT_DIM // 2)]
SQRT_HALF = float(np.sqrt(0.5))
LOG2E = float(np.log2(np.e))


def _layer_spec(layer, shape):
    nd = len(shape)
    return pl.BlockSpec((None,) + tuple(shape), lambda *_: (layer,) + (0,) * nd,
                        pipeline_mode=pl.Buffered(1))


def _resident_spec(shape):
    nd = len(shape)
    return pl.BlockSpec(tuple(shape), lambda *_: (0,) * nd, pipeline_mode=pl.Buffered(1))


def _cast_block_rows(rows, n_steps):
    r = BF16_ROWS
    while r * n_steps < rows or rows % r:
        r += BF16_ROWS
    return r


def _cast_specs(layer, rows, cols, n_steps, step_of):
    r = _cast_block_rows(rows, n_steps)
    last = rows // r - 1
    src = pl.BlockSpec((None, r, cols), lambda *idx: (layer, jnp.minimum(step_of(*idx), last), 0))
    dst = pl.BlockSpec((r, cols), lambda *idx: (jnp.minimum(step_of(*idx), last), 0))
    return src, dst


def _split_bf16(x):
    hi = x.astype(BF16)
    lo = (x - hi.astype(F32)).astype(BF16)
    return hi, lo


def _rope_table_kernel(pos_ref, c_ref, s_ref):
    half = ROT_DIM // 2
    pos = pos_ref[...].astype(F32)
    row = lax.broadcasted_iota(jnp.int32, (ROT_DIM, 1), 0)
    inv = jnp.zeros((ROT_DIM, 1), F32)
    for j, f in enumerate(INV_FREQ):
        inv = jnp.where((row & (half - 1)) == j, f, inv)
    ang = inv * pos
    cs = jnp.where(row < half, jnp.cos(ang), jnp.sin(ang))

    r = lax.broadcasted_iota(jnp.int32, (ROT_DIM, 2 * LANES), 0)
    col = lax.broadcasted_iota(jnp.int32, (ROT_DIM, 2 * LANES), 1)
    table = col // LANES
    in_head = col & (HEAD_DIM - 1)
    freq = col & (half - 1)
    put_cos = (table == 0) & (in_head < ROT_DIM) & (r == freq)
    put_nsin = (table == 1) & (in_head < half) & (r == freq + half)
    put_sin = (table == 1) & (in_head >= half) & (in_head < ROT_DIM) & (r == freq + half)
    place = jnp.where(put_cos | put_sin, 1.0, jnp.where(put_nsin, -1.0, 0.0)).astype(BF16)

    hi, lo = _split_bf16(cs)
    dn = (((0,), (0,)), ((), ()))
    t = (lax.dot_general(hi, place, dn, preferred_element_type=F32)
         + lax.dot_general(lo, place, dn, preferred_element_type=F32))
    lane = lax.broadcasted_iota(jnp.int32, (1, LANES), 1)
    c_ref[...] = jnp.where((lane & (HEAD_DIM - 1)) < ROT_DIM, t[:, 0:LANES], 1.0)
    s_ref[...] = t[:, LANES:2 * LANES]


def _rope_tables(pos_row):
    n = pos_row.shape[1]
    tm = TM_ROPE
    tab = jax.ShapeDtypeStruct((n, LANES), F32)
    spec = pl.BlockSpec((tm, LANES), lambda i: (i, 0))
    return pl.pallas_call(
        _rope_table_kernel,
        out_shape=(tab, tab),
        grid=(n // tm,),
        in_specs=[pl.BlockSpec((1, tm), lambda i: (0, i))],
        out_specs=(spec, spec),
        name="rope_tables",
    )(pos_row)


def _rms(x, gain):
    ms = jnp.mean(x * x, axis=-1, keepdims=True)
    return x * lax.rsqrt(ms + EPS) * gain


def _layer_norm(x, gain, bias):
    mu = jnp.mean(x, axis=-1, keepdims=True)
    xc = x - mu
    var = jnp.mean(xc * xc, axis=-1, keepdims=True)
    return xc * lax.rsqrt(var + EPS) * gain + bias


def _inproj_kernel(x_ref, g_ref, w_ref, qg_ref, kg_ref, lng_ref, lnb_ref,
                   c_ref, s_ref, wout_ref,
                   q_out, kv_out, u_out, vln_out, glu_out, wout_bf_out, w_bf, z_even, z_odd, *, last_is_even):
    i = pl.program_id(0)
    n_tiles = pl.num_programs(0) - 1

    wout_bf_out[...] = wout_ref[...].astype(BF16)

    def project(z_dst):
        h = _rms(x_ref[...], g_ref[...]).astype(BF16)
        z_dst[...] = jnp.dot(h, w_bf[...], preferred_element_type=F32)

    def finish(z_src):
        cos_t, sin_t = c_ref[...], s_ref[...]
        lane = lax.broadcasted_iota(jnp.int32, (1, LANES), 1)
        head0 = lane < HEAD_DIM

        def head_norm_rope(z, gain):
            sq = z * z
            first = jnp.sum(jnp.where(head0, sq, 0.0), axis=-1, keepdims=True)
            both = jnp.sum(sq, axis=-1, keepdims=True)
            ss = jnp.where(head0, first, both - first)
            zn = z * lax.rsqrt(ss * (1.0 / HEAD_DIM) + EPS) * gain
            partner = jnp.where((lane & (ROT_DIM // 2)) == 0,
                                pltpu.roll(zn, LANES - ROT_DIM // 2, 1), pltpu.roll(zn, ROT_DIM // 2, 1))
            return zn * cos_t + partner * sin_t

        q_gain = qg_ref[...] * (LOG2E / math.sqrt(HEAD_DIM))
        for b in range(ATTN_WIDTH // LANES):
            q_out[b] = head_norm_rope(z_src[:, Q_OFF + b * LANES:Q_OFF + (b + 1) * LANES], q_gain).astype(BF16)

        k = head_norm_rope(z_src[:, K_OFF:V_OFF], kg_ref[...])
        v = z_src[:, V_OFF:GM_OFF]
        kv_out[:, 0 * LANES:1 * LANES] = k.astype(BF16)
        kv_out[:, 1 * LANES:2 * LANES] = pltpu.roll(k, HEAD_DIM, 1).astype(BF16)
        kv_out[:, 2 * LANES:3 * LANES] = v.astype(BF16)
        kv_out[:, 3 * LANES:4 * LANES] = pltpu.roll(v, HEAD_DIM, 1).astype(BF16)

        zuv = z_src[:, GM_OFF:CONV_OFF]
        uv = 0.5 * zuv * (1.0 + lax.erf(zuv * SQRT_HALF))
        u_out[...] = uv[:, 0:GM_WIDTH]
        vln_out[...] = _layer_norm(uv[:, GM_WIDTH:], lng_ref[...], lnb_ref[...]).astype(BF16)

        glu_out[...] = z_src[:, CONV_OFF:CONV_OFF + CONV_CH] * jax.nn.sigmoid(z_src[:, CONV_OFF + CONV_CH:])

    inner = jnp.logical_and(i > 0, i < n_tiles)

    @pl.when(i == 0)
    def _():
        w_bf[...] = w_ref[...].astype(BF16)
        project(z_even)

    @pl.when(jnp.logical_and(inner, i % 2 == 1))
    def _():
        project(z_odd)
        finish(z_even)

    @pl.when(jnp.logical_and(inner, i % 2 == 0))
    def _():
        project(z_even)
        finish(z_odd)

    @pl.when(i == n_tiles)
    def _():
        finish(z_even if last_is_even else z_odd)


def _inproj(layer, x, g, w, qg, kg, lng, lnb, tabs, wout):
    n = x.shape[0]
    tm = TM_PROJ
    n_tiles = n // tm
    cur = lambda width: pl.BlockSpec((tm, width), lambda i: (jnp.minimum(i, n_tiles - 1), 0))
    lag = lambda width: pl.BlockSpec((tm, width), lambda i: (jnp.maximum(i - 1, 0), 0))
    out_shape = (
        jax.ShapeDtypeStruct((ATTN_WIDTH // LANES, n, LANES), BF16),
        jax.ShapeDtypeStruct((n, 4 * LANES), BF16),
        jax.ShapeDtypeStruct((n, GM_WIDTH), F32),
        jax.ShapeDtypeStruct((n, GM_WIDTH), BF16),
        jax.ShapeDtypeStruct((n, CONV_CH), F32),
        jax.ShapeDtypeStruct((D_MODEL, D_MODEL), BF16),
    )
    wout_src, wout_dst = _cast_specs(layer, D_MODEL, D_MODEL, n_tiles + 1, lambda i: i)
    return pl.pallas_call(
        functools.partial(_inproj_kernel, last_is_even=(n_tiles - 1) % 2 == 0),
        out_shape=out_shape,
        grid=(n_tiles + 1,),
        in_specs=[cur(D_MODEL), _layer_spec(layer, (1, D_MODEL)), _layer_spec(layer, (D_MODEL, IN_COLS)),
                  _layer_spec(layer, (1, LANES)), _layer_spec(layer, (1, LANES)),
                  _layer_spec(layer, (1, GM_WIDTH)), _layer_spec(layer, (1, GM_WIDTH)),
                  lag(LANES), lag(LANES), wout_src],
        out_specs=(pl.BlockSpec((ATTN_WIDTH // LANES, tm, LANES), lambda i: (0, jnp.maximum(i - 1, 0), 0)),
                   lag(4 * LANES), lag(GM_WIDTH), lag(GM_WIDTH), lag(CONV_CH), wout_dst),
        scratch_shapes=[pltpu.VMEM((D_MODEL, IN_COLS), BF16),
                        pltpu.VMEM((tm, IN_COLS), F32), pltpu.VMEM((tm, IN_COLS), F32)],
        compiler_params=pltpu.CompilerParams(dimension_semantics=("arbitrary",),
                                             vmem_limit_bytes=VMEM_LIMIT),
        name="in_proj",
    )(x, g, w, qg, kg, lng, lnb, *tabs, wout)


def _mixer_kernel(layer, sink_ref, x_ref, q_ref, kvp_ref, kvc_ref, kvn_ref, u_ref, vln_ref,
                  glup_ref, gluc_ref, glun_ref,
                  ws_ref, gmb_ref, cw_ref, cb_ref, clg_ref, clb_ref, og_ref, wout_ref,
                  wgu_ref, wd_ref, wpg_ref, wpp_ref,
                  o_ref, wgu_bf_out, wd_bf_out, wpg_bf_out, wpp_bf_out,
                  ka_s, kb_s, va_s, vb_s, bias_s, s_s, p_s, scale_s, glu_s, y_s, attn_s, gm_s, merged_s):
    tq = x_ref.shape[0]
    nblk = tq // BLOCK
    j = pl.program_id(1)
    first = j == 0
    last = j == pl.num_programs(1) - 1

    wgu_bf_out[...] = wgu_ref[...].astype(BF16)
    wd_bf_out[...] = wd_ref[...].astype(BF16)
    wpg_bf_out[...] = wpg_ref[...].astype(BF16)
    wpp_bf_out[...] = wpp_ref[...].astype(BF16)

    lane = lax.broadcasted_iota(jnp.int32, (1, LANES), 1)
    head0 = lane < HEAD_DIM
    zero = jnp.zeros((), BF16)
    for src, r0, rows in ((kvp_ref, 0, BLOCK), (kvc_ref, BLOCK, tq), (kvn_ref, BLOCK + tq, BLOCK)):
        dst = slice(r0, r0 + rows)
        k, k_sw = src[:, 0 * LANES:1 * LANES], src[:, 1 * LANES:2 * LANES]
        v, v_sw = src[:, 2 * LANES:3 * LANES], src[:, 3 * LANES:4 * LANES]
        ka_s[0, dst, :] = jnp.where(head0, k, zero)
        kb_s[0, dst, :] = jnp.where(head0, zero, k_sw)
        ka_s[1, dst, :] = jnp.where(head0, k_sw, zero)
        kb_s[1, dst, :] = jnp.where(head0, zero, k)
        va_s[0, dst, :] = jnp.where(head0, v, zero)
        vb_s[0, dst, :] = jnp.where(head0, zero, v_sw)
        va_s[1, dst, :] = jnp.where(head0, v_sw, zero)
        vb_s[1, dst, :] = jnp.where(head0, zero, v)

    qi = lax.broadcasted_iota(jnp.int32, (BLOCK, BLOCK), 0)
    kj = lax.broadcasted_iota(jnp.int32, (BLOCK, BLOCK), 1)
    band_lo = jnp.where(kj >= qi, 0.0, NEG_INF)
    band_hi = jnp.where(kj <= qi, 0.0, NEG_INF)
    bias_s[0] = band_lo
    bias_s[1] = band_hi
    bias_s[2] = jnp.where(first, NEG_INF, band_lo)
    bias_s[3] = jnp.where(last, NEG_INF, band_hi)

    n_pairs = ATTN_GROUP // 2

    def scores(nb, h):
        r0 = pl.multiple_of(nb * BLOCK, BLOCK)
        q_st = jnp.concatenate([q_ref[n_pairs * h + pp, pl.ds(r0, BLOCK), :] for pp in range(n_pairs)], axis=0)
        k_st = jnp.concatenate([ka_s[h, pl.ds(r0, 3 * BLOCK), :], kb_s[h, pl.ds(r0, 3 * BLOCK), :]], axis=0)
        s_s[h] = lax.dot_general(q_st, k_st, (((1,), (1,)), ((), ())),
                                 preferred_element_type=F32)

    def softmax_pv(nb, h):
        r0 = pl.multiple_of(nb * BLOCK, BLOCK)
        lo_idx = jnp.where(nb == 0, 2, 0)
        hi_idx = jnp.where(nb == nblk - 1, 3, 1)
        v_st = jnp.concatenate([va_s[h, pl.ds(r0, 3 * BLOCK), :], vb_s[h, pl.ds(r0, 3 * BLOCK), :]], axis=0)
        for pp in range(n_pairs):
            for rc in range(0, BLOCK, SOFTMAX_ROWS):
                rr = slice(pp * BLOCK + rc, pp * BLOCK + rc + SOFTMAX_ROWS)
                br = slice(rc, rc + SOFTMAX_ROWS)
                sinks, maxes, sums = [], [], []
                for half in range(2):
                    sink = sink_ref[layer * ATTN_Q_HEADS + ATTN_GROUP * h + 2 * pp + half] * LOG2E
                    c0 = half * 3 * BLOCK
                    s0 = s_s[h, rr, c0:c0 + BLOCK] + bias_s[lo_idx, br, :]
                    s1 = s_s[h, rr, c0 + BLOCK:c0 + 2 * BLOCK]
                    s2 = s_s[h, rr, c0 + 2 * BLOCK:c0 + 3 * BLOCK] + bias_s[hi_idx, br, :]
                    m = jnp.max(jnp.maximum(jnp.maximum(s0, s1), s2), axis=-1, keepdims=True)
                    m = jnp.maximum(m, sink)
                    e0, e1, e2 = jnp.exp2(s0 - m), jnp.exp2(s1 - m), jnp.exp2(s2 - m)
                    p_s[h, rr, c0:c0 + BLOCK] = e0.astype(BF16)
                    p_s[h, rr, c0 + BLOCK:c0 + 2 * BLOCK] = e1.astype(BF16)
                    p_s[h, rr, c0 + 2 * BLOCK:c0 + 3 * BLOCK] = e2.astype(BF16)
                    sinks.append(sink)
                    maxes.append(m)
                    sums.append(jnp.sum(e0 + e1 + e2, axis=-1, keepdims=True))
                sink_term = jnp.exp2(jnp.where(head0, sinks[0], sinks[1]) - jnp.where(head0, maxes[0], maxes[1]))
                scale_s[h, rr, :] = 1.0 / (jnp.where(head0, sums[0], sums[1]) + sink_term)
        o = jnp.dot(p_s[h], v_st, preferred_element_type=F32) * scale_s[h]
        for pp in range(n_pairs):
            attn_s[n_pairs * h + pp, pl.ds(r0, BLOCK), :] = o[pp * BLOCK:(pp + 1) * BLOCK, :]

    scores(0, 0)

    def attn_step(nb, carry):
        scores(nb, 1)
        softmax_pv(nb, 0)
        scores(jnp.minimum(nb + 1, nblk - 1), 0)
        softmax_pv(nb, 1)
        return carry

    lax.fori_loop(0, nblk, attn_step, 0, unroll=ATTN_UNROLL)

    for nb in range(nblk):
        rows = slice(nb * BLOCK, (nb + 1) * BLOCK)
        for hp in range(GM_HEADS // 2):
            cols = slice(hp * LANES, (hp + 1) * LANES)
            vblk = vln_ref[rows, cols]
            lhs = jnp.concatenate([ws_ref[2 * hp], ws_ref[2 * hp + 1]], axis=1)
            rhs = jnp.concatenate([jnp.where(head0, vblk, zero), jnp.where(head0, zero, vblk)], axis=0)
            sgate = jnp.dot(lhs, rhs, preferred_element_type=F32) + gmb_ref[:, cols]
            gm_s[rows, cols] = u_ref[rows, cols] * sgate

    n_ch = CONV_CH // LANES
    for ch in range(n_ch):
        cl = slice(ch * LANES, (ch + 1) * LANES)
        glu_s[ch, 0:CONV_HALO, :] = jnp.where(first, 0.0, glup_ref[:, cl])
        glu_s[ch, CONV_HALO:CONV_HALO + tq, :] = gluc_ref[:, cl]
        glu_s[ch, CONV_HALO + tq:, :] = jnp.where(last, 0.0, glun_ref[:, cl])
    group = SUBLANES * CONV_ROW_STRIDE

    def conv_step(i, carry):
        base = pl.multiple_of(i * (CONV_GROUPS * group), CONV_GROUPS * group)
        for ch in range(n_ch):
            cl = slice(ch * LANES, (ch + 1) * LANES)
            offs = [g * group + ph for g in range(CONV_GROUPS) for ph in range(CONV_ROW_STRIDE)]
            accs = [jnp.broadcast_to(cb_ref[:, cl], (SUBLANES, LANES)) for _ in offs]
            for t in range(CONV_WIDTH):
                w = cw_ref[t:t + 1, cl]
                for a, off in enumerate(offs):
                    start = base + (off + CONV_HALO - CONV_PAD + t)
                    accs[a] = accs[a] + w * glu_s[ch, pl.ds(start, SUBLANES, stride=CONV_ROW_STRIDE), :]
            for a, off in enumerate(offs):
                y_s[ch, pl.ds(base + off, SUBLANES, stride=CONV_ROW_STRIDE), :] = accs[a]
        return carry

    lax.fori_loop(0, tq // (CONV_GROUPS * group), conv_step, 0)

    rchunk = 128
    for p0 in range(0, tq, PROJ_ROWS):
        for r0 in range(p0, p0 + PROJ_ROWS, rchunk):
            rs = slice(r0, r0 + rchunk)
            y = jnp.concatenate([y_s[ch, rs, :] for ch in range(n_ch)], axis=1)
            y = _layer_norm(y, clg_ref[...], clb_ref[...])
            y = y * jax.nn.sigmoid(y)
            merged_s[rs, ATTN_WIDTH + GM_WIDTH:] = _rms(y, og_ref[:, ATTN_WIDTH + GM_WIDTH:]).astype(BF16)
            attn = jnp.concatenate([attn_s[pb, rs, :] for pb in range(ATTN_WIDTH // LANES)], axis=1)
            merged_s[rs, 0:ATTN_WIDTH] = _rms(attn, og_ref[:, 0:ATTN_WIDTH]).astype(BF16)
            merged_s[rs, ATTN_WIDTH:ATTN_WIDTH + GM_WIDTH] = _rms(
                gm_s[rs, :], og_ref[:, ATTN_WIDTH:ATTN_WIDTH + GM_WIDTH]).astype(BF16)
        ps = slice(p0, p0 + PROJ_ROWS)
        o_ref[ps, :] = x_ref[ps, :] + jnp.dot(merged_s[ps, :], wout_ref[...], preferred_element_type=F32)


def _mixer(layer, x, q, kv, u, vln, glu, sink, ws, gm_bias, cw, cb, clg, clb, og, wout,
           wgu, wd, wpg, wpp, batch, seq):
    n = x.shape[0]
    tq = TQ_MIX
    tiles = seq // tq
    blk_per_tile = tq // BLOCK
    blk_per_seq = seq // BLOCK
    halo_per_tile = tq // CONV_HALO
    halo_per_seq = seq // CONV_HALO

    def row(width):
        return pl.BlockSpec((tq, width), lambda b, j, *_: (b * tiles + j, 0))

    def prev(rows, width, per_tile, per_seq):
        return pl.BlockSpec(
            (rows, width), lambda b, j, *_: (b * per_seq + jnp.maximum(j * per_tile - 1, 0), 0))

    def nxt(rows, width, per_tile, per_seq):
        return pl.BlockSpec(
            (rows, width), lambda b, j, *_: (b * per_seq + jnp.minimum((j + 1) * per_tile, per_seq - 1), 0))

    const = lambda shape: _layer_spec(layer, shape)
    kv_rows = tq + 2 * BLOCK
    step_of = lambda b, j, *_: b * tiles + j
    casts = [_cast_specs(layer, rows, cols, batch * tiles, step_of)
             for rows, cols in ((D_MODEL, 2 * D_FF), (D_FF, D_MODEL), (D_MODEL, D_MODEL), (PLE_DIM, D_MODEL))]
    grid_spec = pltpu.PrefetchScalarGridSpec(
        num_scalar_prefetch=1,
        grid=(batch, tiles),
        in_specs=[
            row(D_MODEL), pl.BlockSpec((ATTN_WIDTH // LANES, tq, LANES), lambda b, j, *_: (0, b * tiles + j, 0)),
            prev(BLOCK, 4 * LANES, blk_per_tile, blk_per_seq), row(4 * LANES),
            nxt(BLOCK, 4 * LANES, blk_per_tile, blk_per_seq),
            row(GM_WIDTH), row(GM_WIDTH),
            prev(CONV_HALO, CONV_CH, halo_per_tile, halo_per_seq), row(CONV_CH),
            nxt(CONV_HALO, CONV_CH, halo_per_tile, halo_per_seq),
            const((GM_HEADS, CHUNK, CHUNK)), const((CHUNK, GM_WIDTH)),
            const((CONV_WIDTH, CONV_CH)), const((1, CONV_CH)), const((1, CONV_CH)), const((1, CONV_CH)),
            const((1, D_MODEL)), _resident_spec((D_MODEL, D_MODEL)),
            *[src for src, _ in casts],
        ],
        out_specs=(row(D_MODEL), *[dst for _, dst in casts]),
        scratch_shapes=[
            pltpu.VMEM((ATTN_KV_HEADS, kv_rows, LANES), BF16), pltpu.VMEM((ATTN_KV_HEADS, kv_rows, LANES), BF16),
            pltpu.VMEM((ATTN_KV_HEADS, kv_rows, LANES), BF16), pltpu.VMEM((ATTN_KV_HEADS, kv_rows, LANES), BF16),
            pltpu.VMEM((4, BLOCK, BLOCK), F32),
            pltpu.VMEM((2, ATTN_GROUP // 2 * BLOCK, 2 * 3 * BLOCK), F32),
            pltpu.VMEM((2, ATTN_GROUP // 2 * BLOCK, 2 * 3 * BLOCK), BF16),
            pltpu.VMEM((2, ATTN_GROUP // 2 * BLOCK, LANES), F32),
            pltpu.VMEM((CONV_CH // LANES, tq + 2 * CONV_HALO, LANES), F32),
            pltpu.VMEM((CONV_CH // LANES, tq, LANES), F32),
            pltpu.VMEM((ATTN_WIDTH // LANES, tq, LANES), F32), pltpu.VMEM((tq, GM_WIDTH), F32),
            pltpu.VMEM((tq, D_MODEL), BF16),
        ],
    )
    return pl.pallas_call(
        functools.partial(_mixer_kernel, layer),
        out_shape=(jax.ShapeDtypeStruct((n, D_MODEL), F32),
                   jax.ShapeDtypeStruct((D_MODEL, 2 * D_FF), BF16), jax.ShapeDtypeStruct((D_FF, D_MODEL), BF16),
                   jax.ShapeDtypeStruct((D_MODEL, D_MODEL), BF16), jax.ShapeDtypeStruct((PLE_DIM, D_MODEL), BF16)),
        grid_spec=grid_spec,
        compiler_params=pltpu.CompilerParams(dimension_semantics=("arbitrary", "arbitrary"),
                                             vmem_limit_bytes=VMEM_LIMIT),
        name="mixer",
    )(sink, x, q, kv, kv, kv, u, vln, glu, glu, glu, ws, gm_bias, cw, cb, clg, clb, og, wout,
      wgu, wd, wpg, wpp)


def _ffn_kernel(x_ref, p_ref, gf_ref, wgu_ref, wd_ref, gp_ref, wpg_ref, wpp_ref, o_ref, act_s):
    x = x_ref[...]
    hn = _rms(x, gf_ref[...]).astype(BF16)
    for c0 in range(0, D_FF, FF_CHUNK):
        gate = jnp.dot(hn, wgu_ref[:, c0:c0 + FF_CHUNK], preferred_element_type=F32)
        up = jnp.dot(hn, wgu_ref[:, D_FF + c0:D_FF + c0 + FF_CHUNK], preferred_element_type=F32)
        act_s[:, c0:c0 + FF_CHUNK] = (gate * jax.nn.sigmoid(gate) * up).astype(BF16)
    x = x + jnp.dot(act_s[...], wd_ref[...], preferred_element_type=F32)

    hp = _rms(x, gp_ref[...]).astype(BF16)
    gate = jax.nn.sigmoid(jnp.dot(hp, wpg_ref[...], preferred_element_type=F32))
    proj = jnp.dot(p_ref[...].astype(BF16), wpp_ref[...], preferred_element_type=F32)
    o_ref[...] = x + proj * gate


def _ffn(layer, x, p, gf, wgu, wd, gp, wpg, wpp):
    n = x.shape[0]
    tm = TM_FFN
    row = lambda width: pl.BlockSpec((tm, width), lambda i: (i, 0))
    return pl.pallas_call(
        _ffn_kernel,
        out_shape=jax.ShapeDtypeStruct((n, D_MODEL), F32),
        grid=(n // tm,),
        in_specs=[row(D_MODEL), pl.BlockSpec((None, tm, PLE_DIM), lambda i: (layer, i, 0)),
                  _layer_spec(layer, (1, D_MODEL)),
                  _resident_spec((D_MODEL, 2 * D_FF)), _resident_spec((D_FF, D_MODEL)),
                  _layer_spec(layer, (1, D_MODEL)), _resident_spec((D_MODEL, D_MODEL)),
                  _resident_spec((PLE_DIM, D_MODEL))],
        out_specs=row(D_MODEL),
        scratch_shapes=[pltpu.VMEM((tm, D_FF), BF16)],
        compiler_params=pltpu.CompilerParams(dimension_semantics=("arbitrary",),
                                             vmem_limit_bytes=VMEM_LIMIT),
        name="ffn_ple",
    )(x, p, gf, wgu, wd, gp, wpg, wpp)


def kernel(x, p, positions, norm_mix_g, w_in, q_norm_g, k_norm_g, sink, gm_ln_g, gm_ln_b, gm_ws, gm_bs, conv_w, conv_b, conv_ln_g, conv_ln_b, out_norm_g, w_out, norm_ffn_g, w_gate_up, w_down, ple_norm_g, w_ple_gate, w_ple_proj):
    batch, seq, d = x.shape
    depth = w_in.shape[0]
    n = batch * seq
    assert d == D_MODEL and seq % TQ_MIX == 0
    assert n % TM_PROJ == 0 and n % TM_FFN == 0 and n % TM_ROPE == 0

    vec = lambda a: a[:, None, :]
    ws_b = gm_ws.astype(BF16)
    qg = vec(jnp.tile(q_norm_g, (1, LANES // HEAD_DIM)))
    kg = vec(jnp.tile(k_norm_g, (1, LANES // HEAD_DIM)))
    gm_bias = jnp.repeat(jnp.swapaxes(gm_bs, 1, 2), HEAD_DIM, axis=2)
    sink_flat = sink.reshape(depth * ATTN_Q_HEADS)
    p_flat = p.reshape(depth, n, PLE_DIM)

    tabs = _rope_tables(positions.reshape(1, n))
    xf = x.reshape(n, d)
    for i in range(depth):
        q, kv, u, vln, glu, w_out_b = _inproj(i, xf, vec(norm_mix_g), w_in, qg, kg, vec(gm_ln_g),
                                                 vec(gm_ln_b), tabs, w_out)
        xf, w_gu_b, w_down_b, w_pg_b, w_pp_b = _mixer(
            i, xf, q, kv, u, vln, glu, sink_flat, ws_b, gm_bias, conv_w, vec(conv_b), vec(conv_ln_g),
            vec(conv_ln_b), vec(out_norm_g), w_out_b, w_gate_up, w_down, w_ple_gate, w_ple_proj, batch, seq)
        xf = _ffn(i, xf, p_flat, vec(norm_ffn_g), w_gu_b, w_down_b, vec(ple_norm_g), w_pg_b, w_pp_b)
    return xf.reshape(batch, seq, d)
```

```python
import functools
import math

import numpy as np
import jax
import jax.numpy as jnp
from jax import lax
from jax.experimental import pallas as pl
from jax.experimental.pallas import tpu as pltpu

F32 = jnp.float32
BF16 = jnp.bfloat16

D_MODEL = 1024
HEAD_DIM = 64
ATTN_WIDTH = 512
ATTN_Q_HEADS = 8
ATTN_KV_HEADS = 2
ATTN_GROUP = 4
KV_WIDTH = 128
WINDOW = 128
BLOCK = 128
ROPE_THETA = 500000.0
ROT_DIM = 16
GM_WIDTH = 256
GM_HEADS = 4
CHUNK = 128
CONV_CH = 256
CONV_WIDTH = 31
CONV_PAD = 15
D_FF = 2816
PLE_DIM = 256
EPS = 1e-6
NEG_INF = -1e30

Q_OFF = 0
K_OFF = Q_OFF + ATTN_WIDTH
V_OFF = K_OFF + KV_WIDTH
GM_OFF = V_OFF + KV_WIDTH
CONV_OFF = GM_OFF + 2 * GM_WIDTH
IN_COLS = CONV_OFF + 2 * CONV_CH

LANES = 128
SUBLANES = 8
BF16_ROWS = 16
CONV_HALO = 16
CONV_ROW_STRIDE = 2
CONV_GROUPS = 8
VMEM_BYTES_V7X = 64 * 1024 * 1024
VMEM_LIMIT = VMEM_BYTES_V7X * 7 // 8

TM_ROPE = 8192
TM_PROJ = 512
TQ_MIX = 1024
TM_FFN = 1024
FF_CHUNK = 256
SOFTMAX_ROWS = 32
PROJ_ROWS = 256
ATTN_UNROLL = 4

INV_FREQ = [float(ROPE_THETA ** (-(2.0 * j) / ROT_DIM)) for j in range(ROT_DIM // 2)]
SQRT_HALF = float(np.sqrt(0.5))
LOG2E = float(np.log2(np.e))


def _layer_spec(layer, shape):
    nd = len(shape)
    return pl.BlockSpec((None,) + tuple(shape), lambda *_: (layer,) + (0,) * nd,
                        pipeline_mode=pl.Buffered(1))


def _resident_spec(shape):
    nd = len(shape)
    return pl.BlockSpec(tuple(shape), lambda *_: (0,) * nd, pipeline_mode=pl.Buffered(1))


def _cast_block_rows(rows, n_steps):
    r = BF16_ROWS
    while r * n_steps < rows or rows % r:
        r += BF16_ROWS
    return r


def _cast_specs(layer, rows, cols, n_steps, step_of):
    r = _cast_block_rows(rows, n_steps)
    last = rows // r - 1
    src = pl.BlockSpec((None, r, cols), lambda *idx: (layer, jnp.minimum(step_of(*idx), last), 0))
    dst = pl.BlockSpec((r, cols), lambda *idx: (jnp.minimum(step_of(*idx), last), 0))
    return src, dst


def _split_bf16(x):
    hi = x.astype(BF16)
    lo = (x - hi.astype(F32)).astype(BF16)
    return hi, lo


def _rope_table_kernel(pos_ref, c_ref, s_ref):
    half = ROT_DIM // 2
    pos = pos_ref[...].astype(F32)
    row = lax.broadcasted_iota(jnp.int32, (ROT_DIM, 1), 0)
    inv = jnp.zeros((ROT_DIM, 1), F32)
    for j, f in enumerate(INV_FREQ):
        inv = jnp.where((row & (half - 1)) == j, f, inv)
    ang = inv * pos
    cs = jnp.where(row < half, jnp.cos(ang), jnp.sin(ang))

    r = lax.broadcasted_iota(jnp.int32, (ROT_DIM, 2 * LANES), 0)
    col = lax.broadcasted_iota(jnp.int32, (ROT_DIM, 2 * LANES), 1)
    table = col // LANES
    in_head = col & (HEAD_DIM - 1)
    freq = col & (half - 1)
    put_cos = (table == 0) & (in_head < ROT_DIM) & (r == freq)
    put_nsin = (table == 1) & (in_head < half) & (r == freq + half)
    put_sin = (table == 1) & (in_head >= half) & (in_head < ROT_DIM) & (r == freq + half)
    place = jnp.where(put_cos | put_sin, 1.0, jnp.where(put_nsin, -1.0, 0.0)).astype(BF16)

    hi, lo = _split_bf16(cs)
    dn = (((0,), (0,)), ((), ()))
    t = (lax.dot_general(hi, place, dn, preferred_element_type=F32)
         + lax.dot_general(lo, place, dn, preferred_element_type=F32))
    lane = lax.broadcasted_iota(jnp.int32, (1, LANES), 1)
    c_ref[...] = jnp.where((lane & (HEAD_DIM - 1)) < ROT_DIM, t[:, 0:LANES], 1.0)
    s_ref[...] = t[:, LANES:2 * LANES]


def _rope_tables(pos_row):
    n = pos_row.shape[1]
    tm = TM_ROPE
    tab = jax.ShapeDtypeStruct((n, LANES), F32)
    spec = pl.BlockSpec((tm, LANES), lambda i: (i, 0))
    return pl.pallas_call(
        _rope_table_kernel,
        out_shape=(tab, tab),
        grid=(n // tm,),
        in_specs=[pl.BlockSpec((1, tm), lambda i: (0, i))],
        out_specs=(spec, spec),
        name="rope_tables",
    )(pos_row)


def _rms(x, gain):
    ms = jnp.mean(x * x, axis=-1, keepdims=True)
    return x * lax.rsqrt(ms + EPS) * gain


def _layer_norm(x, gain, bias):
    mu = jnp.mean(x, axis=-1, keepdims=True)
    xc = x - mu
    var = jnp.mean(xc * xc, axis=-1, keepdims=True)
    return xc * lax.rsqrt(var + EPS) * gain + bias


def _inproj_kernel(x_ref, g_ref, w_ref, qg_ref, kg_ref, lng_ref, lnb_ref,
                   c_ref, s_ref, wout_ref,
                   q_out, kv_out, u_out, vln_out, glu_out, wout_bf_out, w_bf, z_even, z_odd, *, last_is_even):
    i = pl.program_id(0)
    n_tiles = pl.num_programs(0) - 1

    wout_bf_out[...] = wout_ref[...].astype(BF16)

    def project(z_dst):
        h = _rms(x_ref[...], g_ref[...]).astype(BF16)
        z_dst[...] = jnp.dot(h, w_bf[...], preferred_element_type=F32)

    def finish(z_src):
        cos_t, sin_t = c_ref[...], s_ref[...]
        lane = lax.broadcasted_iota(jnp.int32, (1, LANES), 1)
        head0 = lane < HEAD_DIM

        def head_norm_rope(z, gain):
            sq = z * z
            first = jnp.sum(jnp.where(head0, sq, 0.0), axis=-1, keepdims=True)
            both = jnp.sum(sq, axis=-1, keepdims=True)
            ss = jnp.where(head0, first, both - first)
            zn = z * lax.rsqrt(ss * (1.0 / HEAD_DIM) + EPS) * gain
            partner = jnp.where((lane & (ROT_DIM // 2)) == 0,
                                pltpu.roll(zn, LANES - ROT_DIM // 2, 1), pltpu.roll(zn, ROT_DIM // 2, 1))
            return zn * cos_t + partner * sin_t

        q_gain = qg_ref[...] * (LOG2E / math.sqrt(HEAD_DIM))
        for b in range(ATTN_WIDTH // LANES):
            q_out[b] = head_norm_rope(z_src[:, Q_OFF + b * LANES:Q_OFF + (b + 1) * LANES], q_gain).astype(BF16)

        k = head_norm_rope(z_src[:, K_OFF:V_OFF], kg_ref[...])
        v = z_src[:, V_OFF:GM_OFF]
        kv_out[:, 0 * LANES:1 * LANES] = k.astype(BF16)
        kv_out[:, 1 * LANES:2 * LANES] = pltpu.roll(k, HEAD_DIM, 1).astype(BF16)
        kv_out[:, 2 * LANES:3 * LANES] = v.astype(BF16)
        kv_out[:, 3 * LANES:4 * LANES] = pltpu.roll(v, HEAD_DIM, 1).astype(BF16)

        zuv = z_src[:, GM_OFF:CONV_OFF]
        uv = 0.5 * zuv * (1.0 + lax.erf(zuv * SQRT_HALF))
        u_out[...] = uv[:, 0:GM_WIDTH]
        vln_out[...] = _layer_norm(uv[:, GM_WIDTH:], lng_ref[...], lnb_ref[...]).astype(BF16)

        glu_out[...] = z_src[:, CONV_OFF:CONV_OFF + CONV_CH] * jax.nn.sigmoid(z_src[:, CONV_OFF + CONV_CH:])

    inner = jnp.logical_and(i > 0, i < n_tiles)

    @pl.when(i == 0)
    def _():
        w_bf[...] = w_ref[...].astype(BF16)
        project(z_even)

    @pl.when(jnp.logical_and(inner, i % 2 == 1))
    def _():
        project(z_odd)
        finish(z_even)

    @pl.when(jnp.logical_and(inner, i % 2 == 0))
    def _():
        project(z_even)
        finish(z_odd)

    @pl.when(i == n_tiles)
    def _():
        finish(z_even if last_is_even else z_odd)


def _inproj(layer, x, g, w, qg, kg, lng, lnb, tabs, wout):
    n = x.shape[0]
    tm = TM_PROJ
    n_tiles = n // tm
    cur = lambda width: pl.BlockSpec((tm, width), lambda i: (jnp.minimum(i, n_tiles - 1), 0))
    lag = lambda width: pl.BlockSpec((tm, width), lambda i: (jnp.maximum(i - 1, 0), 0))
    out_shape = (
        jax.ShapeDtypeStruct((ATTN_WIDTH // LANES, n, LANES), BF16),
        jax.ShapeDtypeStruct((n, 4 * LANES), BF16),
        jax.ShapeDtypeStruct((n, GM_WIDTH), F32),
        jax.ShapeDtypeStruct((n, GM_WIDTH), BF16),
        jax.ShapeDtypeStruct((n, CONV_CH), F32),
        jax.ShapeDtypeStruct((D_MODEL, D_MODEL), BF16),
    )
    wout_src, wout_dst = _cast_specs(layer, D_MODEL, D_MODEL, n_tiles + 1, lambda i: i)
    return pl.pallas_call(
        functools.partial(_inproj_kernel, last_is_even=(n_tiles - 1) % 2 == 0),
        out_shape=out_shape,
        grid=(n_tiles + 1,),
        in_specs=[cur(D_MODEL), _layer_spec(layer, (1, D_MODEL)), _layer_spec(layer, (D_MODEL, IN_COLS)),
                  _layer_spec(layer, (1, LANES)), _layer_spec(layer, (1, LANES)),
                  _layer_spec(layer, (1, GM_WIDTH)), _layer_spec(layer, (1, GM_WIDTH)),
                  lag(LANES), lag(LANES), wout_src],
        out_specs=(pl.BlockSpec((ATTN_WIDTH // LANES, tm, LANES), lambda i: (0, jnp.maximum(i - 1, 0), 0)),
                   lag(4 * LANES), lag(GM_WIDTH), lag(GM_WIDTH), lag(CONV_CH), wout_dst),
        scratch_shapes=[pltpu.VMEM((D_MODEL, IN_COLS), BF16),
                        pltpu.VMEM((tm, IN_COLS), F32), pltpu.VMEM((tm, IN_COLS), F32)],
        compiler_params=pltpu.CompilerParams(dimension_semantics=("arbitrary",),
                                             vmem_limit_bytes=VMEM_LIMIT),
        name="in_proj",
    )(x, g, w, qg, kg, lng, lnb, *tabs, wout)


def _mixer_kernel(layer, sink_ref, x_ref, q_ref, kvp_ref, kvc_ref, kvn_ref, u_ref, vln_ref,
                  glup_ref, gluc_ref, glun_ref,
                  ws_ref, gmb_ref, cw_ref, cb_ref, clg_ref, clb_ref, og_ref, wout_ref,
                  wgu_ref, wd_ref, wpg_ref, wpp_ref,
                  o_ref, wgu_bf_out, wd_bf_out, wpg_bf_out, wpp_bf_out,
                  ka_s, kb_s, va_s, vb_s, bias_s, s_s, p_s, scale_s, glu_s, y_s, attn_s, gm_s, merged_s):
    tq = x_ref.shape[0]
    nblk = tq // BLOCK
    j = pl.program_id(1)
    first = j == 0
    last = j == pl.num_programs(1) - 1

    wgu_bf_out[...] = wgu_ref[...].astype(BF16)
    wd_bf_out[...] = wd_ref[...].astype(BF16)
    wpg_bf_out[...] = wpg_ref[...].astype(BF16)
    wpp_bf_out[...] = wpp_ref[...].astype(BF16)

    lane = lax.broadcasted_iota(jnp.int32, (1, LANES), 1)
    head0 = lane < HEAD_DIM
    zero = jnp.zeros((), BF16)
    for src, r0, rows in ((kvp_ref, 0, BLOCK), (kvc_ref, BLOCK, tq), (kvn_ref, BLOCK + tq, BLOCK)):
        dst = slice(r0, r0 + rows)
        k, k_sw = src[:, 0 * LANES:1 * LANES], src[:, 1 * LANES:2 * LANES]
        v, v_sw = src[:, 2 * LANES:3 * LANES], src[:, 3 * LANES:4 * LANES]
        ka_s[0, dst, :] = jnp.where(head0, k, zero)
        kb_s[0, dst, :] = jnp.where(head0, zero, k_sw)
        ka_s[1, dst, :] = jnp.where(head0, k_sw, zero)
        kb_s[1, dst, :] = jnp.where(head0, zero, k)
        va_s[0, dst, :] = jnp.where(head0, v, zero)
        vb_s[0, dst, :] = jnp.where(head0, zero, v_sw)
        va_s[1, dst, :] = jnp.where(head0, v_sw, zero)
        vb_s[1, dst, :] = jnp.where(head0, zero, v)

    qi = lax.broadcasted_iota(jnp.int32, (BLOCK, BLOCK), 0)
    kj = lax.broadcasted_iota(jnp.int32, (BLOCK, BLOCK), 1)
    band_lo = jnp.where(kj >= qi, 0.0, NEG_INF)
    band_hi = jnp.where(kj <= qi, 0.0, NEG_INF)
    bias_s[0] = band_lo
    bias_s[1] = band_hi
    bias_s[2] = jnp.where(first, NEG_INF, band_lo)
    bias_s[3] = jnp.where(last, NEG_INF, band_hi)

    n_pairs = ATTN_GROUP // 2

    def scores(nb, h):
        r0 = pl.multiple_of(nb * BLOCK, BLOCK)
        q_st = jnp.concatenate([q_ref[n_pairs * h + pp, pl.ds(r0, BLOCK), :] for pp in range(n_pairs)], axis=0)
        k_st = jnp.concatenate([ka_s[h, pl.ds(r0, 3 * BLOCK), :], kb_s[h, pl.ds(r0, 3 * BLOCK), :]], axis=0)
        s_s[h] = lax.dot_general(q_st, k_st, (((1,), (1,)), ((), ())),
                                 preferred_element_type=F32)

    def softmax_pv(nb, h):
        r0 = pl.multiple_of(nb * BLOCK, BLOCK)
        lo_idx = jnp.where(nb == 0, 2, 0)
        hi_idx = jnp.where(nb == nblk - 1, 3, 1)
        v_st = jnp.concatenate([va_s[h, pl.ds(r0, 3 * BLOCK), :], vb_s[h, pl.ds(r0, 3 * BLOCK), :]], axis=0)
        for pp in range(n_pairs):
            for rc in range(0, BLOCK, SOFTMAX_ROWS):
                rr = slice(pp * BLOCK + rc, pp * BLOCK + rc + SOFTMAX_ROWS)
                br = slice(rc, rc + SOFTMAX_ROWS)
                sinks, maxes, sums = [], [], []
                for half in range(2):
                    sink = sink_ref[layer * ATTN_Q_HEADS + ATTN_GROUP * h + 2 * pp + half] * LOG2E
                    c0 = half * 3 * BLOCK
                    s0 = s_s[h, rr, c0:c0 + BLOCK] + bias_s[lo_idx, br, :]
                    s1 = s_s[h, rr, c0 + BLOCK:c0 + 2 * BLOCK]
                    s2 = s_s[h, rr, c0 + 2 * BLOCK:c0 + 3 * BLOCK] + bias_s[hi_idx, br, :]
                    m = jnp.max(jnp.maximum(jnp.maximum(s0, s1), s2), axis=-1, keepdims=True)
                    m = jnp.maximum(m, sink)
                    e0, e1, e2 = jnp.exp2(s0 - m), jnp.exp2(s1 - m), jnp.exp2(s2 - m)
                    p_s[h, rr, c0:c0 + BLOCK] = e0.astype(BF16)
                    p_s[h, rr, c0 + BLOCK:c0 + 2 * BLOCK] = e1.astype(BF16)
                    p_s[h, rr, c0 + 2 * BLOCK:c0 + 3 * BLOCK] = e2.astype(BF16)
                    sinks.append(sink)
                    maxes.append(m)
                    sums.append(jnp.sum(e0 + e1 + e2, axis=-1, keepdims=True))
                sink_term = jnp.exp2(jnp.where(head0, sinks[0], sinks[1]) - jnp.where(head0, maxes[0], maxes[1]))
                scale_s[h, rr, :] = 1.0 / (jnp.where(head0, sums[0], sums[1]) + sink_term)
        o = jnp.dot(p_s[h], v_st, preferred_element_type=F32) * scale_s[h]
        for pp in range(n_pairs):
            attn_s[n_pairs * h + pp, pl.ds(r0, BLOCK), :] = o[pp * BLOCK:(pp + 1) * BLOCK, :]

    scores(0, 0)

    def attn_step(nb, carry):
        scores(nb, 1)
        softmax_pv(nb, 0)
        scores(jnp.minimum(nb + 1, nblk - 1), 0)
        softmax_pv(nb, 1)
        return carry

    lax.fori_loop(0, nblk, attn_step, 0, unroll=ATTN_UNROLL)

    for nb in range(nblk):
        rows = slice(nb * BLOCK, (nb + 1) * BLOCK)
        for hp in range(GM_HEADS // 2):
            cols = slice(hp * LANES, (hp + 1) * LANES)
            vblk = vln_ref[rows, cols]
            lhs = jnp.concatenate([ws_ref[2 * hp], ws_ref[2 * hp + 1]], axis=1)
            rhs = jnp.concatenate([jnp.where(head0, vblk, zero), jnp.where(head0, zero, vblk)], axis=0)
            sgate = jnp.dot(lhs, rhs, preferred_element_type=F32) + gmb_ref[:, cols]
            gm_s[rows, cols] = u_ref[rows, cols] * sgate

    n_ch = CONV_CH // LANES
    for ch in range(n_ch):
        cl = slice(ch * LANES, (ch + 1) * LANES)
        glu_s[ch, 0:CONV_HALO, :] = jnp.where(first, 0.0, glup_ref[:, cl])
        glu_s[ch, CONV_HALO:CONV_HALO + tq, :] = gluc_ref[:, cl]
        glu_s[ch, CONV_HALO + tq:, :] = jnp.where(last, 0.0, glun_ref[:, cl])
    group = SUBLANES * CONV_ROW_STRIDE

    def conv_step(i, carry):
        base = pl.multiple_of(i * (CONV_GROUPS * group), CONV_GROUPS * group)
        for ch in range(n_ch):
            cl = slice(ch * LANES, (ch + 1) * LANES)
            offs = [g * group + ph for g in range(CONV_GROUPS) for ph in range(CONV_ROW_STRIDE)]
            accs = [jnp.broadcast_to(cb_ref[:, cl], (SUBLANES, LANES)) for _ in offs]
            for t in range(CONV_WIDTH):
                w = cw_ref[t:t + 1, cl]
                for a, off in enumerate(offs):
                    start = base + (off + CONV_HALO - CONV_PAD + t)
                    accs[a] = accs[a] + w * glu_s[ch, pl.ds(start, SUBLANES, stride=CONV_ROW_STRIDE), :]
            for a, off in enumerate(offs):
                y_s[ch, pl.ds(base + off, SUBLANES, stride=CONV_ROW_STRIDE), :] = accs[a]
        return carry

    lax.fori_loop(0, tq // (CONV_GROUPS * group), conv_step, 0)

    rchunk = 128
    for p0 in range(0, tq, PROJ_ROWS):
        for r0 in range(p0, p0 + PROJ_ROWS, rchunk):
            rs = slice(r0, r0 + rchunk)
            y = jnp.concatenate([y_s[ch, rs, :] for ch in range(n_ch)], axis=1)
            y = _layer_norm(y, clg_ref[...], clb_ref[...])
            y = y * jax.nn.sigmoid(y)
            merged_s[rs, ATTN_WIDTH + GM_WIDTH:] = _rms(y, og_ref[:, ATTN_WIDTH + GM_WIDTH:]).astype(BF16)
            attn = jnp.concatenate([attn_s[pb, rs, :] for pb in range(ATTN_WIDTH // LANES)], axis=1)
            merged_s[rs, 0:ATTN_WIDTH] = _rms(attn, og_ref[:, 0:ATTN_WIDTH]).astype(BF16)
            merged_s[rs, ATTN_WIDTH:ATTN_WIDTH + GM_WIDTH] = _rms(
                gm_s[rs, :], og_ref[:, ATTN_WIDTH:ATTN_WIDTH + GM_WIDTH]).astype(BF16)
        ps = slice(p0, p0 + PROJ_ROWS)
        o_ref[ps, :] = x_ref[ps, :] + jnp.dot(merged_s[ps, :], wout_ref[...], preferred_element_type=F32)


def _mixer(layer, x, q, kv, u, vln, glu, sink, ws, gm_bias, cw, cb, clg, clb, og, wout,
           wgu, wd, wpg, wpp, batch, seq):
    n = x.shape[0]
    tq = TQ_MIX
    tiles = seq // tq
    blk_per_tile = tq // BLOCK
    blk_per_seq = seq // BLOCK
    halo_per_tile = tq // CONV_HALO
    halo_per_seq = seq // CONV_HALO

    def row(width):
        return pl.BlockSpec((tq, width), lambda b, j, *_: (b * tiles + j, 0))

    def prev(rows, width, per_tile, per_seq):
        return pl.BlockSpec(
            (rows, width), lambda b, j, *_: (b * per_seq + jnp.maximum(j * per_tile - 1, 0), 0))

    def nxt(rows, width, per_tile, per_seq):
        return pl.BlockSpec(
            (rows, width), lambda b, j, *_: (b * per_seq + jnp.minimum((j + 1) * per_tile, per_seq - 1), 0))

    const = lambda shape: _layer_spec(layer, shape)
    kv_rows = tq + 2 * BLOCK
    step_of = lambda b, j, *_: b * tiles + j
    casts = [_cast_specs(layer, rows, cols, batch * tiles, step_of)
             for rows, cols in ((D_MODEL, 2 * D_FF), (D_FF, D_MODEL), (D_MODEL, D_MODEL), (PLE_DIM, D_MODEL))]
    grid_spec = pltpu.PrefetchScalarGridSpec(
        num_scalar_prefetch=1,
        grid=(batch, tiles),
        in_specs=[
            row(D_MODEL), pl.BlockSpec((ATTN_WIDTH // LANES, tq, LANES), lambda b, j, *_: (0, b * tiles + j, 0)),
            prev(BLOCK, 4 * LANES, blk_per_tile, blk_per_seq), row(4 * LANES),
            nxt(BLOCK, 4 * LANES, blk_per_tile, blk_per_seq),
            row(GM_WIDTH), row(GM_WIDTH),
            prev(CONV_HALO, CONV_CH, halo_per_tile, halo_per_seq), row(CONV_CH),
            nxt(CONV_HALO, CONV_CH, halo_per_tile, halo_per_seq),
            const((GM_HEADS, CHUNK, CHUNK)), const((CHUNK, GM_WIDTH)),
            const((CONV_WIDTH, CONV_CH)), const((1, CONV_CH)), const((1, CONV_CH)), const((1, CONV_CH)),
            const((1, D_MODEL)), _resident_spec((D_MODEL, D_MODEL)),
            *[src for src, _ in casts],
        ],
        out_specs=(row(D_MODEL), *[dst for _, dst in casts]),
        scratch_shapes=[
            pltpu.VMEM((ATTN_KV_HEADS, kv_rows, LANES), BF16), pltpu.VMEM((ATTN_KV_HEADS, kv_rows, LANES), BF16),
            pltpu.VMEM((ATTN_KV_HEADS, kv_rows, LANES), BF16), pltpu.VMEM((ATTN_KV_HEADS, kv_rows, LANES), BF16),
            pltpu.VMEM((4, BLOCK, BLOCK), F32),
            pltpu.VMEM((2, ATTN_GROUP // 2 * BLOCK, 2 * 3 * BLOCK), F32),
            pltpu.VMEM((2, ATTN_GROUP // 2 * BLOCK, 2 * 3 * BLOCK), BF16),
            pltpu.VMEM((2, ATTN_GROUP // 2 * BLOCK, LANES), F32),
            pltpu.VMEM((CONV_CH // LANES, tq + 2 * CONV_HALO, LANES), F32),
            pltpu.VMEM((CONV_CH // LANES, tq, LANES), F32),
            pltpu.VMEM((ATTN_WIDTH // LANES, tq, LANES), F32), pltpu.VMEM((tq, GM_WIDTH), F32),
            pltpu.VMEM((tq, D_MODEL), BF16),
        ],
    )
    return pl.pallas_call(
        functools.partial(_mixer_kernel, layer),
        out_shape=(jax.ShapeDtypeStruct((n, D_MODEL), F32),
                   jax.ShapeDtypeStruct((D_MODEL, 2 * D_FF), BF16), jax.ShapeDtypeStruct((D_FF, D_MODEL), BF16),
                   jax.ShapeDtypeStruct((D_MODEL, D_MODEL), BF16), jax.ShapeDtypeStruct((PLE_DIM, D_MODEL), BF16)),
        grid_spec=grid_spec,
        compiler_params=pltpu.CompilerParams(dimension_semantics=("arbitrary", "arbitrary"),
                                             vmem_limit_bytes=VMEM_LIMIT),
        name="mixer",
    )(sink, x, q, kv, kv, kv, u, vln, glu, glu, glu, ws, gm_bias, cw, cb, clg, clb, og, wout,
      wgu, wd, wpg, wpp)


def _ffn_kernel(x_ref, p_ref, gf_ref, wgu_ref, wd_ref, gp_ref, wpg_ref, wpp_ref, o_ref, act_s):
    x = x_ref[...]
    hn = _rms(x, gf_ref[...]).astype(BF16)
    for c0 in range(0, D_FF, FF_CHUNK):
        gate = jnp.dot(hn, wgu_ref[:, c0:c0 + FF_CHUNK], preferred_element_type=F32)
        up = jnp.dot(hn, wgu_ref[:, D_FF + c0:D_FF + c0 + FF_CHUNK], preferred_element_type=F32)
        act_s[:, c0:c0 + FF_CHUNK] = (gate * jax.nn.sigmoid(gate) * up).astype(BF16)
    x = x + jnp.dot(act_s[...], wd_ref[...], preferred_element_type=F32)

    hp = _rms(x, gp_ref[...]).astype(BF16)
    gate = jax.nn.sigmoid(jnp.dot(hp, wpg_ref[...], preferred_element_type=F32))
    proj = jnp.dot(p_ref[...].astype(BF16), wpp_ref[...], preferred_element_type=F32)
    o_ref[...] = x + proj * gate


def _ffn(layer, x, p, gf, wgu, wd, gp, wpg, wpp):
    n = x.shape[0]
    tm = TM_FFN
    row = lambda width: pl.BlockSpec((tm, width), lambda i: (i, 0))
    return pl.pallas_call(
        _ffn_kernel,
        out_shape=jax.ShapeDtypeStruct((n, D_MODEL), F32),
        grid=(n // tm,),
        in_specs=[row(D_MODEL), pl.BlockSpec((None, tm, PLE_DIM), lambda i: (layer, i, 0)),
                  _layer_spec(layer, (1, D_MODEL)),
                  _resident_spec((D_MODEL, 2 * D_FF)), _resident_spec((D_FF, D_MODEL)),
                  _layer_spec(layer, (1, D_MODEL)), _resident_spec((D_MODEL, D_MODEL)),
                  _resident_spec((PLE_DIM, D_MODEL))],
        out_specs=row(D_MODEL),
        scratch_shapes=[pltpu.VMEM((tm, D_FF), BF16)],
        compiler_params=pltpu.CompilerParams(dimension_semantics=("arbitrary",),
                                             vmem_limit_bytes=VMEM_LIMIT),
        name="ffn_ple",
    )(x, p, gf, wgu, wd, gp, wpg, wpp)


def kernel(x, p, positions, norm_mix_g, w_in, q_norm_g, k_norm_g, sink, gm_ln_g, gm_ln_b, gm_ws, gm_bs, conv_w, conv_b, conv_ln_g, conv_ln_b, out_norm_g, w_out, norm_ffn_g, w_gate_up, w_down, ple_norm_g, w_ple_gate, w_ple_proj):
    batch, seq, d = x.shape
    depth = w_in.shape[0]
    n = batch * seq
    assert d == D_MODEL and seq % TQ_MIX == 0
    assert n % TM_PROJ == 0 and n % TM_FFN == 0 and n % TM_ROPE == 0

    vec = lambda a: a[:, None, :]
    ws_b = gm_ws.astype(BF16)
    qg = vec(jnp.tile(q_norm_g, (1, LANES // HEAD_DIM)))
    kg = vec(jnp.tile(k_norm_g, (1, LANES // HEAD_DIM)))
    gm_bias = jnp.repeat(jnp.swapaxes(gm_bs, 1, 2), HEAD_DIM, axis=2)
    sink_flat = sink.reshape(depth * ATTN_Q_HEADS)
    p_flat = p.reshape(depth, n, PLE_DIM)

    tabs = _rope_tables(positions.reshape(1, n))
    xf = x.reshape(n, d)
    for i in range(depth):
        q, kv, u, vln, glu, w_out_b = _inproj(i, xf, vec(norm_mix_g), w_in, qg, kg, vec(gm_ln_g),
                                                 vec(gm_ln_b), tabs, w_out)
        xf, w_gu_b, w_down_b, w_pg_b, w_pp_b = _mixer(
            i, xf, q, kv, u, vln, glu, sink_flat, ws_b, gm_bias, conv_w, vec(conv_b), vec(conv_ln_g),
            vec(conv_ln_b), vec(out_norm_g), w_out_b, w_gate_up, w_down, w_ple_gate, w_ple_proj, batch, seq)
        xf = _ffn(i, xf, p_flat, vec(norm_ffn_g), w_gu_b, w_down_b, vec(ple_norm_g), w_pg_b, w_pp_b)
    return xf.reshape(batch, seq, d)
```

```python
import functools
import math

import numpy as np
import jax
import jax.numpy as jnp
from jax import lax
from jax.experimental import pallas as pl
from jax.experimental.pallas import tpu as pltpu

F32 = jnp.float32
BF16 = jnp.bfloat16

D_MODEL = 1024
HEAD_DIM = 64
ATTN_WIDTH = 512
ATTN_Q_HEADS = 8
ATTN_KV_HEADS = 2
ATTN_GROUP = 4
KV_WIDTH = 128
WINDOW = 128
BLOCK = 128
ROPE_THETA = 500000.0
ROT_DIM = 16
GM_WIDTH = 256
GM_HEADS = 4
CHUNK = 128
CONV_CH = 256
CONV_WIDTH = 31
CONV_PAD = 15
D_FF = 2816
PLE_DIM = 256
EPS = 1e-6
NEG_INF = -1e30

assert WINDOW == BLOCK and CHUNK == BLOCK

Q_OFF = 0
K_OFF = Q_OFF + ATTN_WIDTH
V_OFF = K_OFF + KV_WIDTH
GM_OFF = V_OFF + KV_WIDTH
CONV_OFF = GM_OFF + 2 * GM_WIDTH
IN_COLS = CONV_OFF + 2 * CONV_CH

LANES = 128
SUBLANES = 8
BF16_ROWS = 16
CONV_HALO = 16
CONV_ROW_STRIDE = 2
CONV_GROUPS = 8
VMEM_BYTES_V7X = 64 * 1024 * 1024
VMEM_LIMIT = VMEM_BYTES_V7X * 7 // 8

TM_ROPE = 8192
TM_PROJ = 512
TQ_MIX = 1024
TM_FFN = 1024
FF_CHUNK = 256
SOFTMAX_ROWS = 32
PROJ_ROWS = 256
ATTN_UNROLL = 4

INV_FREQ = [float(ROPE_THETA ** (-(2.0 * j) / ROT_DIM)) for j in range(ROT_DIM // 2)]
SQRT_HALF = float(np.sqrt(0.5))
LOG2E = float(np.log2(np.e))


def _layer_spec(layer, shape):
    nd = len(shape)
    return pl.BlockSpec((None,) + tuple(shape), lambda *_: (layer,) + (0,) * nd,
                        pipeline_mode=pl.Buffered(1))


def _resident_spec(shape):
    nd = len(shape)
    return pl.BlockSpec(tuple(shape), lambda *_: (0,) * nd, pipeline_mode=pl.Buffered(1))


def _cast_block_rows(rows, n_steps):
    r = BF16_ROWS
    while r * n_steps < rows or rows % r:
        r += BF16_ROWS
    return r


def _cast_specs(layer, rows, cols, n_steps, step_of):
    r = _cast_block_rows(rows, n_steps)
    last = rows // r - 1
    src = pl.BlockSpec((None, r, cols), lambda *idx: (layer, jnp.minimum(step_of(*idx), last), 0))
    dst = pl.BlockSpec((r, cols), lambda *idx: (jnp.minimum(step_of(*idx), last), 0))
    return src, dst


def _split_bf16(x):
    hi = x.astype(BF16)
    lo = (x - hi.astype(F32)).astype(BF16)
    return hi, lo


def _rope_table_kernel(pos_ref, c_ref, s_ref):
    half = ROT_DIM // 2
    pos = pos_ref[...].astype(F32)
    row = lax.broadcasted_iota(jnp.int32, (ROT_DIM, 1), 0)
    inv = jnp.zeros((ROT_DIM, 1), F32)
    for j, f in enumerate(INV_FREQ):
        inv = jnp.where((row & (half - 1)) == j, f, inv)
    ang = inv * pos
    cs = jnp.where(row < half, jnp.cos(ang), jnp.sin(ang))

    r = lax.broadcasted_iota(jnp.int32, (ROT_DIM, 2 * LANES), 0)
    col = lax.broadcasted_iota(jnp.int32, (ROT_DIM, 2 * LANES), 1)
    table = col // LANES
    in_head = col & (HEAD_DIM - 1)
    freq = col & (half - 1)
    put_cos = (table == 0) & (in_head < ROT_DIM) & (r == freq)
    put_nsin = (table == 1) & (in_head < half) & (r == freq + half)
    put_sin = (table == 1) & (in_head >= half) & (in_head < ROT_DIM) & (r == freq + half)
    place = jnp.where(put_cos | put_sin, 1.0, jnp.where(put_nsin, -1.0, 0.0)).astype(BF16)

    hi, lo = _split_bf16(cs)
    dn = (((0,), (0,)), ((), ()))
    t = (lax.dot_general(hi, place, dn, preferred_element_type=F32)
         + lax.dot_general(lo, place, dn, preferred_element_type=F32))
    lane = lax.broadcasted_iota(jnp.int32, (1, LANES), 1)
    c_ref[...] = jnp.where((lane & (HEAD_DIM - 1)) < ROT_DIM, t[:, 0:LANES], 1.0)
    s_ref[...] = t[:, LANES:2 * LANES]


def _rope_tables(pos_row):
    n = pos_row.shape[1]
    tm = TM_ROPE
    tab = jax.ShapeDtypeStruct((n, LANES), F32)
    spec = pl.BlockSpec((tm, LANES), lambda i: (i, 0))
    return pl.pallas_call(
        _rope_table_kernel,
        out_shape=(tab, tab),
        grid=(n // tm,),
        in_specs=[pl.BlockSpec((1, tm), lambda i: (0, i))],
        out_specs=(spec, spec),
        name="rope_tables",
    )(pos_row)


def _rms(x, gain):
    ms = jnp.mean(x * x, axis=-1, keepdims=True)
    return x * lax.rsqrt(ms + EPS) * gain


def _layer_norm(x, gain, bias):
    mu = jnp.mean(x, axis=-1, keepdims=True)
    xc = x - mu
    var = jnp.mean(xc * xc, axis=-1, keepdims=True)
    return xc * lax.rsqrt(var + EPS) * gain + bias


def _inproj_kernel(x_ref, g_ref, w_ref, qg_ref, kg_ref, lng_ref, lnb_ref,
                   c_ref, s_ref, wout_ref,
                   q_out, kv_out, u_out, vln_out, glu_out, wout_bf_out, w_bf, z_even, z_odd, *, last_is_even):
    i = pl.program_id(0)
    n_tiles = pl.num_programs(0) - 1

    wout_bf_out[...] = wout_ref[...].astype(BF16)

    def project(z_dst):
        h = _rms(x_ref[...], g_ref[...]).astype(BF16)
        z_dst[...] = jnp.dot(h, w_bf[...], preferred_element_type=F32)

    def finish(z_src):
        cos_t, sin_t = c_ref[...], s_ref[...]
        lane = lax.broadcasted_iota(jnp.int32, (1, LANES), 1)
        head0 = lane < HEAD_DIM

        def head_norm_rope(z, gain):
            sq = z * z
            first = jnp.sum(jnp.where(head0, sq, 0.0), axis=-1, keepdims=True)
            both = jnp.sum(sq, axis=-1, keepdims=True)
            ss = jnp.where(head0, first, both - first)
            zn = z * lax.rsqrt(ss * (1.0 / HEAD_DIM) + EPS) * gain
            partner = jnp.where((lane & (ROT_DIM // 2)) == 0,
                                pltpu.roll(zn, LANES - ROT_DIM // 2, 1), pltpu.roll(zn, ROT_DIM // 2, 1))
            return zn * cos_t + partner * sin_t

        q_gain = qg_ref[...] * (LOG2E / math.sqrt(HEAD_DIM))
        for b in range(ATTN_WIDTH // LANES):
            q_out[b] = head_norm_rope(z_src[:, Q_OFF + b * LANES:Q_OFF + (b + 1) * LANES], q_gain).astype(BF16)

        k = head_norm_rope(z_src[:, K_OFF:V_OFF], kg_ref[...])
        v = z_src[:, V_OFF:GM_OFF]
        kv_out[:, 0 * LANES:1 * LANES] = k.astype(BF16)
        kv_out[:, 1 * LANES:2 * LANES] = pltpu.roll(k, HEAD_DIM, 1).astype(BF16)
        kv_out[:, 2 * LANES:3 * LANES] = v.astype(BF16)
        kv_out[:, 3 * LANES:4 * LANES] = pltpu.roll(v, HEAD_DIM, 1).astype(BF16)

        zuv = z_src[:, GM_OFF:CONV_OFF]
        uv = 0.5 * zuv * (1.0 + lax.erf(zuv * SQRT_HALF))
        u_out[...] = uv[:, 0:GM_WIDTH]
        vln_out[...] = _layer_norm(uv[:, GM_WIDTH:], lng_ref[...], lnb_ref[...]).astype(BF16)

        glu_out[...] = z_src[:, CONV_OFF:CONV_OFF + CONV_CH] * jax.nn.sigmoid(z_src[:, CONV_OFF + CONV_CH:])

    inner = jnp.logical_and(i > 0, i < n_tiles)

    @pl.when(i == 0)
    def _():
        w_bf[...] = w_ref[...].astype(BF16)
        project(z_even)

    @pl.when(jnp.logical_and(inner, i % 2 == 1))
    def _():
        project(z_odd)
        finish(z_even)

    @pl.when(jnp.logical_and(inner, i % 2 == 0))
    def _():
        project(z_even)
        finish(z_odd)

    @pl.when(i == n_tiles)
    def _():
        finish(z_even if last_is_even else z_odd)


def _inproj(layer, x, g, w, qg, kg, lng, lnb, tabs, wout):
    n = x.shape[0]
    tm = TM_PROJ
    n_tiles = n // tm
    cur = lambda width: pl.BlockSpec((tm, width), lambda i: (jnp.minimum(i, n_tiles - 1), 0))
    lag = lambda width: pl.BlockSpec((tm, width), lambda i: (jnp.maximum(i - 1, 0), 0))
    out_shape = (
        jax.ShapeDtypeStruct((ATTN_WIDTH // LANES, n, LANES), BF16),
        jax.ShapeDtypeStruct((n, 4 * LANES), BF16),
        jax.ShapeDtypeStruct((n, GM_WIDTH), F32),
        jax.ShapeDtypeStruct((n, GM_WIDTH), BF16),
        jax.ShapeDtypeStruct((n, CONV_CH), F32),
        jax.ShapeDtypeStruct((D_MODEL, D_MODEL), BF16),
    )
    wout_src, wout_dst = _cast_specs(layer, D_MODEL, D_MODEL, n_tiles + 1, lambda i: i)
    return pl.pallas_call(
        functools.partial(_inproj_kernel, last_is_even=(n_tiles - 1) % 2 == 0),
        out_shape=out_shape,
        grid=(n_tiles + 1,),
        in_specs=[cur(D_MODEL), _layer_spec(layer, (1, D_MODEL)), _layer_spec(layer, (D_MODEL, IN_COLS)),
                  _layer_spec(layer, (1, LANES)), _layer_spec(layer, (1, LANES)),
                  _layer_spec(layer, (1, GM_WIDTH)), _layer_spec(layer, (1, GM_WIDTH)),
                  lag(LANES), lag(LANES), wout_src],
        out_specs=(pl.BlockSpec((ATTN_WIDTH // LANES, tm, LANES), lambda i: (0, jnp.maximum(i - 1, 0), 0)),
                   lag(4 * LANES), lag(GM_WIDTH), lag(GM_WIDTH), lag(CONV_CH), wout_dst),
        scratch_shapes=[pltpu.VMEM((D_MODEL, IN_COLS), BF16),
                        pltpu.VMEM((tm, IN_COLS), F32), pltpu.VMEM((tm, IN_COLS), F32)],
        compiler_params=pltpu.CompilerParams(dimension_semantics=("arbitrary",),
                                             vmem_limit_bytes=VMEM_LIMIT),
        name="in_proj",
    )(x, g, w, qg, kg, lng, lnb, *tabs, wout)


def _mixer_kernel(layer, sink_ref, x_ref, q_ref, kvp_ref, kvc_ref, kvn_ref, u_ref, vln_ref,
                  glup_ref, gluc_ref, glun_ref,
                  ws_ref, gmb_ref, cw_ref, cb_ref, clg_ref, clb_ref, og_ref, wout_ref,
                  wgu_ref, wd_ref, wpg_ref, wpp_ref,
                  o_ref, wgu_bf_out, wd_bf_out, wpg_bf_out, wpp_bf_out,
                  ka_s, kb_s, va_s, vb_s, bias_s, s_s, p_s, scale_s, glu_s, y_s, attn_s, gm_s, merged_s):
    tq = x_ref.shape[0]
    nblk = tq // BLOCK
    j = pl.program_id(1)
    first = j == 0
    last = j == pl.num_programs(1) - 1

    wgu_bf_out[...] = wgu_ref[...].astype(BF16)
    wd_bf_out[...] = wd_ref[...].astype(BF16)
    wpg_bf_out[...] = wpg_ref[...].astype(BF16)
    wpp_bf_out[...] = wpp_ref[...].astype(BF16)

    lane = lax.broadcasted_iota(jnp.int32, (1, LANES), 1)
    head0 = lane < HEAD_DIM
    zero = jnp.zeros((), BF16)
    for src, r0, rows in ((kvp_ref, 0, BLOCK), (kvc_ref, BLOCK, tq), (kvn_ref, BLOCK + tq, BLOCK)):
        dst = slice(r0, r0 + rows)
        k, k_sw = src[:, 0 * LANES:1 * LANES], src[:, 1 * LANES:2 * LANES]
        v, v_sw = src[:, 2 * LANES:3 * LANES], src[:, 3 * LANES:4 * LANES]
        ka_s[0, dst, :] = jnp.where(head0, k, zero)
        kb_s[0, dst, :] = jnp.where(head0, zero, k_sw)
        ka_s[1, dst, :] = jnp.where(head0, k_sw, zero)
        kb_s[1, dst, :] = jnp.where(head0, zero, k)
        va_s[0, dst, :] = jnp.where(head0, v, zero)
        vb_s[0, dst, :] = jnp.where(head0, zero, v_sw)
        va_s[1, dst, :] = jnp.where(head0, v_sw, zero)
        vb_s[1, dst, :] = jnp.where(head0, zero, v)

    qi = lax.broadcasted_iota(jnp.int32, (BLOCK, BLOCK), 0)
    kj = lax.broadcasted_iota(jnp.int32, (BLOCK, BLOCK), 1)
    band_lo = jnp.where(kj >= qi, 0.0, NEG_INF)
    band_hi = jnp.where(kj <= qi, 0.0, NEG_INF)
    bias_s[0] = band_lo
    bias_s[1] = band_hi
    bias_s[2] = jnp.where(first, NEG_INF, band_lo)
    bias_s[3] = jnp.where(last, NEG_INF, band_hi)

    n_pairs = ATTN_GROUP // 2

    def scores(nb, h):
        r0 = pl.multiple_of(nb * BLOCK, BLOCK)
        q_st = jnp.concatenate([q_ref[n_pairs * h + pp, pl.ds(r0, BLOCK), :] for pp in range(n_pairs)], axis=0)
        k_st = jnp.concatenate([ka_s[h, pl.ds(r0, 3 * BLOCK), :], kb_s[h, pl.ds(r0, 3 * BLOCK), :]], axis=0)
        s_s[h] = lax.dot_general(q_st, k_st, (((1,), (1,)), ((), ())),
                                 preferred_element_type=F32)

    def softmax_pv(nb, h):
        r0 = pl.multiple_of(nb * BLOCK, BLOCK)
        lo_idx = jnp.where(nb == 0, 2, 0)
        hi_idx = jnp.where(nb == nblk - 1, 3, 1)
        v_st = jnp.concatenate([va_s[h, pl.ds(r0, 3 * BLOCK), :], vb_s[h, pl.ds(r0, 3 * BLOCK), :]], axis=0)
        for pp in range(n_pairs):
            for rc in range(0, BLOCK, SOFTMAX_ROWS):
                rr = slice(pp * BLOCK + rc, pp * BLOCK + rc + SOFTMAX_ROWS)
                br = slice(rc, rc + SOFTMAX_ROWS)
                sinks, maxes, sums = [], [], []
                for half in range(2):
                    sink = sink_ref[layer * ATTN_Q_HEADS + ATTN_GROUP * h + 2 * pp + half] * LOG2E
                    c0 = half * 3 * BLOCK
                    s0 = s_s[h, rr, c0:c0 + BLOCK] + bias_s[lo_idx, br, :]
                    s1 = s_s[h, rr, c0 + BLOCK:c0 + 2 * BLOCK]
                    s2 = s_s[h, rr, c0 + 2 * BLOCK:c0 + 3 * BLOCK] + bias_s[hi_idx, br, :]
                    m = jnp.max(jnp.maximum(jnp.maximum(s0, s1), s2), axis=-1, keepdims=True)
                    m = jnp.maximum(m, sink)
                    e0, e1, e2 = jnp.exp2(s0 - m), jnp.exp2(s1 - m), jnp.exp2(s2 - m)
                    p_s[h, rr, c0:c0 + BLOCK] = e0.astype(BF16)
                    p_s[h, rr, c0 + BLOCK:c0 + 2 * BLOCK] = e1.astype(BF16)
                    p_s[h, rr, c0 + 2 * BLOCK:c0 + 3 * BLOCK] = e2.astype(BF16)
                    sinks.append(sink)
                    maxes.append(m)
                    sums.append(jnp.sum(e0 + e1 + e2, axis=-1, keepdims=True))
                sink_term = jnp.exp2(jnp.where(head0, sinks[0], sinks[1]) - jnp.where(head0, maxes[0], maxes[1]))
                scale_s[h, rr, :] = 1.0 / (jnp.where(head0, sums[0], sums[1]) + sink_term)
        o = jnp.dot(p_s[h], v_st, preferred_element_type=F32) * scale_s[h]
        for pp in range(n_pairs):
            attn_s[n_pairs * h + pp, pl.ds(r0, BLOCK), :] = o[pp * BLOCK:(pp + 1) * BLOCK, :]

    scores(0, 0)

    def attn_step(nb, carry):
        scores(nb, 1)
        softmax_pv(nb, 0)
        scores(jnp.minimum(nb + 1, nblk - 1), 0)
        softmax_pv(nb, 1)
        return carry

    lax.fori_loop(0, nblk, attn_step, 0, unroll=ATTN_UNROLL)

    for nb in range(nblk):
        rows = slice(nb * BLOCK, (nb + 1) * BLOCK)
        for hp in range(GM_HEADS // 2):
            cols = slice(hp * LANES, (hp + 1) * LANES)
            vblk = vln_ref[rows, cols]
            lhs = jnp.concatenate([ws_ref[2 * hp], ws_ref[2 * hp + 1]], axis=1)
            rhs = jnp.concatenate([jnp.where(head0, vblk, zero), jnp.where(head0, zero, vblk)], axis=0)
            sgate = jnp.dot(lhs, rhs, preferred_element_type=F32) + gmb_ref[:, cols]
            gm_s[rows, cols] = u_ref[rows, cols] * sgate

    n_ch = CONV_CH // LANES
    for ch in range(n_ch):
        cl = slice(ch * LANES, (ch + 1) * LANES)
        glu_s[ch, 0:CONV_HALO, :] = jnp.where(first, 0.0, glup_ref[:, cl])
        glu_s[ch, CONV_HALO:CONV_HALO + tq, :] = gluc_ref[:, cl]
        glu_s[ch, CONV_HALO + tq:, :] = jnp.where(last, 0.0, glun_ref[:, cl])
    group = SUBLANES * CONV_ROW_STRIDE

    def conv_step(i, carry):
        base = pl.multiple_of(i * (CONV_GROUPS * group), CONV_GROUPS * group)
        for ch in range(n_ch):
            cl = slice(ch * LANES, (ch + 1) * LANES)
            offs = [g * group + ph for g in range(CONV_GROUPS) for ph in range(CONV_ROW_STRIDE)]
            accs = [jnp.broadcast_to(cb_ref[:, cl], (SUBLANES, LANES)) for _ in offs]
            for t in range(CONV_WIDTH):
                w = cw_ref[t:t + 1, cl]
                for a, off in enumerate(offs):
                    start = base + (off + CONV_HALO - CONV_PAD + t)
                    accs[a] = accs[a] + w * glu_s[ch, pl.ds(start, SUBLANES, stride=CONV_ROW_STRIDE), :]
            for a, off in enumerate(offs):
                y_s[ch, pl.ds(base + off, SUBLANES, stride=CONV_ROW_STRIDE), :] = accs[a]
        return carry

    lax.fori_loop(0, tq // (CONV_GROUPS * group), conv_step, 0)

    rchunk = 128
    for p0 in range(0, tq, PROJ_ROWS):
        for r0 in range(p0, p0 + PROJ_ROWS, rchunk):
            rs = slice(r0, r0 + rchunk)
            y = jnp.concatenate([y_s[ch, rs, :] for ch in range(n_ch)], axis=1)
            y = _layer_norm(y, clg_ref[...], clb_ref[...])
            y = y * jax.nn.sigmoid(y)
            merged_s[rs, ATTN_WIDTH + GM_WIDTH:] = _rms(y, og_ref[:, ATTN_WIDTH + GM_WIDTH:]).astype(BF16)
            attn = jnp.concatenate([attn_s[pb, rs, :] for pb in range(ATTN_WIDTH // LANES)], axis=1)
            merged_s[rs, 0:ATTN_WIDTH] = _rms(attn, og_ref[:, 0:ATTN_WIDTH]).astype(BF16)
            merged_s[rs, ATTN_WIDTH:ATTN_WIDTH + GM_WIDTH] = _rms(
                gm_s[rs, :], og_ref[:, ATTN_WIDTH:ATTN_WIDTH + GM_WIDTH]).astype(BF16)
        ps = slice(p0, p0 + PROJ_ROWS)
        o_ref[ps, :] = x_ref[ps, :] + jnp.dot(merged_s[ps, :], wout_ref[...], preferred_element_type=F32)


def _mixer(layer, x, q, kv, u, vln, glu, sink, ws, gm_bias, cw, cb, clg, clb, og, wout,
           wgu, wd, wpg, wpp, batch, seq):
    n = x.shape[0]
    tq = TQ_MIX
    tiles = seq // tq
    blk_per_tile = tq // BLOCK
    blk_per_seq = seq // BLOCK
    halo_per_tile = tq // CONV_HALO
    halo_per_seq = seq // CONV_HALO

    def row(width):
        return pl.BlockSpec((tq, width), lambda b, j, *_: (b * tiles + j, 0))

    def prev(rows, width, per_tile, per_seq):
        return pl.BlockSpec(
            (rows, width), lambda b, j, *_: (b * per_seq + jnp.maximum(j * per_tile - 1, 0), 0))

    def nxt(rows, width, per_tile, per_seq):
        return pl.BlockSpec(
            (rows, width), lambda b, j, *_: (b * per_seq + jnp.minimum((j + 1) * per_tile, per_seq - 1), 0))

    const = lambda shape: _layer_spec(layer, shape)
    kv_rows = tq + 2 * BLOCK
    step_of = lambda b, j, *_: b * tiles + j
    casts = [_cast_specs(layer, rows, cols, batch * tiles, step_of)
             for rows, cols in ((D_MODEL, 2 * D_FF), (D_FF, D_MODEL), (D_MODEL, D_MODEL), (PLE_DIM, D_MODEL))]
    grid_spec = pltpu.PrefetchScalarGridSpec(
        num_scalar_prefetch=1,
        grid=(batch, tiles),
        in_specs=[
            row(D_MODEL), pl.BlockSpec((ATTN_WIDTH // LANES, tq, LANES), lambda b, j, *_: (0, b * tiles + j, 0)),
            prev(BLOCK, 4 * LANES, blk_per_tile, blk_per_seq), row(4 * LANES),
            nxt(BLOCK, 4 * LANES, blk_per_tile, blk_per_seq),
            row(GM_WIDTH), row(GM_WIDTH),
            prev(CONV_HALO, CONV_CH, halo_per_tile, halo_per_seq), row(CONV_CH),
            nxt(CONV_HALO, CONV_CH, halo_per_tile, halo_per_seq),
            const((GM_HEADS, CHUNK, CHUNK)), const((CHUNK, GM_WIDTH)),
            const((CONV_WIDTH, CONV_CH)), const((1, CONV_CH)), const((1, CONV_CH)), const((1, CONV_CH)),
            const((1, D_MODEL)), _resident_spec((D_MODEL, D_MODEL)),
            *[src for src, _ in casts],
        ],
        out_specs=(row(D_MODEL), *[dst for _, dst in casts]),
        scratch_shapes=[
            pltpu.VMEM((ATTN_KV_HEADS, kv_rows, LANES), BF16), pltpu.VMEM((ATTN_KV_HEADS, kv_rows, LANES), BF16),
            pltpu.VMEM((ATTN_KV_HEADS, kv_rows, LANES), BF16), pltpu.VMEM((ATTN_KV_HEADS, kv_rows, LANES), BF16),
            pltpu.VMEM((4, BLOCK, BLOCK), F32),
            pltpu.VMEM((2, ATTN_GROUP // 2 * BLOCK, 2 * 3 * BLOCK), F32),
            pltpu.VMEM((2, ATTN_GROUP // 2 * BLOCK, 2 * 3 * BLOCK), BF16),
            pltpu.VMEM((2, ATTN_GROUP // 2 * BLOCK, LANES), F32),
            pltpu.VMEM((CONV_CH // LANES, tq + 2 * CONV_HALO, LANES), F32),
            pltpu.VMEM((CONV_CH // LANES, tq, LANES), F32),
            pltpu.VMEM((ATTN_WIDTH // LANES, tq, LANES), F32), pltpu.VMEM((tq, GM_WIDTH), F32),
            pltpu.VMEM((tq, D_MODEL), BF16),
        ],
    )
    return pl.pallas_call(
        functools.partial(_mixer_kernel, layer),
        out_shape=(jax.ShapeDtypeStruct((n, D_MODEL), F32),
                   jax.ShapeDtypeStruct((D_MODEL, 2 * D_FF), BF16), jax.ShapeDtypeStruct((D_FF, D_MODEL), BF16),
                   jax.ShapeDtypeStruct((D_MODEL, D_MODEL), BF16), jax.ShapeDtypeStruct((PLE_DIM, D_MODEL), BF16)),
        grid_spec=grid_spec,
        compiler_params=pltpu.CompilerParams(dimension_semantics=("arbitrary", "arbitrary"),
                                             vmem_limit_bytes=VMEM_LIMIT),
        name="mixer",
    )(sink, x, q, kv, kv, kv, u, vln, glu, glu, glu, ws, gm_bias, cw, cb, clg, clb, og, wout,
      wgu, wd, wpg, wpp)


def _ffn_kernel(x_ref, p_ref, gf_ref, wgu_ref, wd_ref, gp_ref, wpg_ref, wpp_ref, o_ref, act_s):
    x = x_ref[...]
    hn = _rms(x, gf_ref[...]).astype(BF16)
    for c0 in range(0, D_FF, FF_CHUNK):
        gate = jnp.dot(hn, wgu_ref[:, c0:c0 + FF_CHUNK], preferred_element_type=F32)
        up = jnp.dot(hn, wgu_ref[:, D_FF + c0:D_FF + c0 + FF_CHUNK], preferred_element_type=F32)
        act_s[:, c0:c0 + FF_CHUNK] = (gate * jax.nn.sigmoid(gate) * up).astype(BF16)
    x = x + jnp.dot(act_s[...], wd_ref[...], preferred_element_type=F32)

    hp = _rms(x, gp_ref[...]).astype(BF16)
    gate = jax.nn.sigmoid(jnp.dot(hp, wpg_ref[...], preferred_element_type=F32))
    proj = jnp.dot(p_ref[...].astype(BF16), wpp_ref[...], preferred_element_type=F32)
    o_ref[...] = x + proj * gate


def _ffn(layer, x, p, gf, wgu, wd, gp, wpg, wpp):
    n = x.shape[0]
    tm = TM_FFN
    row = lambda width: pl.BlockSpec((tm, width), lambda i: (i, 0))
    return pl.pallas_call(
        _ffn_kernel,
        out_shape=jax.ShapeDtypeStruct((n, D_MODEL), F32),
        grid=(n // tm,),
        in_specs=[row(D_MODEL), pl.BlockSpec((None, tm, PLE_DIM), lambda i: (layer, i, 0)),
                  _layer_spec(layer, (1, D_MODEL)),
                  _resident_spec((D_MODEL, 2 * D_FF)), _resident_spec((D_FF, D_MODEL)),
                  _layer_spec(layer, (1, D_MODEL)), _resident_spec((D_MODEL, D_MODEL)),
                  _resident_spec((PLE_DIM, D_MODEL))],
        out_specs=row(D_MODEL),
        scratch_shapes=[pltpu.VMEM((tm, D_FF), BF16)],
        compiler_params=pltpu.CompilerParams(dimension_semantics=("arbitrary",),
                                             vmem_limit_bytes=VMEM_LIMIT),
        name="ffn_ple",
    )(x, p, gf, wgu, wd, gp, wpg, wpp)


def kernel(x, p, positions, norm_mix_g, w_in, q_norm_g, k_norm_g, sink, gm_ln_g, gm_ln_b, gm_ws, gm_bs, conv_w, conv_b, conv_ln_g, conv_ln_b, out_norm_g, w_out, norm_ffn_g, w_gate_up, w_down, ple_norm_g, w_ple_gate, w_ple_proj):
    batch, seq, d = x.shape
    depth = w_in.shape[0]
    n = batch * seq
    assert d == D_MODEL and seq % TQ_MIX == 0
    assert n % TM_PROJ == 0 and n % TM_FFN == 0 and n % TM_ROPE == 0

    vec = lambda a: a[:, None, :]
    ws_b = gm_ws.astype(BF16)
    qg = vec(jnp.tile(q_norm_g, (1, LANES // HEAD_DIM)))
    kg = vec(jnp.tile(k_norm_g, (1, LANES // HEAD_DIM)))
    gm_bias = jnp.repeat(jnp.swapaxes(gm_bs, 1, 2), HEAD_DIM, axis=2)
    sink_flat = sink.reshape(depth * ATTN_Q_HEADS)
    p_flat = p.reshape(depth, n, PLE_DIM)

    tabs = _rope_tables(positions.reshape(1, n))
    xf = x.reshape(n, d)
    for i in range(depth):
        q, kv, u, vln, glu, w_out_b = _inproj(i, xf, vec(norm_mix_g), w_in, qg, kg, vec(gm_ln_g),
                                                 vec(gm_ln_b), tabs, w_out)
        xf, w_gu_b, w_down_b, w_pg_b, w_pp_b = _mixer(
            i, xf, q, kv, u, vln, glu, sink_flat, ws_b, gm_bias, conv_w, vec(conv_b), vec(conv_ln_g),
            vec(conv_ln_b), vec(out_norm_g), w_out_b, w_gate_up, w_down, w_ple_gate, w_ple_proj, batch, seq)
        xf = _ffn(i, xf, p_flat, vec(norm_ffn_g), w_gu_b, w_down_b, vec(ple_norm_g), w_pg_b, w_pp_b)
    return xf.reshape(batch, seq, d)
```

```python
import functools
import math

import numpy as np
import jax
import jax.numpy as jnp
from jax import lax
from jax.experimental import pallas as pl
from jax.experimental.pallas import tpu as pltpu

F32 = jnp.float32
BF16 = jnp.bfloat16

D_MODEL = 1024
HEAD_DIM = 64
ATTN_WIDTH = 512
ATTN_Q_HEADS = 8
ATTN_KV_HEADS = 2
ATTN_GROUP = 4
KV_WIDTH = 128
WINDOW = 128
BLOCK = 128
ROPE_THETA = 500000.0
ROT_DIM = 16
GM_WIDTH = 256
GM_HEADS = 4
CHUNK = 128
CONV_CH = 256
CONV_WIDTH = 31
CONV_PAD = 15
D_FF = 2816
PLE_DIM = 256
EPS = 1e-6
NEG_INF = -1e30

assert WINDOW == BLOCK and CHUNK == BLOCK

Q_OFF = 0
K_OFF = Q_OFF + ATTN_WIDTH
V_OFF = K_OFF + KV_WIDTH
GM_OFF = V_OFF + KV_WIDTH
CONV_OFF = GM_OFF + 2 * GM_WIDTH
IN_COLS = CONV_OFF + 2 * CONV_CH

LANES = 128
SUBLANES = 8
BF16_ROWS = 16
CONV_HALO = 16
CONV_ROW_STRIDE = 2
CONV_GROUPS = 8
VMEM_BYTES_V7X = 64 * 1024 * 1024
VMEM_LIMIT = VMEM_BYTES_V7X * 7 // 8

TM_ROPE = 8192
TM_PROJ = 512
TQ_MIX = 1024
TM_FFN = 1024
FF_CHUNK = 256
SOFTMAX_ROWS = 32
PROJ_ROWS = 256
ATTN_UNROLL = 8

INV_FREQ = [float(ROPE_THETA ** (-(2.0 * j) / ROT_DIM)) for j in range(ROT_DIM // 2)]
SQRT_HALF = float(np.sqrt(0.5))
LOG2E = float(np.log2(np.e))


def _layer_spec(layer, shape):
    nd = len(shape)
    return pl.BlockSpec((None,) + tuple(shape), lambda *_: (layer,) + (0,) * nd,
                        pipeline_mode=pl.Buffered(1))


def _resident_spec(shape):
    nd = len(shape)
    return pl.BlockSpec(tuple(shape), lambda *_: (0,) * nd, pipeline_mode=pl.Buffered(1))


def _cast_block_rows(rows, n_steps):
    r = BF16_ROWS
    while r * n_steps < rows or rows % r:
        r += BF16_ROWS
    return r


def _cast_specs(layer, rows, cols, n_steps, step_of):
    r = _cast_block_rows(rows, n_steps)
    last = rows // r - 1
    src = pl.BlockSpec((None, r, cols), lambda *idx: (layer, jnp.minimum(step_of(*idx), last), 0))
    dst = pl.BlockSpec((r, cols), lambda *idx: (jnp.minimum(step_of(*idx), last), 0))
    return src, dst


def _split_bf16(x):
    hi = x.astype(BF16)
    lo = (x - hi.astype(F32)).astype(BF16)
    return hi, lo


def _rope_table_kernel(pos_ref, c_ref, s_ref):
    half = ROT_DIM // 2
    pos = pos_ref[...].astype(F32)
    row = lax.broadcasted_iota(jnp.int32, (ROT_DIM, 1), 0)
    inv = jnp.zeros((ROT_DIM, 1), F32)
    for j, f in enumerate(INV_FREQ):
        inv = jnp.where((row & (half - 1)) == j, f, inv)
    ang = inv * pos
    cs = jnp.where(row < half, jnp.cos(ang), jnp.sin(ang))

    r = lax.broadcasted_iota(jnp.int32, (ROT_DIM, 2 * LANES), 0)
    col = lax.broadcasted_iota(jnp.int32, (ROT_DIM, 2 * LANES), 1)
    table = col // LANES
    in_head = col & (HEAD_DIM - 1)
    freq = col & (half - 1)
    put_cos = (table == 0) & (in_head < ROT_DIM) & (r == freq)
    put_nsin = (table == 1) & (in_head < half) & (r == freq + half)
    put_sin = (table == 1) & (in_head >= half) & (in_head < ROT_DIM) & (r == freq + half)
    place = jnp.where(put_cos | put_sin, 1.0, jnp.where(put_nsin, -1.0, 0.0)).astype(BF16)

    hi, lo = _split_bf16(cs)
    dn = (((0,), (0,)), ((), ()))
    t = (lax.dot_general(hi, place, dn, preferred_element_type=F32)
         + lax.dot_general(lo, place, dn, preferred_element_type=F32))
    lane = lax.broadcasted_iota(jnp.int32, (1, LANES), 1)
    c_ref[...] = jnp.where((lane & (HEAD_DIM - 1)) < ROT_DIM, t[:, 0:LANES], 1.0)
    s_ref[...] = t[:, LANES:2 * LANES]


def _rope_tables(pos_row):
    n = pos_row.shape[1]
    tm = TM_ROPE
    tab = jax.ShapeDtypeStruct((n, LANES), F32)
    spec = pl.BlockSpec((tm, LANES), lambda i: (i, 0))
    return pl.pallas_call(
        _rope_table_kernel,
        out_shape=(tab, tab),
        grid=(n // tm,),
        in_specs=[pl.BlockSpec((1, tm), lambda i: (0, i))],
        out_specs=(spec, spec),
        name="rope_tables",
    )(pos_row)


def _rms(x, gain):
    ms = jnp.mean(x * x, axis=-1, keepdims=True)
    return x * lax.rsqrt(ms + EPS) * gain


def _layer_norm(x, gain, bias):
    mu = jnp.mean(x, axis=-1, keepdims=True)
    xc = x - mu
    var = jnp.mean(xc * xc, axis=-1, keepdims=True)
    return xc * lax.rsqrt(var + EPS) * gain + bias


def _inproj_kernel(x_ref, g_ref, w_ref, qg_ref, kg_ref, lng_ref, lnb_ref,
                   c_ref, s_ref, wout_ref,
                   q_out, kv_out, u_out, vln_out, glu_out, wout_bf_out, w_bf, z_even, z_odd, *, last_is_even):
    i = pl.program_id(0)
    n_tiles = pl.num_programs(0) - 1

    wout_bf_out[...] = wout_ref[...].astype(BF16)

    def project(z_dst):
        h = _rms(x_ref[...], g_ref[...]).astype(BF16)
        z_dst[...] = jnp.dot(h, w_bf[...], preferred_element_type=F32)

    def finish(z_src):
        cos_t, sin_t = c_ref[...], s_ref[...]
        lane = lax.broadcasted_iota(jnp.int32, (1, LANES), 1)
        head0 = lane < HEAD_DIM

        def head_norm_rope(z, gain):
            sq = z * z
            first = jnp.sum(jnp.where(head0, sq, 0.0), axis=-1, keepdims=True)
            both = jnp.sum(sq, axis=-1, keepdims=True)
            ss = jnp.where(head0, first, both - first)
            zn = z * lax.rsqrt(ss * (1.0 / HEAD_DIM) + EPS) * gain
            partner = jnp.where((lane & (ROT_DIM // 2)) == 0,
                                pltpu.roll(zn, LANES - ROT_DIM // 2, 1), pltpu.roll(zn, ROT_DIM // 2, 1))
            return zn * cos_t + partner * sin_t

        q_gain = qg_ref[...] * (LOG2E / math.sqrt(HEAD_DIM))
        for b in range(ATTN_WIDTH // LANES):
            q_out[b] = head_norm_rope(z_src[:, Q_OFF + b * LANES:Q_OFF + (b + 1) * LANES], q_gain).astype(BF16)

        k = head_norm_rope(z_src[:, K_OFF:V_OFF], kg_ref[...])
        v = z_src[:, V_OFF:GM_OFF]
        kv_out[:, 0 * LANES:1 * LANES] = k.astype(BF16)
        kv_out[:, 1 * LANES:2 * LANES] = pltpu.roll(k, HEAD_DIM, 1).astype(BF16)
        kv_out[:, 2 * LANES:3 * LANES] = v.astype(BF16)
        kv_out[:, 3 * LANES:4 * LANES] = pltpu.roll(v, HEAD_DIM, 1).astype(BF16)

        zuv = z_src[:, GM_OFF:CONV_OFF]
        uv = 0.5 * zuv * (1.0 + lax.erf(zuv * SQRT_HALF))
        u_out[...] = uv[:, 0:GM_WIDTH]
        vln_out[...] = _layer_norm(uv[:, GM_WIDTH:], lng_ref[...], lnb_ref[...]).astype(BF16)

        glu_out[...] = z_src[:, CONV_OFF:CONV_OFF + CONV_CH] * jax.nn.sigmoid(z_src[:, CONV_OFF + CONV_CH:])

    inner = jnp.logical_and(i > 0, i < n_tiles)

    @pl.when(i == 0)
    def _():
        w_bf[...] = w_ref[...].astype(BF16)
        project(z_even)

    @pl.when(jnp.logical_and(inner, i % 2 == 1))
    def _():
        project(z_odd)
        finish(z_even)

    @pl.when(jnp.logical_and(inner, i % 2 == 0))
    def _():
        project(z_even)
        finish(z_odd)

    @pl.when(i == n_tiles)
    def _():
        finish(z_even if last_is_even else z_odd)


def _inproj(layer, x, g, w, qg, kg, lng, lnb, tabs, wout):
    n = x.shape[0]
    tm = TM_PROJ
    n_tiles = n // tm
    cur = lambda width: pl.BlockSpec((tm, width), lambda i: (jnp.minimum(i, n_tiles - 1), 0))
    lag = lambda width: pl.BlockSpec((tm, width), lambda i: (jnp.maximum(i - 1, 0), 0))
    out_shape = (
        jax.ShapeDtypeStruct((ATTN_WIDTH // LANES, n, LANES), BF16),
        jax.ShapeDtypeStruct((n, 4 * LANES), BF16),
        jax.ShapeDtypeStruct((n, GM_WIDTH), F32),
        jax.ShapeDtypeStruct((n, GM_WIDTH), BF16),
        jax.ShapeDtypeStruct((n, CONV_CH), F32),
        jax.ShapeDtypeStruct((D_MODEL, D_MODEL), BF16),
    )
    wout_src, wout_dst = _cast_specs(layer, D_MODEL, D_MODEL, n_tiles + 1, lambda i: i)
    return pl.pallas_call(
        functools.partial(_inproj_kernel, last_is_even=(n_tiles - 1) % 2 == 0),
        out_shape=out_shape,
        grid=(n_tiles + 1,),
        in_specs=[cur(D_MODEL), _layer_spec(layer, (1, D_MODEL)), _layer_spec(layer, (D_MODEL, IN_COLS)),
                  _layer_spec(layer, (1, LANES)), _layer_spec(layer, (1, LANES)),
                  _layer_spec(layer, (1, GM_WIDTH)), _layer_spec(layer, (1, GM_WIDTH)),
                  lag(LANES), lag(LANES), wout_src],
        out_specs=(pl.BlockSpec((ATTN_WIDTH // LANES, tm, LANES), lambda i: (0, jnp.maximum(i - 1, 0), 0)),
                   lag(4 * LANES), lag(GM_WIDTH), lag(GM_WIDTH), lag(CONV_CH), wout_dst),
        scratch_shapes=[pltpu.VMEM((D_MODEL, IN_COLS), BF16),
                        pltpu.VMEM((tm, IN_COLS), F32), pltpu.VMEM((tm, IN_COLS), F32)],
        compiler_params=pltpu.CompilerParams(dimension_semantics=("arbitrary",),
                                             vmem_limit_bytes=VMEM_LIMIT),
        name="in_proj",
    )(x, g, w, qg, kg, lng, lnb, *tabs, wout)


def _mixer_kernel(layer, sink_ref, x_ref, q_ref, kvp_ref, kvc_ref, kvn_ref, u_ref, vln_ref,
                  glup_ref, gluc_ref, glun_ref,
                  ws_ref, gmb_ref, cw_ref, cb_ref, clg_ref, clb_ref, og_ref, wout_ref,
                  wgu_ref, wd_ref, wpg_ref, wpp_ref,
                  o_ref, wgu_bf_out, wd_bf_out, wpg_bf_out, wpp_bf_out,
                  ka_s, kb_s, va_s, vb_s, bias_s, s_s, p_s, scale_s, glu_s, y_s, attn_s, gm_s, merged_s):
    tq = x_ref.shape[0]
    nblk = tq // BLOCK
    j = pl.program_id(1)
    first = j == 0
    last = j == pl.num_programs(1) - 1

    wgu_bf_out[...] = wgu_ref[...].astype(BF16)
    wd_bf_out[...] = wd_ref[...].astype(BF16)
    wpg_bf_out[...] = wpg_ref[...].astype(BF16)
    wpp_bf_out[...] = wpp_ref[...].astype(BF16)

    lane = lax.broadcasted_iota(jnp.int32, (1, LANES), 1)
    head0 = lane < HEAD_DIM
    zero = jnp.zeros((), BF16)
    for src, r0, rows in ((kvp_ref, 0, BLOCK), (kvc_ref, BLOCK, tq), (kvn_ref, BLOCK + tq, BLOCK)):
        dst = slice(r0, r0 + rows)
        k, k_sw = src[:, 0 * LANES:1 * LANES], src[:, 1 * LANES:2 * LANES]
        v, v_sw = src[:, 2 * LANES:3 * LANES], src[:, 3 * LANES:4 * LANES]
        ka_s[0, dst, :] = jnp.where(head0, k, zero)
        kb_s[0, dst, :] = jnp.where(head0, zero, k_sw)
        ka_s[1, dst, :] = jnp.where(head0, k_sw, zero)
        kb_s[1, dst, :] = jnp.where(head0, zero, k)
        va_s[0, dst, :] = jnp.where(head0, v, zero)
        vb_s[0, dst, :] = jnp.where(head0, zero, v_sw)
        va_s[1, dst, :] = jnp.where(head0, v_sw, zero)
        vb_s[1, dst, :] = jnp.where(head0, zero, v)

    qi = lax.broadcasted_iota(jnp.int32, (BLOCK, BLOCK), 0)
    kj = lax.broadcasted_iota(jnp.int32, (BLOCK, BLOCK), 1)
    band_lo = jnp.where(kj >= qi, 0.0, NEG_INF)
    band_hi = jnp.where(kj <= qi, 0.0, NEG_INF)
    bias_s[0] = band_lo
    bias_s[1] = band_hi
    bias_s[2] = jnp.where(first, NEG_INF, band_lo)
    bias_s[3] = jnp.where(last, NEG_INF, band_hi)

    n_pairs = ATTN_GROUP // 2

    def scores(nb, h):
        r0 = pl.multiple_of(nb * BLOCK, BLOCK)
        q_st = jnp.concatenate([q_ref[n_pairs * h + pp, pl.ds(r0, BLOCK), :] for pp in range(n_pairs)], axis=0)
        k_st = jnp.concatenate([ka_s[h, pl.ds(r0, 3 * BLOCK), :], kb_s[h, pl.ds(r0, 3 * BLOCK), :]], axis=0)
        s_s[h] = lax.dot_general(q_st, k_st, (((1,), (1,)), ((), ())),
                                 preferred_element_type=F32)

    def softmax_pv(nb, h):
        r0 = pl.multiple_of(nb * BLOCK, BLOCK)
        lo_idx = jnp.where(nb == 0, 2, 0)
        hi_idx = jnp.where(nb == nblk - 1, 3, 1)
        v_st = jnp.concatenate([va_s[h, pl.ds(r0, 3 * BLOCK), :], vb_s[h, pl.ds(r0, 3 * BLOCK), :]], axis=0)
        for pp in range(n_pairs):
            for rc in range(0, BLOCK, SOFTMAX_ROWS):
                rr = slice(pp * BLOCK + rc, pp * BLOCK + rc + SOFTMAX_ROWS)
                br = slice(rc, rc + SOFTMAX_ROWS)
                sinks, maxes, sums = [], [], []
                for half in range(2):
                    sink = sink_ref[layer * ATTN_Q_HEADS + ATTN_GROUP * h + 2 * pp + half] * LOG2E
                    c0 = half * 3 * BLOCK
                    s0 = s_s[h, rr, c0:c0 + BLOCK] + bias_s[lo_idx, br, :]
                    s1 = s_s[h, rr, c0 + BLOCK:c0 + 2 * BLOCK]
                    s2 = s_s[h, rr, c0 + 2 * BLOCK:c0 + 3 * BLOCK] + bias_s[hi_idx, br, :]
                    m = jnp.max(jnp.maximum(jnp.maximum(s0, s1), s2), axis=-1, keepdims=True)
                    m = jnp.maximum(m, sink)
                    e0, e1, e2 = jnp.exp2(s0 - m), jnp.exp2(s1 - m), jnp.exp2(s2 - m)
                    p_s[h, rr, c0:c0 + BLOCK] = e0.astype(BF16)
                    p_s[h, rr, c0 + BLOCK:c0 + 2 * BLOCK] = e1.astype(BF16)
                    p_s[h, rr, c0 + 2 * BLOCK:c0 + 3 * BLOCK] = e2.astype(BF16)
                    sinks.append(sink)
                    maxes.append(m)
                    sums.append(jnp.sum(e0 + e1 + e2, axis=-1, keepdims=True))
                sink_term = jnp.exp2(jnp.where(head0, sinks[0], sinks[1]) - jnp.where(head0, maxes[0], maxes[1]))
                scale_s[h, rr, :] = 1.0 / (jnp.where(head0, sums[0], sums[1]) + sink_term)
        o = jnp.dot(p_s[h], v_st, preferred_element_type=F32) * scale_s[h]
        for pp in range(n_pairs):
            attn_s[n_pairs * h + pp, pl.ds(r0, BLOCK), :] = o[pp * BLOCK:(pp + 1) * BLOCK, :]

    scores(0, 0)

    def attn_step(nb, carry):
        scores(nb, 1)
        softmax_pv(nb, 0)
        scores(jnp.minimum(nb + 1, nblk - 1), 0)
        softmax_pv(nb, 1)
        return carry

    lax.fori_loop(0, nblk, attn_step, 0, unroll=ATTN_UNROLL)

    for nb in range(nblk):
        rows = slice(nb * BLOCK, (nb + 1) * BLOCK)
        for hp in range(GM_HEADS // 2):
            cols = slice(hp * LANES, (hp + 1) * LANES)
            vblk = vln_ref[rows, cols]
            lhs = jnp.concatenate([ws_ref[2 * hp], ws_ref[2 * hp + 1]], axis=1)
            rhs = jnp.concatenate([jnp.where(head0, vblk, zero), jnp.where(head0, zero, vblk)], axis=0)
            sgate = jnp.dot(lhs, rhs, preferred_element_type=F32) + gmb_ref[:, cols]
            gm_s[rows, cols] = u_ref[rows, cols] * sgate

    n_ch = CONV_CH // LANES
    for ch in range(n_ch):
        cl = slice(ch * LANES, (ch + 1) * LANES)
        glu_s[ch, 0:CONV_HALO, :] = jnp.where(first, 0.0, glup_ref[:, cl])
        glu_s[ch, CONV_HALO:CONV_HALO + tq, :] = gluc_ref[:, cl]
        glu_s[ch, CONV_HALO + tq:, :] = jnp.where(last, 0.0, glun_ref[:, cl])
    group = SUBLANES * CONV_ROW_STRIDE

    def conv_step(i, carry):
        base = pl.multiple_of(i * (CONV_GROUPS * group), CONV_GROUPS * group)
        for ch in range(n_ch):
            cl = slice(ch * LANES, (ch + 1) * LANES)
            offs = [g * group + ph for g in range(CONV_GROUPS) for ph in range(CONV_ROW_STRIDE)]
            accs = [jnp.broadcast_to(cb_ref[:, cl], (SUBLANES, LANES)) for _ in offs]
            for t in range(CONV_WIDTH):
                w = cw_ref[t:t + 1, cl]
                for a, off in enumerate(offs):
                    start = base + (off + CONV_HALO - CONV_PAD + t)
                    accs[a] = accs[a] + w * glu_s[ch, pl.ds(start, SUBLANES, stride=CONV_ROW_STRIDE), :]
            for a, off in enumerate(offs):
                y_s[ch, pl.ds(base + off, SUBLANES, stride=CONV_ROW_STRIDE), :] = accs[a]
        return carry

    lax.fori_loop(0, tq // (CONV_GROUPS * group), conv_step, 0)

    rchunk = 128
    for p0 in range(0, tq, PROJ_ROWS):
        for r0 in range(p0, p0 + PROJ_ROWS, rchunk):
            rs = slice(r0, r0 + rchunk)
            y = jnp.concatenate([y_s[ch, rs, :] for ch in range(n_ch)], axis=1)
            y = _layer_norm(y, clg_ref[...], clb_ref[...])
            y = y * jax.nn.sigmoid(y)
            merged_s[rs, ATTN_WIDTH + GM_WIDTH:] = _rms(y, og_ref[:, ATTN_WIDTH + GM_WIDTH:]).astype(BF16)
            attn = jnp.concatenate([attn_s[pb, rs, :] for pb in range(ATTN_WIDTH // LANES)], axis=1)
            merged_s[rs, 0:ATTN_WIDTH] = _rms(attn, og_ref[:, 0:ATTN_WIDTH]).astype(BF16)
            merged_s[rs, ATTN_WIDTH:ATTN_WIDTH + GM_WIDTH] = _rms(
                gm_s[rs, :], og_ref[:, ATTN_WIDTH:ATTN_WIDTH + GM_WIDTH]).astype(BF16)
        ps = slice(p0, p0 + PROJ_ROWS)
        o_ref[ps, :] = x_ref[ps, :] + jnp.dot(merged_s[ps, :], wout_ref[...], preferred_element_type=F32)


def _mixer(layer, x, q, kv, u, vln, glu, sink, ws, gm_bias, cw, cb, clg, clb, og, wout,
           wgu, wd, wpg, wpp, batch, seq):
    n = x.shape[0]
    tq = TQ_MIX
    tiles = seq // tq
    blk_per_tile = tq // BLOCK
    blk_per_seq = seq // BLOCK
    halo_per_tile = tq // CONV_HALO
    halo_per_seq = seq // CONV_HALO

    def row(width):
        return pl.BlockSpec((tq, width), lambda b, j, *_: (b * tiles + j, 0))

    def prev(rows, width, per_tile, per_seq):
        return pl.BlockSpec(
            (rows, width), lambda b, j, *_: (b * per_seq + jnp.maximum(j * per_tile - 1, 0), 0))

    def nxt(rows, width, per_tile, per_seq):
        return pl.BlockSpec(
            (rows, width), lambda b, j, *_: (b * per_seq + jnp.minimum((j + 1) * per_tile, per_seq - 1), 0))

    const = lambda shape: _layer_spec(layer, shape)
    kv_rows = tq + 2 * BLOCK
    step_of = lambda b, j, *_: b * tiles + j
    casts = [_cast_specs(layer, rows, cols, batch * tiles, step_of)
             for rows, cols in ((D_MODEL, 2 * D_FF), (D_FF, D_MODEL), (D_MODEL, D_MODEL), (PLE_DIM, D_MODEL))]
    grid_spec = pltpu.PrefetchScalarGridSpec(
        num_scalar_prefetch=1,
        grid=(batch, tiles),
        in_specs=[
            row(D_MODEL), pl.BlockSpec((ATTN_WIDTH // LANES, tq, LANES), lambda b, j, *_: (0, b * tiles + j, 0)),
            prev(BLOCK, 4 * LANES, blk_per_tile, blk_per_seq), row(4 * LANES),
            nxt(BLOCK, 4 * LANES, blk_per_tile, blk_per_seq),
            row(GM_WIDTH), row(GM_WIDTH),
            prev(CONV_HALO, CONV_CH, halo_per_tile, halo_per_seq), row(CONV_CH),
            nxt(CONV_HALO, CONV_CH, halo_per_tile, halo_per_seq),
            const((GM_HEADS, CHUNK, CHUNK)), const((CHUNK, GM_WIDTH)),
            const((CONV_WIDTH, CONV_CH)), const((1, CONV_CH)), const((1, CONV_CH)), const((1, CONV_CH)),
            const((1, D_MODEL)), _resident_spec((D_MODEL, D_MODEL)),
            *[src for src, _ in casts],
        ],
        out_specs=(row(D_MODEL), *[dst for _, dst in casts]),
        scratch_shapes=[
            pltpu.VMEM((ATTN_KV_HEADS, kv_rows, LANES), BF16), pltpu.VMEM((ATTN_KV_HEADS, kv_rows, LANES), BF16),
            pltpu.VMEM((ATTN_KV_HEADS, kv_rows, LANES), BF16), pltpu.VMEM((ATTN_KV_HEADS, kv_rows, LANES), BF16),
            pltpu.VMEM((4, BLOCK, BLOCK), F32),
            pltpu.VMEM((2, ATTN_GROUP // 2 * BLOCK, 2 * 3 * BLOCK), F32),
            pltpu.VMEM((2, ATTN_GROUP // 2 * BLOCK, 2 * 3 * BLOCK), BF16),
            pltpu.VMEM((2, ATTN_GROUP // 2 * BLOCK, LANES), F32),
            pltpu.VMEM((CONV_CH // LANES, tq + 2 * CONV_HALO, LANES), F32),
            pltpu.VMEM((CONV_CH // LANES, tq, LANES), F32),
            pltpu.VMEM((ATTN_WIDTH // LANES, tq, LANES), F32), pltpu.VMEM((tq, GM_WIDTH), F32),
            pltpu.VMEM((tq, D_MODEL), BF16),
        ],
    )
    return pl.pallas_call(
        functools.partial(_mixer_kernel, layer),
        out_shape=(jax.ShapeDtypeStruct((n, D_MODEL), F32),
                   jax.ShapeDtypeStruct((D_MODEL, 2 * D_FF), BF16), jax.ShapeDtypeStruct((D_FF, D_MODEL), BF16),
                   jax.ShapeDtypeStruct((D_MODEL, D_MODEL), BF16), jax.ShapeDtypeStruct((PLE_DIM, D_MODEL), BF16)),
        grid_spec=grid_spec,
        compiler_params=pltpu.CompilerParams(dimension_semantics=("arbitrary", "arbitrary"),
                                             vmem_limit_bytes=VMEM_LIMIT),
        name="mixer",
    )(sink, x, q, kv, kv, kv, u, vln, glu, glu, glu, ws, gm_bias, cw, cb, clg, clb, og, wout,
      wgu, wd, wpg, wpp)


def _ffn_kernel(x_ref, p_ref, gf_ref, wgu_ref, wd_ref, gp_ref, wpg_ref, wpp_ref, o_ref, act_s):
    x = x_ref[...]
    hn = _rms(x, gf_ref[...]).astype(BF16)
    for c0 in range(0, D_FF, FF_CHUNK):
        gate = jnp.dot(hn, wgu_ref[:, c0:c0 + FF_CHUNK], preferred_element_type=F32)
        up = jnp.dot(hn, wgu_ref[:, D_FF + c0:D_FF + c0 + FF_CHUNK], preferred_element_type=F32)
        act_s[:, c0:c0 + FF_CHUNK] = (gate * jax.nn.sigmoid(gate) * up).astype(BF16)
    x = x + jnp.dot(act_s[...], wd_ref[...], preferred_element_type=F32)

    hp = _rms(x, gp_ref[...]).astype(BF16)
    gate = jax.nn.sigmoid(jnp.dot(hp, wpg_ref[...], preferred_element_type=F32))
    proj = jnp.dot(p_ref[...].astype(BF16), wpp_ref[...], preferred_element_type=F32)
    o_ref[...] = x + proj * gate


def _ffn(layer, x, p, gf, wgu, wd, gp, wpg, wpp):
    n = x.shape[0]
    tm = TM_FFN
    row = lambda width: pl.BlockSpec((tm, width), lambda i: (i, 0))
    return pl.pallas_call(
        _ffn_kernel,
        out_shape=jax.ShapeDtypeStruct((n, D_MODEL), F32),
        grid=(n // tm,),
        in_specs=[row(D_MODEL), pl.BlockSpec((None, tm, PLE_DIM), lambda i: (layer, i, 0)),
                  _layer_spec(layer, (1, D_MODEL)),
                  _resident_spec((D_MODEL, 2 * D_FF)), _resident_spec((D_FF, D_MODEL)),
                  _layer_spec(layer, (1, D_MODEL)), _resident_spec((D_MODEL, D_MODEL)),
                  _resident_spec((PLE_DIM, D_MODEL))],
        out_specs=row(D_MODEL),
        scratch_shapes=[pltpu.VMEM((tm, D_FF), BF16)],
        compiler_params=pltpu.CompilerParams(dimension_semantics=("arbitrary",),
                                             vmem_limit_bytes=VMEM_LIMIT),
        name="ffn_ple",
    )(x, p, gf, wgu, wd, gp, wpg, wpp)


def kernel(x, p, positions, norm_mix_g, w_in, q_norm_g, k_norm_g, sink, gm_ln_g, gm_ln_b, gm_ws, gm_bs, conv_w, conv_b, conv_ln_g, conv_ln_b, out_norm_g, w_out, norm_ffn_g, w_gate_up, w_down, ple_norm_g, w_ple_gate, w_ple_proj):
    batch, seq, d = x.shape
    depth = w_in.shape[0]
    n = batch * seq
    assert d == D_MODEL and seq % TQ_MIX == 0
    assert n % TM_PROJ == 0 and n % TM_FFN == 0 and n % TM_ROPE == 0

    vec = lambda a: a[:, None, :]
    ws_b = gm_ws.astype(BF16)
    qg = vec(jnp.tile(q_norm_g, (1, LANES // HEAD_DIM)))
    kg = vec(jnp.tile(k_norm_g, (1, LANES // HEAD_DIM)))
    gm_bias = jnp.repeat(jnp.swapaxes(gm_bs, 1, 2), HEAD_DIM, axis=2)
    sink_flat = sink.reshape(depth * ATTN_Q_HEADS)
    p_flat = p.reshape(depth, n, PLE_DIM)

    tabs = _rope_tables(positions.reshape(1, n))
    xf = x.reshape(n, d)
    for i in range(depth):
        q, kv, u, vln, glu, w_out_b = _inproj(i, xf, vec(norm_mix_g), w_in, qg, kg, vec(gm_ln_g),
                                                 vec(gm_ln_b), tabs, w_out)
        xf, w_gu_b, w_down_b, w_pg_b, w_pp_b = _mixer(
            i, xf, q, kv, u, vln, glu, sink_flat, ws_b, gm_bias, conv_w, vec(conv_b), vec(conv_ln_g),
            vec(conv_ln_b), vec(out_norm_g), w_out_b, w_gate_up, w_down, w_ple_gate, w_ple_proj, batch, seq)
        xf = _ffn(i, xf, p_flat, vec(norm_ffn_g), w_gu_b, w_down_b, vec(ple_norm_g), w_pg_b, w_pp_b)
    return xf.reshape(batch, seq, d)
```

```python
import functools
import math

import numpy as np
import jax
import jax.numpy as jnp
from jax import lax
from jax.experimental import pallas as pl
from jax.experimental.pallas import tpu as pltpu

F32 = jnp.float32
BF16 = jnp.bfloat16

D_MODEL = 1024
HEAD_DIM = 64
ATTN_WIDTH = 512
ATTN_Q_HEADS = 8
ATTN_KV_HEADS = 2
ATTN_GROUP = 4
KV_WIDTH = 128
WINDOW = 128
BLOCK = 128
ROPE_THETA = 500000.0
ROT_DIM = 16
GM_WIDTH = 256
GM_HEADS = 4
CHUNK = 128
CONV_CH = 256
CONV_WIDTH = 31
CONV_PAD = 15
D_FF = 2816
PLE_DIM = 256
EPS = 1e-6
NEG_INF = -1e30

assert WINDOW == BLOCK and CHUNK == BLOCK

Q_OFF = 0
K_OFF = Q_OFF + ATTN_WIDTH
V_OFF = K_OFF + KV_WIDTH
GM_OFF = V_OFF + KV_WIDTH
CONV_OFF = GM_OFF + 2 * GM_WIDTH
IN_COLS = CONV_OFF + 2 * CONV_CH

LANES = 128
SUBLANES = 8
BF16_ROWS = 16
CONV_HALO = 16
CONV_ROW_STRIDE = 2
CONV_GROUPS = 8
VMEM_BYTES_V7X = 64 * 1024 * 1024
VMEM_LIMIT = VMEM_BYTES_V7X * 7 // 8

TM_ROPE = 8192
TM_PROJ = 512
TQ_MIX = 1024
TM_FFN = 1024
FF_CHUNK = 256
SOFTMAX_ROWS = 32
PROJ_ROWS = 256

INV_FREQ = [float(ROPE_THETA ** (-(2.0 * j) / ROT_DIM)) for j in range(ROT_DIM // 2)]
SQRT_HALF = float(np.sqrt(0.5))
LOG2E = float(np.log2(np.e))


def _layer_spec(layer, shape):
    nd = len(shape)
    return pl.BlockSpec((None,) + tuple(shape), lambda *_: (layer,) + (0,) * nd,
                        pipeline_mode=pl.Buffered(1))


def _resident_spec(shape):
    nd = len(shape)
    return pl.BlockSpec(tuple(shape), lambda *_: (0,) * nd, pipeline_mode=pl.Buffered(1))


def _cast_block_rows(rows, n_steps):
    r = BF16_ROWS
    while r * n_steps < rows or rows % r:
        r += BF16_ROWS
    return r


def _cast_specs(layer, rows, cols, n_steps, step_of):
    r = _cast_block_rows(rows, n_steps)
    last = rows // r - 1
    src = pl.BlockSpec((None, r, cols), lambda *idx: (layer, jnp.minimum(step_of(*idx), last), 0))
    dst = pl.BlockSpec((r, cols), lambda *idx: (jnp.minimum(step_of(*idx), last), 0))
    return src, dst


def _split_bf16(x):
    hi = x.astype(BF16)
    lo = (x - hi.astype(F32)).astype(BF16)
    return hi, lo


def _rope_table_kernel(pos_ref, c_ref, s_ref):
    half = ROT_DIM // 2
    pos = pos_ref[...].astype(F32)
    row = lax.broadcasted_iota(jnp.int32, (ROT_DIM, 1), 0)
    inv = jnp.zeros((ROT_DIM, 1), F32)
    for j, f in enumerate(INV_FREQ):
        inv = jnp.where((row & (half - 1)) == j, f, inv)
    ang = inv * pos
    cs = jnp.where(row < half, jnp.cos(ang), jnp.sin(ang))

    r = lax.broadcasted_iota(jnp.int32, (ROT_DIM, 2 * LANES), 0)
    col = lax.broadcasted_iota(jnp.int32, (ROT_DIM, 2 * LANES), 1)
    table = col // LANES
    in_head = col & (HEAD_DIM - 1)
    freq = col & (half - 1)
    put_cos = (table == 0) & (in_head < ROT_DIM) & (r == freq)
    put_nsin = (table == 1) & (in_head < half) & (r == freq + half)
    put_sin = (table == 1) & (in_head >= half) & (in_head < ROT_DIM) & (r == freq + half)
    place = jnp.where(put_cos | put_sin, 1.0, jnp.where(put_nsin, -1.0, 0.0)).astype(BF16)

    hi, lo = _split_bf16(cs)
    dn = (((0,), (0,)), ((), ()))
    t = (lax.dot_general(hi, place, dn, preferred_element_type=F32)
         + lax.dot_general(lo, place, dn, preferred_element_type=F32))
    lane = lax.broadcasted_iota(jnp.int32, (1, LANES), 1)
    c_ref[...] = jnp.where((lane & (HEAD_DIM - 1)) < ROT_DIM, t[:, 0:LANES], 1.0)
    s_ref[...] = t[:, LANES:2 * LANES]


def _rope_tables(pos_row):
    n = pos_row.shape[1]
    tm = TM_ROPE
    tab = jax.ShapeDtypeStruct((n, LANES), F32)
    spec = pl.BlockSpec((tm, LANES), lambda i: (i, 0))
    return pl.pallas_call(
        _rope_table_kernel,
        out_shape=(tab, tab),
        grid=(n // tm,),
        in_specs=[pl.BlockSpec((1, tm), lambda i: (0, i))],
        out_specs=(spec, spec),
        name="rope_tables",
    )(pos_row)


def _rms(x, gain):
    ms = jnp.mean(x * x, axis=-1, keepdims=True)
    return x * lax.rsqrt(ms + EPS) * gain


def _layer_norm(x, gain, bias):
    mu = jnp.mean(x, axis=-1, keepdims=True)
    xc = x - mu
    var = jnp.mean(xc * xc, axis=-1, keepdims=True)
    return xc * lax.rsqrt(var + EPS) * gain + bias


def _inproj_kernel(x_ref, g_ref, w_ref, qg_ref, kg_ref, lng_ref, lnb_ref,
                   c_ref, s_ref, wout_ref,
                   q_out, kv_out, u_out, vln_out, glu_out, wout_bf_out, w_bf, z_even, z_odd, *, last_is_even):
    i = pl.program_id(0)
    n_tiles = pl.num_programs(0) - 1

    wout_bf_out[...] = wout_ref[...].astype(BF16)

    def project(z_dst):
        h = _rms(x_ref[...], g_ref[...]).astype(BF16)
        z_dst[...] = jnp.dot(h, w_bf[...], preferred_element_type=F32)

    def finish(z_src):
        cos_t, sin_t = c_ref[...], s_ref[...]
        lane = lax.broadcasted_iota(jnp.int32, (1, LANES), 1)
        head0 = lane < HEAD_DIM

        def head_norm_rope(z, gain):
            sq = z * z
            first = jnp.sum(jnp.where(head0, sq, 0.0), axis=-1, keepdims=True)
            both = jnp.sum(sq, axis=-1, keepdims=True)
            ss = jnp.where(head0, first, both - first)
            zn = z * lax.rsqrt(ss * (1.0 / HEAD_DIM) + EPS) * gain
            partner = jnp.where((lane & (ROT_DIM // 2)) == 0,
                                pltpu.roll(zn, LANES - ROT_DIM // 2, 1), pltpu.roll(zn, ROT_DIM // 2, 1))
            return zn * cos_t + partner * sin_t

        q_gain = qg_ref[...] * (LOG2E / math.sqrt(HEAD_DIM))
        for b in range(ATTN_WIDTH // LANES):
            q_out[b] = head_norm_rope(z_src[:, Q_OFF + b * LANES:Q_OFF + (b + 1) * LANES], q_gain).astype(BF16)

        k = head_norm_rope(z_src[:, K_OFF:V_OFF], kg_ref[...])
        v = z_src[:, V_OFF:GM_OFF]
        kv_out[:, 0 * LANES:1 * LANES] = k.astype(BF16)
        kv_out[:, 1 * LANES:2 * LANES] = pltpu.roll(k, HEAD_DIM, 1).astype(BF16)
        kv_out[:, 2 * LANES:3 * LANES] = v.astype(BF16)
        kv_out[:, 3 * LANES:4 * LANES] = pltpu.roll(v, HEAD_DIM, 1).astype(BF16)

        zuv = z_src[:, GM_OFF:CONV_OFF]
        uv = 0.5 * zuv * (1.0 + lax.erf(zuv * SQRT_HALF))
        u_out[...] = uv[:, 0:GM_WIDTH]
        vln_out[...] = _layer_norm(uv[:, GM_WIDTH:], lng_ref[...], lnb_ref[...]).astype(BF16)

        glu_out[...] = z_src[:, CONV_OFF:CONV_OFF + CONV_CH] * jax.nn.sigmoid(z_src[:, CONV_OFF + CONV_CH:])

    inner = jnp.logical_and(i > 0, i < n_tiles)

    @pl.when(i == 0)
    def _():
        w_bf[...] = w_ref[...].astype(BF16)
        project(z_even)

    @pl.when(jnp.logical_and(inner, i % 2 == 1))
    def _():
        project(z_odd)
        finish(z_even)

    @pl.when(jnp.logical_and(inner, i % 2 == 0))
    def _():
        project(z_even)
        finish(z_odd)

    @pl.when(i == n_tiles)
    def _():
        finish(z_even if last_is_even else z_odd)


def _inproj(layer, x, g, w, qg, kg, lng, lnb, tabs, wout):
    n = x.shape[0]
    tm = TM_PROJ
    n_tiles = n // tm
    cur = lambda width: pl.BlockSpec((tm, width), lambda i: (jnp.minimum(i, n_tiles - 1), 0))
    lag = lambda width: pl.BlockSpec((tm, width), lambda i: (jnp.maximum(i - 1, 0), 0))
    out_shape = (
        jax.ShapeDtypeStruct((ATTN_WIDTH // LANES, n, LANES), BF16),
        jax.ShapeDtypeStruct((n, 4 * LANES), BF16),
        jax.ShapeDtypeStruct((n, GM_WIDTH), F32),
        jax.ShapeDtypeStruct((n, GM_WIDTH), BF16),
        jax.ShapeDtypeStruct((n, CONV_CH), F32),
        jax.ShapeDtypeStruct((D_MODEL, D_MODEL), BF16),
    )
    wout_src, wout_dst = _cast_specs(layer, D_MODEL, D_MODEL, n_tiles + 1, lambda i: i)
    return pl.pallas_call(
        functools.partial(_inproj_kernel, last_is_even=(n_tiles - 1) % 2 == 0),
        out_shape=out_shape,
        grid=(n_tiles + 1,),
        in_specs=[cur(D_MODEL), _layer_spec(layer, (1, D_MODEL)), _layer_spec(layer, (D_MODEL, IN_COLS)),
                  _layer_spec(layer, (1, LANES)), _layer_spec(layer, (1, LANES)),
                  _layer_spec(layer, (1, GM_WIDTH)), _layer_spec(layer, (1, GM_WIDTH)),
                  lag(LANES), lag(LANES), wout_src],
        out_specs=(pl.BlockSpec((ATTN_WIDTH // LANES, tm, LANES), lambda i: (0, jnp.maximum(i - 1, 0), 0)),
                   lag(4 * LANES), lag(GM_WIDTH), lag(GM_WIDTH), lag(CONV_CH), wout_dst),
        scratch_shapes=[pltpu.VMEM((D_MODEL, IN_COLS), BF16),
                        pltpu.VMEM((tm, IN_COLS), F32), pltpu.VMEM((tm, IN_COLS), F32)],
        compiler_params=pltpu.CompilerParams(dimension_semantics=("arbitrary",),
                                             vmem_limit_bytes=VMEM_LIMIT),
        name="in_proj",
    )(x, g, w, qg, kg, lng, lnb, *tabs, wout)


def _mixer_kernel(layer, sink_ref, x_ref, q_ref, kvp_ref, kvc_ref, kvn_ref, u_ref, vln_ref,
                  glup_ref, gluc_ref, glun_ref,
                  ws_ref, gmb_ref, cw_ref, cb_ref, clg_ref, clb_ref, og_ref, wout_ref,
                  wgu_ref, wd_ref, wpg_ref, wpp_ref,
                  o_ref, wgu_bf_out, wd_bf_out, wpg_bf_out, wpp_bf_out,
                  ka_s, kb_s, va_s, vb_s, bias_s, s_s, p_s, scale_s, glu_s, y_s, attn_s, gm_s, merged_s):
    tq = x_ref.shape[0]
    nblk = tq // BLOCK
    j = pl.program_id(1)
    first = j == 0
    last = j == pl.num_programs(1) - 1

    wgu_bf_out[...] = wgu_ref[...].astype(BF16)
    wd_bf_out[...] = wd_ref[...].astype(BF16)
    wpg_bf_out[...] = wpg_ref[...].astype(BF16)
    wpp_bf_out[...] = wpp_ref[...].astype(BF16)

    lane = lax.broadcasted_iota(jnp.int32, (1, LANES), 1)
    head0 = lane < HEAD_DIM
    zero = jnp.zeros((), BF16)
    for src, r0, rows in ((kvp_ref, 0, BLOCK), (kvc_ref, BLOCK, tq), (kvn_ref, BLOCK + tq, BLOCK)):
        dst = slice(r0, r0 + rows)
        k, k_sw = src[:, 0 * LANES:1 * LANES], src[:, 1 * LANES:2 * LANES]
        v, v_sw = src[:, 2 * LANES:3 * LANES], src[:, 3 * LANES:4 * LANES]
        ka_s[0, dst, :] = jnp.where(head0, k, zero)
        kb_s[0, dst, :] = jnp.where(head0, zero, k_sw)
        ka_s[1, dst, :] = jnp.where(head0, k_sw, zero)
        kb_s[1, dst, :] = jnp.where(head0, zero, k)
        va_s[0, dst, :] = jnp.where(head0, v, zero)
        vb_s[0, dst, :] = jnp.where(head0, zero, v_sw)
        va_s[1, dst, :] = jnp.where(head0, v_sw, zero)
        vb_s[1, dst, :] = jnp.where(head0, zero, v)

    qi = lax.broadcasted_iota(jnp.int32, (BLOCK, BLOCK), 0)
    kj = lax.broadcasted_iota(jnp.int32, (BLOCK, BLOCK), 1)
    band_lo = jnp.where(kj >= qi, 0.0, NEG_INF)
    band_hi = jnp.where(kj <= qi, 0.0, NEG_INF)
    bias_s[0] = band_lo
    bias_s[1] = band_hi
    bias_s[2] = jnp.where(first, NEG_INF, band_lo)
    bias_s[3] = jnp.where(last, NEG_INF, band_hi)

    n_pairs = ATTN_GROUP // 2

    def scores(nb, h):
        r0 = nb * BLOCK
        q_st = jnp.concatenate([q_ref[n_pairs * h + pp, pl.ds(r0, BLOCK), :] for pp in range(n_pairs)], axis=0)
        k_st = jnp.concatenate([ka_s[h, pl.ds(r0, 3 * BLOCK), :], kb_s[h, pl.ds(r0, 3 * BLOCK), :]], axis=0)
        s_s[h] = lax.dot_general(q_st, k_st, (((1,), (1,)), ((), ())),
                                 preferred_element_type=F32)

    def softmax_pv(nb, h):
        r0 = nb * BLOCK
        lo_idx = jnp.where(nb == 0, 2, 0)
        hi_idx = jnp.where(nb == nblk - 1, 3, 1)
        v_st = jnp.concatenate([va_s[h, pl.ds(r0, 3 * BLOCK), :], vb_s[h, pl.ds(r0, 3 * BLOCK), :]], axis=0)
        for pp in range(n_pairs):
            for rc in range(0, BLOCK, SOFTMAX_ROWS):
                rr = slice(pp * BLOCK + rc, pp * BLOCK + rc + SOFTMAX_ROWS)
                br = slice(rc, rc + SOFTMAX_ROWS)
                sinks, maxes, sums = [], [], []
                for half in range(2):
                    sink = sink_ref[layer * ATTN_Q_HEADS + ATTN_GROUP * h + 2 * pp + half] * LOG2E
                    c0 = half * 3 * BLOCK
                    s0 = s_s[h, rr, c0:c0 + BLOCK] + bias_s[lo_idx, br, :]
                    s1 = s_s[h, rr, c0 + BLOCK:c0 + 2 * BLOCK]
                    s2 = s_s[h, rr, c0 + 2 * BLOCK:c0 + 3 * BLOCK] + bias_s[hi_idx, br, :]
                    m = jnp.max(jnp.maximum(jnp.maximum(s0, s1), s2), axis=-1, keepdims=True)
                    m = jnp.maximum(m, sink)
                    e0, e1, e2 = jnp.exp2(s0 - m), jnp.exp2(s1 - m), jnp.exp2(s2 - m)
                    p_s[h, rr, c0:c0 + BLOCK] = e0.astype(BF16)
                    p_s[h, rr, c0 + BLOCK:c0 + 2 * BLOCK] = e1.astype(BF16)
                    p_s[h, rr, c0 + 2 * BLOCK:c0 + 3 * BLOCK] = e2.astype(BF16)
                    sinks.append(sink)
                    maxes.append(m)
                    sums.append(jnp.sum(e0 + e1 + e2, axis=-1, keepdims=True))
                sink_term = jnp.exp2(jnp.where(head0, sinks[0], sinks[1]) - jnp.where(head0, maxes[0], maxes[1]))
                scale_s[h, rr, :] = 1.0 / (jnp.where(head0, sums[0], sums[1]) + sink_term)
        o = jnp.dot(p_s[h], v_st, preferred_element_type=F32) * scale_s[h]
        for pp in range(n_pairs):
            attn_s[n_pairs * h + pp, pl.ds(r0, BLOCK), :] = o[pp * BLOCK:(pp + 1) * BLOCK, :]

    scores(0, 0)

    for nb in range(nblk):
        scores(nb, 1)
        softmax_pv(nb, 0)
        if nb + 1 < nblk:
            scores(nb + 1, 0)
        softmax_pv(nb, 1)

    for nb in range(nblk):
        rows = slice(nb * BLOCK, (nb + 1) * BLOCK)
        for hp in range(GM_HEADS // 2):
            cols = slice(hp * LANES, (hp + 1) * LANES)
            vblk = vln_ref[rows, cols]
            lhs = jnp.concatenate([ws_ref[2 * hp], ws_ref[2 * hp + 1]], axis=1)
            rhs = jnp.concatenate([jnp.where(head0, vblk, zero), jnp.where(head0, zero, vblk)], axis=0)
            sgate = jnp.dot(lhs, rhs, preferred_element_type=F32) + gmb_ref[:, cols]
            gm_s[rows, cols] = u_ref[rows, cols] * sgate

    n_ch = CONV_CH // LANES
    for ch in range(n_ch):
        cl = slice(ch * LANES, (ch + 1) * LANES)
        glu_s[ch, 0:CONV_HALO, :] = jnp.where(first, 0.0, glup_ref[:, cl])
        glu_s[ch, CONV_HALO:CONV_HALO + tq, :] = gluc_ref[:, cl]
        glu_s[ch, CONV_HALO + tq:, :] = jnp.where(last, 0.0, glun_ref[:, cl])
    group = SUBLANES * CONV_ROW_STRIDE

    def conv_step(i, carry):
        base = pl.multiple_of(i * (CONV_GROUPS * group), CONV_GROUPS * group)
        for ch in range(n_ch):
            cl = slice(ch * LANES, (ch + 1) * LANES)
            offs = [g * group + ph for g in range(CONV_GROUPS) for ph in range(CONV_ROW_STRIDE)]
            accs = [jnp.broadcast_to(cb_ref[:, cl], (SUBLANES, LANES)) for _ in offs]
            for t in range(CONV_WIDTH):
                w = cw_ref[t:t + 1, cl]
                for a, off in enumerate(offs):
                    start = base + (off + CONV_HALO - CONV_PAD + t)
                    accs[a] = accs[a] + w * glu_s[ch, pl.ds(start, SUBLANES, stride=CONV_ROW_STRIDE), :]
            for a, off in enumerate(offs):
                y_s[ch, pl.ds(base + off, SUBLANES, stride=CONV_ROW_STRIDE), :] = accs[a]
        return carry

    lax.fori_loop(0, tq // (CONV_GROUPS * group), conv_step, 0)

    rchunk = 128
    for p0 in range(0, tq, PROJ_ROWS):
        for r0 in range(p0, p0 + PROJ_ROWS, rchunk):
            rs = slice(r0, r0 + rchunk)
            y = jnp.concatenate([y_s[ch, rs, :] for ch in range(n_ch)], axis=1)
            y = _layer_norm(y, clg_ref[...], clb_ref[...])
            y = y * jax.nn.sigmoid(y)
            merged_s[rs, ATTN_WIDTH + GM_WIDTH:] = _rms(y, og_ref[:, ATTN_WIDTH + GM_WIDTH:]).astype(BF16)
            attn = jnp.concatenate([attn_s[pb, rs, :] for pb in range(ATTN_WIDTH // LANES)], axis=1)
            merged_s[rs, 0:ATTN_WIDTH] = _rms(attn, og_ref[:, 0:ATTN_WIDTH]).astype(BF16)
            merged_s[rs, ATTN_WIDTH:ATTN_WIDTH + GM_WIDTH] = _rms(
                gm_s[rs, :], og_ref[:, ATTN_WIDTH:ATTN_WIDTH + GM_WIDTH]).astype(BF16)
        ps = slice(p0, p0 + PROJ_ROWS)
        o_ref[ps, :] = x_ref[ps, :] + jnp.dot(merged_s[ps, :], wout_ref[...], preferred_element_type=F32)


def _mixer(layer, x, q, kv, u, vln, glu, sink, ws, gm_bias, cw, cb, clg, clb, og, wout,
           wgu, wd, wpg, wpp, batch, seq):
    n = x.shape[0]
    tq = TQ_MIX
    tiles = seq // tq
    blk_per_tile = tq // BLOCK
    blk_per_seq = seq // BLOCK
    halo_per_tile = tq // CONV_HALO
    halo_per_seq = seq // CONV_HALO

    def row(width):
        return pl.BlockSpec((tq, width), lambda b, j, *_: (b * tiles + j, 0))

    def prev(rows, width, per_tile, per_seq):
        return pl.BlockSpec(
            (rows, width), lambda b, j, *_: (b * per_seq + jnp.maximum(j * per_tile - 1, 0), 0))

    def nxt(rows, width, per_tile, per_seq):
        return pl.BlockSpec(
            (rows, width), lambda b, j, *_: (b * per_seq + jnp.minimum((j + 1) * per_tile, per_seq - 1), 0))

    const = lambda shape: _layer_spec(layer, shape)
    kv_rows = tq + 2 * BLOCK
    step_of = lambda b, j, *_: b * tiles + j
    casts = [_cast_specs(layer, rows, cols, batch * tiles, step_of)
             for rows, cols in ((D_MODEL, 2 * D_FF), (D_FF, D_MODEL), (D_MODEL, D_MODEL), (PLE_DIM, D_MODEL))]
    grid_spec = pltpu.PrefetchScalarGridSpec(
        num_scalar_prefetch=1,
        grid=(batch, tiles),
        in_specs=[
            row(D_MODEL), pl.BlockSpec((ATTN_WIDTH // LANES, tq, LANES), lambda b, j, *_: (0, b * tiles + j, 0)),
            prev(BLOCK, 4 * LANES, blk_per_tile, blk_per_seq), row(4 * LANES),
            nxt(BLOCK, 4 * LANES, blk_per_tile, blk_per_seq),
            row(GM_WIDTH), row(GM_WIDTH),
            prev(CONV_HALO, CONV_CH, halo_per_tile, halo_per_seq), row(CONV_CH),
            nxt(CONV_HALO, CONV_CH, halo_per_tile, halo_per_seq),
            const((GM_HEADS, CHUNK, CHUNK)), const((CHUNK, GM_WIDTH)),
            const((CONV_WIDTH, CONV_CH)), const((1, CONV_CH)), const((1, CONV_CH)), const((1, CONV_CH)),
            const((1, D_MODEL)), _resident_spec((D_MODEL, D_MODEL)),
            *[src for src, _ in casts],
        ],
        out_specs=(row(D_MODEL), *[dst for _, dst in casts]),
        scratch_shapes=[
            pltpu.VMEM((ATTN_KV_HEADS, kv_rows, LANES), BF16), pltpu.VMEM((ATTN_KV_HEADS, kv_rows, LANES), BF16),
            pltpu.VMEM((ATTN_KV_HEADS, kv_rows, LANES), BF16), pltpu.VMEM((ATTN_KV_HEADS, kv_rows, LANES), BF16),
            pltpu.VMEM((4, BLOCK, BLOCK), F32),
            pltpu.VMEM((2, ATTN_GROUP // 2 * BLOCK, 2 * 3 * BLOCK), F32),
            pltpu.VMEM((2, ATTN_GROUP // 2 * BLOCK, 2 * 3 * BLOCK), BF16),
            pltpu.VMEM((2, ATTN_GROUP // 2 * BLOCK, LANES), F32),
            pltpu.VMEM((CONV_CH // LANES, tq + 2 * CONV_HALO, LANES), F32),
            pltpu.VMEM((CONV_CH // LANES, tq, LANES), F32),
            pltpu.VMEM((ATTN_WIDTH // LANES, tq, LANES), F32), pltpu.VMEM((tq, GM_WIDTH), F32),
            pltpu.VMEM((tq, D_MODEL), BF16),
        ],
    )
    return pl.pallas_call(
        functools.partial(_mixer_kernel, layer),
        out_shape=(jax.ShapeDtypeStruct((n, D_MODEL), F32),
                   jax.ShapeDtypeStruct((D_MODEL, 2 * D_FF), BF16), jax.ShapeDtypeStruct((D_FF, D_MODEL), BF16),
                   jax.ShapeDtypeStruct((D_MODEL, D_MODEL), BF16), jax.ShapeDtypeStruct((PLE_DIM, D_MODEL), BF16)),
        grid_spec=grid_spec,
        compiler_params=pltpu.CompilerParams(dimension_semantics=("arbitrary", "arbitrary"),
                                             vmem_limit_bytes=VMEM_LIMIT),
        name="mixer",
    )(sink, x, q, kv, kv, kv, u, vln, glu, glu, glu, ws, gm_bias, cw, cb, clg, clb, og, wout,
      wgu, wd, wpg, wpp)


def _ffn_kernel(x_ref, p_ref, gf_ref, wgu_ref, wd_ref, gp_ref, wpg_ref, wpp_ref, o_ref, act_s):
    x = x_ref[...]
    hn = _rms(x, gf_ref[...]).astype(BF16)
    for c0 in range(0, D_FF, FF_CHUNK):
        gate = jnp.dot(hn, wgu_ref[:, c0:c0 + FF_CHUNK], preferred_element_type=F32)
        up = jnp.dot(hn, wgu_ref[:, D_FF + c0:D_FF + c0 + FF_CHUNK], preferred_element_type=F32)
        act_s[:, c0:c0 + FF_CHUNK] = (gate * jax.nn.sigmoid(gate) * up).astype(BF16)
    x = x + jnp.dot(act_s[...], wd_ref[...], preferred_element_type=F32)

    hp = _rms(x, gp_ref[...]).astype(BF16)
    gate = jax.nn.sigmoid(jnp.dot(hp, wpg_ref[...], preferred_element_type=F32))
    proj = jnp.dot(p_ref[...].astype(BF16), wpp_ref[...], preferred_element_type=F32)
    o_ref[...] = x + proj * gate


def _ffn(layer, x, p, gf, wgu, wd, gp, wpg, wpp):
    n = x.shape[0]
    tm = TM_FFN
    row = lambda width: pl.BlockSpec((tm, width), lambda i: (i, 0))
    return pl.pallas_call(
        _ffn_kernel,
        out_shape=jax.ShapeDtypeStruct((n, D_MODEL), F32),
        grid=(n // tm,),
        in_specs=[row(D_MODEL), pl.BlockSpec((None, tm, PLE_DIM), lambda i: (layer, i, 0)),
                  _layer_spec(layer, (1, D_MODEL)),
                  _resident_spec((D_MODEL, 2 * D_FF)), _resident_spec((D_FF, D_MODEL)),
                  _layer_spec(layer, (1, D_MODEL)), _resident_spec((D_MODEL, D_MODEL)),
                  _resident_spec((PLE_DIM, D_MODEL))],
        out_specs=row(D_MODEL),
        scratch_shapes=[pltpu.VMEM((tm, D_FF), BF16)],
        compiler_params=pltpu.CompilerParams(dimension_semantics=("arbitrary",),
                                             vmem_limit_bytes=VMEM_LIMIT),
        name="ffn_ple",
    )(x, p, gf, wgu, wd, gp, wpg, wpp)


def kernel(x, p, positions, norm_mix_g, w_in, q_norm_g, k_norm_g, sink, gm_ln_g, gm_ln_b, gm_ws, gm_bs, conv_w, conv_b, conv_ln_g, conv_ln_b, out_norm_g, w_out, norm_ffn_g, w_gate_up, w_down, ple_norm_g, w_ple_gate, w_ple_proj):
    batch, seq, d = x.shape
    depth = w_in.shape[0]
    n = batch * seq
    assert d == D_MODEL and seq % TQ_MIX == 0
    assert n % TM_PROJ == 0 and n % TM_FFN == 0 and n % TM_ROPE == 0

    vec = lambda a: a[:, None, :]
    ws_b = gm_ws.astype(BF16)
    qg = vec(jnp.tile(q_norm_g, (1, LANES // HEAD_DIM)))
    kg = vec(jnp.tile(k_norm_g, (1, LANES // HEAD_DIM)))
    gm_bias = jnp.repeat(jnp.swapaxes(gm_bs, 1, 2), HEAD_DIM, axis=2)
    sink_flat = sink.reshape(depth * ATTN_Q_HEADS)
    p_flat = p.reshape(depth, n, PLE_DIM)

    tabs = _rope_tables(positions.reshape(1, n))
    xf = x.reshape(n, d)
    for i in range(depth):
        q, kv, u, vln, glu, w_out_b = _inproj(i, xf, vec(norm_mix_g), w_in, qg, kg, vec(gm_ln_g),
                                                 vec(gm_ln_b), tabs, w_out)
        xf, w_gu_b, w_down_b, w_pg_b, w_pp_b = _mixer(
            i, xf, q, kv, u, vln, glu, sink_flat, ws_b, gm_bias, conv_w, vec(conv_b), vec(conv_ln_g),
            vec(conv_ln_b), vec(out_norm_g), w_out_b, w_gate_up, w_down, w_ple_gate, w_ple_proj, batch, seq)
        xf = _ffn(i, xf, p_flat, vec(norm_ffn_g), w_gu_b, w_down_b, vec(ple_norm_g), w_pg_b, w_pp_b)
    return xf.reshape(batch, seq, d)
```
